```python
import jax, jax.numpy as jnp
from jax import lax
import numpy as np

D_MODEL = 2048
BATCH = 8
SEQ = 8192
DEPTH = 1

NORM_EPS = 1e-5
D_SSM = 2048
SSD_HEAD_DIM = 64
SSD_HEADS = D_SSM // SSD_HEAD_DIM
SSD_GROUPS = 4
SSD_HEADS_PER_GROUP = SSD_HEADS // SSD_GROUPS
SSD_STATE = 128
CONV_WIDTH = 4
CHUNK = 256
D_CONV_CH = D_SSM + 2 * SSD_GROUPS * SSD_STATE
D_POOL = 2048
POOL_WINDOWS = (2, 4, 8, 16)
POOL_GROUPS = len(POOL_WINDOWS)
POOL_GROUP_DIM = D_POOL // POOL_GROUPS
D_MIX = D_SSM + D_POOL
D_IN_PROJ = D_SSM + D_CONV_CH + SSD_HEADS + D_POOL
D_FF = -(-8 * D_MODEL // (3 * 256)) * 256

kernel_name = "hybrid_ssd_multiscale_pool_block"


def rms_norm(x, w):
    xf = x.astype(jnp.float32)
    y = xf * lax.rsqrt(jnp.mean(xf * xf, axis=-1, keepdims=True) + NORM_EPS)
    return (y * w.astype(jnp.float32)).astype(x.dtype)


def causal_depthwise_conv(u, w, b):
    ch = u.shape[-1]
    out = lax.conv_general_dilated(
        u, w.astype(u.dtype).reshape(CONV_WIDTH, 1, ch),
        window_strides=(1,), padding=[(CONV_WIDTH - 1, 0)],
        dimension_numbers=("NWC", "WIO", "NWC"), feature_group_count=ch)
    return out + b.astype(u.dtype)


def ssd_chunked_scan(xh, dt, a, b_mat, c_mat):
    bsz, seqlen = xh.shape[:2]
    pad = (-seqlen) % CHUNK
    if pad:
        padw = lambda t: jnp.pad(t, [(0, 0), (0, pad)] + [(0, 0)] * (t.ndim - 2))
        xh, dt, b_mat, c_mat = padw(xh), padw(dt), padw(b_mat), padw(c_mat)
    nc = (seqlen + pad) // CHUNK
    rs = lambda t: t.reshape((bsz, nc, CHUNK) + t.shape[2:])
    xh, dt, b_mat, c_mat = rs(xh), rs(dt), rs(b_mat), rs(c_mat)

    a_cum = jnp.cumsum(dt * a, axis=2)
    xdt = xh * dt[..., None]

    seg = a_cum[:, :, :, None] - a_cum[:, :, None, :]
    causal = jnp.tril(jnp.ones((CHUNK, CHUNK), dtype=bool))[:, :, None, None]
    decay = jnp.exp(jnp.where(causal, seg, -jnp.inf))
    cb = jnp.einsum("bclgn,bcsgn->bclsg", c_mat, b_mat)
    y_diag = jnp.einsum("bclsg,bclsgr,bcsgrp->bclgrp", cb, decay, xdt)

    decay_to_end = jnp.exp(a_cum[:, :, -1:] - a_cum)
    states = jnp.einsum("bclgn,bclgr,bclgrp->bcgrpn", b_mat, decay_to_end, xdt)
    chunk_decay = jnp.exp(a_cum[:, :, -1])

    def step(h, inp):
        s, dcy = inp
        return h * dcy[..., None, None] + s, h
    h0 = jnp.zeros(states.shape[:1] + states.shape[2:], jnp.float32)
    _, prev = lax.scan(step, h0, (jnp.moveaxis(states, 1, 0), jnp.moveaxis(chunk_decay, 1, 0)))
    prev = jnp.moveaxis(prev, 0, 1)

    y_off = jnp.einsum("bclgn,bcgrpn,bclgr->bclgrp", c_mat, prev, jnp.exp(a_cum))
    y = (y_diag + y_off).reshape((bsz, nc * CHUNK) + xh.shape[3:])
    return y[:, :seqlen]


def gated_group_rmsnorm(y, z, w):
    g = y * jax.nn.silu(z.astype(jnp.float32))
    shp = g.shape
    g = g.reshape(shp[:-1] + (SSD_GROUPS, shp[-1] // SSD_GROUPS))
    g = g * lax.rsqrt(jnp.mean(g * g, axis=-1, keepdims=True) + NORM_EPS)
    return g.reshape(shp) * w.astype(jnp.float32)


def multiscale_causal_pool(u):
    uf = u.astype(jnp.float32)
    seqlen = u.shape[1]
    cs = jnp.pad(jnp.cumsum(uf, axis=1), ((0, 0), (1, 0), (0, 0)))
    t = jnp.arange(seqlen)
    outs = []
    for gi, w in enumerate(POOL_WINDOWS):
        sl = slice(gi * POOL_GROUP_DIM, (gi + 1) * POOL_GROUP_DIM)
        csg = cs[..., sl]
        start = jnp.maximum(t + 1 - w, 0)
        win_sum = csg[:, 1:] - csg[:, start]
        count = jnp.minimum(t + 1, w).astype(jnp.float32)
        outs.append(win_sum / count[None, :, None] - uf[..., sl])
    return jnp.stack(outs, axis=2)


def hybrid_mixer(h, w_in, conv_w, conv_b, dt_bias, a_log, d_skip, ssd_norm_w,
                 pool_w, pool_scale, w_out):
    bsz, seqlen, _ = h.shape
    proj = h @ w_in.astype(h.dtype)
    z, xbc, dt_raw, u = jnp.split(
        proj, [D_SSM, D_SSM + D_CONV_CH, D_SSM + D_CONV_CH + SSD_HEADS], axis=-1)

    xbc = jax.nn.silu(causal_depthwise_conv(xbc, conv_w, conv_b)).astype(jnp.float32)
    xs, bm, cm = jnp.split(xbc, [D_SSM, D_SSM + SSD_GROUPS * SSD_STATE], axis=-1)
    dt = jax.nn.softplus(dt_raw.astype(jnp.float32) + dt_bias.astype(jnp.float32))
    a = -jnp.exp(a_log.astype(jnp.float32))
    xh = xs.reshape(bsz, seqlen, SSD_GROUPS, SSD_HEADS_PER_GROUP, SSD_HEAD_DIM)
    y = ssd_chunked_scan(
        xh,
        dt.reshape(bsz, seqlen, SSD_GROUPS, SSD_HEADS_PER_GROUP),
        a.reshape(SSD_GROUPS, SSD_HEADS_PER_GROUP),
        bm.reshape(bsz, seqlen, SSD_GROUPS, SSD_STATE),
        cm.reshape(bsz, seqlen, SSD_GROUPS, SSD_STATE))
    y = y + d_skip.astype(jnp.float32).reshape(SSD_GROUPS, SSD_HEADS_PER_GROUP)[..., None] * xh
    y_ssd = gated_group_rmsnorm(y.reshape(bsz, seqlen, D_SSM), z, ssd_norm_w)

    pooled = multiscale_causal_pool(u)
    y_pool = jnp.einsum("blgc,gcd->blgd", pooled, pool_w.astype(jnp.float32))
    y_pool = y_pool.reshape(bsz, seqlen, D_POOL) * pool_scale.astype(jnp.float32)

    mixed = jnp.concatenate([y_ssd, y_pool], axis=-1).astype(h.dtype)
    return mixed @ w_out.astype(h.dtype)


def swiglu(h, w_gate, w_up, w_down):
    return (jax.nn.silu(h @ w_gate.astype(h.dtype)) * (h @ w_up.astype(h.dtype))) @ w_down.astype(h.dtype)


def _fwd_setup_inputs(seed: int = 0) -> dict:
    key = jax.random.key(seed)
    ks = jax.random.split(key, 20)
    f32 = jnp.float32
    nrm = lambda k, shp, s: jax.random.normal(k, shp, f32) * s
    dt_init = jnp.exp(jax.random.uniform(ks[5], (DEPTH, SSD_HEADS), f32,
                                         np.log(1e-3), np.log(1e-1)))
    return {
        "x": jax.random.normal(ks[0], (BATCH, SEQ, D_MODEL), f32),
        "attn_norm_w": 1.0 + nrm(ks[1], (DEPTH, D_MODEL), 0.02),
        "w_in": nrm(ks[2], (DEPTH, D_MODEL, D_IN_PROJ), D_MODEL ** -0.5),
        "conv_w": nrm(ks[3], (DEPTH, CONV_WIDTH, D_CONV_CH), CONV_WIDTH ** -0.5),
        "conv_b": nrm(ks[4], (DEPTH, D_CONV_CH), 0.02),
        "dt_bias": dt_init + jnp.log(-jnp.expm1(-dt_init)),
        "a_log": jnp.log(jax.random.uniform(ks[6], (DEPTH, SSD_HEADS), f32, 1.0, 16.0)),
        "d_skip": 1.0 + nrm(ks[7], (DEPTH, SSD_HEADS), 0.02),
        "ssd_norm_w": 1.0 + nrm(ks[8], (DEPTH, D_SSM), 0.02),
        "pool_w": nrm(ks[9], (DEPTH, POOL_GROUPS, POOL_GROUP_DIM, POOL_GROUP_DIM), POOL_GROUP_DIM ** -0.5),
        "pool_scale": 1.0 + nrm(ks[10], (DEPTH, D_POOL), 0.02),
        "w_out": nrm(ks[11], (DEPTH, D_MIX, D_MODEL), D_MIX ** -0.5),
        "ffn_norm_w": 1.0 + nrm(ks[12], (DEPTH, D_MODEL), 0.02),
        "w_gate": nrm(ks[13], (DEPTH, D_MODEL, D_FF), D_MODEL ** -0.5),
        "w_up": nrm(ks[14], (DEPTH, D_MODEL, D_FF), D_MODEL ** -0.5),
        "w_down": nrm(ks[15], (DEPTH, D_FF, D_MODEL), D_FF ** -0.5),
        "final_norm_w": 1.0 + nrm(ks[16], (D_MODEL,), 0.02),
    }


def _fwd_reference(x, attn_norm_w, w_in, conv_w, conv_b, dt_bias, a_log, d_skip,
              ssd_norm_w, pool_w, pool_scale, w_out, ffn_norm_w, w_gate, w_up,
              w_down, final_norm_w):
    h = x
    for i in range(DEPTH):
        h = h + hybrid_mixer(rms_norm(h, attn_norm_w[i]), w_in[i], conv_w[i], conv_b[i],
                             dt_bias[i], a_log[i], d_skip[i], ssd_norm_w[i],
                             pool_w[i], pool_scale[i], w_out[i])
        h = h + swiglu(rms_norm(h, ffn_norm_w[i]), w_gate[i], w_up[i], w_down[i])
    return rms_norm(h, final_norm_w)


import jax as _jax
import jax.numpy as _jnp

TWIN_FORMAT = 'train_step'
FWD_PARAMS = ['x', 'attn_norm_w', 'w_in', 'conv_w', 'conv_b', 'dt_bias', 'a_log', 'd_skip', 'ssd_norm_w', 'pool_w', 'pool_scale', 'w_out', 'ffn_norm_w', 'w_gate', 'w_up', 'w_down', 'final_norm_w']
TWIN_WEIGHTS = ['attn_norm_w', 'w_in', 'conv_w', 'conv_b', 'dt_bias', 'a_log', 'd_skip', 'ssd_norm_w', 'pool_w', 'pool_scale', 'w_out', 'ffn_norm_w', 'w_gate', 'w_up', 'w_down', 'final_norm_w']
TWIN_DIFF_INPUT = 'x'
TWIN_INPUTS = ['x', 'attn_norm_w', 'w_in', 'conv_w', 'conv_b', 'dt_bias', 'a_log', 'd_skip', 'ssd_norm_w', 'pool_w', 'pool_scale', 'w_out', 'ffn_norm_w', 'w_gate', 'w_up', 'w_down', 'final_norm_w', 'loss_target', 'm_attn_norm_w', 'm_w_in', 'm_conv_w', 'm_conv_b', 'm_dt_bias', 'm_a_log', 'm_d_skip', 'm_ssd_norm_w', 'm_pool_w', 'm_pool_scale', 'm_w_out', 'm_ffn_norm_w', 'm_w_gate', 'm_w_up', 'm_w_down', 'm_final_norm_w', 'v_attn_norm_w', 'v_w_in', 'v_conv_w', 'v_conv_b', 'v_dt_bias', 'v_a_log', 'v_d_skip', 'v_ssd_norm_w', 'v_pool_w', 'v_pool_scale', 'v_w_out', 'v_ffn_norm_w', 'v_w_gate', 'v_w_up', 'v_w_down', 'v_final_norm_w']
TWIN_OUTPUTS = ['loss', 'grad_x', 'grad_attn_norm_w', 'grad_w_in', 'grad_conv_w', 'grad_conv_b', 'grad_dt_bias', 'grad_a_log', 'grad_d_skip', 'grad_ssd_norm_w', 'grad_pool_w', 'grad_pool_scale', 'grad_w_out', 'grad_ffn_norm_w', 'grad_w_gate', 'grad_w_up', 'grad_w_down', 'grad_final_norm_w', 'delta_attn_norm_w', 'delta_w_in', 'delta_conv_w', 'delta_conv_b', 'delta_dt_bias', 'delta_a_log', 'delta_d_skip', 'delta_ssd_norm_w', 'delta_pool_w', 'delta_pool_scale', 'delta_w_out', 'delta_ffn_norm_w', 'delta_w_gate', 'delta_w_up', 'delta_w_down', 'delta_final_norm_w', 'new_m_attn_norm_w', 'new_m_w_in', 'new_m_conv_w', 'new_m_conv_b', 'new_m_dt_bias', 'new_m_a_log', 'new_m_d_skip', 'new_m_ssd_norm_w', 'new_m_pool_w', 'new_m_pool_scale', 'new_m_w_out', 'new_m_ffn_norm_w', 'new_m_w_gate', 'new_m_w_up', 'new_m_w_down', 'new_m_final_norm_w', 'new_v_attn_norm_w', 'new_v_w_in', 'new_v_conv_w', 'new_v_conv_b', 'new_v_dt_bias', 'new_v_a_log', 'new_v_d_skip', 'new_v_ssd_norm_w', 'new_v_pool_w', 'new_v_pool_scale', 'new_v_w_out', 'new_v_ffn_norm_w', 'new_v_w_gate', 'new_v_w_up', 'new_v_w_down', 'new_v_final_norm_w']
TWIN_LEAF_KINDS = {'loss': 'loss', 'grad_x': 'grad_x', 'grad_attn_norm_w': 'grad_w', 'grad_w_in': 'grad_w', 'grad_conv_w': 'grad_w', 'grad_conv_b': 'grad_w', 'grad_dt_bias': 'grad_w', 'grad_a_log': 'grad_w', 'grad_d_skip': 'grad_w', 'grad_ssd_norm_w': 'grad_w', 'grad_pool_w': 'grad_w', 'grad_pool_scale': 'grad_w', 'grad_w_out': 'grad_w', 'grad_ffn_norm_w': 'grad_w', 'grad_w_gate': 'grad_w', 'grad_w_up': 'grad_w', 'grad_w_down': 'grad_w', 'grad_final_norm_w': 'grad_w', 'delta_attn_norm_w': 'delta_w', 'delta_w_in': 'delta_w', 'delta_conv_w': 'delta_w', 'delta_conv_b': 'delta_w', 'delta_dt_bias': 'delta_w', 'delta_a_log': 'delta_w', 'delta_d_skip': 'delta_w', 'delta_ssd_norm_w': 'delta_w', 'delta_pool_w': 'delta_w', 'delta_pool_scale': 'delta_w', 'delta_w_out': 'delta_w', 'delta_ffn_norm_w': 'delta_w', 'delta_w_gate': 'delta_w', 'delta_w_up': 'delta_w', 'delta_w_down': 'delta_w', 'delta_final_norm_w': 'delta_w', 'new_m_attn_norm_w': 'new_m', 'new_m_w_in': 'new_m', 'new_m_conv_w': 'new_m', 'new_m_conv_b': 'new_m', 'new_m_dt_bias': 'new_m', 'new_m_a_log': 'new_m', 'new_m_d_skip': 'new_m', 'new_m_ssd_norm_w': 'new_m', 'new_m_pool_w': 'new_m', 'new_m_pool_scale': 'new_m', 'new_m_w_out': 'new_m', 'new_m_ffn_norm_w': 'new_m', 'new_m_w_gate': 'new_m', 'new_m_w_up': 'new_m', 'new_m_w_down': 'new_m', 'new_m_final_norm_w': 'new_m', 'new_v_attn_norm_w': 'new_v', 'new_v_w_in': 'new_v', 'new_v_conv_w': 'new_v', 'new_v_conv_b': 'new_v', 'new_v_dt_bias': 'new_v', 'new_v_a_log': 'new_v', 'new_v_d_skip': 'new_v', 'new_v_ssd_norm_w': 'new_v', 'new_v_pool_w': 'new_v', 'new_v_pool_scale': 'new_v', 'new_v_w_out': 'new_v', 'new_v_ffn_norm_w': 'new_v', 'new_v_w_gate': 'new_v', 'new_v_w_up': 'new_v', 'new_v_w_down': 'new_v', 'new_v_final_norm_w': 'new_v'}


def _forward(args):
    return _fwd_reference(*[args[k] for k in FWD_PARAMS])


def _output_shape():
    def fwd():
        inp = _fwd_setup_inputs(0)
        return _fwd_reference(*[inp[k] for k in FWD_PARAMS])
    out = _jax.eval_shape(fwd)
    return out.shape, out.dtype

N_MICROBATCH = 1
ADAM_LR = 0.001
ADAM_B1 = 0.9
ADAM_B2 = 0.999
ADAM_EPS = 1e-08
ADAM_WD = 0.01
ADAM_STEP = 10
PER_EXAMPLE_BATCH_AXIS = {'x': 0, 'loss_target': 0}
SHARED_INPUTS = []
_WEIGHT_DTYPES = {'attn_norm_w': _jnp.float32, 'w_in': _jnp.float32, 'conv_w': _jnp.float32, 'conv_b': _jnp.float32, 'dt_bias': _jnp.float32, 'a_log': _jnp.float32, 'd_skip': _jnp.float32, 'ssd_norm_w': _jnp.float32, 'pool_w': _jnp.float32, 'pool_scale': _jnp.float32, 'w_out': _jnp.float32, 'ffn_norm_w': _jnp.float32, 'w_gate': _jnp.float32, 'w_up': _jnp.float32, 'w_down': _jnp.float32, 'final_norm_w': _jnp.float32}
MOMENT_SCALE = {'attn_norm_w': 1.250245e-01, 'w_in': 6.508516e-02, 'conv_w': 6.090982e-02, 'conv_b': 9.105736e-02, 'dt_bias': 1.790654e-01, 'a_log': 1.998955e-01, 'd_skip': 3.810445e-01, 'ssd_norm_w': 7.338261e-02, 'pool_w': 6.171850e-02, 'pool_scale': 6.318369e-02, 'w_out': 9.341076e-02, 'ffn_norm_w': 7.263289e-02, 'w_gate': 3.118440e-02, 'w_up': 3.018374e-02, 'w_down': 4.998104e-02, 'final_norm_w': 3.196881e+01}


def _to_microbatches(a, axis):
    t = _jnp.moveaxis(a, axis, 0)
    t = t.reshape((N_MICROBATCH, t.shape[0] // N_MICROBATCH) + t.shape[1:])
    return _jnp.moveaxis(t, 1, axis + 1)


def setup_inputs(seed: int = 0) -> dict:
    inp = _fwd_setup_inputs(seed)
    key = _jax.random.fold_in(_jax.random.key(seed), 7919)
    shape, _ = _output_shape()
    out = dict(inp)
    out["loss_target"] = _jax.random.normal(_jax.random.fold_in(key, 0), shape, _jnp.float32)
    for i, name in enumerate(TWIN_WEIGHTS):
        w = inp[name].astype(_jnp.float32)
        if MOMENT_SCALE is None:
            s = _jnp.sqrt(_jnp.mean(_jnp.square(w)) + 1e-30)
        else:
            s = MOMENT_SCALE[name]
        km, kv = _jax.random.split(_jax.random.fold_in(key, i + 1))
        out[name] = w
        out["m_" + name] = s * _jax.random.normal(km, w.shape, _jnp.float32)
        out["v_" + name] = (s * s) * _jax.random.uniform(kv, w.shape, _jnp.float32, 0.5, 1.5)
    if N_MICROBATCH > 1:
        for name, axis in PER_EXAMPLE_BATCH_AXIS.items():
            out[name] = _to_microbatches(out[name], axis)
    return {'x': out['x'], 'attn_norm_w': out['attn_norm_w'], 'w_in': out['w_in'], 'conv_w': out['conv_w'], 'conv_b': out['conv_b'], 'dt_bias': out['dt_bias'], 'a_log': out['a_log'], 'd_skip': out['d_skip'], 'ssd_norm_w': out['ssd_norm_w'], 'pool_w': out['pool_w'], 'pool_scale': out['pool_scale'], 'w_out': out['w_out'], 'ffn_norm_w': out['ffn_norm_w'], 'w_gate': out['w_gate'], 'w_up': out['w_up'], 'w_down': out['w_down'], 'final_norm_w': out['final_norm_w'], 'loss_target': out['loss_target'], 'm_attn_norm_w': out['m_attn_norm_w'], 'm_w_in': out['m_w_in'], 'm_conv_w': out['m_conv_w'], 'm_conv_b': out['m_conv_b'], 'm_dt_bias': out['m_dt_bias'], 'm_a_log': out['m_a_log'], 'm_d_skip': out['m_d_skip'], 'm_ssd_norm_w': out['m_ssd_norm_w'], 'm_pool_w': out['m_pool_w'], 'm_pool_scale': out['m_pool_scale'], 'm_w_out': out['m_w_out'], 'm_ffn_norm_w': out['m_ffn_norm_w'], 'm_w_gate': out['m_w_gate'], 'm_w_up': out['m_w_up'], 'm_w_down': out['m_w_down'], 'm_final_norm_w': out['m_final_norm_w'], 'v_attn_norm_w': out['v_attn_norm_w'], 'v_w_in': out['v_w_in'], 'v_conv_w': out['v_conv_w'], 'v_conv_b': out['v_conv_b'], 'v_dt_bias': out['v_dt_bias'], 'v_a_log': out['v_a_log'], 'v_d_skip': out['v_d_skip'], 'v_ssd_norm_w': out['v_ssd_norm_w'], 'v_pool_w': out['v_pool_w'], 'v_pool_scale': out['v_pool_scale'], 'v_w_out': out['v_w_out'], 'v_ffn_norm_w': out['v_ffn_norm_w'], 'v_w_gate': out['v_w_gate'], 'v_w_up': out['v_w_up'], 'v_w_down': out['v_w_down'], 'v_final_norm_w': out['v_final_norm_w']}


def _loss(weights, diff, rest, loss_target):
    with _jax.named_scope("forward"):
        args = {**rest, TWIN_DIFF_INPUT: diff, **{k: w.astype(_WEIGHT_DTYPES[k]) for k, w in weights.items()}}
        y = _forward(args)
    with _jax.named_scope("loss_head"):
        err = _jnp.square(y.astype(_jnp.float32) - loss_target)
        return 0.5 * _jnp.sum(_jnp.mean(err, axis=-1)) if err.ndim else 0.5 * err


def _adamw(w, g, m, v):
    m = ADAM_B1 * m + (1.0 - ADAM_B1) * g
    v = ADAM_B2 * v + (1.0 - ADAM_B2) * _jnp.square(g)
    m_hat = m / (1.0 - ADAM_B1 ** ADAM_STEP)
    v_hat = v / (1.0 - ADAM_B2 ** ADAM_STEP)
    delta = -ADAM_LR * (m_hat / (_jnp.sqrt(v_hat) + ADAM_EPS) + ADAM_WD * w)
    return delta, m, v


def reference(x, attn_norm_w, w_in, conv_w, conv_b, dt_bias, a_log, d_skip, ssd_norm_w, pool_w, pool_scale, w_out, ffn_norm_w, w_gate, w_up, w_down, final_norm_w, loss_target, m_attn_norm_w, m_w_in, m_conv_w, m_conv_b, m_dt_bias, m_a_log, m_d_skip, m_ssd_norm_w, m_pool_w, m_pool_scale, m_w_out, m_ffn_norm_w, m_w_gate, m_w_up, m_w_down, m_final_norm_w, v_attn_norm_w, v_w_in, v_conv_w, v_conv_b, v_dt_bias, v_a_log, v_d_skip, v_ssd_norm_w, v_pool_w, v_pool_scale, v_w_out, v_ffn_norm_w, v_w_gate, v_w_up, v_w_down, v_final_norm_w):
    given = dict(x=x, attn_norm_w=attn_norm_w, w_in=w_in, conv_w=conv_w, conv_b=conv_b, dt_bias=dt_bias, a_log=a_log, d_skip=d_skip, ssd_norm_w=ssd_norm_w, pool_w=pool_w, pool_scale=pool_scale, w_out=w_out, ffn_norm_w=ffn_norm_w, w_gate=w_gate, w_up=w_up, w_down=w_down, final_norm_w=final_norm_w, loss_target=loss_target, m_attn_norm_w=m_attn_norm_w, m_w_in=m_w_in, m_conv_w=m_conv_w, m_conv_b=m_conv_b, m_dt_bias=m_dt_bias, m_a_log=m_a_log, m_d_skip=m_d_skip, m_ssd_norm_w=m_ssd_norm_w, m_pool_w=m_pool_w, m_pool_scale=m_pool_scale, m_w_out=m_w_out, m_ffn_norm_w=m_ffn_norm_w, m_w_gate=m_w_gate, m_w_up=m_w_up, m_w_down=m_w_down, m_final_norm_w=m_final_norm_w, v_attn_norm_w=v_attn_norm_w, v_w_in=v_w_in, v_conv_w=v_conv_w, v_conv_b=v_conv_b, v_dt_bias=v_dt_bias, v_a_log=v_a_log, v_d_skip=v_d_skip, v_ssd_norm_w=v_ssd_norm_w, v_pool_w=v_pool_w, v_pool_scale=v_pool_scale, v_w_out=v_w_out, v_ffn_norm_w=v_ffn_norm_w, v_w_gate=v_w_gate, v_w_up=v_w_up, v_w_down=v_w_down, v_final_norm_w=v_final_norm_w)
    weights = {n: given[n] for n in TWIN_WEIGHTS}
    shared = {n: given[n] for n in SHARED_INPUTS}
    per_example = {n: given[n] for n in ['x']}
    grad_fn = _jax.value_and_grad(_loss, argnums=(0, 1))

    def one_microbatch(ex, loss_target):
        ex = dict(ex)
        diff = ex.pop(TWIN_DIFF_INPUT)
        return grad_fn(weights, diff, {**shared, **ex}, loss_target)

    if N_MICROBATCH == 1:
        loss, (grad_w, grad_x) = one_microbatch(per_example, given["loss_target"])
    else:
        def body(carry, xs):
            loss_sum, grad_sum = carry
            l_k, (gw_k, gx_k) = one_microbatch(xs[0], xs[1])
            with _jax.named_scope("update"):
                return (loss_sum + l_k, _jax.tree.map(_jnp.add, grad_sum, gw_k)), gx_k

        init = (_jnp.zeros((), _jnp.float32), _jax.tree.map(_jnp.zeros_like, weights))
        (loss, grad_w), grad_x = _jax.lax.scan(body, init, (per_example, given["loss_target"]))
    with _jax.named_scope("update"):
        delta_w, new_m, new_v = {}, {}, {}
        for n in TWIN_WEIGHTS:
            delta_w[n], new_m[n], new_v[n] = _adamw(weights[n], grad_w[n], given["m_" + n], given["v_" + n])
    return (loss, grad_x, *[grad_w[n] for n in TWIN_WEIGHTS], *[delta_w[n] for n in TWIN_WEIGHTS],
            *[new_m[n] for n in TWIN_WEIGHTS], *[new_v[n] for n in TWIN_WEIGHTS])
```

```python
import functools

import jax
import jax.numpy as jnp
from jax import lax
from jax.experimental import pallas as pl
from jax.experimental.pallas import tpu as pltpu

F32 = jnp.float32
BF16 = jnp.bfloat16

NORM_EPS = 1e-5
HEAD_DIM = 64
STATE = 128
CHUNK = 256
GROUPS = 4
CONV_K = 4
POOL_WINDOWS = (2, 4, 8, 16)
POOL_HALO = 16
CONV_HALO = 8
LANE = 128
N_DEV = 8
ADAM_LR, ADAM_B1, ADAM_B2, ADAM_EPS, ADAM_WD, ADAM_STEP = 0.001, 0.9, 0.999, 1e-08, 0.01, 10
NEG_BIG = -1e30
MESH_ID = pl.DeviceIdType.MESH


def _tile(n, pref, mult):
    if n <= pref:
        return n
    t = (pref // mult) * mult
    while t >= mult:
        if n % t == 0:
            return t
        t -= mult
    raise ValueError(f"no tile for {n} (pref {pref}, mult {mult})")


def _params(*sem):
    return pltpu.CompilerParams(dimension_semantics=sem)


def _sigmoid(x):
    return 1.0 / (1.0 + jnp.exp(-x))


def _softplus(x):
    return jnp.maximum(x, 0.0) + jnp.log(1.0 + jnp.exp(-jnp.abs(x)))


def _dot(a, b, ta=False, tb=False):
    dn = (((0 if ta else 1,), (1 if tb else 0,)), ((), ()))
    return lax.dot_general(a.astype(BF16), b.astype(BF16), dn, preferred_element_type=F32)


def _mm(a, b, *, name, ta=False, tb=False, out_dtype=F32, tm=1024, tn=1024, tk=2048, residual=None):
    m = a.shape[1] if ta else a.shape[0]
    k = a.shape[0] if ta else a.shape[1]
    n = b.shape[0] if tb else b.shape[1]
    assert k == (b.shape[1] if tb else b.shape[0])
    tm, tn = _tile(m, tm, LANE if ta else 8), _tile(n, tn, LANE)
    tk = _tile(k, tk, LANE if not (ta and tb) else LANE)
    nk = k // tk
    has_res = residual is not None

    def body(*refs):
        a_ref, b_ref = refs[0], refs[1]
        r_ref = refs[2] if has_res else None
        o_ref = refs[3] if has_res else refs[2]
        part = _dot(a_ref[...], b_ref[...], ta, tb)

        def finish(acc):
            if has_res:
                acc = acc + r_ref[...]
            o_ref[...] = acc.astype(o_ref.dtype)

        if nk == 1:
            finish(part)
        else:
            acc_ref = refs[-1]
            kk = pl.program_id(2)

            @pl.when(kk == 0)
            def _():
                acc_ref[...] = part

            @pl.when(kk > 0)
            def _():
                acc_ref[...] += part

            @pl.when(kk == nk - 1)
            def _():
                finish(acc_ref[...])

    a_spec = pl.BlockSpec((tk, tm), lambda i, j, kk: (kk, i)) if ta else pl.BlockSpec((tm, tk), lambda i, j, kk: (i, kk))
    b_spec = pl.BlockSpec((tn, tk), lambda i, j, kk: (j, kk)) if tb else pl.BlockSpec((tk, tn), lambda i, j, kk: (kk, j))
    o_spec = pl.BlockSpec((tm, tn), lambda i, j, kk: (i, j))
    in_specs = [a_spec, b_spec] + ([o_spec] if has_res else [])
    args = (a, b) + ((residual,) if has_res else ())
    return pl.pallas_call(
        body, name=name, grid=(m // tm, n // tn, nk), in_specs=in_specs, out_specs=o_spec,
        out_shape=jax.ShapeDtypeStruct((m, n), out_dtype),
        scratch_shapes=[pltpu.VMEM((tm, tn), F32)] if nk > 1 else [],
        compiler_params=_params("parallel", "parallel", "arbitrary"),
    )(*args)


def _gate_up(h, wg, wu, *, name, tm=1024, tn=512):
    m, k = h.shape
    n = wg.shape[1]
    tm, tn = _tile(m, tm, 8), _tile(n, tn, LANE)

    def body(h_ref, wg_ref, wu_ref, g_ref, u_ref, a_ref):
        hb = h_ref[...]
        g = _dot(hb, wg_ref[...])
        u = _dot(hb, wu_ref[...])
        g_ref[...] = g
        u_ref[...] = u
        a_ref[...] = (g * _sigmoid(g) * u).astype(a_ref.dtype)

    o_spec = pl.BlockSpec((tm, tn), lambda i, j: (i, j))
    w_spec = pl.BlockSpec((k, tn), lambda i, j: (0, j))
    return pl.pallas_call(
        body, name=name, grid=(m // tm, n // tn),
        in_specs=[pl.BlockSpec((tm, k), lambda i, j: (i, 0)), w_spec, w_spec], out_specs=[o_spec, o_spec, o_spec],
        out_shape=[jax.ShapeDtypeStruct((m, n), F32), jax.ShapeDtypeStruct((m, n), F32), jax.ShapeDtypeStruct((m, n), BF16)],
        compiler_params=_params("parallel", "parallel"),
    )(h, wg, wu)


def _gate_up_bwd(dh, wd, g, u, *, name, tm=1024, tn=512):
    m, k = dh.shape
    n = wd.shape[0]
    tm, tn = _tile(m, tm, 8), _tile(n, tn, LANE)

    def body(dh_ref, wd_ref, g_ref, u_ref, dg_ref, du_ref):
        da = _dot(dh_ref[...], wd_ref[...], tb=True)
        g = g_ref[...]
        s = _sigmoid(g)
        dg_ref[...] = (da * u_ref[...] * (s * (1.0 + g * (1.0 - s)))).astype(dg_ref.dtype)
        du_ref[...] = (da * (g * s)).astype(du_ref.dtype)

    o_spec = pl.BlockSpec((tm, tn), lambda i, j: (i, j))
    return pl.pallas_call(
        body, name=name, grid=(m // tm, n // tn),
        in_specs=[pl.BlockSpec((tm, k), lambda i, j: (i, 0)), pl.BlockSpec((tn, k), lambda i, j: (j, 0)), o_spec, o_spec],
        out_specs=[o_spec, o_spec],
        out_shape=[jax.ShapeDtypeStruct((m, n), BF16), jax.ShapeDtypeStruct((m, n), BF16)],
        compiler_params=_params("parallel", "parallel"),
    )(dh, wd, g, u)


def _rms_fwd(x, w, *, name, tl=512):
    l, d = x.shape
    tl = _tile(l, tl, 16)

    def body(x_ref, w_ref, o_ref):
        xv = x_ref[...]
        r = lax.rsqrt(jnp.mean(xv * xv, axis=-1, keepdims=True) + NORM_EPS)
        o_ref[...] = (xv * r * w_ref[...]).astype(o_ref.dtype)

    row = pl.BlockSpec((tl, d), lambda i: (i, 0))
    return pl.pallas_call(
        body, name=name, grid=(l // tl,), in_specs=[row, pl.BlockSpec((1, d), lambda i: (0, 0))], out_specs=row,
        out_shape=jax.ShapeDtypeStruct((l, d), BF16), compiler_params=_params("parallel"),
    )(x, w)


def _rms_bwd(x, w, dy, add, *, name, tl=256):
    l, d = x.shape
    tl = _tile(l, tl, 16)

    def body(x_ref, w_ref, dy_ref, add_ref, dx_ref, dxb_ref, dw_ref):
        i = pl.program_id(0)
        xv = x_ref[...]
        r = lax.rsqrt(jnp.mean(xv * xv, axis=-1, keepdims=True) + NORM_EPS)
        xh = xv * r
        dyv = dy_ref[...].astype(F32)
        dxh = dyv * w_ref[...]
        dx = r * (dxh - xh * jnp.mean(dxh * xh, axis=-1, keepdims=True)) + add_ref[...]
        dx_ref[...] = dx
        dxb_ref[...] = dx.astype(BF16)
        part = jnp.sum(dyv * xh, axis=0, keepdims=True)

        @pl.when(i == 0)
        def _():
            dw_ref[...] = part

        @pl.when(i > 0)
        def _():
            dw_ref[...] += part

    row = pl.BlockSpec((tl, d), lambda i: (i, 0))
    vec = pl.BlockSpec((1, d), lambda i: (0, 0))
    return pl.pallas_call(
        body, name=name, grid=(l // tl,), in_specs=[row, vec, row, row], out_specs=[row, row, vec],
        out_shape=[jax.ShapeDtypeStruct((l, d), F32), jax.ShapeDtypeStruct((l, d), BF16), jax.ShapeDtypeStruct((1, d), F32)],
        compiler_params=_params("arbitrary"),
    )(x, w, dy, add)


def _loss_head(h, w, target, *, name, tl=256):
    l, d = h.shape
    tl = _tile(l, tl, 16)

    def body(x_ref, w_ref, t_ref, loss_ref, dx_ref, dxb_ref, dw_ref):
        i = pl.program_id(0)
        xv = x_ref[...]
        r = lax.rsqrt(jnp.mean(xv * xv, axis=-1, keepdims=True) + NORM_EPS)
        xh = xv * r
        wv = w_ref[...]
        diff = xh * wv - t_ref[...]
        lpart = 0.5 * jnp.sum(jnp.mean(diff * diff, axis=-1, keepdims=True), axis=0, keepdims=True)
        dyv = diff * (1.0 / d)
        dxh = dyv * wv
        dx = r * (dxh - xh * jnp.mean(dxh * xh, axis=-1, keepdims=True))
        dx_ref[...] = dx
        dxb_ref[...] = dx.astype(BF16)
        part = jnp.sum(dyv * xh, axis=0, keepdims=True)

        @pl.when(i == 0)
        def _():
            dw_ref[...] = part
            loss_ref[...] = lpart

        @pl.when(i > 0)
        def _():
            dw_ref[...] += part
            loss_ref[...] += lpart

    row = pl.BlockSpec((tl, d), lambda i: (i, 0))
    vec = pl.BlockSpec((1, d), lambda i: (0, 0))
    one = pl.BlockSpec((1, 1), lambda i: (0, 0))
    return pl.pallas_call(
        body, name=name, grid=(l // tl,), in_specs=[row, vec, row], out_specs=[one, row, row, vec],
        out_shape=[jax.ShapeDtypeStruct((1, 1), F32), jax.ShapeDtypeStruct((l, d), F32),
                   jax.ShapeDtypeStruct((l, d), BF16), jax.ShapeDtypeStruct((1, d), F32)],
        compiler_params=_params("arbitrary"),
    )(h, w, target)


def _conv_pre(ext, w_ref):
    acc = ext * w_ref[CONV_K - 1:CONV_K, :]
    for k in range(CONV_K - 1):
        acc = acc + pltpu.roll(ext, CONV_K - 1 - k, axis=0) * w_ref[k:k + 1, :]
    return acc


def _conv_fwd(proj, col0, cw, cb, *, name, tl=512, tc=512):
    l = proj.shape[0]
    c = cw.shape[1]
    tl, tc = _tile(l, tl, 16), _tile(c, tc, LANE)
    assert col0 % tc == 0
    off, hb = col0 // tc, tl // CONV_HALO

    def body(x_ref, halo_ref, w_ref, b_ref, o_ref):
        i = pl.program_id(0)
        halo = jnp.where(i > 0, halo_ref[...], 0.0)
        ext = jnp.concatenate([halo, x_ref[...]], axis=0)
        pre = _conv_pre(ext, w_ref)[CONV_HALO:] + b_ref[...]
        o_ref[...] = pre * _sigmoid(pre)

    return pl.pallas_call(
        body, name=name, grid=(l // tl, c // tc),
        in_specs=[pl.BlockSpec((tl, tc), lambda i, j: (i, off + j)),
                  pl.BlockSpec((CONV_HALO, tc), lambda i, j: (jnp.maximum(i * hb - 1, 0), off + j)),
                  pl.BlockSpec((CONV_K, tc), lambda i, j: (0, j)), pl.BlockSpec((1, tc), lambda i, j: (0, j))],
        out_specs=pl.BlockSpec((tl, tc), lambda i, j: (i, j)),
        out_shape=jax.ShapeDtypeStruct((l, c), F32), compiler_params=_params("parallel", "parallel"),
    )(proj, proj, cw, cb)


def _conv_bwd(proj, col0, cw, cb, dout, *, name, tl=512, tc=512):
    l = proj.shape[0]
    c = dout.shape[1]
    tl, tc = _tile(l, tl, 16), _tile(c, tc, LANE)
    assert col0 % tc == 0
    off, hb, nt = col0 // tc, tl // CONV_HALO, l // tl
    n_ext = tl + CONV_HALO

    def body(x_ref, prev_ref, nxt_ref, d_ref, dn_ref, w_ref, b_ref, dx_ref, dw_ref, db_ref):
        i = pl.program_id(1)
        last = i == nt - 1
        halo = jnp.where(i > 0, prev_ref[...], 0.0)
        ext = jnp.concatenate([halo, x_ref[...], nxt_ref[...]], axis=0)
        pre = _conv_pre(ext, w_ref)[CONV_HALO:] + b_ref[...]
        dext = jnp.concatenate([d_ref[...], jnp.where(last, 0.0, dn_ref[...])], axis=0)
        s = _sigmoid(pre)
        dpre = dext * (s * (1.0 + pre * (1.0 - s)))
        dx = dpre * w_ref[CONV_K - 1:CONV_K, :]
        for k in range(CONV_K - 1):
            dx = dx + pltpu.roll(dpre, n_ext - (CONV_K - 1 - k), axis=0) * w_ref[k:k + 1, :]
        dx_ref[...] = dx[:tl].astype(dx_ref.dtype)
        dp = dpre[:tl]
        rows = [jnp.sum(dp * pltpu.roll(ext, CONV_K - 1 - k, axis=0)[CONV_HALO:CONV_HALO + tl], axis=0, keepdims=True)
                for k in range(CONV_K - 1)]
        rows.append(jnp.sum(dp * ext[CONV_HALO:CONV_HALO + tl], axis=0, keepdims=True))
        dwp = jnp.concatenate(rows, axis=0)
        dbp = jnp.sum(dp, axis=0, keepdims=True)

        @pl.when(i == 0)
        def _():
            dw_ref[...] = dwp
            db_ref[...] = dbp

        @pl.when(i > 0)
        def _():
            dw_ref[...] += dwp
            db_ref[...] += dbp

    cur = lambda j, i: (i, j)
    nxt = lambda j, i: (jnp.minimum((i + 1) * hb, nt * hb - 1), j)
    return pl.pallas_call(
        body, name=name, grid=(c // tc, nt),
        in_specs=[pl.BlockSpec((tl, tc), lambda j, i: (i, off + j)),
                  pl.BlockSpec((CONV_HALO, tc), lambda j, i: (jnp.maximum(i * hb - 1, 0), off + j)),
                  pl.BlockSpec((CONV_HALO, tc), lambda j, i: (jnp.minimum((i + 1) * hb, nt * hb - 1), off + j)),
                  pl.BlockSpec((tl, tc), cur), pl.BlockSpec((CONV_HALO, tc), nxt),
                  pl.BlockSpec((CONV_K, tc), lambda j, i: (0, j)), pl.BlockSpec((1, tc), lambda j, i: (0, j))],
        out_specs=[pl.BlockSpec((tl, tc), cur), pl.BlockSpec((CONV_K, tc), lambda j, i: (0, j)),
                   pl.BlockSpec((1, tc), lambda j, i: (0, j))],
        out_shape=[jax.ShapeDtypeStruct((l, c), BF16), jax.ShapeDtypeStruct((CONV_K, c), F32),
                   jax.ShapeDtypeStruct((1, c), F32)],
        compiler_params=_params("parallel", "arbitrary"),
    )(proj, proj, proj, dout, dout, cw, cb)


def _pool_fwd(proj, col0, d_pool, *, name, tl=256):
    l = proj.shape[0]
    tl = _tile(l, tl, POOL_HALO)
    assert col0 % d_pool == 0
    off, hb, pg = col0 // d_pool, tl // POOL_HALO, d_pool // GROUPS

    def body(u_ref, halo_ref, o_ref):
        i = pl.program_id(0)
        halo = jnp.where(i > 0, halo_ref[...], 0.0)
        ext = jnp.concatenate([halo, u_ref[...]], axis=0)
        t = i * tl + lax.broadcasted_iota(jnp.int32, (tl, 1), 0)
        s, width = ext, 1
        for gi, win in enumerate(POOL_WINDOWS):
            while width < win:
                s = s + pltpu.roll(s, width, axis=0)
                width *= 2
            cnt = jnp.minimum(t + 1, win).astype(F32)
            sl = slice(gi * pg, (gi + 1) * pg)
            o_ref[:, sl] = (s[POOL_HALO:, sl] / cnt - ext[POOL_HALO:, sl]).astype(o_ref.dtype)

    return pl.pallas_call(
        body, name=name, grid=(l // tl,),
        in_specs=[pl.BlockSpec((tl, d_pool), lambda i: (i, off)),
                  pl.BlockSpec((POOL_HALO, d_pool), lambda i: (jnp.maximum(i * hb - 1, 0), off))],
        out_specs=pl.BlockSpec((tl, d_pool), lambda i: (i, 0)),
        out_shape=jax.ShapeDtypeStruct((l, d_pool), BF16), compiler_params=_params("parallel"),
    )(proj, proj)


def _pool_bwd(dp, *, name, tl=256):
    l, d_pool = dp.shape
    tl = _tile(l, tl, POOL_HALO)
    hb, nt, pg = tl // POOL_HALO, l // tl, d_pool // GROUPS
    n_ext = tl + POOL_HALO

    def body(d_ref, nxt_ref, o_ref):
        i = pl.program_id(0)
        ext = jnp.concatenate([d_ref[...], jnp.where(i == nt - 1, 0.0, nxt_ref[...])], axis=0)
        t = i * tl + lax.broadcasted_iota(jnp.int32, (n_ext, 1), 0)
        for gi, win in enumerate(POOL_WINDOWS):
            sl = slice(gi * pg, (gi + 1) * pg)
            dg = ext[:, sl]
            s = dg / jnp.minimum(t + 1, win).astype(F32)
            width = 1
            while width < win:
                s = s + pltpu.roll(s, n_ext - width, axis=0)
                width *= 2
            o_ref[:, sl] = (s[:tl] - dg[:tl]).astype(o_ref.dtype)

    return pl.pallas_call(
        body, name=name, grid=(nt,),
        in_specs=[pl.BlockSpec((tl, d_pool), lambda i: (i, 0)),
                  pl.BlockSpec((POOL_HALO, d_pool), lambda i: (jnp.minimum((i + 1) * hb, nt * hb - 1), 0))],
        out_specs=pl.BlockSpec((tl, d_pool), lambda i: (i, 0)),
        out_shape=jax.ShapeDtypeStruct((l, d_pool), BF16), compiler_params=_params("parallel"),
    )(dp, dp)


def _pool_mix_fwd(pooled, pw, scale, mixed, col0, *, name, tm=1024):
    l, d_pool = pooled.shape
    pg = d_pool // GROUPS
    tm = _tile(l, tm, 16)
    assert col0 % pg == 0
    off = col0 // pg

    def body(a_ref, w_ref, s_ref, mixed_in, raw_ref, mixed_ref):
        del mixed_in
        raw = _dot(a_ref[...], w_ref[...])
        raw_ref[...] = raw
        mixed_ref[...] = (raw * s_ref[...]).astype(mixed_ref.dtype)

    blk = pl.BlockSpec((tm, pg), lambda i, g: (i, g))
    return pl.pallas_call(
        body, name=name, grid=(l // tm, GROUPS),
        in_specs=[blk, pl.BlockSpec((None, pg, pg), lambda i, g: (g, 0, 0)), pl.BlockSpec((1, pg), lambda i, g: (0, g)),
                  pl.BlockSpec(memory_space=pl.ANY)],
        out_specs=[blk, pl.BlockSpec((tm, pg), lambda i, g: (i, off + g))],
        out_shape=[jax.ShapeDtypeStruct((l, d_pool), F32), jax.ShapeDtypeStruct(mixed.shape, mixed.dtype)],
        input_output_aliases={3: 1}, compiler_params=_params("parallel", "parallel"),
    )(pooled, pw, scale, mixed)


def _pool_scale_bwd(dmixed, col0, raw, scale, *, name, tl=512):
    l, d_pool = raw.shape
    tl, tc = _tile(l, tl, 16), _tile(d_pool, 512, LANE)
    assert col0 % tc == 0
    off = col0 // tc

    def body(d_ref, raw_ref, s_ref, o_ref, ds_ref):
        i = pl.program_id(1)
        dv = d_ref[...]
        o_ref[...] = (dv * s_ref[...]).astype(o_ref.dtype)
        part = jnp.sum(dv * raw_ref[...], axis=0, keepdims=True)

        @pl.when(i == 0)
        def _():
            ds_ref[...] = part

        @pl.when(i > 0)
        def _():
            ds_ref[...] += part

    blk = pl.BlockSpec((tl, tc), lambda j, i: (i, j))
    vec = pl.BlockSpec((1, tc), lambda j, i: (0, j))
    return pl.pallas_call(
        body, name=name, grid=(d_pool // tc, l // tl),
        in_specs=[pl.BlockSpec((tl, tc), lambda j, i: (i, off + j)), blk, vec], out_specs=[blk, vec],
        out_shape=[jax.ShapeDtypeStruct((l, d_pool), BF16), jax.ShapeDtypeStruct((1, d_pool), F32)],
        compiler_params=_params("parallel", "arbitrary"),
    )(dmixed, raw, scale)


def _group_mm_nt(a, w, *, name, tm=1024):
    l, d_pool = a.shape
    pg = d_pool // GROUPS
    tm = _tile(l, tm, 16)

    def body(a_ref, w_ref, o_ref):
        o_ref[...] = _dot(a_ref[...], w_ref[...], tb=True)

    blk = pl.BlockSpec((tm, pg), lambda i, g: (i, g))
    return pl.pallas_call(
        body, name=name, grid=(l // tm, GROUPS),
        in_specs=[blk, pl.BlockSpec((None, pg, pg), lambda i, g: (g, 0, 0))], out_specs=blk,
        out_shape=jax.ShapeDtypeStruct((l, d_pool), F32), compiler_params=_params("parallel", "parallel"),
    )(a, w)


def _group_mm_tn(a, b, *, name, tk=1024):
    l, d_pool = a.shape
    pg = d_pool // GROUPS
    tk = _tile(l, tk, 16)

    def body(a_ref, b_ref, o_ref):
        kk = pl.program_id(1)
        part = _dot(a_ref[...], b_ref[...], ta=True)

        @pl.when(kk == 0)
        def _():
            o_ref[...] = part

        @pl.when(kk > 0)
        def _():
            o_ref[...] += part

    blk = pl.BlockSpec((tk, pg), lambda g, kk: (kk, g))
    return pl.pallas_call(
        body, name=name, grid=(GROUPS, l // tk), in_specs=[blk, blk],
        out_specs=pl.BlockSpec((None, pg, pg), lambda g, kk: (g, 0, 0)),
        out_shape=jax.ShapeDtypeStruct((GROUPS, pg, pg), F32), compiler_params=_params("parallel", "arbitrary"),
    )(a, b)


def _gnorm_fwd(y, proj, wn, d_mix, *, name, tl=512):
    l, d_ssm = y.shape
    gw = d_ssm // GROUPS
    tl = _tile(l, tl, 16)

    def body(y_ref, z_ref, w_ref, o_ref):
        z = z_ref[...]
        gv = y_ref[...] * (z * _sigmoid(z))
        r = lax.rsqrt(jnp.mean(gv * gv, axis=-1, keepdims=True) + NORM_EPS)
        o_ref[...] = (gv * r * w_ref[...]).astype(o_ref.dtype)

    blk = pl.BlockSpec((tl, gw), lambda i, g: (i, g))
    return pl.pallas_call(
        body, name=name, grid=(l // tl, GROUPS), in_specs=[blk, blk, pl.BlockSpec((1, gw), lambda i, g: (0, g))],
        out_specs=blk, out_shape=jax.ShapeDtypeStruct((l, d_mix), BF16), compiler_params=_params("parallel", "parallel"),
    )(y, proj, wn)


def _gnorm_bwd(dmixed, y, proj, wn, *, name, tl=512):
    l, d_ssm = y.shape
    gw = d_ssm // GROUPS
    tl = _tile(l, tl, 16)

    def body(d_ref, y_ref, z_ref, w_ref, dy_ref, dz_ref, dw_ref):
        i = pl.program_id(1)
        z, yv, dv = z_ref[...], y_ref[...], d_ref[...]
        s = _sigmoid(z)
        sz = z * s
        gv = yv * sz
        r = lax.rsqrt(jnp.mean(gv * gv, axis=-1, keepdims=True) + NORM_EPS)
        gh = gv * r
        dgh = dv * w_ref[...]
        dg = r * (dgh - gh * jnp.mean(dgh * gh, axis=-1, keepdims=True))
        dy_ref[...] = dg * sz
        dz_ref[...] = (dg * yv * (s * (1.0 + z * (1.0 - s)))).astype(dz_ref.dtype)
        part = jnp.sum(dv * gh, axis=0, keepdims=True)

        @pl.when(i == 0)
        def _():
            dw_ref[...] = part

        @pl.when(i > 0)
        def _():
            dw_ref[...] += part

    blk = pl.BlockSpec((tl, gw), lambda g, i: (i, g))
    vec = pl.BlockSpec((1, gw), lambda g, i: (0, g))
    return pl.pallas_call(
        body, name=name, grid=(GROUPS, l // tl), in_specs=[blk, blk, blk, vec], out_specs=[blk, blk, vec],
        out_shape=[jax.ShapeDtypeStruct((l, d_ssm), F32), jax.ShapeDtypeStruct((l, d_ssm), BF16),
                   jax.ShapeDtypeStruct((1, d_ssm), F32)],
        compiler_params=_params("parallel", "arbitrary"),
    )(dmixed, y, proj, wn)


def _split3(v):
    hi = v.astype(BF16)
    r1 = v - hi.astype(F32)
    mid = r1.astype(BF16)
    lo = (r1 - mid.astype(F32)).astype(BF16)
    return hi, mid, lo


def _tri_dot(tri, v):
    hi, mid, lo = _split3(v)
    d = lambda p: jnp.dot(tri, p, preferred_element_type=F32)
    return d(hi) + d(mid) + d(lo)


def _dot_tri(v, tri):
    hi, mid, lo = _split3(v)
    d = lambda p: jnp.dot(p, tri, preferred_element_type=F32)
    return d(hi) + d(mid) + d(lo)


def _ssd_specs(hpg, nc, order):
    gw, q = hpg * HEAD_DIM, CHUNK
    b_off, c_off = (GROUPS * gw) // STATE, (GROUPS * gw) // STATE + GROUPS
    xs = pl.BlockSpec((q, gw), lambda g, c: (order(c), g))
    bm = pl.BlockSpec((q, STATE), lambda g, c: (order(c), b_off + g))
    cm = pl.BlockSpec((q, STATE), lambda g, c: (order(c), c_off + g))
    dt = pl.BlockSpec((None, q, hpg), lambda g, c: (g, order(c), 0))
    dtt = pl.BlockSpec((None, hpg, q), lambda g, c: (g, 0, order(c)))
    hrow = pl.BlockSpec((None, 1, hpg), lambda g, c: (g, 0, 0))
    hcol = pl.BlockSpec((None, hpg, 1), lambda g, c: (g, 0, 0))
    st = pl.BlockSpec((None, None, hpg, STATE, HEAD_DIM), lambda g, c: (g, order(c), 0, 0, 0))
    return xs, bm, cm, dt, dtt, hrow, hcol, st


def _ssd_common(dtr_ref, dtrt_ref, bias_ref, biast_ref, alog_ref, alogt_ref):
    q = CHUNK
    dt = _softplus(dtr_ref[...] + bias_ref[...])
    dtt = _softplus(dtrt_ref[...] + biast_ref[...])
    a = -jnp.exp(alog_ref[...])
    at = -jnp.exp(alogt_ref[...])
    row = lax.broadcasted_iota(jnp.int32, (q, q), 0)
    col = lax.broadcasted_iota(jnp.int32, (q, q), 1)
    causal = row >= col
    lower = causal.astype(BF16)
    upper = (row <= col).astype(BF16)
    acum = _tri_dot(lower, dt * a)
    acum_t = _dot_tri(dtt * at, upper)
    return dt, a, acum, acum_t, causal, lower, upper


def _ssd_fwd(xbc, dtr, dtrt, bias, biast, alog, alogt, dskip, *, name):
    l = xbc.shape[0]
    hpg = dtr.shape[2]
    gw, q, nc = hpg * HEAD_DIM, CHUNK, l // CHUNK
    xs_s, bm_s, cm_s, dt_s, dtt_s, hrow, hcol, st_s = _ssd_specs(hpg, nc, lambda c: c)

    def body(xs_ref, b_ref, c_ref, dtr_ref, dtrt_ref, bias_ref, biast_ref, alog_ref, alogt_ref, dsk_ref,
             y_ref, st_ref, h_ref):
        @pl.when(pl.program_id(1) == 0)
        def _():
            h_ref[...] = jnp.zeros_like(h_ref)

        dt, _, acum, acum_t, causal, _, _ = _ssd_common(dtr_ref, dtrt_ref, bias_ref, biast_ref, alog_ref, alogt_ref)
        bmat, cmat = b_ref[...], c_ref[...]
        cb = _dot(cmat, bmat, tb=True)
        cbf = cmat.astype(BF16)
        a_last = acum[q - 1:q, :]
        dsk = dsk_ref[...]
        for j in range(hpg):
            a_col, a_row = acum[:, j:j + 1], acum_t[j:j + 1, :]
            lm = jnp.exp(jnp.where(causal, a_col - a_row, NEG_BIG))
            xj = xs_ref[:, j * HEAD_DIM:(j + 1) * HEAD_DIM]
            xdt = (xj * dt[:, j:j + 1]).astype(BF16)
            hj = h_ref[j]
            st_ref[j] = hj
            yj = _dot(cb * lm, xdt) + _dot(cbf, hj) * jnp.exp(a_col) + dsk[:, j:j + 1] * xj
            y_ref[:, j * HEAD_DIM:(j + 1) * HEAD_DIM] = yj
            al = a_last[:, j:j + 1]
            be = bmat * jnp.exp(al - a_col)
            h_ref[j] = jnp.exp(al) * hj + _dot(be, xdt, ta=True)

    return pl.pallas_call(
        body, name=name, grid=(GROUPS, nc),
        in_specs=[xs_s, bm_s, cm_s, dt_s, dtt_s, hrow, hcol, hrow, hcol, hrow], out_specs=[xs_s, st_s],
        out_shape=[jax.ShapeDtypeStruct((l, GROUPS * gw), F32),
                   jax.ShapeDtypeStruct((GROUPS, nc, hpg, STATE, HEAD_DIM), F32)],
        scratch_shapes=[pltpu.VMEM((hpg, STATE, HEAD_DIM), F32)],
        compiler_params=_params("parallel", "arbitrary"),
    )(xbc, xbc, xbc, dtr, dtrt, bias, biast, alog, alogt, dskip)


def _ssd_bwd(xbc, dtr, dtrt, bias, biast, alog, alogt, dskip, states, dy, *, name):
    l = xbc.shape[0]
    hpg = dtr.shape[2]
    gw, q, nc = hpg * HEAD_DIM, CHUNK, l // CHUNK
    rev = lambda c: nc - 1 - c
    xs_s, bm_s, cm_s, dt_s, dtt_s, hrow, hcol, st_s = _ssd_specs(hpg, nc, rev)
    bc_s = pl.BlockSpec((q, STATE), lambda g, c: (rev(c), g))

    def body(xs_ref, b_ref, c_ref, dtr_ref, dtrt_ref, bias_ref, biast_ref, alog_ref, alogt_ref, dsk_ref, st_ref, dy_ref,
             dxs_ref, db_ref, dc_ref, ddtr_ref, dbias_ref, dalog_ref, ddsk_ref, dh_ref):
        step = pl.program_id(1)

        @pl.when(step == 0)
        def _():
            dh_ref[...] = jnp.zeros_like(dh_ref)

        dt, a, acum, acum_t, causal, lower, upper = _ssd_common(
            dtr_ref, dtrt_ref, bias_ref, biast_ref, alog_ref, alogt_ref)
        bmat, cmat = b_ref[...], c_ref[...]
        cb = _dot(cmat, bmat, tb=True)
        cbf = cmat.astype(BF16)
        a_last = acum[q - 1:q, :]
        dsk = dsk_ref[...]
        lane_h = lax.broadcasted_iota(jnp.int32, (1, hpg), 1)
        row_q = lax.broadcasted_iota(jnp.int32, (q, 1), 0)
        strict = lax.broadcasted_iota(jnp.int32, (q, q), 0) > lax.broadcasted_iota(jnp.int32, (q, q), 1)
        dcb = jnp.zeros((q, q), F32)
        dc_acc = jnp.zeros((q, STATE), F32)
        db_acc = jnp.zeros((q, STATE), F32)
        d_corner_all = jnp.zeros((q, hpg), F32)
        da_local_all = jnp.zeros((q, hpg), F32)
        d_dt_x = jnp.zeros((q, hpg), F32)
        d_dsk = jnp.zeros((1, hpg), F32)
        for j in range(hpg):
            onehot = (lane_h == j).astype(F32)
            a_col, a_row = acum[:, j:j + 1], acum_t[j:j + 1, :]
            al = a_last[:, j:j + 1]
            lm = jnp.exp(jnp.where(causal, a_col - a_row, NEG_BIG))
            mmat = cb * lm
            xj = xs_ref[:, j * HEAD_DIM:(j + 1) * HEAD_DIM]
            dtj = dt[:, j:j + 1]
            xdt = (xj * dtj).astype(BF16)
            ea = jnp.exp(a_col)
            ee = jnp.exp(al - a_col)
            gam = jnp.exp(al)
            be = bmat * ee
            hj = st_ref[j]
            dhj = dh_ref[j]
            dyj = dy_ref[:, j * HEAD_DIM:(j + 1) * HEAD_DIM]
            dyb = dyj.astype(BF16)
            dxdt = _dot(mmat, dyb, ta=True) + _dot(be, dhj)
            dm = _dot(dyb, xdt, tb=True)
            dcb = dcb + dm * lm
            wmat = dm * mmat
            corner = jnp.dot(upper, wmat.astype(BF16), preferred_element_type=F32)
            d_corner = jnp.sum(jnp.where(strict, corner, 0.0), axis=1, keepdims=True)
            zc = _dot(cbf, hj)
            dz = dyj * ea
            dc_acc = dc_acc + _dot(dz, hj, tb=True)
            dprev = _dot(cbf, dz, ta=True)
            da_off = jnp.sum(dyj * zc, axis=1, keepdims=True) * ea
            dbe = _dot(xdt, dhj, tb=True)
            db_acc = db_acc + dbe * ee
            de_e = jnp.sum(dbe * be, axis=1, keepdims=True)
            dgam_g = jnp.sum(jnp.sum(dhj * hj, axis=1, keepdims=True), axis=0, keepdims=True) * gam
            tail = jnp.sum(de_e, axis=0, keepdims=True) + dgam_g
            da_local = da_off - de_e + jnp.where(row_q == q - 1, tail, 0.0)
            d_corner_all = d_corner_all + d_corner * onehot
            da_local_all = da_local_all + da_local * onehot
            d_dt_x = d_dt_x + jnp.sum(dxdt * xj, axis=1, keepdims=True) * onehot
            dxs_ref[:, j * HEAD_DIM:(j + 1) * HEAD_DIM] = dxdt * dtj + dsk[:, j:j + 1] * dyj
            d_dsk = d_dsk + jnp.sum(jnp.sum(dyj * xj, axis=1, keepdims=True), axis=0, keepdims=True) * onehot
            dh_ref[j] = gam * dhj + dprev
        dc_ref[...] = dc_acc + _dot(dcb, bmat)
        db_ref[...] = db_acc + _dot(dcb, cmat, ta=True)
        d_dta = d_corner_all + _tri_dot(upper, da_local_all)
        d_dt = d_dta * a + d_dt_x
        sg = _sigmoid(dtr_ref[...] + bias_ref[...])
        d_raw = d_dt * sg
        ddtr_ref[...] = d_raw
        p_bias = jnp.sum(d_raw, axis=0, keepdims=True)
        p_alog = jnp.sum(d_dta * dt, axis=0, keepdims=True) * a

        @pl.when(step == 0)
        def _():
            dbias_ref[...] = p_bias
            dalog_ref[...] = p_alog
            ddsk_ref[...] = d_dsk

        @pl.when(step > 0)
        def _():
            dbias_ref[...] += p_bias
            dalog_ref[...] += p_alog
            ddsk_ref[...] += d_dsk

    return pl.pallas_call(
        body, name=name, grid=(GROUPS, nc),
        in_specs=[xs_s, bm_s, cm_s, dt_s, dtt_s, hrow, hcol, hrow, hcol, hrow, st_s, xs_s],
        out_specs=[xs_s, bc_s, bc_s, dt_s, hrow, hrow, hrow],
        out_shape=[jax.ShapeDtypeStruct((l, GROUPS * gw), F32), jax.ShapeDtypeStruct((l, GROUPS * STATE), F32),
                   jax.ShapeDtypeStruct((l, GROUPS * STATE), F32), jax.ShapeDtypeStruct((GROUPS, l, hpg), F32),
                   jax.ShapeDtypeStruct((GROUPS, 1, hpg), F32), jax.ShapeDtypeStruct((GROUPS, 1, hpg), F32),
                   jax.ShapeDtypeStruct((GROUPS, 1, hpg), F32)],
        scratch_shapes=[pltpu.VMEM((hpg, STATE, HEAD_DIM), F32)],
        compiler_params=_params("parallel", "arbitrary"),
    )(xbc, xbc, xbc, dtr, dtrt, bias, biast, alog, alogt, dskip, states, dy)


def _peer(k):
    x, y, c = lax.axis_index("x"), lax.axis_index("y"), lax.axis_index("c")
    return (1 - x if k & 4 else x, 1 - y if k & 2 else y, 1 - c if k & 1 else c)


def _my_index():
    return 4 * lax.axis_index("x") + 2 * lax.axis_index("y") + lax.axis_index("c")


def _exchange(srcs, scatter, *, name):
    n = len(srcs)
    shapes = [s.shape[1:] if scatter else s.shape for s in srcs]

    def body(*refs):
        src, dst = refs[:n], refs[n:2 * n]
        send_sems, recv_sems, local_sems = refs[2 * n:]
        me = _my_index()
        copies = []
        for t in range(n):
            mine = src[t].at[me] if scatter else src[t]
            lc = pltpu.make_async_copy(mine, dst[t].at[me], local_sems.at[t])
            lc.start()
            copies.append(lc)
        remote = []
        for k in range(1, N_DEV):
            peer = _peer(k)
            pidx = me ^ k
            for t in range(n):
                rc = pltpu.make_async_remote_copy(
                    src_ref=src[t].at[pidx] if scatter else src[t], dst_ref=dst[t].at[me],
                    send_sem=send_sems.at[t, k - 1], recv_sem=recv_sems.at[t, k - 1],
                    device_id=peer, device_id_type=MESH_ID)
                rc.start()
                remote.append(rc)
        for rc in remote:
            rc.wait_send()
        for rc in remote:
            rc.wait_recv()
        for lc in copies:
            lc.wait()

    anyspec = pl.BlockSpec(memory_space=pl.ANY)
    return pl.pallas_call(
        body, name=name, in_specs=[anyspec] * n, out_specs=[anyspec] * n,
        out_shape=[jax.ShapeDtypeStruct((N_DEV,) + tuple(sh), s.dtype) for sh, s in zip(shapes, srcs)],
        scratch_shapes=[pltpu.SemaphoreType.DMA((n, N_DEV - 1)), pltpu.SemaphoreType.DMA((n, N_DEV - 1)),
                        pltpu.SemaphoreType.DMA((n,))],
    )(*srcs)


def _adamw(parts, w, m, v, *, name, tr=256):
    r, c = w.shape
    tr = _tile(r, tr, 16)
    c1, c2 = 1.0 / (1.0 - ADAM_B1 ** ADAM_STEP), 1.0 / (1.0 - ADAM_B2 ** ADAM_STEP)

    def body(p_ref, w_ref, m_ref, v_ref, g_ref, d_ref, nm_ref, nv_ref):
        g = p_ref[0].astype(F32)
        for p in range(1, N_DEV):
            g = g + p_ref[p].astype(F32)
        nm = ADAM_B1 * m_ref[...] + (1.0 - ADAM_B1) * g
        nv = ADAM_B2 * v_ref[...] + (1.0 - ADAM_B2) * (g * g)
        g_ref[...] = g
        nm_ref[...] = nm
        nv_ref[...] = nv
        d_ref[...] = -ADAM_LR * ((nm * c1) / (jnp.sqrt(nv * c2) + ADAM_EPS) + ADAM_WD * w_ref[...])

    blk = pl.BlockSpec((tr, c), lambda i: (i, 0))
    out = jax.ShapeDtypeStruct((r, c), F32)
    return pl.pallas_call(
        body, name=name, grid=(r // tr,), in_specs=[pl.BlockSpec((N_DEV, tr, c), lambda i: (0, i, 0)), blk, blk, blk],
        out_specs=[blk, blk, blk, blk], out_shape=[out, out, out, out], compiler_params=_params("parallel"),
    )(parts, w, m, v)


def kernel(x, attn_norm_w, w_in, conv_w, conv_b, dt_bias, a_log, d_skip, ssd_norm_w, pool_w, pool_scale, w_out, ffn_norm_w, w_gate, w_up, w_down, final_norm_w, loss_target, m_attn_norm_w, m_w_in, m_conv_w, m_conv_b, m_dt_bias, m_a_log, m_d_skip, m_ssd_norm_w, m_pool_w, m_pool_scale, m_w_out, m_ffn_norm_w, m_w_gate, m_w_up, m_w_down, m_final_norm_w, v_attn_norm_w, v_w_in, v_conv_w, v_conv_b, v_dt_bias, v_a_log, v_d_skip, v_ssd_norm_w, v_pool_w, v_pool_scale, v_w_out, v_ffn_norm_w, v_w_gate, v_w_up, v_w_down, v_final_norm_w):
    l, d = x.shape[1], x.shape[2]
    heads = dt_bias.shape[1]
    hpg = heads // GROUPS
    d_ssm = heads * HEAD_DIM
    conv_ch = conv_b.shape[1]
    d_pool = pool_scale.shape[1]
    pg = d_pool // GROUPS
    d_mix = d_ssm + d_pool
    d_ff = w_gate.shape[2] * N_DEV
    d_in = w_in.shape[2] * N_DEV
    dt_pad = -(-heads // LANE) * LANE
    o_u, o_xbc, o_dt = d_ssm, d_ssm + d_pool, d_ssm + d_pool + conv_ch
    d_inp = o_dt + dt_pad
    tn_in = _tile(d_inp, 2560, LANE)
    g_dt, g_u = d_ssm + conv_ch, d_ssm + conv_ch + heads

    x2, tgt = x[0], loss_target[0]

    shards = [w_in[0].astype(BF16), pool_w[0].astype(BF16), w_out[0].astype(BF16), w_gate[0].astype(BF16),
              w_up[0].astype(BF16), w_down[0].astype(BF16), conv_w[0]]
    gi, gp, go, gg, gu, gd, gc = _exchange(shards, False, name="gather_weights")
    win = jnp.transpose(gi, (1, 0, 2)).reshape(d, d_in)
    wp = jnp.concatenate([win[:, :d_ssm], win[:, g_u:], win[:, d_ssm:g_dt], win[:, g_dt:g_u],
                          jnp.zeros((d, dt_pad - heads), BF16)], axis=1)
    pw = jnp.transpose(gp, (1, 0, 2, 3)).reshape(GROUPS, pg, pg)
    wo = go.reshape(d_mix, d)
    wg = jnp.transpose(gg, (1, 0, 2)).reshape(d, d_ff)
    wu = jnp.transpose(gu, (1, 0, 2)).reshape(d, d_ff)
    wd = gd.reshape(d_ff, d)
    cw = jnp.transpose(gc, (1, 0, 2)).reshape(CONV_K, conv_ch)

    def per_group(vec):
        r = vec.reshape(GROUPS, 1, hpg)
        return r, jnp.transpose(r, (0, 2, 1))

    bias_r, bias_c = per_group(dt_bias)
    alog_r, alog_c = per_group(a_log)
    dsk_r, _ = per_group(d_skip)

    h0 = _rms_fwd(x2, attn_norm_w, name="attn_norm")
    proj = _mm(h0, wp, name="in_proj", tm=512, tn=tn_in, tk=d)
    xbc = _conv_fwd(proj, o_xbc, cw, conv_b, name="conv_fwd")
    dtr = jnp.transpose(proj[:, o_dt:o_dt + heads].reshape(l, GROUPS, hpg), (1, 0, 2))
    dtrt = jnp.transpose(dtr, (0, 2, 1))
    y, states = _ssd_fwd(xbc, dtr, dtrt, bias_r, bias_c, alog_r, alog_c, dsk_r, name="ssd_fwd")
    mixed = _gnorm_fwd(y, proj, ssd_norm_w, d_mix, name="gated_norm")
    pooled = _pool_fwd(proj, o_u, d_pool, name="pool_fwd")
    pool_raw, mixed = _pool_mix_fwd(pooled, pw, pool_scale, mixed, d_ssm, name="pool_mix")
    h1 = _mm(mixed, wo, name="out_proj", residual=x2)
    h1n = _rms_fwd(h1, ffn_norm_w, name="ffn_norm")
    gate, up, act = _gate_up(h1n, wg, wu, name="gate_up")
    h2 = _mm(act, wd, name="down_proj", tk=_tile(d_ff, 1536, LANE), residual=h1)

    loss11, dh2, dh2b, g_final = _loss_head(h2, final_norm_w.reshape(1, d), tgt, name="loss_head")
    dgate, dup = _gate_up_bwd(dh2b, wd, gate, up, name="gate_up_bwd")
    g_wd = _mm(act, dh2b, name="grad_w_down", ta=True, tm=_tile(d_ff, 1536, LANE), tn=1024, tk=1024)
    dh1n = _mm(dgate, wg, name="dgate_wg", tb=True, tk=_tile(d_ff, 1536, LANE))
    dh1n = _mm(dup, wu, name="dup_wu", tb=True, tk=_tile(d_ff, 1536, LANE), residual=dh1n)
    g_wg = _mm(h1n, dgate, name="grad_w_gate", ta=True, tm=1024, tn=_tile(d_ff, 1536, LANE), tk=1024)
    g_wu = _mm(h1n, dup, name="grad_w_up", ta=True, tm=1024, tn=_tile(d_ff, 1536, LANE), tk=1024)
    dh1, dh1b, g_ffn = _rms_bwd(h1, ffn_norm_w, dh1n, dh2, name="ffn_norm_bwd")
    dmixed = _mm(dh1b, wo, name="dmixed", tb=True, tk=d)
    g_wo = _mm(mixed, dh1b, name="grad_w_out", ta=True, tm=1024, tn=1024, tk=1024)
    draw, g_pscale = _pool_scale_bwd(dmixed, d_ssm, pool_raw, pool_scale, name="pool_scale_bwd")
    dpooled = _group_mm_nt(draw, pw, name="dpooled")
    g_pw = _group_mm_tn(pooled, draw, name="grad_pool_w")
    du = _pool_bwd(dpooled, name="pool_bwd")
    dy, dz, g_ssdn = _gnorm_bwd(dmixed, y, proj, ssd_norm_w, name="gated_norm_bwd")
    dxs, dbm, dcm, ddtr, g_bias, g_alog, g_dsk = _ssd_bwd(
        xbc, dtr, dtrt, bias_r, bias_c, alog_r, alog_c, dsk_r, states, dy, name="ssd_bwd")
    n_bc = GROUPS * STATE
    segs = [(dxs, 0, d_ssm), (dbm, d_ssm, n_bc), (dcm, d_ssm + n_bc, n_bc)]
    dxbc, g_cw, g_cb = [], [], []
    for si, (dseg, c0, width) in enumerate(segs):
        a, b, c_ = _conv_bwd(proj, o_xbc + c0, cw[:, c0:c0 + width], conv_b[:, c0:c0 + width], dseg, name=f"conv_bwd{si}")
        dxbc.append(a)
        g_cw.append(b)
        g_cb.append(c_)
    ddt = jnp.transpose(ddtr, (1, 0, 2)).reshape(l, heads).astype(BF16)
    dproj = jnp.concatenate([dz, du] + dxbc + [ddt, jnp.zeros((l, dt_pad - heads), BF16)], axis=1)
    dh0 = _mm(dproj, wp, name="dh0", tb=True, tk=tn_in)
    g_wp = _mm(h0, dproj, name="grad_w_in", ta=True, tm=512, tn=tn_in, tk=1024)
    dx, _, g_attn = _rms_bwd(x2, attn_norm_w, dh0, dh1, name="attn_norm_bwd")

    g_win = jnp.concatenate([g_wp[:, :d_ssm], g_wp[:, o_xbc:o_dt + heads], g_wp[:, o_u:o_xbc]], axis=1)
    def cols8(g):
        return jnp.transpose(g.reshape(g.shape[0], N_DEV, g.shape[1] // N_DEV), (1, 0, 2))
    g_cwf = jnp.concatenate(g_cw, axis=1)
    sends = [cols8(g_win).astype(BF16),
             jnp.transpose(g_pw.reshape(GROUPS, N_DEV, pg // N_DEV, pg), (1, 0, 2, 3)).astype(BF16),
             g_wo.reshape(N_DEV, d_mix // N_DEV, d).astype(BF16), cols8(g_wg).astype(BF16), cols8(g_wu).astype(BF16),
             g_wd.reshape(N_DEV, d_ff // N_DEV, d).astype(BF16), cols8(g_cwf)]
    smalls = [g_attn, jnp.concatenate(g_cb, axis=1), g_bias.reshape(1, heads), g_alog.reshape(1, heads),
              g_dsk.reshape(1, heads), g_ssdn, g_pscale, g_ffn, g_final]
    small_w = [attn_norm_w, conv_b, dt_bias, a_log, d_skip, ssd_norm_w, pool_scale, ffn_norm_w, final_norm_w.reshape(1, d)]
    small_m = [m_attn_norm_w, m_conv_b, m_dt_bias, m_a_log, m_d_skip, m_ssd_norm_w, m_pool_scale, m_ffn_norm_w,
               m_final_norm_w.reshape(1, d)]
    small_v = [v_attn_norm_w, v_conv_b, v_dt_bias, v_a_log, v_d_skip, v_ssd_norm_w, v_pool_scale, v_ffn_norm_w,
               v_final_norm_w.reshape(1, d)]
    sizes = [s.shape[1] for s in smalls]
    n_small = sum(sizes)
    n_pad = -(-n_small // (16 * LANE)) * (16 * LANE)
    rows = n_pad // LANE
    def pack(vs):
        return jnp.pad(jnp.concatenate(vs, axis=1), ((0, 0), (0, n_pad - n_small))).reshape(rows, LANE)
    r_in, r_pw, r_wo, r_wg, r_wu, r_wd, r_cw = _exchange(sends, True, name="scatter_grads")
    (r_small,) = _exchange([pack(smalls)], False, name="gather_small_grads")

    def big(parts, w, m, v, nm):
        shp = w.shape
        r2 = lambda t: t.reshape(-1, shp[-1])
        outs = _adamw(parts.reshape(N_DEV, -1, shp[-1]), r2(w), r2(m), r2(v), name=nm)
        return [o.reshape(shp) for o in outs]

    res = {
        "w_in": big(r_in, w_in, m_w_in, v_w_in, "adamw_w_in"),
        "conv_w": big(r_cw, conv_w, m_conv_w, v_conv_w, "adamw_conv_w"),
        "pool_w": big(r_pw, pool_w, m_pool_w, v_pool_w, "adamw_pool_w"),
        "w_out": big(r_wo, w_out, m_w_out, v_w_out, "adamw_w_out"),
        "w_gate": big(r_wg, w_gate, m_w_gate, v_w_gate, "adamw_w_gate"),
        "w_up": big(r_wu, w_up, m_w_up, v_w_up, "adamw_w_up"),
        "w_down": big(r_wd, w_down, m_w_down, v_w_down, "adamw_w_down"),
    }
    s_out = _adamw(r_small, pack(small_w), pack(small_m), pack(small_v), name="adamw_small")
    names = ["attn_norm_w", "conv_b", "dt_bias", "a_log", "d_skip", "ssd_norm_w", "pool_scale", "ffn_norm_w", "final_norm_w"]
    offs = [sum(sizes[:i]) for i in range(len(sizes))]
    for i, nm in enumerate(names):
        shp = (d,) if nm == "final_norm_w" else (1, sizes[i])
        res[nm] = [o.reshape(1, n_pad)[:, offs[i]:offs[i] + sizes[i]].reshape(shp) for o in s_out]

    loss = lax.psum(loss11[0, 0], ("x", "y", "c"))
    order = ["attn_norm_w", "w_in", "conv_w", "conv_b", "dt_bias", "a_log", "d_skip", "ssd_norm_w", "pool_w", "pool_scale",
             "w_out", "ffn_norm_w", "w_gate", "w_up", "w_down", "final_norm_w"]
    outs = [loss, dx.reshape(x.shape)]
    for part in range(4):
        outs += [res[nm][part] for nm in order]
    return tuple(outs)
```

```python
import functools

import jax
import jax.numpy as jnp
from jax import lax
from jax.experimental import pallas as pl
from jax.experimental.pallas import tpu as pltpu

F32 = jnp.float32
BF16 = jnp.bfloat16

NORM_EPS = 1e-5
HEAD_DIM = 64
STATE = 128
CHUNK = 256
GROUPS = 4
CONV_K = 4
POOL_WINDOWS = (2, 4, 8, 16)
POOL_HALO = 16
CONV_HALO = 8
LANE = 128
N_DEV = 8
ADAM_LR, ADAM_B1, ADAM_B2, ADAM_EPS, ADAM_WD, ADAM_STEP = 0.001, 0.9, 0.999, 1e-08, 0.01, 10
NEG_BIG = -1e30
MESH_ID = pl.DeviceIdType.MESH


def _tile(n, pref, mult):
    if n <= pref:
        return n
    t = (pref // mult) * mult
    while t >= mult:
        if n % t == 0:
            return t
        t -= mult
    raise ValueError(f"no tile for {n} (pref {pref}, mult {mult})")


def _params(*sem):
    return pltpu.CompilerParams(dimension_semantics=sem)


ANY = pl.BlockSpec(memory_space=pl.ANY)


def _peer(k):
    x, y, c = lax.axis_index("x"), lax.axis_index("y"), lax.axis_index("c")
    return (1 - x if k & 4 else x, 1 - y if k & 2 else y, 1 - c if k & 1 else c)


def _my_index():
    return 4 * lax.axis_index("x") + 2 * lax.axis_index("y") + lax.axis_index("c")


class _Ride:
    def __init__(self, srcs, scatter):
        self.srcs, self.scatter, self.n = list(srcs), scatter, len(srcs)
        self.in_specs = [ANY] * self.n
        self.out_specs = [ANY] * self.n
        self.out_shape = [jax.ShapeDtypeStruct((N_DEV,) + tuple(s.shape[1:] if scatter else s.shape), s.dtype)
                          for s in self.srcs]
        self.scratch = [pltpu.SemaphoreType.DMA((self.n, N_DEV - 1)), pltpu.SemaphoreType.DMA((self.n, N_DEV - 1)),
                        pltpu.SemaphoreType.DMA((self.n,))]

    def _copies(self, src, dst, sems):
        send_sems, recv_sems, local_sems = sems
        me = _my_index()
        local = [pltpu.make_async_copy(src[t].at[me] if self.scatter else src[t], dst[t].at[me], local_sems.at[t])
                 for t in range(self.n)]
        remote = []
        for k in range(1, N_DEV):
            peer, pidx = _peer(k), me ^ k
            for t in range(self.n):
                remote.append(pltpu.make_async_remote_copy(
                    src_ref=src[t].at[pidx] if self.scatter else src[t], dst_ref=dst[t].at[me],
                    send_sem=send_sems.at[t, k - 1], recv_sem=recv_sems.at[t, k - 1],
                    device_id=peer, device_id_type=MESH_ID))
        return local, remote

    def start(self, src, dst, sems):
        local, remote = self._copies(src, dst, sems)
        for cp in local + remote:
            cp.start()

    def finish(self, src, dst, sems):
        local, remote = self._copies(src, dst, sems)
        for rc in remote:
            rc.wait_send()
        for rc in remote:
            rc.wait_recv()
        for lc in local:
            lc.wait()


def _call(body, *, name, grid, in_specs, out_specs, out_shape, args, sem, scratch=(), ride=None):
    in_specs, out_specs, out_shape, scratch = list(in_specs), list(out_specs), list(out_shape), list(scratch)
    if ride is None:
        return pl.pallas_call(body, name=name, grid=grid, in_specs=in_specs, out_specs=out_specs, out_shape=out_shape,
                              scratch_shapes=scratch, compiler_params=_params(*sem))(*args)
    n_in, n_out, n_sc, nr = len(in_specs), len(out_specs), len(scratch), ride.n

    def full(*refs):
        ins, csrc = refs[:n_in], refs[n_in:n_in + nr]
        o0 = n_in + nr
        outs, cdst = refs[o0:o0 + n_out], refs[o0 + n_out:o0 + n_out + nr]
        s0 = o0 + n_out + nr
        sc, csem = refs[s0:s0 + n_sc], refs[s0 + n_sc:]
        ids = [pl.program_id(ax) for ax in range(len(grid))]
        first = functools.reduce(jnp.logical_and, [i == 0 for i in ids])
        last = functools.reduce(jnp.logical_and, [i == g - 1 for i, g in zip(ids, grid)])

        @pl.when(first)
        def _():
            ride.start(csrc, cdst, csem)

        body(*ins, *outs, *sc)

        @pl.when(last)
        def _():
            ride.finish(csrc, cdst, csem)

    return pl.pallas_call(
        full, name=name, grid=grid, in_specs=in_specs + ride.in_specs, out_specs=out_specs + ride.out_specs,
        out_shape=out_shape + ride.out_shape, scratch_shapes=scratch + ride.scratch,
        compiler_params=_params(*(["arbitrary"] * len(grid))),
    )(*args, *ride.srcs)


def _sigmoid(x):
    return 1.0 / (1.0 + jnp.exp(-x))


def _softplus(x):
    return jnp.maximum(x, 0.0) + jnp.log(1.0 + jnp.exp(-jnp.abs(x)))


def _dot(a, b, ta=False, tb=False):
    dn = (((0 if ta else 1,), (1 if tb else 0,)), ((), ()))
    return lax.dot_general(a.astype(BF16), b.astype(BF16), dn, preferred_element_type=F32)


def _mm(a, b, *, name, ta=False, tb=False, out_dtype=F32, tm=1024, tn=1024, tk=2048, residual=None, ride=None):
    m = a.shape[1] if ta else a.shape[0]
    k = a.shape[0] if ta else a.shape[1]
    n = b.shape[0] if tb else b.shape[1]
    assert k == (b.shape[1] if tb else b.shape[0])
    tm, tn = _tile(m, tm, LANE if ta else 8), _tile(n, tn, LANE)
    tk = _tile(k, tk, LANE if not (ta and tb) else LANE)
    nk = k // tk
    has_res = residual is not None

    def body(*refs):
        a_ref, b_ref = refs[0], refs[1]
        r_ref = refs[2] if has_res else None
        o_ref = refs[3] if has_res else refs[2]
        part = _dot(a_ref[...], b_ref[...], ta, tb)

        def finish(acc):
            if has_res:
                acc = acc + r_ref[...]
            o_ref[...] = acc.astype(o_ref.dtype)

        if nk == 1:
            finish(part)
        else:
            acc_ref = refs[-1]
            kk = pl.program_id(2)

            @pl.when(kk == 0)
            def _():
                acc_ref[...] = part

            @pl.when(kk > 0)
            def _():
                acc_ref[...] += part

            @pl.when(kk == nk - 1)
            def _():
                finish(acc_ref[...])

    a_spec = pl.BlockSpec((tk, tm), lambda i, j, kk: (kk, i)) if ta else pl.BlockSpec((tm, tk), lambda i, j, kk: (i, kk))
    b_spec = pl.BlockSpec((tn, tk), lambda i, j, kk: (j, kk)) if tb else pl.BlockSpec((tk, tn), lambda i, j, kk: (kk, j))
    o_spec = pl.BlockSpec((tm, tn), lambda i, j, kk: (i, j))
    in_specs = [a_spec, b_spec] + ([o_spec] if has_res else [])
    args = (a, b) + ((residual,) if has_res else ())
    res = _call(body, name=name, grid=(m // tm, n // tn, nk), in_specs=in_specs, out_specs=[o_spec],
                out_shape=[jax.ShapeDtypeStruct((m, n), out_dtype)], args=args,
                scratch=[pltpu.VMEM((tm, tn), F32)] if nk > 1 else [], sem=("parallel", "parallel", "arbitrary"), ride=ride)
    return res[0] if ride is None else res


def _gate_up(h, wg, wu, *, name, tm=1024, ride=None):
    m, k = h.shape
    ns, _, c = wg.shape
    tm = _tile(m, tm, 16)

    def body(h_ref, wg_ref, wu_ref, g_ref, u_ref, a_ref):
        hb = h_ref[...]
        g = _dot(hb, wg_ref[...])
        u = _dot(hb, wu_ref[...])
        g_ref[...] = g
        u_ref[...] = u
        a_ref[...] = (g * _sigmoid(g) * u).astype(a_ref.dtype)

    o_spec = pl.BlockSpec((None, tm, c), lambda i, s: (s, i, 0))
    w_spec = pl.BlockSpec((None, k, c), lambda i, s: (s, 0, 0))
    return _call(body, name=name, grid=(m // tm, ns),
                 in_specs=[pl.BlockSpec((tm, k), lambda i, s: (i, 0)), w_spec, w_spec], out_specs=[o_spec, o_spec, o_spec],
                 out_shape=[jax.ShapeDtypeStruct((ns, m, c), F32), jax.ShapeDtypeStruct((ns, m, c), F32),
                            jax.ShapeDtypeStruct((ns, m, c), BF16)],
                 args=(h, wg, wu), sem=("parallel", "parallel"), ride=ride)


def _gate_up_bwd(dh, wd, g, u, *, name, tm=1024):
    m, k = dh.shape
    ns, c, _ = wd.shape
    tm = _tile(m, tm, 16)

    def body(dh_ref, wd_ref, g_ref, u_ref, dg_ref, du_ref):
        da = _dot(dh_ref[...], wd_ref[...], tb=True)
        g = g_ref[...]
        s = _sigmoid(g)
        dg_ref[...] = (da * u_ref[...] * (s * (1.0 + g * (1.0 - s)))).astype(dg_ref.dtype)
        du_ref[...] = (da * (g * s)).astype(du_ref.dtype)

    o_spec = pl.BlockSpec((None, tm, c), lambda i, s: (s, i, 0))
    return _call(body, name=name, grid=(m // tm, ns),
                 in_specs=[pl.BlockSpec((tm, k), lambda i, s: (i, 0)), pl.BlockSpec((None, c, k), lambda i, s: (s, 0, 0)),
                           o_spec, o_spec],
                 out_specs=[o_spec, o_spec],
                 out_shape=[jax.ShapeDtypeStruct((ns, m, c), BF16), jax.ShapeDtypeStruct((ns, m, c), BF16)],
                 args=(dh, wd, g, u), sem=("parallel", "parallel"))


def _mm_shards(pairs, *, name, tb=False, residual=None, tm=1024, tn=1024, ride=None):
    ns, m, c = pairs[0][0].shape
    n = pairs[0][1].shape[1] if tb else pairs[0][1].shape[2]
    tm, tn = _tile(m, tm, 16), _tile(n, tn, LANE)
    npair = len(pairs)
    has_res = residual is not None

    def body(*refs):
        o_ref, acc_ref = refs[-2], refs[-1]
        s = pl.program_id(2)
        part = _dot(refs[0][...], refs[1][...], tb=tb)
        for p in range(1, npair):
            part = part + _dot(refs[2 * p][...], refs[2 * p + 1][...], tb=tb)

        @pl.when(s == 0)
        def _():
            acc_ref[...] = part

        @pl.when(s > 0)
        def _():
            acc_ref[...] += part

        @pl.when(s == ns - 1)
        def _():
            acc = acc_ref[...]
            if has_res:
                acc = acc + refs[2 * npair][...]
            o_ref[...] = acc

    a_spec = pl.BlockSpec((None, tm, c), lambda i, j, s: (s, i, 0))
    b_spec = (pl.BlockSpec((None, tn, c), lambda i, j, s: (s, j, 0)) if tb
              else pl.BlockSpec((None, c, tn), lambda i, j, s: (s, 0, j)))
    o_spec = pl.BlockSpec((tm, tn), lambda i, j, s: (i, j))
    args = [t for pr in pairs for t in pr] + ([residual] if has_res else [])
    res = _call(body, name=name, grid=(m // tm, n // tn, ns),
                in_specs=[a_spec, b_spec] * npair + ([o_spec] if has_res else []), out_specs=[o_spec],
                out_shape=[jax.ShapeDtypeStruct((m, n), F32)], args=args, scratch=[pltpu.VMEM((tm, tn), F32)],
                sem=("parallel", "parallel", "arbitrary"), ride=ride)
    return res[0] if ride is None else res


def _grad_shards(a, b, *, name, tk=1024):
    a3 = a.ndim == 3
    ns, l, c = a.shape if a3 else b.shape
    d = (b if a3 else a).shape[1]
    tk = _tile(l, tk, 16)
    nk = l // tk

    def body(a_ref, b_ref, o_ref, acc_ref):
        kk = pl.program_id(1)
        part = _dot(a_ref[...], b_ref[...], ta=True)

        @pl.when(kk == 0)
        def _():
            acc_ref[...] = part

        @pl.when(kk > 0)
        def _():
            acc_ref[...] += part

        @pl.when(kk == nk - 1)
        def _():
            o_ref[...] = acc_ref[...].astype(o_ref.dtype)

    s3 = pl.BlockSpec((None, tk, c), lambda s, kk: (s, kk, 0))
    s2 = pl.BlockSpec((tk, d), lambda s, kk: (kk, 0))
    out = (c, d) if a3 else (d, c)
    return pl.pallas_call(
        body, name=name, grid=(ns, nk), in_specs=[s3, s2] if a3 else [s2, s3],
        out_specs=pl.BlockSpec((None,) + out, lambda s, kk: (s, 0, 0)),
        out_shape=jax.ShapeDtypeStruct((ns,) + out, BF16), scratch_shapes=[pltpu.VMEM(out, F32)],
        compiler_params=_params("parallel", "arbitrary"),
    )(a, b)


def _rms_fwd(x, w, *, name, tl=512):
    l, d = x.shape
    tl = _tile(l, tl, 16)

    def body(x_ref, w_ref, o_ref):
        xv = x_ref[...]
        r = lax.rsqrt(jnp.mean(xv * xv, axis=-1, keepdims=True) + NORM_EPS)
        o_ref[...] = (xv * r * w_ref[...]).astype(o_ref.dtype)

    row = pl.BlockSpec((tl, d), lambda i: (i, 0))
    return pl.pallas_call(
        body, name=name, grid=(l // tl,), in_specs=[row, pl.BlockSpec((1, d), lambda i: (0, 0))], out_specs=row,
        out_shape=jax.ShapeDtypeStruct((l, d), BF16), compiler_params=_params("parallel"),
    )(x, w)


def _rms_bwd(x, w, dy, add, *, name, tl=256):
    l, d = x.shape
    tl = _tile(l, tl, 16)

    def body(x_ref, w_ref, dy_ref, add_ref, dx_ref, dxb_ref, dw_ref):
        i = pl.program_id(0)
        xv = x_ref[...]
        r = lax.rsqrt(jnp.mean(xv * xv, axis=-1, keepdims=True) + NORM_EPS)
        xh = xv * r
        dyv = dy_ref[...].astype(F32)
        dxh = dyv * w_ref[...]
        dx = r * (dxh - xh * jnp.mean(dxh * xh, axis=-1, keepdims=True)) + add_ref[...]
        dx_ref[...] = dx
        dxb_ref[...] = dx.astype(BF16)
        part = jnp.sum(dyv * xh, axis=0, keepdims=True)

        @pl.when(i == 0)
        def _():
            dw_ref[...] = part

        @pl.when(i > 0)
        def _():
            dw_ref[...] += part

    row = pl.BlockSpec((tl, d), lambda i: (i, 0))
    vec = pl.BlockSpec((1, d), lambda i: (0, 0))
    return pl.pallas_call(
        body, name=name, grid=(l // tl,), in_specs=[row, vec, row, row], out_specs=[row, row, vec],
        out_shape=[jax.ShapeDtypeStruct((l, d), F32), jax.ShapeDtypeStruct((l, d), BF16), jax.ShapeDtypeStruct((1, d), F32)],
        compiler_params=_params("arbitrary"),
    )(x, w, dy, add)


def _loss_head(h, w, target, *, name, tl=256):
    l, d = h.shape
    tl = _tile(l, tl, 16)

    def body(x_ref, w_ref, t_ref, loss_ref, dx_ref, dxb_ref, dw_ref):
        i = pl.program_id(0)
        xv = x_ref[...]
        r = lax.rsqrt(jnp.mean(xv * xv, axis=-1, keepdims=True) + NORM_EPS)
        xh = xv * r
        wv = w_ref[...]
        diff = xh * wv - t_ref[...]
        lpart = 0.5 * jnp.sum(jnp.mean(diff * diff, axis=-1, keepdims=True), axis=0, keepdims=True)
        dyv = diff * (1.0 / d)
        dxh = dyv * wv
        dx = r * (dxh - xh * jnp.mean(dxh * xh, axis=-1, keepdims=True))
        dx_ref[...] = dx
        dxb_ref[...] = dx.astype(BF16)
        part = jnp.sum(dyv * xh, axis=0, keepdims=True)

        @pl.when(i == 0)
        def _():
            dw_ref[...] = part
            loss_ref[...] = lpart

        @pl.when(i > 0)
        def _():
            dw_ref[...] += part
            loss_ref[...] += lpart

    row = pl.BlockSpec((tl, d), lambda i: (i, 0))
    vec = pl.BlockSpec((1, d), lambda i: (0, 0))
    one = pl.BlockSpec((1, 1), lambda i: (0, 0))
    return pl.pallas_call(
        body, name=name, grid=(l // tl,), in_specs=[row, vec, row], out_specs=[one, row, row, vec],
        out_shape=[jax.ShapeDtypeStruct((1, 1), F32), jax.ShapeDtypeStruct((l, d), F32),
                   jax.ShapeDtypeStruct((l, d), BF16), jax.ShapeDtypeStruct((1, d), F32)],
        compiler_params=_params("arbitrary"),
    )(h, w, target)


def _conv_pre(ext, w_ref):
    acc = ext * w_ref[CONV_K - 1:CONV_K, :]
    for k in range(CONV_K - 1):
        acc = acc + pltpu.roll(ext, CONV_K - 1 - k, axis=0) * w_ref[k:k + 1, :]
    return acc


def _conv_fwd(proj, col0, cw, cb, *, name, tl=512, tc=512):
    l = proj.shape[0]
    c = cw.shape[1]
    tl, tc = _tile(l, tl, 16), _tile(c, tc, LANE)
    assert col0 % tc == 0
    off, hb = col0 // tc, tl // CONV_HALO

    def body(x_ref, halo_ref, w_ref, b_ref, o_ref):
        i = pl.program_id(0)
        halo = jnp.where(i > 0, halo_ref[...], 0.0)
        ext = jnp.concatenate([halo, x_ref[...]], axis=0)
        pre = _conv_pre(ext, w_ref)[CONV_HALO:] + b_ref[...]
        o_ref[...] = pre * _sigmoid(pre)

    return pl.pallas_call(
        body, name=name, grid=(l // tl, c // tc),
        in_specs=[pl.BlockSpec((tl, tc), lambda i, j: (i, off + j)),
                  pl.BlockSpec((CONV_HALO, tc), lambda i, j: (jnp.maximum(i * hb - 1, 0), off + j)),
                  pl.BlockSpec((CONV_K, tc), lambda i, j: (0, j)), pl.BlockSpec((1, tc), lambda i, j: (0, j))],
        out_specs=pl.BlockSpec((tl, tc), lambda i, j: (i, j)),
        out_shape=jax.ShapeDtypeStruct((l, c), F32), compiler_params=_params("parallel", "parallel"),
    )(proj, proj, cw, cb)


def _conv_bwd(proj, col0, cw, cb, dout, *, name, tl=512, tc=512):
    l = proj.shape[0]
    c = dout.shape[1]
    tl, tc = _tile(l, tl, 16), _tile(c, tc, LANE)
    assert col0 % tc == 0
    off, hb, nt = col0 // tc, tl // CONV_HALO, l // tl
    n_ext = tl + CONV_HALO

    def body(x_ref, prev_ref, nxt_ref, d_ref, dn_ref, w_ref, b_ref, dx_ref, dw_ref, db_ref):
        i = pl.program_id(1)
        last = i == nt - 1
        halo = jnp.where(i > 0, prev_ref[...], 0.0)
        ext = jnp.concatenate([halo, x_ref[...], nxt_ref[...]], axis=0)
        pre = _conv_pre(ext, w_ref)[CONV_HALO:] + b_ref[...]
        dext = jnp.concatenate([d_ref[...], jnp.where(last, 0.0, dn_ref[...])], axis=0)
        s = _sigmoid(pre)
        dpre = dext * (s * (1.0 + pre * (1.0 - s)))
        dx = dpre * w_ref[CONV_K - 1:CONV_K, :]
        for k in range(CONV_K - 1):
            dx = dx + pltpu.roll(dpre, n_ext - (CONV_K - 1 - k), axis=0) * w_ref[k:k + 1, :]
        dx_ref[...] = dx[:tl].astype(dx_ref.dtype)
        dp = dpre[:tl]
        rows = [jnp.sum(dp * pltpu.roll(ext, CONV_K - 1 - k, axis=0)[CONV_HALO:CONV_HALO + tl], axis=0, keepdims=True)
                for k in range(CONV_K - 1)]
        rows.append(jnp.sum(dp * ext[CONV_HALO:CONV_HALO + tl], axis=0, keepdims=True))
        dwp = jnp.concatenate(rows, axis=0)
        dbp = jnp.sum(dp, axis=0, keepdims=True)

        @pl.when(i == 0)
        def _():
            dw_ref[...] = dwp
            db_ref[...] = dbp

        @pl.when(i > 0)
        def _():
            dw_ref[...] += dwp
            db_ref[...] += dbp

    cur = lambda j, i: (i, j)
    nxt = lambda j, i: (jnp.minimum((i + 1) * hb, nt * hb - 1), j)
    return pl.pallas_call(
        body, name=name, grid=(c // tc, nt),
        in_specs=[pl.BlockSpec((tl, tc), lambda j, i: (i, off + j)),
                  pl.BlockSpec((CONV_HALO, tc), lambda j, i: (jnp.maximum(i * hb - 1, 0), off + j)),
                  pl.BlockSpec((CONV_HALO, tc), lambda j, i: (jnp.minimum((i + 1) * hb, nt * hb - 1), off + j)),
                  pl.BlockSpec((tl, tc), cur), pl.BlockSpec((CONV_HALO, tc), nxt),
                  pl.BlockSpec((CONV_K, tc), lambda j, i: (0, j)), pl.BlockSpec((1, tc), lambda j, i: (0, j))],
        out_specs=[pl.BlockSpec((tl, tc), cur), pl.BlockSpec((CONV_K, tc), lambda j, i: (0, j)),
                   pl.BlockSpec((1, tc), lambda j, i: (0, j))],
        out_shape=[jax.ShapeDtypeStruct((l, c), BF16), jax.ShapeDtypeStruct((CONV_K, c), F32),
                   jax.ShapeDtypeStruct((1, c), F32)],
        compiler_params=_params("parallel", "arbitrary"),
    )(proj, proj, proj, dout, dout, cw, cb)


def _pool_fwd(proj, col0, d_pool, *, name, tl=256):
    l = proj.shape[0]
    tl = _tile(l, tl, POOL_HALO)
    assert col0 % d_pool == 0
    off, hb, pg = col0 // d_pool, tl // POOL_HALO, d_pool // GROUPS

    def body(u_ref, halo_ref, o_ref):
        i = pl.program_id(0)
        halo = jnp.where(i > 0, halo_ref[...], 0.0)
        ext = jnp.concatenate([halo, u_ref[...]], axis=0)
        t = i * tl + lax.broadcasted_iota(jnp.int32, (tl, 1), 0)
        s, width = ext, 1
        for gi, win in enumerate(POOL_WINDOWS):
            while width < win:
                s = s + pltpu.roll(s, width, axis=0)
                width *= 2
            cnt = jnp.minimum(t + 1, win).astype(F32)
            sl = slice(gi * pg, (gi + 1) * pg)
            o_ref[:, sl] = (s[POOL_HALO:, sl] / cnt - ext[POOL_HALO:, sl]).astype(o_ref.dtype)

    return pl.pallas_call(
        body, name=name, grid=(l // tl,),
        in_specs=[pl.BlockSpec((tl, d_pool), lambda i: (i, off)),
                  pl.BlockSpec((POOL_HALO, d_pool), lambda i: (jnp.maximum(i * hb - 1, 0), off))],
        out_specs=pl.BlockSpec((tl, d_pool), lambda i: (i, 0)),
        out_shape=jax.ShapeDtypeStruct((l, d_pool), BF16), compiler_params=_params("parallel"),
    )(proj, proj)


def _pool_bwd(dp, *, name, tl=256):
    l, d_pool = dp.shape
    tl = _tile(l, tl, POOL_HALO)
    hb, nt, pg = tl // POOL_HALO, l // tl, d_pool // GROUPS
    n_ext = tl + POOL_HALO

    def body(d_ref, nxt_ref, o_ref):
        i = pl.program_id(0)
        ext = jnp.concatenate([d_ref[...], jnp.where(i == nt - 1, 0.0, nxt_ref[...])], axis=0)
        t = i * tl + lax.broadcasted_iota(jnp.int32, (n_ext, 1), 0)
        for gi, win in enumerate(POOL_WINDOWS):
            sl = slice(gi * pg, (gi + 1) * pg)
            dg = ext[:, sl]
            s = dg / jnp.minimum(t + 1, win).astype(F32)
            width = 1
            while width < win:
                s = s + pltpu.roll(s, n_ext - width, axis=0)
                width *= 2
            o_ref[:, sl] = (s[:tl] - dg[:tl]).astype(o_ref.dtype)

    return pl.pallas_call(
        body, name=name, grid=(nt,),
        in_specs=[pl.BlockSpec((tl, d_pool), lambda i: (i, 0)),
                  pl.BlockSpec((POOL_HALO, d_pool), lambda i: (jnp.minimum((i + 1) * hb, nt * hb - 1), 0))],
        out_specs=pl.BlockSpec((tl, d_pool), lambda i: (i, 0)),
        out_shape=jax.ShapeDtypeStruct((l, d_pool), BF16), compiler_params=_params("parallel"),
    )(dp, dp)


def _pool_mix_fwd(pooled, pw, scale, mixed, col0, *, name, tm=1024):
    l, d_pool = pooled.shape
    pg = d_pool // GROUPS
    tm = _tile(l, tm, 16)
    assert col0 % pg == 0
    off = col0 // pg

    def body(a_ref, w_ref, s_ref, mixed_in, raw_ref, mixed_ref):
        del mixed_in
        raw = _dot(a_ref[...], w_ref[...])
        raw_ref[...] = raw
        mixed_ref[...] = (raw * s_ref[...]).astype(mixed_ref.dtype)

    blk = pl.BlockSpec((tm, pg), lambda i, g: (i, g))
    return pl.pallas_call(
        body, name=name, grid=(l // tm, GROUPS),
        in_specs=[blk, pl.BlockSpec((None, pg, pg), lambda i, g: (g, 0, 0)), pl.BlockSpec((1, pg), lambda i, g: (0, g)),
                  pl.BlockSpec(memory_space=pl.ANY)],
        out_specs=[blk, pl.BlockSpec((tm, pg), lambda i, g: (i, off + g))],
        out_shape=[jax.ShapeDtypeStruct((l, d_pool), F32), jax.ShapeDtypeStruct(mixed.shape, mixed.dtype)],
        input_output_aliases={3: 1}, compiler_params=_params("parallel", "parallel"),
    )(pooled, pw, scale, mixed)


def _pool_scale_bwd(dmixed, col0, raw, scale, *, name, tl=512):
    l, d_pool = raw.shape
    tl, tc = _tile(l, tl, 16), _tile(d_pool, 512, LANE)
    assert col0 % tc == 0
    off = col0 // tc

    def body(d_ref, raw_ref, s_ref, o_ref, ds_ref):
        i = pl.program_id(1)
        dv = d_ref[...]
        o_ref[...] = (dv * s_ref[...]).astype(o_ref.dtype)
        part = jnp.sum(dv * raw_ref[...], axis=0, keepdims=True)

        @pl.when(i == 0)
        def _():
            ds_ref[...] = part

        @pl.when(i > 0)
        def _():
            ds_ref[...] += part

    blk = pl.BlockSpec((tl, tc), lambda j, i: (i, j))
    vec = pl.BlockSpec((1, tc), lambda j, i: (0, j))
    return pl.pallas_call(
        body, name=name, grid=(d_pool // tc, l // tl),
        in_specs=[pl.BlockSpec((tl, tc), lambda j, i: (i, off + j)), blk, vec], out_specs=[blk, vec],
        out_shape=[jax.ShapeDtypeStruct((l, d_pool), BF16), jax.ShapeDtypeStruct((1, d_pool), F32)],
        compiler_params=_params("parallel", "arbitrary"),
    )(dmixed, raw, scale)


def _group_mm_nt(a, w, *, name, tm=1024):
    l, d_pool = a.shape
    pg = d_pool // GROUPS
    tm = _tile(l, tm, 16)

    def body(a_ref, w_ref, o_ref):
        o_ref[...] = _dot(a_ref[...], w_ref[...], tb=True)

    blk = pl.BlockSpec((tm, pg), lambda i, g: (i, g))
    return pl.pallas_call(
        body, name=name, grid=(l // tm, GROUPS),
        in_specs=[blk, pl.BlockSpec((None, pg, pg), lambda i, g: (g, 0, 0))], out_specs=blk,
        out_shape=jax.ShapeDtypeStruct((l, d_pool), F32), compiler_params=_params("parallel", "parallel"),
    )(a, w)


def _group_mm_tn(a, b, *, name, tk=1024):
    l, d_pool = a.shape
    pg = d_pool // GROUPS
    tk = _tile(l, tk, 16)

    def body(a_ref, b_ref, o_ref):
        kk = pl.program_id(1)
        part = _dot(a_ref[...], b_ref[...], ta=True)

        @pl.when(kk == 0)
        def _():
            o_ref[...] = part

        @pl.when(kk > 0)
        def _():
            o_ref[...] += part

    blk = pl.BlockSpec((tk, pg), lambda g, kk: (kk, g))
    return pl.pallas_call(
        body, name=name, grid=(GROUPS, l // tk), in_specs=[blk, blk],
        out_specs=pl.BlockSpec((None, pg, pg), lambda g, kk: (g, 0, 0)),
        out_shape=jax.ShapeDtypeStruct((GROUPS, pg, pg), F32), compiler_params=_params("parallel", "arbitrary"),
    )(a, b)


def _gnorm_fwd(y, proj, wn, d_mix, *, name, tl=512):
    l, d_ssm = y.shape
    gw = d_ssm // GROUPS
    tl = _tile(l, tl, 16)

    def body(y_ref, z_ref, w_ref, o_ref):
        z = z_ref[...]
        gv = y_ref[...] * (z * _sigmoid(z))
        r = lax.rsqrt(jnp.mean(gv * gv, axis=-1, keepdims=True) + NORM_EPS)
        o_ref[...] = (gv * r * w_ref[...]).astype(o_ref.dtype)

    blk = pl.BlockSpec((tl, gw), lambda i, g: (i, g))
    return pl.pallas_call(
        body, name=name, grid=(l // tl, GROUPS), in_specs=[blk, blk, pl.BlockSpec((1, gw), lambda i, g: (0, g))],
        out_specs=blk, out_shape=jax.ShapeDtypeStruct((l, d_mix), BF16), compiler_params=_params("parallel", "parallel"),
    )(y, proj, wn)


def _gnorm_bwd(dmixed, y, proj, wn, *, name, tl=512):
    l, d_ssm = y.shape
    gw = d_ssm // GROUPS
    tl = _tile(l, tl, 16)

    def body(d_ref, y_ref, z_ref, w_ref, dy_ref, dz_ref, dw_ref):
        i = pl.program_id(1)
        z, yv, dv = z_ref[...], y_ref[...], d_ref[...]
        s = _sigmoid(z)
        sz = z * s
        gv = yv * sz
        r = lax.rsqrt(jnp.mean(gv * gv, axis=-1, keepdims=True) + NORM_EPS)
        gh = gv * r
        dgh = dv * w_ref[...]
        dg = r * (dgh - gh * jnp.mean(dgh * gh, axis=-1, keepdims=True))
        dy_ref[...] = dg * sz
        dz_ref[...] = (dg * yv * (s * (1.0 + z * (1.0 - s)))).astype(dz_ref.dtype)
        part = jnp.sum(dv * gh, axis=0, keepdims=True)

        @pl.when(i == 0)
        def _():
            dw_ref[...] = part

        @pl.when(i > 0)
        def _():
            dw_ref[...] += part

    blk = pl.BlockSpec((tl, gw), lambda g, i: (i, g))
    vec = pl.BlockSpec((1, gw), lambda g, i: (0, g))
    return pl.pallas_call(
        body, name=name, grid=(GROUPS, l // tl), in_specs=[blk, blk, blk, vec], out_specs=[blk, blk, vec],
        out_shape=[jax.ShapeDtypeStruct((l, d_ssm), F32), jax.ShapeDtypeStruct((l, d_ssm), BF16),
                   jax.ShapeDtypeStruct((1, d_ssm), F32)],
        compiler_params=_params("parallel", "arbitrary"),
    )(dmixed, y, proj, wn)


def _split3(v):
    hi = v.astype(BF16)
    r1 = v - hi.astype(F32)
    mid = r1.astype(BF16)
    lo = (r1 - mid.astype(F32)).astype(BF16)
    return hi, mid, lo


def _tri_dot(tri, v):
    hi, mid, lo = _split3(v)
    d = lambda p: jnp.dot(tri, p, preferred_element_type=F32)
    return d(hi) + d(mid) + d(lo)


def _dot_tri(v, tri):
    hi, mid, lo = _split3(v)
    d = lambda p: jnp.dot(p, tri, preferred_element_type=F32)
    return d(hi) + d(mid) + d(lo)


def _ssd_specs(hpg, nc, order):
    gw, q = hpg * HEAD_DIM, CHUNK
    b_off, c_off = (GROUPS * gw) // STATE, (GROUPS * gw) // STATE + GROUPS
    xs = pl.BlockSpec((q, gw), lambda g, c: (order(c), g))
    bm = pl.BlockSpec((q, STATE), lambda g, c: (order(c), b_off + g))
    cm = pl.BlockSpec((q, STATE), lambda g, c: (order(c), c_off + g))
    dt = pl.BlockSpec((None, q, hpg), lambda g, c: (g, order(c), 0))
    dtt = pl.BlockSpec((None, hpg, q), lambda g, c: (g, 0, order(c)))
    hrow = pl.BlockSpec((None, 1, hpg), lambda g, c: (g, 0, 0))
    hcol = pl.BlockSpec((None, hpg, 1), lambda g, c: (g, 0, 0))
    st = pl.BlockSpec((None, None, hpg, STATE, HEAD_DIM), lambda g, c: (g, order(c), 0, 0, 0))
    return xs, bm, cm, dt, dtt, hrow, hcol, st


def _ssd_common(dtr_ref, dtrt_ref, bias_ref, biast_ref, alog_ref, alogt_ref):
    q = CHUNK
    dt = _softplus(dtr_ref[...] + bias_ref[...])
    dtt = _softplus(dtrt_ref[...] + biast_ref[...])
    a = -jnp.exp(alog_ref[...])
    at = -jnp.exp(alogt_ref[...])
    row = lax.broadcasted_iota(jnp.int32, (q, q), 0)
    col = lax.broadcasted_iota(jnp.int32, (q, q), 1)
    causal = row >= col
    lower = causal.astype(BF16)
    upper = (row <= col).astype(BF16)
    acum = _tri_dot(lower, dt * a)
    acum_t = _dot_tri(dtt * at, upper)
    return dt, a, acum, acum_t, causal, lower, upper


def _ssd_fwd(xbc, dtr, dtrt, bias, biast, alog, alogt, dskip, *, name, ride=None):
    l = xbc.shape[0]
    hpg = dtr.shape[2]
    gw, q, nc = hpg * HEAD_DIM, CHUNK, l // CHUNK
    xs_s, bm_s, cm_s, dt_s, dtt_s, hrow, hcol, st_s = _ssd_specs(hpg, nc, lambda c: c)

    def body(xs_ref, b_ref, c_ref, dtr_ref, dtrt_ref, bias_ref, biast_ref, alog_ref, alogt_ref, dsk_ref,
             y_ref, st_ref, h_ref):
        @pl.when(pl.program_id(1) == 0)
        def _():
            h_ref[...] = jnp.zeros_like(h_ref)

        dt, _, acum, acum_t, causal, _, _ = _ssd_common(dtr_ref, dtrt_ref, bias_ref, biast_ref, alog_ref, alogt_ref)
        bmat, cmat = b_ref[...], c_ref[...]
        cb = _dot(cmat, bmat, tb=True)
        cbf = cmat.astype(BF16)
        a_last = acum[q - 1:q, :]
        dsk = dsk_ref[...]
        for j in range(hpg):
            a_col, a_row = acum[:, j:j + 1], acum_t[j:j + 1, :]
            lm = jnp.exp(jnp.where(causal, a_col - a_row, NEG_BIG))
            xj = xs_ref[:, j * HEAD_DIM:(j + 1) * HEAD_DIM]
            xdt = (xj * dt[:, j:j + 1]).astype(BF16)
            hj = h_ref[j]
            st_ref[j] = hj
            yj = _dot(cb * lm, xdt) + _dot(cbf, hj) * jnp.exp(a_col) + dsk[:, j:j + 1] * xj
            y_ref[:, j * HEAD_DIM:(j + 1) * HEAD_DIM] = yj
            al = a_last[:, j:j + 1]
            be = bmat * jnp.exp(al - a_col)
            h_ref[j] = jnp.exp(al) * hj + _dot(be, xdt, ta=True)

    return _call(
        body, name=name, grid=(GROUPS, nc),
        in_specs=[xs_s, bm_s, cm_s, dt_s, dtt_s, hrow, hcol, hrow, hcol, hrow], out_specs=[xs_s, st_s],
        out_shape=[jax.ShapeDtypeStruct((l, GROUPS * gw), F32),
                   jax.ShapeDtypeStruct((GROUPS, nc, hpg, STATE, HEAD_DIM), F32)],
        scratch=[pltpu.VMEM((hpg, STATE, HEAD_DIM), F32)], sem=("parallel", "arbitrary"),
        args=(xbc, xbc, xbc, dtr, dtrt, bias, biast, alog, alogt, dskip), ride=ride)


def _ssd_bwd(xbc, dtr, dtrt, bias, biast, alog, alogt, dskip, states, dy, *, name, ride=None):
    l = xbc.shape[0]
    hpg = dtr.shape[2]
    gw, q, nc = hpg * HEAD_DIM, CHUNK, l // CHUNK
    rev = lambda c: nc - 1 - c
    xs_s, bm_s, cm_s, dt_s, dtt_s, hrow, hcol, st_s = _ssd_specs(hpg, nc, rev)
    bc_s = pl.BlockSpec((q, STATE), lambda g, c: (rev(c), g))

    def body(xs_ref, b_ref, c_ref, dtr_ref, dtrt_ref, bias_ref, biast_ref, alog_ref, alogt_ref, dsk_ref, st_ref, dy_ref,
             dxs_ref, db_ref, dc_ref, ddtr_ref, dbias_ref, dalog_ref, ddsk_ref, dh_ref):
        step = pl.program_id(1)

        @pl.when(step == 0)
        def _():
            dh_ref[...] = jnp.zeros_like(dh_ref)

        dt, a, acum, acum_t, causal, lower, upper = _ssd_common(
            dtr_ref, dtrt_ref, bias_ref, biast_ref, alog_ref, alogt_ref)
        bmat, cmat = b_ref[...], c_ref[...]
        cb = _dot(cmat, bmat, tb=True)
        cbf = cmat.astype(BF16)
        a_last = acum[q - 1:q, :]
        dsk = dsk_ref[...]
        lane_h = lax.broadcasted_iota(jnp.int32, (1, hpg), 1)
        row_q = lax.broadcasted_iota(jnp.int32, (q, 1), 0)
        strict = lax.broadcasted_iota(jnp.int32, (q, q), 0) > lax.broadcasted_iota(jnp.int32, (q, q), 1)
        dcb = jnp.zeros((q, q), F32)
        dc_acc = jnp.zeros((q, STATE), F32)
        db_acc = jnp.zeros((q, STATE), F32)
        d_corner_all = jnp.zeros((q, hpg), F32)
        da_local_all = jnp.zeros((q, hpg), F32)
        d_dt_x = jnp.zeros((q, hpg), F32)
        d_dsk = jnp.zeros((1, hpg), F32)
        for j in range(hpg):
            onehot = (lane_h == j).astype(F32)
            a_col, a_row = acum[:, j:j + 1], acum_t[j:j + 1, :]
            al = a_last[:, j:j + 1]
            lm = jnp.exp(jnp.where(causal, a_col - a_row, NEG_BIG))
            mmat = cb * lm
            xj = xs_ref[:, j * HEAD_DIM:(j + 1) * HEAD_DIM]
            dtj = dt[:, j:j + 1]
            xdt = (xj * dtj).astype(BF16)
            ea = jnp.exp(a_col)
            ee = jnp.exp(al - a_col)
            gam = jnp.exp(al)
            be = bmat * ee
            hj = st_ref[j]
            dhj = dh_ref[j]
            dyj = dy_ref[:, j * HEAD_DIM:(j + 1) * HEAD_DIM]
            dyb = dyj.astype(BF16)
            dxdt = _dot(mmat, dyb, ta=True) + _dot(be, dhj)
            dm = _dot(dyb, xdt, tb=True)
            dcb = dcb + dm * lm
            wmat = dm * mmat
            corner = jnp.dot(upper, wmat.astype(BF16), preferred_element_type=F32)
            d_corner = jnp.sum(jnp.where(strict, corner, 0.0), axis=1, keepdims=True)
            zc = _dot(cbf, hj)
            dz = dyj * ea
            dc_acc = dc_acc + _dot(dz, hj, tb=True)
            dprev = _dot(cbf, dz, ta=True)
            da_off = jnp.sum(dyj * zc, axis=1, keepdims=True) * ea
            dbe = _dot(xdt, dhj, tb=True)
            db_acc = db_acc + dbe * ee
            de_e = jnp.sum(dbe * be, axis=1, keepdims=True)
            dgam_g = jnp.sum(jnp.sum(dhj * hj, axis=1, keepdims=True), axis=0, keepdims=True) * gam
            tail = jnp.sum(de_e, axis=0, keepdims=True) + dgam_g
            da_local = da_off - de_e + jnp.where(row_q == q - 1, tail, 0.0)
            d_corner_all = d_corner_all + d_corner * onehot
            da_local_all = da_local_all + da_local * onehot
            d_dt_x = d_dt_x + jnp.sum(dxdt * xj, axis=1, keepdims=True) * onehot
            dxs_ref[:, j * HEAD_DIM:(j + 1) * HEAD_DIM] = dxdt * dtj + dsk[:, j:j + 1] * dyj
            d_dsk = d_dsk + jnp.sum(jnp.sum(dyj * xj, axis=1, keepdims=True), axis=0, keepdims=True) * onehot
            dh_ref[j] = gam * dhj + dprev
        dc_ref[...] = dc_acc + _dot(dcb, bmat)
        db_ref[...] = db_acc + _dot(dcb, cmat, ta=True)
        d_dta = d_corner_all + _tri_dot(upper, da_local_all)
        d_dt = d_dta * a + d_dt_x
        sg = _sigmoid(dtr_ref[...] + bias_ref[...])
        d_raw = d_dt * sg
        ddtr_ref[...] = d_raw
        p_bias = jnp.sum(d_raw, axis=0, keepdims=True)
        p_alog = jnp.sum(d_dta * dt, axis=0, keepdims=True) * a

        @pl.when(step == 0)
        def _():
            dbias_ref[...] = p_bias
            dalog_ref[...] = p_alog
            ddsk_ref[...] = d_dsk

        @pl.when(step > 0)
        def _():
            dbias_ref[...] += p_bias
            dalog_ref[...] += p_alog
            ddsk_ref[...] += d_dsk

    return _call(
        body, name=name, grid=(GROUPS, nc),
        in_specs=[xs_s, bm_s, cm_s, dt_s, dtt_s, hrow, hcol, hrow, hcol, hrow, st_s, xs_s],
        out_specs=[xs_s, bc_s, bc_s, dt_s, hrow, hrow, hrow],
        out_shape=[jax.ShapeDtypeStruct((l, GROUPS * gw), F32), jax.ShapeDtypeStruct((l, GROUPS * STATE), F32),
                   jax.ShapeDtypeStruct((l, GROUPS * STATE), F32), jax.ShapeDtypeStruct((GROUPS, l, hpg), F32),
                   jax.ShapeDtypeStruct((GROUPS, 1, hpg), F32), jax.ShapeDtypeStruct((GROUPS, 1, hpg), F32),
                   jax.ShapeDtypeStruct((GROUPS, 1, hpg), F32)],
        scratch=[pltpu.VMEM((hpg, STATE, HEAD_DIM), F32)], sem=("parallel", "arbitrary"),
        args=(xbc, xbc, xbc, dtr, dtrt, bias, biast, alog, alogt, dskip, states, dy), ride=ride)


def _exchange(srcs, scatter, *, name):
    ride = _Ride(srcs, scatter)

    def body(*refs):
        src, dst, sems = refs[:ride.n], refs[ride.n:2 * ride.n], refs[2 * ride.n:]
        ride.start(src, dst, sems)
        ride.finish(src, dst, sems)

    return pl.pallas_call(body, name=name, in_specs=ride.in_specs, out_specs=ride.out_specs, out_shape=ride.out_shape,
                          scratch_shapes=ride.scratch)(*srcs)


def _adamw(parts, w, m, v, *, name, tr=256):
    r, c = w.shape
    tr = _tile(r, tr, 16)
    c1, c2 = 1.0 / (1.0 - ADAM_B1 ** ADAM_STEP), 1.0 / (1.0 - ADAM_B2 ** ADAM_STEP)

    def body(p_ref, w_ref, m_ref, v_ref, g_ref, d_ref, nm_ref, nv_ref):
        g = p_ref[0].astype(F32)
        for p in range(1, N_DEV):
            g = g + p_ref[p].astype(F32)
        nm = ADAM_B1 * m_ref[...] + (1.0 - ADAM_B1) * g
        nv = ADAM_B2 * v_ref[...] + (1.0 - ADAM_B2) * (g * g)
        g_ref[...] = g
        nm_ref[...] = nm
        nv_ref[...] = nv
        d_ref[...] = -ADAM_LR * ((nm * c1) / (jnp.sqrt(nv * c2) + ADAM_EPS) + ADAM_WD * w_ref[...])

    blk = pl.BlockSpec((tr, c), lambda i: (i, 0))
    out = jax.ShapeDtypeStruct((r, c), F32)
    return pl.pallas_call(
        body, name=name, grid=(r // tr,), in_specs=[pl.BlockSpec((N_DEV, tr, c), lambda i: (0, i, 0)), blk, blk, blk],
        out_specs=[blk, blk, blk, blk], out_shape=[out, out, out, out], compiler_params=_params("parallel"),
    )(parts, w, m, v)


def kernel(x, attn_norm_w, w_in, conv_w, conv_b, dt_bias, a_log, d_skip, ssd_norm_w, pool_w, pool_scale, w_out, ffn_norm_w, w_gate, w_up, w_down, final_norm_w, loss_target, m_attn_norm_w, m_w_in, m_conv_w, m_conv_b, m_dt_bias, m_a_log, m_d_skip, m_ssd_norm_w, m_pool_w, m_pool_scale, m_w_out, m_ffn_norm_w, m_w_gate, m_w_up, m_w_down, m_final_norm_w, v_attn_norm_w, v_w_in, v_conv_w, v_conv_b, v_dt_bias, v_a_log, v_d_skip, v_ssd_norm_w, v_pool_w, v_pool_scale, v_w_out, v_ffn_norm_w, v_w_gate, v_w_up, v_w_down, v_final_norm_w):
    l, d = x.shape[1], x.shape[2]
    heads = dt_bias.shape[1]
    hpg = heads // GROUPS
    d_ssm = heads * HEAD_DIM
    conv_ch = conv_b.shape[1]
    d_pool = pool_scale.shape[1]
    pg = d_pool // GROUPS
    d_mix = d_ssm + d_pool
    d_ff = w_gate.shape[2] * N_DEV
    d_in = w_in.shape[2] * N_DEV
    dt_pad = -(-heads // LANE) * LANE
    o_u, o_xbc, o_dt = d_ssm, d_ssm + d_pool, d_ssm + d_pool + conv_ch
    d_inp = o_dt + dt_pad
    tn_in = _tile(d_inp, 2560, LANE)
    g_dt, g_u = d_ssm + conv_ch, d_ssm + conv_ch + heads

    x2, tgt = x[0], loss_target[0]

    gi, gp, gc = _exchange([w_in[0].astype(BF16), pool_w[0].astype(BF16), conv_w[0]], False, name="gather_w_in")
    win = jnp.transpose(gi, (1, 0, 2)).reshape(d, d_in)
    wp = jnp.concatenate([win[:, :d_ssm], win[:, g_u:], win[:, d_ssm:g_dt], win[:, g_dt:g_u],
                          jnp.zeros((d, dt_pad - heads), BF16)], axis=1)
    pw = jnp.transpose(gp, (1, 0, 2, 3)).reshape(GROUPS, pg, pg)
    cw = jnp.transpose(gc, (1, 0, 2)).reshape(CONV_K, conv_ch)

    def per_group(vec):
        r = vec.reshape(GROUPS, 1, hpg)
        return r, jnp.transpose(r, (0, 2, 1))

    bias_r, bias_c = per_group(dt_bias)
    alog_r, alog_c = per_group(a_log)
    dsk_r, _ = per_group(d_skip)

    h0 = _rms_fwd(x2, attn_norm_w, name="attn_norm")
    proj, go, gg = _mm(h0, wp, name="in_proj", tm=512, tn=tn_in, tk=d,
                       ride=_Ride([w_out[0].astype(BF16), w_gate[0].astype(BF16)], False))
    wo = go.reshape(d_mix, d)
    xbc = _conv_fwd(proj, o_xbc, cw, conv_b, name="conv_fwd")
    dtr = jnp.transpose(proj[:, o_dt:o_dt + heads].reshape(l, GROUPS, hpg), (1, 0, 2))
    dtrt = jnp.transpose(dtr, (0, 2, 1))
    y, states, gu = _ssd_fwd(xbc, dtr, dtrt, bias_r, bias_c, alog_r, alog_c, dsk_r, name="ssd_fwd",
                             ride=_Ride([w_up[0].astype(BF16)], False))
    mixed = _gnorm_fwd(y, proj, ssd_norm_w, d_mix, name="gated_norm")
    pooled = _pool_fwd(proj, o_u, d_pool, name="pool_fwd")
    pool_raw, mixed = _pool_mix_fwd(pooled, pw, pool_scale, mixed, d_ssm, name="pool_mix")
    h1 = _mm(mixed, wo, name="out_proj", residual=x2)
    h1n = _rms_fwd(h1, ffn_norm_w, name="ffn_norm")
    gate, up, act, gd = _gate_up(h1n, gg, gu, name="gate_up", ride=_Ride([w_down[0].astype(BF16)], False))
    h2 = _mm_shards([(act, gd)], name="down_proj", residual=h1)

    loss11, dh2, dh2b, g_final = _loss_head(h2, final_norm_w.reshape(1, d), tgt, name="loss_head")
    dgate, dup = _gate_up_bwd(dh2b, gd, gate, up, name="gate_up_bwd")
    g_wd = _grad_shards(act, dh2b, name="grad_w_down")
    dh1n, r_wd = _mm_shards([(dgate, gg), (dup, gu)], name="dgate_dup", tb=True, ride=_Ride([g_wd], True))
    g_wg = _grad_shards(h1n, dgate, name="grad_w_gate")
    g_wu = _grad_shards(h1n, dup, name="grad_w_up")
    dh1, dh1b, g_ffn = _rms_bwd(h1, ffn_norm_w, dh1n, dh2, name="ffn_norm_bwd")
    dmixed = _mm(dh1b, wo, name="dmixed", tb=True, tk=d)
    g_wo = _mm(mixed, dh1b, name="grad_w_out", ta=True, out_dtype=BF16, tm=1024, tn=1024, tk=1024)
    draw, g_pscale = _pool_scale_bwd(dmixed, d_ssm, pool_raw, pool_scale, name="pool_scale_bwd")
    dpooled = _group_mm_nt(draw, pw, name="dpooled")
    g_pw = _group_mm_tn(pooled, draw, name="grad_pool_w")
    du = _pool_bwd(dpooled, name="pool_bwd")
    dy, dz, g_ssdn = _gnorm_bwd(dmixed, y, proj, ssd_norm_w, name="gated_norm_bwd")
    s_pw = jnp.transpose(g_pw.reshape(GROUPS, N_DEV, pg // N_DEV, pg), (1, 0, 2, 3)).astype(BF16)
    dxs, dbm, dcm, ddtr, g_bias, g_alog, g_dsk, r_wg, r_wu, r_wo, r_pw = _ssd_bwd(
        xbc, dtr, dtrt, bias_r, bias_c, alog_r, alog_c, dsk_r, states, dy, name="ssd_bwd",
        ride=_Ride([g_wg, g_wu, g_wo.reshape(N_DEV, d_mix // N_DEV, d), s_pw], True))
    n_bc = GROUPS * STATE
    segs = [(dxs, 0, d_ssm), (dbm, d_ssm, n_bc), (dcm, d_ssm + n_bc, n_bc)]
    dxbc, g_cw, g_cb = [], [], []
    for si, (dseg, c0, width) in enumerate(segs):
        a, b, c_ = _conv_bwd(proj, o_xbc + c0, cw[:, c0:c0 + width], conv_b[:, c0:c0 + width], dseg, name=f"conv_bwd{si}")
        dxbc.append(a)
        g_cw.append(b)
        g_cb.append(c_)
    ddt = jnp.transpose(ddtr, (1, 0, 2)).reshape(l, heads).astype(BF16)
    dproj = jnp.concatenate([dz, du] + dxbc + [ddt, jnp.zeros((l, dt_pad - heads), BF16)], axis=1)
    g_wp = _mm(h0, dproj, name="grad_w_in", ta=True, out_dtype=BF16, tm=512, tn=tn_in, tk=1024)
    g_win = jnp.concatenate([g_wp[:, :d_ssm], g_wp[:, o_xbc:o_dt + heads], g_wp[:, o_u:o_xbc]], axis=1)
    def cols8(g):
        return jnp.transpose(g.reshape(g.shape[0], N_DEV, g.shape[1] // N_DEV), (1, 0, 2))
    dh0, r_in, r_cw = _mm(dproj, wp, name="dh0", tb=True, tk=tn_in,
                          ride=_Ride([cols8(g_win), cols8(jnp.concatenate(g_cw, axis=1))], True))
    dx, _, g_attn = _rms_bwd(x2, attn_norm_w, dh0, dh1, name="attn_norm_bwd")

    smalls =[g_attn, jnp.concatenate(g_cb, axis=1), g_bias.reshape(1, heads), g_alog.reshape(1, heads),
              g_dsk.reshape(1, heads), g_ssdn, g_pscale, g_ffn, g_final]
    small_w = [attn_norm_w, conv_b, dt_bias, a_log, d_skip, ssd_norm_w, pool_scale, ffn_norm_w, final_norm_w.reshape(1, d)]
    small_m = [m_attn_norm_w, m_conv_b, m_dt_bias, m_a_log, m_d_skip, m_ssd_norm_w, m_pool_scale, m_ffn_norm_w,
               m_final_norm_w.reshape(1, d)]
    small_v = [v_attn_norm_w, v_conv_b, v_dt_bias, v_a_log, v_d_skip, v_ssd_norm_w, v_pool_scale, v_ffn_norm_w,
               v_final_norm_w.reshape(1, d)]
    sizes = [s.shape[1] for s in smalls]
    n_small = sum(sizes)
    n_pad = -(-n_small // (16 * LANE)) * (16 * LANE)
    rows = n_pad // LANE
    def pack(vs):
        return jnp.pad(jnp.concatenate(vs, axis=1), ((0, 0), (0, n_pad - n_small))).reshape(rows, LANE)
    (r_small,) = _exchange([pack(smalls)], False, name="gather_small_grads")

    def big(parts, w, m, v, nm):
        shp = w.shape
        r2 = lambda t: t.reshape(-1, shp[-1])
        outs = _adamw(parts.reshape(N_DEV, -1, shp[-1]), r2(w), r2(m), r2(v), name=nm)
        return [o.reshape(shp) for o in outs]

    res = {
        "w_in": big(r_in, w_in, m_w_in, v_w_in, "adamw_w_in"),
        "conv_w": big(r_cw, conv_w, m_conv_w, v_conv_w, "adamw_conv_w"),
        "pool_w": big(r_pw, pool_w, m_pool_w, v_pool_w, "adamw_pool_w"),
        "w_out": big(r_wo, w_out, m_w_out, v_w_out, "adamw_w_out"),
        "w_gate": big(r_wg, w_gate, m_w_gate, v_w_gate, "adamw_w_gate"),
        "w_up": big(r_wu, w_up, m_w_up, v_w_up, "adamw_w_up"),
        "w_down": big(r_wd, w_down, m_w_down, v_w_down, "adamw_w_down"),
    }
    s_out = _adamw(r_small, pack(small_w), pack(small_m), pack(small_v), name="adamw_small")
    names = ["attn_norm_w", "conv_b", "dt_bias", "a_log", "d_skip", "ssd_norm_w", "pool_scale", "ffn_norm_w", "final_norm_w"]
    offs = [sum(sizes[:i]) for i in range(len(sizes))]
    for i, nm in enumerate(names):
        shp = (d,) if nm == "final_norm_w" else (1, sizes[i])
        res[nm] = [o.reshape(1, n_pad)[:, offs[i]:offs[i] + sizes[i]].reshape(shp) for o in s_out]

    loss = lax.psum(loss11[0, 0], ("x", "y", "c"))
    order = ["attn_norm_w", "w_in", "conv_w", "conv_b", "dt_bias", "a_log", "d_skip", "ssd_norm_w", "pool_w", "pool_scale",
             "w_out", "ffn_norm_w", "w_gate", "w_up", "w_down", "final_norm_w"]
    outs = [loss, dx.reshape(x.shape)]
    for part in range(4):
        outs += [res[nm][part] for nm in order]
    return tuple(outs)
```

```python
import functools

import jax
import jax.numpy as jnp
from jax import lax
from jax.experimental import pallas as pl
from jax.experimental.pallas import tpu as pltpu

F32 = jnp.float32
BF16 = jnp.bfloat16

NORM_EPS = 1e-5
HEAD_DIM = 64
STATE = 128
CHUNK = 256
GROUPS = 4
CONV_K = 4
POOL_WINDOWS = (2, 4, 8, 16)
POOL_HALO = 16
CONV_HALO = 8
LANE = 128
N_DEV = 8
ADAM_LR, ADAM_B1, ADAM_B2, ADAM_EPS, ADAM_WD, ADAM_STEP = 0.001, 0.9, 0.999, 1e-08, 0.01, 10
NEG_BIG = -1e30
MESH_ID = pl.DeviceIdType.MESH


def _tile(n, pref, mult):
    if n <= pref:
        return n
    t = (pref // mult) * mult
    while t >= mult:
        if n % t == 0:
            return t
        t -= mult
    raise ValueError(f"no tile for {n} (pref {pref}, mult {mult})")


def _params(*sem):
    return pltpu.CompilerParams(dimension_semantics=sem)


ANY = pl.BlockSpec(memory_space=pl.ANY)


def _peer(k):
    x, y, c = lax.axis_index("x"), lax.axis_index("y"), lax.axis_index("c")
    return (1 - x if k & 4 else x, 1 - y if k & 2 else y, 1 - c if k & 1 else c)


def _my_index():
    return 4 * lax.axis_index("x") + 2 * lax.axis_index("y") + lax.axis_index("c")


class _Ride:
    def __init__(self, srcs, scatter):
        self.srcs, self.scatter, self.n = list(srcs), scatter, len(srcs)
        self.two_level = not scatter
        self.in_specs = [ANY] * self.n
        self.out_specs = [ANY] * self.n
        self.out_shape = [jax.ShapeDtypeStruct((N_DEV,) + tuple(s.shape[1:] if scatter else s.shape), s.dtype)
                          for s in self.srcs]
        self.scratch = [pltpu.SemaphoreType.DMA((self.n, N_DEV - 1)), pltpu.SemaphoreType.DMA((self.n, N_DEV - 1)),
                        pltpu.SemaphoreType.DMA((self.n,))]

    def _copies(self, src, dst, sems):
        send_sems, recv_sems, local_sems = sems
        me = _my_index()
        local = [pltpu.make_async_copy(src[t].at[me] if self.scatter else src[t], dst[t].at[me], local_sems.at[t])
                 for t in range(self.n)]
        remote = {}
        for k in range(1, N_DEV):
            peer, pidx = _peer(k), me ^ k
            remote[k] = [pltpu.make_async_remote_copy(
                src_ref=src[t].at[pidx] if self.scatter else src[t], dst_ref=dst[t].at[me],
                send_sem=send_sems.at[t, k - 1], recv_sem=recv_sems.at[t, k - 1],
                device_id=peer, device_id_type=MESH_ID) for t in range(self.n)]
        return local, remote

    def _forwards(self, dst, sems):
        send_sems, recv_sems, _ = sems
        me = _my_index()
        return {k: [pltpu.make_async_remote_copy(
            src_ref=dst[t].at[me ^ k], dst_ref=dst[t].at[me ^ k],
            send_sem=send_sems.at[t, (k ^ 1) - 1], recv_sem=recv_sems.at[t, (k ^ 1) - 1],
            device_id=_peer(1), device_id_type=MESH_ID) for t in range(self.n)] for k in (2, 4, 6)}

    def start(self, src, dst, sems):
        local, remote = self._copies(src, dst, sems)
        for cp in local:
            cp.start()
        for k in (2, 4, 6, 1) if self.two_level else range(1, N_DEV):
            for cp in remote[k]:
                cp.start()

    def finish(self, src, dst, sems):
        local, remote = self._copies(src, dst, sems)
        if self.two_level:
            fwd = self._forwards(dst, sems)
            for k in (2, 4, 6):
                for t in range(self.n):
                    remote[k][t].wait_recv()
                    fwd[k][t].start()
            for k in (1, 3, 5, 7):
                for cp in remote[k]:
                    cp.wait_recv()
            for k in (1, 2, 4, 6):
                for cp in remote[k]:
                    cp.wait_send()
            for k in (2, 4, 6):
                for cp in fwd[k]:
                    cp.wait_send()
        else:
            for k in range(1, N_DEV):
                for cp in remote[k]:
                    cp.wait_send()
            for k in range(1, N_DEV):
                for cp in remote[k]:
                    cp.wait_recv()
        for lc in local:
            lc.wait()


def _call(body, *, name, grid, in_specs, out_specs, out_shape, args, sem, scratch=(), ride=None):
    in_specs, out_specs, out_shape, scratch = list(in_specs), list(out_specs), list(out_shape), list(scratch)
    if ride is None:
        return pl.pallas_call(body, name=name, grid=grid, in_specs=in_specs, out_specs=out_specs, out_shape=out_shape,
                              scratch_shapes=scratch, compiler_params=_params(*sem))(*args)
    n_in, n_out, n_sc, nr = len(in_specs), len(out_specs), len(scratch), ride.n

    def full(*refs):
        ins, csrc = refs[:n_in], refs[n_in:n_in + nr]
        o0 = n_in + nr
        outs, cdst = refs[o0:o0 + n_out], refs[o0 + n_out:o0 + n_out + nr]
        s0 = o0 + n_out + nr
        sc, csem = refs[s0:s0 + n_sc], refs[s0 + n_sc:]
        ids = [pl.program_id(ax) for ax in range(len(grid))]
        first = functools.reduce(jnp.logical_and, [i == 0 for i in ids])
        last = functools.reduce(jnp.logical_and, [i == g - 1 for i, g in zip(ids, grid)])

        @pl.when(first)
        def _():
            ride.start(csrc, cdst, csem)

        body(*ins, *outs, *sc)

        @pl.when(last)
        def _():
            ride.finish(csrc, cdst, csem)

    return pl.pallas_call(
        full, name=name, grid=grid, in_specs=in_specs + ride.in_specs, out_specs=out_specs + ride.out_specs,
        out_shape=out_shape + ride.out_shape, scratch_shapes=scratch + ride.scratch,
        compiler_params=_params(*(["arbitrary"] * len(grid))),
    )(*args, *ride.srcs)


def _sigmoid(x):
    return 1.0 / (1.0 + jnp.exp(-x))


def _softplus(x):
    return jnp.maximum(x, 0.0) + jnp.log(1.0 + jnp.exp(-jnp.abs(x)))


def _dot(a, b, ta=False, tb=False):
    dn = (((0 if ta else 1,), (1 if tb else 0,)), ((), ()))
    return lax.dot_general(a.astype(BF16), b.astype(BF16), dn, preferred_element_type=F32)


def _mm(a, b, *, name, ta=False, tb=False, out_dtype=F32, tm=1024, tn=1024, tk=2048, residual=None, ride=None):
    m = a.shape[1] if ta else a.shape[0]
    k = a.shape[0] if ta else a.shape[1]
    n = b.shape[0] if tb else b.shape[1]
    assert k == (b.shape[1] if tb else b.shape[0])
    tm, tn = _tile(m, tm, LANE if ta else 8), _tile(n, tn, LANE)
    tk = _tile(k, tk, LANE if not (ta and tb) else LANE)
    nk = k // tk
    has_res = residual is not None

    def body(*refs):
        a_ref, b_ref = refs[0], refs[1]
        r_ref = refs[2] if has_res else None
        o_ref = refs[3] if has_res else refs[2]
        part = _dot(a_ref[...], b_ref[...], ta, tb)

        def finish(acc):
            if has_res:
                acc = acc + r_ref[...]
            o_ref[...] = acc.astype(o_ref.dtype)

        if nk == 1:
            finish(part)
        else:
            acc_ref = refs[-1]
            kk = pl.program_id(2)

            @pl.when(kk == 0)
            def _():
                acc_ref[...] = part

            @pl.when(kk > 0)
            def _():
                acc_ref[...] += part

            @pl.when(kk == nk - 1)
            def _():
                finish(acc_ref[...])

    a_spec = pl.BlockSpec((tk, tm), lambda i, j, kk: (kk, i)) if ta else pl.BlockSpec((tm, tk), lambda i, j, kk: (i, kk))
    b_spec = pl.BlockSpec((tn, tk), lambda i, j, kk: (j, kk)) if tb else pl.BlockSpec((tk, tn), lambda i, j, kk: (kk, j))
    o_spec = pl.BlockSpec((tm, tn), lambda i, j, kk: (i, j))
    in_specs = [a_spec, b_spec] + ([o_spec] if has_res else [])
    args = (a, b) + ((residual,) if has_res else ())
    res = _call(body, name=name, grid=(m // tm, n // tn, nk), in_specs=in_specs, out_specs=[o_spec],
                out_shape=[jax.ShapeDtypeStruct((m, n), out_dtype)], args=args,
                scratch=[pltpu.VMEM((tm, tn), F32)] if nk > 1 else [], sem=("parallel", "parallel", "arbitrary"), ride=ride)
    return res[0] if ride is None else res


def _gate_up(h, wg, wu, *, name, tm=1024, ride=None):
    m, k = h.shape
    ns, _, c = wg.shape
    tm = _tile(m, tm, 16)

    def body(h_ref, wg_ref, wu_ref, g_ref, u_ref, a_ref):
        hb = h_ref[...]
        g = _dot(hb, wg_ref[...])
        u = _dot(hb, wu_ref[...])
        g_ref[...] = g
        u_ref[...] = u
        a_ref[...] = (g * _sigmoid(g) * u).astype(a_ref.dtype)

    o_spec = pl.BlockSpec((None, tm, c), lambda i, s: (s, i, 0))
    w_spec = pl.BlockSpec((None, k, c), lambda i, s: (s, 0, 0))
    return _call(body, name=name, grid=(m // tm, ns),
                 in_specs=[pl.BlockSpec((tm, k), lambda i, s: (i, 0)), w_spec, w_spec], out_specs=[o_spec, o_spec, o_spec],
                 out_shape=[jax.ShapeDtypeStruct((ns, m, c), F32), jax.ShapeDtypeStruct((ns, m, c), F32),
                            jax.ShapeDtypeStruct((ns, m, c), BF16)],
                 args=(h, wg, wu), sem=("parallel", "parallel"), ride=ride)


def _gate_up_bwd(dh, wd, g, u, *, name, tm=1024):
    m, k = dh.shape
    ns, c, _ = wd.shape
    tm = _tile(m, tm, 16)

    def body(dh_ref, wd_ref, g_ref, u_ref, dg_ref, du_ref):
        da = _dot(dh_ref[...], wd_ref[...], tb=True)
        g = g_ref[...]
        s = _sigmoid(g)
        dg_ref[...] = (da * u_ref[...] * (s * (1.0 + g * (1.0 - s)))).astype(dg_ref.dtype)
        du_ref[...] = (da * (g * s)).astype(du_ref.dtype)

    o_spec = pl.BlockSpec((None, tm, c), lambda i, s: (s, i, 0))
    return _call(body, name=name, grid=(m // tm, ns),
                 in_specs=[pl.BlockSpec((tm, k), lambda i, s: (i, 0)), pl.BlockSpec((None, c, k), lambda i, s: (s, 0, 0)),
                           o_spec, o_spec],
                 out_specs=[o_spec, o_spec],
                 out_shape=[jax.ShapeDtypeStruct((ns, m, c), BF16), jax.ShapeDtypeStruct((ns, m, c), BF16)],
                 args=(dh, wd, g, u), sem=("parallel", "parallel"))


def _mm_shards(pairs, *, name, tb=False, residual=None, tm=1024, tn=1024, ride=None):
    ns, m, c = pairs[0][0].shape
    n = pairs[0][1].shape[1] if tb else pairs[0][1].shape[2]
    tm, tn = _tile(m, tm, 16), _tile(n, tn, LANE)
    npair = len(pairs)
    has_res = residual is not None

    def body(*refs):
        o_ref, acc_ref = refs[-2], refs[-1]
        s = pl.program_id(2)
        part = _dot(refs[0][...], refs[1][...], tb=tb)
        for p in range(1, npair):
            part = part + _dot(refs[2 * p][...], refs[2 * p + 1][...], tb=tb)

        @pl.when(s == 0)
        def _():
            acc_ref[...] = part

        @pl.when(s > 0)
        def _():
            acc_ref[...] += part

        @pl.when(s == ns - 1)
        def _():
            acc = acc_ref[...]
            if has_res:
                acc = acc + refs[2 * npair][...]
            o_ref[...] = acc

    a_spec = pl.BlockSpec((None, tm, c), lambda i, j, s: (s, i, 0))
    b_spec = (pl.BlockSpec((None, tn, c), lambda i, j, s: (s, j, 0)) if tb
              else pl.BlockSpec((None, c, tn), lambda i, j, s: (s, 0, j)))
    o_spec = pl.BlockSpec((tm, tn), lambda i, j, s: (i, j))
    args = [t for pr in pairs for t in pr] + ([residual] if has_res else [])
    res = _call(body, name=name, grid=(m // tm, n // tn, ns),
                in_specs=[a_spec, b_spec] * npair + ([o_spec] if has_res else []), out_specs=[o_spec],
                out_shape=[jax.ShapeDtypeStruct((m, n), F32)], args=args, scratch=[pltpu.VMEM((tm, tn), F32)],
                sem=("parallel", "parallel", "arbitrary"), ride=ride)
    return res[0] if ride is None else res


def _grad_shards(a, b, *, name, tk=1024):
    a3 = a.ndim == 3
    ns, l, c = a.shape if a3 else b.shape
    d = (b if a3 else a).shape[1]
    tk = _tile(l, tk, 16)
    nk = l // tk

    def body(a_ref, b_ref, o_ref, acc_ref):
        kk = pl.program_id(1)
        part = _dot(a_ref[...], b_ref[...], ta=True)

        @pl.when(kk == 0)
        def _():
            acc_ref[...] = part

        @pl.when(kk > 0)
        def _():
            acc_ref[...] += part

        @pl.when(kk == nk - 1)
        def _():
            o_ref[...] = acc_ref[...].astype(o_ref.dtype)

    s3 = pl.BlockSpec((None, tk, c), lambda s, kk: (s, kk, 0))
    s2 = pl.BlockSpec((tk, d), lambda s, kk: (kk, 0))
    out = (c, d) if a3 else (d, c)
    return pl.pallas_call(
        body, name=name, grid=(ns, nk), in_specs=[s3, s2] if a3 else [s2, s3],
        out_specs=pl.BlockSpec((None,) + out, lambda s, kk: (s, 0, 0)),
        out_shape=jax.ShapeDtypeStruct((ns,) + out, BF16), scratch_shapes=[pltpu.VMEM(out, F32)],
        compiler_params=_params("parallel", "arbitrary"),
    )(a, b)


def _rms_fwd(x, w, *, name, tl=512):
    l, d = x.shape
    tl = _tile(l, tl, 16)

    def body(x_ref, w_ref, o_ref):
        xv = x_ref[...]
        r = lax.rsqrt(jnp.mean(xv * xv, axis=-1, keepdims=True) + NORM_EPS)
        o_ref[...] = (xv * r * w_ref[...]).astype(o_ref.dtype)

    row = pl.BlockSpec((tl, d), lambda i: (i, 0))
    return pl.pallas_call(
        body, name=name, grid=(l // tl,), in_specs=[row, pl.BlockSpec((1, d), lambda i: (0, 0))], out_specs=row,
        out_shape=jax.ShapeDtypeStruct((l, d), BF16), compiler_params=_params("parallel"),
    )(x, w)


def _rms_bwd(x, w, dy, add, *, name, tl=256):
    l, d = x.shape
    tl = _tile(l, tl, 16)

    def body(x_ref, w_ref, dy_ref, add_ref, dx_ref, dxb_ref, dw_ref):
        i = pl.program_id(0)
        xv = x_ref[...]
        r = lax.rsqrt(jnp.mean(xv * xv, axis=-1, keepdims=True) + NORM_EPS)
        xh = xv * r
        dyv = dy_ref[...].astype(F32)
        dxh = dyv * w_ref[...]
        dx = r * (dxh - xh * jnp.mean(dxh * xh, axis=-1, keepdims=True)) + add_ref[...]
        dx_ref[...] = dx
        dxb_ref[...] = dx.astype(BF16)
        part = jnp.sum(dyv * xh, axis=0, keepdims=True)

        @pl.when(i == 0)
        def _():
            dw_ref[...] = part

        @pl.when(i > 0)
        def _():
            dw_ref[...] += part

    row = pl.BlockSpec((tl, d), lambda i: (i, 0))
    vec = pl.BlockSpec((1, d), lambda i: (0, 0))
    return pl.pallas_call(
        body, name=name, grid=(l // tl,), in_specs=[row, vec, row, row], out_specs=[row, row, vec],
        out_shape=[jax.ShapeDtypeStruct((l, d), F32), jax.ShapeDtypeStruct((l, d), BF16), jax.ShapeDtypeStruct((1, d), F32)],
        compiler_params=_params("arbitrary"),
    )(x, w, dy, add)


def _loss_head(h, w, target, *, name, tl=256):
    l, d = h.shape
    tl = _tile(l, tl, 16)

    def body(x_ref, w_ref, t_ref, loss_ref, dx_ref, dxb_ref, dw_ref):
        i = pl.program_id(0)
        xv = x_ref[...]
        r = lax.rsqrt(jnp.mean(xv * xv, axis=-1, keepdims=True) + NORM_EPS)
        xh = xv * r
        wv = w_ref[...]
        diff = xh * wv - t_ref[...]
        lpart = 0.5 * jnp.sum(jnp.mean(diff * diff, axis=-1, keepdims=True), axis=0, keepdims=True)
        dyv = diff * (1.0 / d)
        dxh = dyv * wv
        dx = r * (dxh - xh * jnp.mean(dxh * xh, axis=-1, keepdims=True))
        dx_ref[...] = dx
        dxb_ref[...] = dx.astype(BF16)
        part = jnp.sum(dyv * xh, axis=0, keepdims=True)

        @pl.when(i == 0)
        def _():
            dw_ref[...] = part
            loss_ref[...] = lpart

        @pl.when(i > 0)
        def _():
            dw_ref[...] += part
            loss_ref[...] += lpart

    row = pl.BlockSpec((tl, d), lambda i: (i, 0))
    vec = pl.BlockSpec((1, d), lambda i: (0, 0))
    one = pl.BlockSpec((1, 1), lambda i: (0, 0))
    return pl.pallas_call(
        body, name=name, grid=(l // tl,), in_specs=[row, vec, row], out_specs=[one, row, row, vec],
        out_shape=[jax.ShapeDtypeStruct((1, 1), F32), jax.ShapeDtypeStruct((l, d), F32),
                   jax.ShapeDtypeStruct((l, d), BF16), jax.ShapeDtypeStruct((1, d), F32)],
        compiler_params=_params("arbitrary"),
    )(h, w, target)


def _conv_pre(ext, w_ref):
    acc = ext * w_ref[CONV_K - 1:CONV_K, :]
    for k in range(CONV_K - 1):
        acc = acc + pltpu.roll(ext, CONV_K - 1 - k, axis=0) * w_ref[k:k + 1, :]
    return acc


def _conv_fwd(proj, col0, cw, cb, *, name, tl=512, tc=512):
    l = proj.shape[0]
    c = cw.shape[1]
    tl, tc = _tile(l, tl, 16), _tile(c, tc, LANE)
    assert col0 % tc == 0
    off, hb = col0 // tc, tl // CONV_HALO

    def body(x_ref, halo_ref, w_ref, b_ref, o_ref):
        i = pl.program_id(0)
        halo = jnp.where(i > 0, halo_ref[...], 0.0)
        ext = jnp.concatenate([halo, x_ref[...]], axis=0)
        pre = _conv_pre(ext, w_ref)[CONV_HALO:] + b_ref[...]
        o_ref[...] = pre * _sigmoid(pre)

    return pl.pallas_call(
        body, name=name, grid=(l // tl, c // tc),
        in_specs=[pl.BlockSpec((tl, tc), lambda i, j: (i, off + j)),
                  pl.BlockSpec((CONV_HALO, tc), lambda i, j: (jnp.maximum(i * hb - 1, 0), off + j)),
                  pl.BlockSpec((CONV_K, tc), lambda i, j: (0, j)), pl.BlockSpec((1, tc), lambda i, j: (0, j))],
        out_specs=pl.BlockSpec((tl, tc), lambda i, j: (i, j)),
        out_shape=jax.ShapeDtypeStruct((l, c), F32), compiler_params=_params("parallel", "parallel"),
    )(proj, proj, cw, cb)


def _conv_bwd(proj, col0, cw, cb, dout, dproj, *, name, tl=512, tc=512):
    l = proj.shape[0]
    c = dout.shape[1]
    tl, tc = _tile(l, tl, 16), _tile(c, tc, LANE)
    assert col0 % tc == 0
    off, hb, nt = col0 // tc, tl // CONV_HALO, l // tl
    n_ext = tl + CONV_HALO

    def body(x_ref, prev_ref, nxt_ref, d_ref, dn_ref, w_ref, b_ref, dproj_in, dx_ref, dw_ref, db_ref):
        del dproj_in
        i = pl.program_id(1)
        last = i == nt - 1
        halo = jnp.where(i > 0, prev_ref[...], 0.0)
        ext = jnp.concatenate([halo, x_ref[...], nxt_ref[...]], axis=0)
        pre = _conv_pre(ext, w_ref)[CONV_HALO:] + b_ref[...]
        dext = jnp.concatenate([d_ref[...], jnp.where(last, 0.0, dn_ref[...])], axis=0)
        s = _sigmoid(pre)
        dpre = dext * (s * (1.0 + pre * (1.0 - s)))
        dx = dpre * w_ref[CONV_K - 1:CONV_K, :]
        for k in range(CONV_K - 1):
            dx = dx + pltpu.roll(dpre, n_ext - (CONV_K - 1 - k), axis=0) * w_ref[k:k + 1, :]
        dx_ref[...] = dx[:tl].astype(dx_ref.dtype)
        dp = dpre[:tl]
        rows = [jnp.sum(dp * pltpu.roll(ext, CONV_K - 1 - k, axis=0)[CONV_HALO:CONV_HALO + tl], axis=0, keepdims=True)
                for k in range(CONV_K - 1)]
        rows.append(jnp.sum(dp * ext[CONV_HALO:CONV_HALO + tl], axis=0, keepdims=True))
        dwp = jnp.concatenate(rows, axis=0)
        dbp = jnp.sum(dp, axis=0, keepdims=True)

        @pl.when(i == 0)
        def _():
            dw_ref[...] = dwp
            db_ref[...] = dbp

        @pl.when(i > 0)
        def _():
            dw_ref[...] += dwp
            db_ref[...] += dbp

    cur = lambda j, i: (i, j)
    nxt = lambda j, i: (jnp.minimum((i + 1) * hb, nt * hb - 1), j)
    return pl.pallas_call(
        body, name=name, grid=(c // tc, nt),
        in_specs=[pl.BlockSpec((tl, tc), lambda j, i: (i, off + j)),
                  pl.BlockSpec((CONV_HALO, tc), lambda j, i: (jnp.maximum(i * hb - 1, 0), off + j)),
                  pl.BlockSpec((CONV_HALO, tc), lambda j, i: (jnp.minimum((i + 1) * hb, nt * hb - 1), off + j)),
                  pl.BlockSpec((tl, tc), cur), pl.BlockSpec((CONV_HALO, tc), nxt),
                  pl.BlockSpec((CONV_K, tc), lambda j, i: (0, j)), pl.BlockSpec((1, tc), lambda j, i: (0, j)), ANY],
        out_specs=[pl.BlockSpec((tl, tc), lambda j, i: (i, off + j)), pl.BlockSpec((CONV_K, tc), lambda j, i: (0, j)),
                   pl.BlockSpec((1, tc), lambda j, i: (0, j))],
        out_shape=[jax.ShapeDtypeStruct(dproj.shape, dproj.dtype), jax.ShapeDtypeStruct((CONV_K, c), F32),
                   jax.ShapeDtypeStruct((1, c), F32)],
        input_output_aliases={7: 0}, compiler_params=_params("parallel", "arbitrary"),
    )(proj, proj, proj, dout, dout, cw, cb, dproj)


def _pool_fwd(proj, col0, d_pool, *, name, tl=256):
    l = proj.shape[0]
    tl = _tile(l, tl, POOL_HALO)
    assert col0 % d_pool == 0
    off, hb, pg = col0 // d_pool, tl // POOL_HALO, d_pool // GROUPS

    def body(u_ref, halo_ref, o_ref):
        i = pl.program_id(0)
        halo = jnp.where(i > 0, halo_ref[...], 0.0)
        ext = jnp.concatenate([halo, u_ref[...]], axis=0)
        t = i * tl + lax.broadcasted_iota(jnp.int32, (tl, 1), 0)
        s, width = ext, 1
        for gi, win in enumerate(POOL_WINDOWS):
            while width < win:
                s = s + pltpu.roll(s, width, axis=0)
                width *= 2
            cnt = jnp.minimum(t + 1, win).astype(F32)
            sl = slice(gi * pg, (gi + 1) * pg)
            o_ref[:, sl] = (s[POOL_HALO:, sl] / cnt - ext[POOL_HALO:, sl]).astype(o_ref.dtype)

    return pl.pallas_call(
        body, name=name, grid=(l // tl,),
        in_specs=[pl.BlockSpec((tl, d_pool), lambda i: (i, off)),
                  pl.BlockSpec((POOL_HALO, d_pool), lambda i: (jnp.maximum(i * hb - 1, 0), off))],
        out_specs=pl.BlockSpec((tl, d_pool), lambda i: (i, 0)),
        out_shape=jax.ShapeDtypeStruct((l, d_pool), BF16), compiler_params=_params("parallel"),
    )(proj, proj)


def _pool_bwd(dp, dproj, col0, *, name, tl=256):
    l, d_pool = dp.shape
    tl = _tile(l, tl, POOL_HALO)
    assert col0 % d_pool == 0
    off, hb, nt, pg = col0 // d_pool, tl // POOL_HALO, l // tl, d_pool // GROUPS
    n_ext = tl + POOL_HALO

    def body(d_ref, nxt_ref, dproj_in, o_ref):
        del dproj_in
        i = pl.program_id(0)
        ext = jnp.concatenate([d_ref[...], jnp.where(i == nt - 1, 0.0, nxt_ref[...])], axis=0)
        t = i * tl + lax.broadcasted_iota(jnp.int32, (n_ext, 1), 0)
        for gi, win in enumerate(POOL_WINDOWS):
            sl = slice(gi * pg, (gi + 1) * pg)
            dg = ext[:, sl]
            s = dg / jnp.minimum(t + 1, win).astype(F32)
            width = 1
            while width < win:
                s = s + pltpu.roll(s, n_ext - width, axis=0)
                width *= 2
            o_ref[:, sl] = (s[:tl] - dg[:tl]).astype(o_ref.dtype)

    return pl.pallas_call(
        body, name=name, grid=(nt,),
        in_specs=[pl.BlockSpec((tl, d_pool), lambda i: (i, 0)),
                  pl.BlockSpec((POOL_HALO, d_pool), lambda i: (jnp.minimum((i + 1) * hb, nt * hb - 1), 0)), ANY],
        out_specs=pl.BlockSpec((tl, d_pool), lambda i: (i, off)),
        out_shape=jax.ShapeDtypeStruct(dproj.shape, dproj.dtype), input_output_aliases={2: 0},
        compiler_params=_params("parallel"),
    )(dp, dp, dproj)


def _put_cols(src, dst, col0, *, name, tl=1024):
    l, c = src.shape
    tl = _tile(l, tl, 16)
    assert col0 % c == 0 and c % LANE == 0

    def body(s_ref, dst_in, o_ref):
        del dst_in
        o_ref[...] = s_ref[...].astype(o_ref.dtype)

    return pl.pallas_call(
        body, name=name, grid=(l // tl,), in_specs=[pl.BlockSpec((tl, c), lambda i: (i, 0)), ANY],
        out_specs=pl.BlockSpec((tl, c), lambda i: (i, col0 // c)), out_shape=jax.ShapeDtypeStruct(dst.shape, dst.dtype),
        input_output_aliases={1: 0}, compiler_params=_params("parallel"),
    )(src, dst)


def _pool_mix_fwd(pooled, pw, scale, mixed, col0, *, name, tm=1024):
    l, d_pool = pooled.shape
    pg = d_pool // GROUPS
    tm = _tile(l, tm, 16)
    assert col0 % pg == 0
    off = col0 // pg

    def body(a_ref, w_ref, s_ref, mixed_in, raw_ref, mixed_ref):
        del mixed_in
        raw = _dot(a_ref[...], w_ref[...])
        raw_ref[...] = raw
        mixed_ref[...] = (raw * s_ref[...]).astype(mixed_ref.dtype)

    blk = pl.BlockSpec((tm, pg), lambda i, g: (i, g))
    return pl.pallas_call(
        body, name=name, grid=(l // tm, GROUPS),
        in_specs=[blk, pl.BlockSpec((None, pg, pg), lambda i, g: (g, 0, 0)), pl.BlockSpec((1, pg), lambda i, g: (0, g)),
                  pl.BlockSpec(memory_space=pl.ANY)],
        out_specs=[blk, pl.BlockSpec((tm, pg), lambda i, g: (i, off + g))],
        out_shape=[jax.ShapeDtypeStruct((l, d_pool), F32), jax.ShapeDtypeStruct(mixed.shape, mixed.dtype)],
        input_output_aliases={3: 1}, compiler_params=_params("parallel", "parallel"),
    )(pooled, pw, scale, mixed)


def _pool_scale_bwd(dmixed, col0, raw, scale, *, name, tl=512):
    l, d_pool = raw.shape
    tl, tc = _tile(l, tl, 16), _tile(d_pool, 512, LANE)
    assert col0 % tc == 0
    off = col0 // tc

    def body(d_ref, raw_ref, s_ref, o_ref, ds_ref):
        i = pl.program_id(1)
        dv = d_ref[...]
        o_ref[...] = (dv * s_ref[...]).astype(o_ref.dtype)
        part = jnp.sum(dv * raw_ref[...], axis=0, keepdims=True)

        @pl.when(i == 0)
        def _():
            ds_ref[...] = part

        @pl.when(i > 0)
        def _():
            ds_ref[...] += part

    blk = pl.BlockSpec((tl, tc), lambda j, i: (i, j))
    vec = pl.BlockSpec((1, tc), lambda j, i: (0, j))
    return pl.pallas_call(
        body, name=name, grid=(d_pool // tc, l // tl),
        in_specs=[pl.BlockSpec((tl, tc), lambda j, i: (i, off + j)), blk, vec], out_specs=[blk, vec],
        out_shape=[jax.ShapeDtypeStruct((l, d_pool), BF16), jax.ShapeDtypeStruct((1, d_pool), F32)],
        compiler_params=_params("parallel", "arbitrary"),
    )(dmixed, raw, scale)


def _group_mm_nt(a, w, *, name, tm=1024):
    l, d_pool = a.shape
    pg = d_pool // GROUPS
    tm = _tile(l, tm, 16)

    def body(a_ref, w_ref, o_ref):
        o_ref[...] = _dot(a_ref[...], w_ref[...], tb=True)

    blk = pl.BlockSpec((tm, pg), lambda i, g: (i, g))
    return pl.pallas_call(
        body, name=name, grid=(l // tm, GROUPS),
        in_specs=[blk, pl.BlockSpec((None, pg, pg), lambda i, g: (g, 0, 0))], out_specs=blk,
        out_shape=jax.ShapeDtypeStruct((l, d_pool), F32), compiler_params=_params("parallel", "parallel"),
    )(a, w)


def _group_mm_tn(a, b, *, name, tk=1024):
    l, d_pool = a.shape
    pg = d_pool // GROUPS
    tk = _tile(l, tk, 16)

    def body(a_ref, b_ref, o_ref):
        kk = pl.program_id(1)
        part = _dot(a_ref[...], b_ref[...], ta=True)

        @pl.when(kk == 0)
        def _():
            o_ref[...] = part

        @pl.when(kk > 0)
        def _():
            o_ref[...] += part

    blk = pl.BlockSpec((tk, pg), lambda g, kk: (kk, g))
    return pl.pallas_call(
        body, name=name, grid=(GROUPS, l // tk), in_specs=[blk, blk],
        out_specs=pl.BlockSpec((None, pg, pg), lambda g, kk: (g, 0, 0)),
        out_shape=jax.ShapeDtypeStruct((GROUPS, pg, pg), F32), compiler_params=_params("parallel", "arbitrary"),
    )(a, b)


def _gnorm_fwd(y, proj, wn, d_mix, *, name, tl=512):
    l, d_ssm = y.shape
    gw = d_ssm // GROUPS
    tl = _tile(l, tl, 16)

    def body(y_ref, z_ref, w_ref, o_ref):
        z = z_ref[...]
        gv = y_ref[...] * (z * _sigmoid(z))
        r = lax.rsqrt(jnp.mean(gv * gv, axis=-1, keepdims=True) + NORM_EPS)
        o_ref[...] = (gv * r * w_ref[...]).astype(o_ref.dtype)

    blk = pl.BlockSpec((tl, gw), lambda i, g: (i, g))
    return pl.pallas_call(
        body, name=name, grid=(l // tl, GROUPS), in_specs=[blk, blk, pl.BlockSpec((1, gw), lambda i, g: (0, g))],
        out_specs=blk, out_shape=jax.ShapeDtypeStruct((l, d_mix), BF16), compiler_params=_params("parallel", "parallel"),
    )(y, proj, wn)


def _gnorm_bwd(dmixed, y, proj, wn, *, name, tl=512):
    l, d_ssm = y.shape
    gw = d_ssm // GROUPS
    tl = _tile(l, tl, 16)

    def body(d_ref, y_ref, z_ref, w_ref, dy_ref, dz_ref, dw_ref):
        i = pl.program_id(1)
        z, yv, dv = z_ref[...], y_ref[...], d_ref[...]
        s = _sigmoid(z)
        sz = z * s
        gv = yv * sz
        r = lax.rsqrt(jnp.mean(gv * gv, axis=-1, keepdims=True) + NORM_EPS)
        gh = gv * r
        dgh = dv * w_ref[...]
        dg = r * (dgh - gh * jnp.mean(dgh * gh, axis=-1, keepdims=True))
        dy_ref[...] = dg * sz
        dz_ref[...] = (dg * yv * (s * (1.0 + z * (1.0 - s)))).astype(dz_ref.dtype)
        part = jnp.sum(dv * gh, axis=0, keepdims=True)

        @pl.when(i == 0)
        def _():
            dw_ref[...] = part

        @pl.when(i > 0)
        def _():
            dw_ref[...] += part

    blk = pl.BlockSpec((tl, gw), lambda g, i: (i, g))
    vec = pl.BlockSpec((1, gw), lambda g, i: (0, g))
    return pl.pallas_call(
        body, name=name, grid=(GROUPS, l // tl), in_specs=[blk, blk, blk, vec], out_specs=[blk, blk, vec],
        out_shape=[jax.ShapeDtypeStruct((l, d_ssm), F32), jax.ShapeDtypeStruct(proj.shape, BF16),
                   jax.ShapeDtypeStruct((1, d_ssm), F32)],
        compiler_params=_params("parallel", "arbitrary"),
    )(dmixed, y, proj, wn)


def _split3(v):
    hi = v.astype(BF16)
    r1 = v - hi.astype(F32)
    mid = r1.astype(BF16)
    lo = (r1 - mid.astype(F32)).astype(BF16)
    return hi, mid, lo


def _tri_dot(tri, v):
    hi, mid, lo = _split3(v)
    d = lambda p: jnp.dot(tri, p, preferred_element_type=F32)
    return d(hi) + d(mid) + d(lo)


def _dot_tri(v, tri):
    hi, mid, lo = _split3(v)
    d = lambda p: jnp.dot(p, tri, preferred_element_type=F32)
    return d(hi) + d(mid) + d(lo)


def _split2(v):
    hi = v.astype(BF16)
    return hi, (v - hi.astype(F32)).astype(BF16)


def _scan_specs(hpg, nc, order):
    gw, q = hpg * HEAD_DIM, CHUNK
    b_off, c_off = (GROUPS * gw) // STATE, (GROUPS * gw) // STATE + GROUPS
    xs = pl.BlockSpec((q, gw), lambda g, c: (order(c), g))
    bm = pl.BlockSpec((q, STATE), lambda g, c: (order(c), b_off + g))
    cm = pl.BlockSpec((q, STATE), lambda g, c: (order(c), c_off + g))
    dtt = pl.BlockSpec((None, hpg, q), lambda g, c: (g, 0, order(c)))
    vec = pl.BlockSpec((1, gw), lambda g, c: (0, g))
    hcol = pl.BlockSpec((None, hpg, 1), lambda g, c: (g, 0, 0))
    st = pl.BlockSpec((None, None, STATE, gw), lambda g, c: (g, order(c), 0, 0))
    return xs, bm, cm, dtt, vec, hcol, st


def _scan_common(dtx_ref, dtrt_ref, biasx_ref, biast_ref, alogx_ref, alogt_ref):
    q = CHUNK
    dt = _softplus(dtx_ref[...] + biasx_ref[...])
    a = -jnp.exp(alogx_ref[...])
    dtt = _softplus(dtrt_ref[...] + biast_ref[...])
    at = -jnp.exp(alogt_ref[...])
    row = lax.broadcasted_iota(jnp.int32, (q, q), 0)
    col = lax.broadcasted_iota(jnp.int32, (q, q), 1)
    lower = (row >= col).astype(BF16)
    upper = (row <= col).astype(BF16)
    acum = _tri_dot(lower, dt * a)
    acum_t = _dot_tri(dtt * at, upper)
    return dt, a, acum, acum_t, row, col, upper


def _scan_fwd(xbc, dtx, dtrt, biasx, biast, alogx, alogt, dskx, *, name, ride=None):
    l = xbc.shape[0]
    hpg = dtrt.shape[1]
    gw, q, nc = hpg * HEAD_DIM, CHUNK, l // CHUNK
    assert hpg % 2 == 0
    xs_s, bm_s, cm_s, dtt_s, vec, hcol, st_s = _scan_specs(hpg, nc, lambda c: c)

    def body(xs_ref, b_ref, c_ref, dtx_ref, dtrt_ref, biasx_ref, biast_ref, alogx_ref, alogt_ref, dsk_ref,
             y_ref, st_ref, h_ref):
        @pl.when(pl.program_id(1) == 0)
        def _():
            h_ref[...] = jnp.zeros_like(h_ref)

        dt, _, acum, acum_t, row, col, _ = _scan_common(dtx_ref, dtrt_ref, biasx_ref, biast_ref, alogx_ref, alogt_ref)
        causal = row >= col
        xs = xs_ref[...]
        xdt = xs * dt
        xdtb = xdt.astype(BF16)
        bmat, cmat = b_ref[...], c_ref[...]
        cb = _dot(cmat, bmat, tb=True)
        h_all = h_ref[...]
        st_ref[...] = h_all
        rest = _dot(cmat, h_all) * jnp.exp(acum) + dsk_ref[...] * xs
        first = lax.broadcasted_iota(jnp.int32, (q, 2 * HEAD_DIM), 1) < HEAD_DIM
        for jp in range(hpg // 2):
            pair = slice(2 * jp * HEAD_DIM, (2 * jp + 2) * HEAD_DIM)
            xp = xdtb[:, pair]
            ys = []
            for j in (2 * jp, 2 * jp + 1):
                a_col, a_row = acum[:, j * HEAD_DIM:j * HEAD_DIM + 1], acum_t[j:j + 1, :]
                lm = jnp.exp(jnp.where(causal, a_col - a_row, NEG_BIG))
                ys.append(_dot(cb * lm, xp))
            y_ref[:, pair] = jnp.where(first, ys[0], ys[1]) + rest[:, pair]
        a_last = acum[q - 1:q, :]
        h_ref[...] = jnp.exp(a_last) * h_all + _dot(bmat, xdt * jnp.exp(a_last - acum), ta=True)

    return _call(
        body, name=name, grid=(GROUPS, nc),
        in_specs=[xs_s, bm_s, cm_s, xs_s, dtt_s, vec, hcol, vec, hcol, vec], out_specs=[xs_s, st_s],
        out_shape=[jax.ShapeDtypeStruct((l, GROUPS * gw), F32), jax.ShapeDtypeStruct((GROUPS, nc, STATE, gw), F32)],
        scratch=[pltpu.VMEM((STATE, gw), F32)], sem=("parallel", "arbitrary"),
        args=(xbc, xbc, xbc, dtx, dtrt, biasx, biast, alogx, alogt, dskx), ride=ride)


def _scan_bwd(xbc, dtx, dtrt, biasx, biast, alogx, alogt, dskx, states, dy, *, name, ride=None):
    l = xbc.shape[0]
    hpg = dtrt.shape[1]
    gw, q, nc = hpg * HEAD_DIM, CHUNK, l // CHUNK
    rev = lambda c: nc - 1 - c
    xs_s, bm_s, cm_s, dtt_s, vec, hcol, st_s = _scan_specs(hpg, nc, rev)
    bc_s = pl.BlockSpec((q, STATE), lambda g, c: (rev(c), g))

    def body(xs_ref, b_ref, c_ref, dtx_ref, dtrt_ref, biasx_ref, biast_ref, alogx_ref, alogt_ref, dsk_ref, st_ref, dy_ref,
             dxs_ref, db_ref, dc_ref, ddt_ref, dbias_ref, dalog_ref, ddsk_ref, dh_ref):
        step = pl.program_id(1)

        @pl.when(step == 0)
        def _():
            dh_ref[...] = jnp.zeros_like(dh_ref)

        dt, a, acum, acum_t, row, col, upper = _scan_common(dtx_ref, dtrt_ref, biasx_ref, biast_ref, alogx_ref, alogt_ref)
        causal, anti, strict = row >= col, row <= col, row > col
        xs, dyv = xs_ref[...], dy_ref[...]
        xdt = xs * dt
        xdtb, dyb = xdt.astype(BF16), dyv.astype(BF16)
        bmat, cmat = b_ref[...], c_ref[...]
        cb = _dot(cmat, bmat, tb=True)
        cbt = _dot(bmat, cmat, tb=True)
        h_all, dh = st_ref[...], dh_ref[...]
        a_last = acum[q - 1:q, :]
        ea, ee, gam = jnp.exp(acum), jnp.exp(a_last - acum), jnp.exp(a_last)
        zc = _dot(cmat, h_all)
        bdh = _dot(bmat, dh)
        dz = dyv * ea
        xe = xdt * ee
        dc_state = _dot(dz, h_all, tb=True)
        db_state = _dot(xe, dh, tb=True)
        dh_ref[...] = gam * dh + _dot(cmat, dz, ta=True)
        lane = lax.broadcasted_iota(jnp.int32, (q, 2 * HEAD_DIM), 1)
        first = lane < HEAD_DIM
        ones = jnp.ones((q // 2, 2 * HEAD_DIM), BF16)
        dcb = jnp.zeros((q, q), F32)
        dxdt_parts, da_parts = [], []
        for jp in range(hpg // 2):
            pair = slice(2 * jp * HEAD_DIM, (2 * jp + 2) * HEAD_DIM)
            xp, dyp = xdtb[:, pair], dyb[:, pair]
            dxs_r, das_r = [], []
            for r, j in enumerate((2 * jp, 2 * jp + 1)):
                seg = acum[:, j * HEAD_DIM:j * HEAD_DIM + 1] - acum_t[j:j + 1, :]
                lm = jnp.exp(jnp.where(causal, seg, NEG_BIG))
                lmt = jnp.exp(jnp.where(anti, -seg, NEG_BIG))
                mm, mmt = cb * lm, cbt * lmt
                keep = first if r == 0 else jnp.logical_not(first)
                dyk = jnp.where(keep, dyp, 0)
                dxs_r.append(_dot(mmt, dyp))
                dm = _dot(dyk, xp, tb=True)
                dcb = dcb + dm * lm
                corner = jnp.dot(upper, (dm * mm).astype(BF16), preferred_element_type=F32)
                corner = jnp.where(strict, corner, 0.0)
                chi, clo = _split2(corner[:, :q // 2] + corner[:, q // 2:])
                das_r.append(jnp.dot(chi, ones, preferred_element_type=F32) + jnp.dot(clo, ones, preferred_element_type=F32))
            dxdt_parts.append(jnp.where(first, dxs_r[0], dxs_r[1]))
            da_parts.append(jnp.where(first, das_r[0], das_r[1]))
        dxdt = jnp.concatenate(dxdt_parts, axis=1) + bdh * ee
        da_intra = jnp.concatenate(da_parts, axis=1)
        dxs_ref[...] = dxdt * dt + dsk_ref[...] * dyv
        dc_ref[...] = dc_state + _dot(dcb, bmat)
        db_ref[...] = db_state + _dot(dcb, cmat, ta=True)
        ri = lax.broadcasted_iota(jnp.int32, (gw, gw), 0) // HEAD_DIM
        ci = lax.broadcasted_iota(jnp.int32, (gw, gw), 1) // HEAD_DIM
        blockdiag = (ri == ci).astype(BF16)

        def head_sum(v):
            hi, lo = _split2(v)
            return (jnp.dot(hi, blockdiag, preferred_element_type=F32)
                    + jnp.dot(lo, blockdiag, preferred_element_type=F32))

        xt = xe * bdh
        small = jnp.concatenate([
            jnp.sum(xt, axis=0, keepdims=True) + gam * jnp.sum(dh * h_all, axis=0, keepdims=True),
            jnp.sum(dyv * xs, axis=0, keepdims=True), jnp.zeros((6, gw), F32)], axis=0)
        small = head_sum(small)
        rows = lax.broadcasted_iota(jnp.int32, (q, 1), 0)
        da_local = head_sum(dyv * zc * ea - xt) + jnp.where(rows == q - 1, small[0:1, :], 0.0)
        hi, lo = _split2(da_local)
        d_dta = (da_intra + jnp.dot(upper, hi, preferred_element_type=F32)
                 + jnp.dot(upper, lo, preferred_element_type=F32))
        d_raw = (d_dta * a + head_sum(dxdt * xs)) * _sigmoid(dtx_ref[...] + biasx_ref[...])
        ddt_ref[...] = d_raw
        p_bias = jnp.sum(d_raw, axis=0, keepdims=True)
        p_alog = jnp.sum(d_dta * dt, axis=0, keepdims=True) * a
        p_dsk = small[1:2, :]

        @pl.when(step == 0)
        def _():
            dbias_ref[...] = p_bias
            dalog_ref[...] = p_alog
            ddsk_ref[...] = p_dsk

        @pl.when(step > 0)
        def _():
            dbias_ref[...] += p_bias
            dalog_ref[...] += p_alog
            ddsk_ref[...] += p_dsk

    wide = jax.ShapeDtypeStruct((l, GROUPS * gw), F32)
    narrow = jax.ShapeDtypeStruct((l, GROUPS * STATE), F32)
    vshape = jax.ShapeDtypeStruct((1, GROUPS * gw), F32)
    return _call(
        body, name=name, grid=(GROUPS, nc),
        in_specs=[xs_s, bm_s, cm_s, xs_s, dtt_s, vec, hcol, vec, hcol, vec, st_s, xs_s],
        out_specs=[xs_s, bc_s, bc_s, xs_s, vec, vec, vec],
        out_shape=[wide, narrow, narrow, wide, vshape, vshape, vshape],
        scratch=[pltpu.VMEM((STATE, gw), F32)], sem=("parallel", "arbitrary"),
        args=(xbc, xbc, xbc, dtx, dtrt, biasx, biast, alogx, alogt, dskx, states, dy), ride=ride)


def _exchange(srcs, scatter, *, name):
    ride = _Ride(srcs, scatter)

    def body(*refs):
        src, dst, sems = refs[:ride.n], refs[ride.n:2 * ride.n], refs[2 * ride.n:]
        ride.start(src, dst, sems)
        ride.finish(src, dst, sems)

    return pl.pallas_call(body, name=name, in_specs=ride.in_specs, out_specs=ride.out_specs, out_shape=ride.out_shape,
                          scratch_shapes=ride.scratch)(*srcs)


def _adamw(parts, w, m, v, *, name, tr=256):
    r, c = w.shape
    tr = _tile(r, tr, 16)
    c1, c2 = 1.0 / (1.0 - ADAM_B1 ** ADAM_STEP), 1.0 / (1.0 - ADAM_B2 ** ADAM_STEP)

    def body(p_ref, w_ref, m_ref, v_ref, g_ref, d_ref, nm_ref, nv_ref):
        g = p_ref[0].astype(F32)
        for p in range(1, N_DEV):
            g = g + p_ref[p].astype(F32)
        nm = ADAM_B1 * m_ref[...] + (1.0 - ADAM_B1) * g
        nv = ADAM_B2 * v_ref[...] + (1.0 - ADAM_B2) * (g * g)
        g_ref[...] = g
        nm_ref[...] = nm
        nv_ref[...] = nv
        d_ref[...] = -ADAM_LR * ((nm * c1) / (jnp.sqrt(nv * c2) + ADAM_EPS) + ADAM_WD * w_ref[...])

    blk = pl.BlockSpec((tr, c), lambda i: (i, 0))
    out = jax.ShapeDtypeStruct((r, c), F32)
    return pl.pallas_call(
        body, name=name, grid=(r // tr,), in_specs=[pl.BlockSpec((N_DEV, tr, c), lambda i: (0, i, 0)), blk, blk, blk],
        out_specs=[blk, blk, blk, blk], out_shape=[out, out, out, out], compiler_params=_params("parallel"),
    )(parts, w, m, v)


def kernel(x, attn_norm_w, w_in, conv_w, conv_b, dt_bias, a_log, d_skip, ssd_norm_w, pool_w, pool_scale, w_out, ffn_norm_w, w_gate, w_up, w_down, final_norm_w, loss_target, m_attn_norm_w, m_w_in, m_conv_w, m_conv_b, m_dt_bias, m_a_log, m_d_skip, m_ssd_norm_w, m_pool_w, m_pool_scale, m_w_out, m_ffn_norm_w, m_w_gate, m_w_up, m_w_down, m_final_norm_w, v_attn_norm_w, v_w_in, v_conv_w, v_conv_b, v_dt_bias, v_a_log, v_d_skip, v_ssd_norm_w, v_pool_w, v_pool_scale, v_w_out, v_ffn_norm_w, v_w_gate, v_w_up, v_w_down, v_final_norm_w):
    l, d = x.shape[1], x.shape[2]
    heads = dt_bias.shape[1]
    hpg = heads // GROUPS
    d_ssm = heads * HEAD_DIM
    conv_ch = conv_b.shape[1]
    d_pool = pool_scale.shape[1]
    pg = d_pool // GROUPS
    d_mix = d_ssm + d_pool
    d_ff = w_gate.shape[2] * N_DEV
    d_in = w_in.shape[2] * N_DEV
    dt_pad = -(-heads // LANE) * LANE
    o_u, o_xbc, o_dt = d_ssm, d_ssm + d_pool, d_ssm + d_pool + conv_ch
    d_inp = o_dt + dt_pad
    tn_in = _tile(d_inp, 2560, LANE)
    g_dt, g_u = d_ssm + conv_ch, d_ssm + conv_ch + heads

    x2, tgt = x[0], loss_target[0]

    gi, gp, gc = _exchange([w_in[0].astype(BF16), pool_w[0].astype(BF16), conv_w[0]], False, name="gather_w_in")
    win = jnp.transpose(gi, (1, 0, 2)).reshape(d, d_in)
    wp = jnp.concatenate([win[:, :d_ssm], win[:, g_u:], win[:, d_ssm:g_dt], win[:, g_dt:g_u],
                          jnp.zeros((d, dt_pad - heads), BF16)], axis=1)
    pw = jnp.transpose(gp, (1, 0, 2, 3)).reshape(GROUPS, pg, pg)
    cw = jnp.transpose(gc, (1, 0, 2)).reshape(CONV_K, conv_ch)

    def expand(vec):
        return jnp.repeat(vec, HEAD_DIM, axis=1)

    def per_head(vec):
        return vec[:, ::HEAD_DIM]

    bias_x, alog_x, dsk_x = expand(dt_bias), expand(a_log), expand(d_skip)
    bias_c, alog_c = dt_bias.reshape(GROUPS, hpg, 1), a_log.reshape(GROUPS, hpg, 1)

    h0 = _rms_fwd(x2, attn_norm_w, name="attn_norm")
    proj, go, gg = _mm(h0, wp, name="in_proj", tm=512, tn=tn_in, tk=d,
                       ride=_Ride([w_out[0].astype(BF16), w_gate[0].astype(BF16)], False))
    wo = go.reshape(d_mix, d)
    xbc = _conv_fwd(proj, o_xbc, cw, conv_b, name="conv_fwd")
    dt_raw = proj[:, o_dt:o_dt + heads]
    dtx = expand(dt_raw)
    dtrt = jnp.transpose(dt_raw.reshape(l, GROUPS, hpg), (1, 2, 0))
    y, states, gu = _scan_fwd(xbc, dtx, dtrt, bias_x, bias_c, alog_x, alog_c, dsk_x, name="ssd_fwd",
                              ride=_Ride([w_up[0].astype(BF16)], False))
    mixed = _gnorm_fwd(y, proj, ssd_norm_w, d_mix, name="gated_norm")
    pooled = _pool_fwd(proj, o_u, d_pool, name="pool_fwd")
    pool_raw, mixed = _pool_mix_fwd(pooled, pw, pool_scale, mixed, d_ssm, name="pool_mix")
    h1 = _mm(mixed, wo, name="out_proj", residual=x2)
    h1n = _rms_fwd(h1, ffn_norm_w, name="ffn_norm")
    gate, up, act, gd = _gate_up(h1n, gg, gu, name="gate_up", ride=_Ride([w_down[0].astype(BF16)], False))
    h2 = _mm_shards([(act, gd)], name="down_proj", residual=h1)

    loss11, dh2, dh2b, g_final = _loss_head(h2, final_norm_w.reshape(1, d), tgt, name="loss_head")
    dgate, dup = _gate_up_bwd(dh2b, gd, gate, up, name="gate_up_bwd")
    g_wd = _grad_shards(act, dh2b, name="grad_w_down")
    dh1n, r_wd = _mm_shards([(dgate, gg), (dup, gu)], name="dgate_dup", tb=True, ride=_Ride([g_wd], True))
    g_wg = _grad_shards(h1n, dgate, name="grad_w_gate")
    g_wu = _grad_shards(h1n, dup, name="grad_w_up")
    dh1, dh1b, g_ffn = _rms_bwd(h1, ffn_norm_w, dh1n, dh2, name="ffn_norm_bwd")
    dmixed = _mm(dh1b, wo, name="dmixed", tb=True, tk=d)
    g_wo = _mm(mixed, dh1b, name="grad_w_out", ta=True, out_dtype=BF16, tm=1024, tn=1024, tk=1024)
    draw, g_pscale = _pool_scale_bwd(dmixed, d_ssm, pool_raw, pool_scale, name="pool_scale_bwd")
    dpooled = _group_mm_nt(draw, pw, name="dpooled")
    g_pw = _group_mm_tn(pooled, draw, name="grad_pool_w")
    dy, dproj, g_ssdn = _gnorm_bwd(dmixed, y, proj, ssd_norm_w, name="gated_norm_bwd")
    dproj = _pool_bwd(dpooled, dproj, o_u, name="pool_bwd")
    s_pw = jnp.transpose(g_pw.reshape(GROUPS, N_DEV, pg // N_DEV, pg), (1, 0, 2, 3)).astype(BF16)
    dxs, dbm, dcm, ddtx, g_bias, g_alog, g_dsk, r_wg, r_wu, r_wo, r_pw = _scan_bwd(
        xbc, dtx, dtrt, bias_x, bias_c, alog_x, alog_c, dsk_x, states, dy, name="ssd_bwd",
        ride=_Ride([g_wg, g_wu, g_wo.reshape(N_DEV, d_mix // N_DEV, d), s_pw], True))
    n_bc = GROUPS * STATE
    segs = [(dxs, 0, d_ssm), (dbm, d_ssm, n_bc), (dcm, d_ssm + n_bc, n_bc)]
    g_cw, g_cb = [], []
    for si, (dseg, c0, width) in enumerate(segs):
        dproj, b, c_ = _conv_bwd(proj, o_xbc + c0, cw[:, c0:c0 + width], conv_b[:, c0:c0 + width], dseg, dproj,
                                 name=f"conv_bwd{si}")
        g_cw.append(b)
        g_cb.append(c_)
    ddt = jnp.pad(per_head(ddtx), ((0, 0), (0, dt_pad - heads)))
    dproj = _put_cols(ddt, dproj, o_dt, name="put_ddt")
    g_wp = _mm(h0, dproj, name="grad_w_in", ta=True, out_dtype=BF16, tm=512, tn=tn_in, tk=1024)
    g_win = jnp.concatenate([g_wp[:, :d_ssm], g_wp[:, o_xbc:o_dt + heads], g_wp[:, o_u:o_xbc]], axis=1)
    def cols8(g):
        return jnp.transpose(g.reshape(g.shape[0], N_DEV, g.shape[1] // N_DEV), (1, 0, 2))
    dh0, r_in, r_cw = _mm(dproj, wp, name="dh0", tb=True, tk=tn_in,
                          ride=_Ride([cols8(g_win), cols8(jnp.concatenate(g_cw, axis=1))], True))
    dx, _, g_attn = _rms_bwd(x2, attn_norm_w, dh0, dh1, name="attn_norm_bwd")

    smalls = [g_attn, jnp.concatenate(g_cb, axis=1), per_head(g_bias), per_head(g_alog), per_head(g_dsk), g_ssdn,
              g_pscale, g_ffn, g_final]
    small_w = [attn_norm_w, conv_b, dt_bias, a_log, d_skip, ssd_norm_w, pool_scale, ffn_norm_w, final_norm_w.reshape(1, d)]
    small_m = [m_attn_norm_w, m_conv_b, m_dt_bias, m_a_log, m_d_skip, m_ssd_norm_w, m_pool_scale, m_ffn_norm_w,
               m_final_norm_w.reshape(1, d)]
    small_v = [v_attn_norm_w, v_conv_b, v_dt_bias, v_a_log, v_d_skip, v_ssd_norm_w, v_pool_scale, v_ffn_norm_w,
               v_final_norm_w.reshape(1, d)]
    sizes = [s.shape[1] for s in smalls]
    n_small = sum(sizes)
    n_pad = -(-n_small // (16 * LANE)) * (16 * LANE)
    rows = n_pad // LANE
    def pack(vs):
        return jnp.pad(jnp.concatenate(vs, axis=1), ((0, 0), (0, n_pad - n_small))).reshape(rows, LANE)
    (r_small,) = _exchange([pack(smalls)], False, name="gather_small_grads")

    def big(parts, w, m, v, nm):
        shp = w.shape
        r2 = lambda t: t.reshape(-1, shp[-1])
        outs = _adamw(parts.reshape(N_DEV, -1, shp[-1]), r2(w), r2(m), r2(v), name=nm)
        return [o.reshape(shp) for o in outs]

    res = {
        "w_in": big(r_in, w_in, m_w_in, v_w_in, "adamw_w_in"),
        "conv_w": big(r_cw, conv_w, m_conv_w, v_conv_w, "adamw_conv_w"),
        "pool_w": big(r_pw, pool_w, m_pool_w, v_pool_w, "adamw_pool_w"),
        "w_out": big(r_wo, w_out, m_w_out, v_w_out, "adamw_w_out"),
        "w_gate": big(r_wg, w_gate, m_w_gate, v_w_gate, "adamw_w_gate"),
        "w_up": big(r_wu, w_up, m_w_up, v_w_up, "adamw_w_up"),
        "w_down": big(r_wd, w_down, m_w_down, v_w_down, "adamw_w_down"),
    }
    s_out = _adamw(r_small, pack(small_w), pack(small_m), pack(small_v), name="adamw_small")
    names = ["attn_norm_w", "conv_b", "dt_bias", "a_log", "d_skip", "ssd_norm_w", "pool_scale", "ffn_norm_w", "final_norm_w"]
    offs = [sum(sizes[:i]) for i in range(len(sizes))]
    for i, nm in enumerate(names):
        shp = (d,) if nm == "final_norm_w" else (1, sizes[i])
        res[nm] = [o.reshape(1, n_pad)[:, offs[i]:offs[i] + sizes[i]].reshape(shp) for o in s_out]

    loss = lax.psum(loss11[0, 0], ("x", "y", "c"))
    order = ["attn_norm_w", "w_in", "conv_w", "conv_b", "dt_bias", "a_log", "d_skip", "ssd_norm_w", "pool_w", "pool_scale",
             "w_out", "ffn_norm_w", "w_gate", "w_up", "w_down", "final_norm_w"]
    outs = [loss, dx.reshape(x.shape)]
    for part in range(4):
        outs += [res[nm][part] for nm in order]
    return tuple(outs)
```

```python
import functools

import jax
import jax.numpy as jnp
from jax import lax
from jax.experimental import pallas as pl
from jax.experimental.pallas import tpu as pltpu

F32 = jnp.float32
BF16 = jnp.bfloat16

NORM_EPS = 1e-5
HEAD_DIM = 64
STATE = 128
CHUNK = 256
GROUPS = 4
CONV_K = 4
POOL_WINDOWS = (2, 4, 8, 16)
POOL_HALO = 16
CONV_HALO = 8
LANE = 128
N_DEV = 8
ADAM_LR, ADAM_B1, ADAM_B2, ADAM_EPS, ADAM_WD, ADAM_STEP = 0.001, 0.9, 0.999, 1e-08, 0.01, 10
NEG_BIG = -1e30
EPILOGUE_ROWS = 256
MESH_ID = pl.DeviceIdType.MESH


def _tile(n, pref, mult):
    if n <= pref:
        return n
    t = (pref // mult) * mult
    while t >= mult:
        if n % t == 0:
            return t
        t -= mult
    raise ValueError(f"no tile for {n} (pref {pref}, mult {mult})")


def _params(*sem):
    return pltpu.CompilerParams(dimension_semantics=sem)


ANY = pl.BlockSpec(memory_space=pl.ANY)


def _peer(k):
    x, y, c = lax.axis_index("x"), lax.axis_index("y"), lax.axis_index("c")
    return (1 - x if k & 4 else x, 1 - y if k & 2 else y, 1 - c if k & 1 else c)


def _my_index():
    return 4 * lax.axis_index("x") + 2 * lax.axis_index("y") + lax.axis_index("c")


class _Ride:
    def __init__(self, srcs, scatter):
        self.srcs, self.scatter, self.n = list(srcs), scatter, len(srcs)
        self.two_level = not scatter
        self.in_specs = [ANY] * self.n
        self.out_specs = [ANY] * self.n
        self.out_shape = [jax.ShapeDtypeStruct((N_DEV,) + tuple(s.shape[1:] if scatter else s.shape), s.dtype)
                          for s in self.srcs]
        self.scratch = [pltpu.SemaphoreType.DMA((self.n, N_DEV - 1)), pltpu.SemaphoreType.DMA((self.n, N_DEV - 1)),
                        pltpu.SemaphoreType.DMA((self.n,))]

    def _copies(self, src, dst, sems):
        send_sems, recv_sems, local_sems = sems
        me = _my_index()
        local = [pltpu.make_async_copy(src[t].at[me] if self.scatter else src[t], dst[t].at[me], local_sems.at[t])
                 for t in range(self.n)]
        remote = {}
        for k in range(1, N_DEV):
            peer, pidx = _peer(k), me ^ k
            remote[k] = [pltpu.make_async_remote_copy(
                src_ref=src[t].at[pidx] if self.scatter else src[t], dst_ref=dst[t].at[me],
                send_sem=send_sems.at[t, k - 1], recv_sem=recv_sems.at[t, k - 1],
                device_id=peer, device_id_type=MESH_ID) for t in range(self.n)]
        return local, remote

    def _forwards(self, dst, sems):
        send_sems, recv_sems, _ = sems
        me = _my_index()
        return {k: [pltpu.make_async_remote_copy(
            src_ref=dst[t].at[me ^ k], dst_ref=dst[t].at[me ^ k],
            send_sem=send_sems.at[t, (k ^ 1) - 1], recv_sem=recv_sems.at[t, (k ^ 1) - 1],
            device_id=_peer(1), device_id_type=MESH_ID) for t in range(self.n)] for k in (2, 4, 6)}

    def start(self, src, dst, sems):
        local, remote = self._copies(src, dst, sems)
        for cp in local:
            cp.start()
        for k in (2, 4, 6, 1) if self.two_level else range(1, N_DEV):
            for cp in remote[k]:
                cp.start()

    def finish(self, src, dst, sems):
        local, remote = self._copies(src, dst, sems)
        if self.two_level:
            fwd = self._forwards(dst, sems)
            for k in (2, 4, 6):
                for t in range(self.n):
                    remote[k][t].wait_recv()
                    fwd[k][t].start()
            for k in (1, 3, 5, 7):
                for cp in remote[k]:
                    cp.wait_recv()
            for k in (1, 2, 4, 6):
                for cp in remote[k]:
                    cp.wait_send()
            for k in (2, 4, 6):
                for cp in fwd[k]:
                    cp.wait_send()
        else:
            for k in range(1, N_DEV):
                for cp in remote[k]:
                    cp.wait_send()
            for k in range(1, N_DEV):
                for cp in remote[k]:
                    cp.wait_recv()
        for lc in local:
            lc.wait()


def _call(body, *, name, grid, in_specs, out_specs, out_shape, args, sem, scratch=(), ride=None):
    in_specs, out_specs, out_shape, scratch = list(in_specs), list(out_specs), list(out_shape), list(scratch)
    if ride is None:
        return pl.pallas_call(body, name=name, grid=grid, in_specs=in_specs, out_specs=out_specs, out_shape=out_shape,
                              scratch_shapes=scratch, compiler_params=_params(*sem))(*args)
    n_in, n_out, n_sc, nr = len(in_specs), len(out_specs), len(scratch), ride.n

    def full(*refs):
        ins, csrc = refs[:n_in], refs[n_in:n_in + nr]
        o0 = n_in + nr
        outs, cdst = refs[o0:o0 + n_out], refs[o0 + n_out:o0 + n_out + nr]
        s0 = o0 + n_out + nr
        sc, csem = refs[s0:s0 + n_sc], refs[s0 + n_sc:]
        ids = [pl.program_id(ax) for ax in range(len(grid))]
        first = functools.reduce(jnp.logical_and, [i == 0 for i in ids])
        last = functools.reduce(jnp.logical_and, [i == g - 1 for i, g in zip(ids, grid)])

        @pl.when(first)
        def _():
            ride.start(csrc, cdst, csem)

        body(*ins, *outs, *sc)

        @pl.when(last)
        def _():
            ride.finish(csrc, cdst, csem)

    return pl.pallas_call(
        full, name=name, grid=grid, in_specs=in_specs + ride.in_specs, out_specs=out_specs + ride.out_specs,
        out_shape=out_shape + ride.out_shape, scratch_shapes=scratch + ride.scratch,
        compiler_params=_params(*(["arbitrary"] * len(grid))),
    )(*args, *ride.srcs)


def _sigmoid(x):
    return 1.0 / (1.0 + jnp.exp(-x))


def _softplus(x):
    return jnp.maximum(x, 0.0) + jnp.log(1.0 + jnp.exp(-jnp.abs(x)))


def _dot(a, b, ta=False, tb=False):
    dn = (((0 if ta else 1,), (1 if tb else 0,)), ((), ()))
    return lax.dot_general(a.astype(BF16), b.astype(BF16), dn, preferred_element_type=F32)


def _mm(a, b, *, name, ta=False, tb=False, out_dtype=F32, tm=1024, tn=1024, tk=2048, residual=None, ride=None):
    m = a.shape[1] if ta else a.shape[0]
    k = a.shape[0] if ta else a.shape[1]
    n = b.shape[0] if tb else b.shape[1]
    assert k == (b.shape[1] if tb else b.shape[0])
    tm, tn = _tile(m, tm, LANE if ta else 8), _tile(n, tn, LANE)
    tk = _tile(k, tk, LANE if not (ta and tb) else LANE)
    nk = k // tk
    has_res = residual is not None

    def body(*refs):
        a_ref, b_ref = refs[0], refs[1]
        r_ref = refs[2] if has_res else None
        o_ref = refs[3] if has_res else refs[2]
        part = _dot(a_ref[...], b_ref[...], ta, tb)

        def finish(acc):
            if has_res:
                acc = acc + r_ref[...]
            o_ref[...] = acc.astype(o_ref.dtype)

        if nk == 1:
            finish(part)
        else:
            acc_ref = refs[-1]
            kk = pl.program_id(2)

            @pl.when(kk == 0)
            def _():
                acc_ref[...] = part

            @pl.when(kk > 0)
            def _():
                acc_ref[...] += part

            @pl.when(kk == nk - 1)
            def _():
                finish(acc_ref[...])

    a_spec = pl.BlockSpec((tk, tm), lambda i, j, kk: (kk, i)) if ta else pl.BlockSpec((tm, tk), lambda i, j, kk: (i, kk))
    b_spec = pl.BlockSpec((tn, tk), lambda i, j, kk: (j, kk)) if tb else pl.BlockSpec((tk, tn), lambda i, j, kk: (kk, j))
    o_spec = pl.BlockSpec((tm, tn), lambda i, j, kk: (i, j))
    in_specs = [a_spec, b_spec] + ([o_spec] if has_res else [])
    args = (a, b) + ((residual,) if has_res else ())
    res = _call(body, name=name, grid=(m // tm, n // tn, nk), in_specs=in_specs, out_specs=[o_spec],
                out_shape=[jax.ShapeDtypeStruct((m, n), out_dtype)], args=args,
                scratch=[pltpu.VMEM((tm, tn), F32)] if nk > 1 else [], sem=("parallel", "parallel", "arbitrary"), ride=ride)
    return res[0] if ride is None else res


def _gate_up(h, wg, wu, *, name, tm=1024, ride=None):
    m, k = h.shape
    ns, _, c = wg.shape
    tm = _tile(m, tm, 16)

    tr = _tile(tm, EPILOGUE_ROWS, 16)

    def body(h_ref, wg_ref, wu_ref, g_ref, u_ref, a_ref):
        for r in range(tm // tr):
            rows = slice(r * tr, (r + 1) * tr)
            hb = h_ref[rows, :]
            g = _dot(hb, wg_ref[...])
            u = _dot(hb, wu_ref[...])
            g_ref[rows, :] = g
            u_ref[rows, :] = u
            a_ref[rows, :] = (g * _sigmoid(g) * u).astype(a_ref.dtype)

    o_spec = pl.BlockSpec((None, tm, c), lambda i, s: (s, i, 0))
    w_spec = pl.BlockSpec((None, k, c), lambda i, s: (s, 0, 0))
    return _call(body, name=name, grid=(m // tm, ns),
                 in_specs=[pl.BlockSpec((tm, k), lambda i, s: (i, 0)), w_spec, w_spec], out_specs=[o_spec, o_spec, o_spec],
                 out_shape=[jax.ShapeDtypeStruct((ns, m, c), F32), jax.ShapeDtypeStruct((ns, m, c), F32),
                            jax.ShapeDtypeStruct((ns, m, c), BF16)],
                 args=(h, wg, wu), sem=("parallel", "parallel"), ride=ride)


def _gate_up_bwd(dh, wd, g, u, *, name, tm=1024):
    m, k = dh.shape
    ns, c, _ = wd.shape
    tm = _tile(m, tm, 16)

    tr = _tile(tm, EPILOGUE_ROWS, 16)

    def body(dh_ref, wd_ref, g_ref, u_ref, dg_ref, du_ref):
        for r in range(tm // tr):
            rows = slice(r * tr, (r + 1) * tr)
            da = _dot(dh_ref[rows, :], wd_ref[...], tb=True)
            g = g_ref[rows, :]
            s = _sigmoid(g)
            dg_ref[rows, :] = (da * u_ref[rows, :] * (s * (1.0 + g * (1.0 - s)))).astype(dg_ref.dtype)
            du_ref[rows, :] = (da * (g * s)).astype(du_ref.dtype)

    o_spec = pl.BlockSpec((None, tm, c), lambda i, s: (s, i, 0))
    return _call(body, name=name, grid=(m // tm, ns),
                 in_specs=[pl.BlockSpec((tm, k), lambda i, s: (i, 0)), pl.BlockSpec((None, c, k), lambda i, s: (s, 0, 0)),
                           o_spec, o_spec],
                 out_specs=[o_spec, o_spec],
                 out_shape=[jax.ShapeDtypeStruct((ns, m, c), BF16), jax.ShapeDtypeStruct((ns, m, c), BF16)],
                 args=(dh, wd, g, u), sem=("parallel", "parallel"))


def _mm_shards(pairs, *, name, tb=False, residual=None, tm=1024, tn=1024, per_step=2, ride=None):
    ns, m, c = pairs[0][0].shape
    n = pairs[0][1].shape[1] if tb else pairs[0][1].shape[2]
    tm, tn = _tile(m, tm, 16), _tile(n, tn, LANE)
    npair = len(pairs)
    has_res = residual is not None
    assert ns % per_step == 0
    nsteps = ns // per_step

    def body(*refs):
        o_ref, acc_ref = refs[-2], refs[-1]
        s = pl.program_id(2)
        part = None
        for p in range(npair):
            for q in range(per_step):
                term = _dot(refs[2 * p][q], refs[2 * p + 1][q], tb=tb)
                part = term if part is None else part + term

        @pl.when(s == 0)
        def _():
            acc_ref[...] = part

        @pl.when(s > 0)
        def _():
            acc_ref[...] += part

        @pl.when(s == nsteps - 1)
        def _():
            acc = acc_ref[...]
            if has_res:
                acc = acc + refs[2 * npair][...]
            o_ref[...] = acc

    a_spec = pl.BlockSpec((per_step, tm, c), lambda i, j, s: (s, i, 0))
    b_spec = (pl.BlockSpec((per_step, tn, c), lambda i, j, s: (s, j, 0)) if tb
              else pl.BlockSpec((per_step, c, tn), lambda i, j, s: (s, 0, j)))
    o_spec = pl.BlockSpec((tm, tn), lambda i, j, s: (i, j))
    args = [t for pr in pairs for t in pr] + ([residual] if has_res else [])
    res = _call(body, name=name, grid=(m // tm, n // tn, nsteps),
                in_specs=[a_spec, b_spec] * npair + ([o_spec] if has_res else []), out_specs=[o_spec],
                out_shape=[jax.ShapeDtypeStruct((m, n), F32)], args=args, scratch=[pltpu.VMEM((tm, tn), F32)],
                sem=("parallel", "parallel", "arbitrary"), ride=ride)
    return res[0] if ride is None else res


def _grad_shards(a, b, *, name, tk=2048):
    a3 = a.ndim == 3
    ns, l, c = a.shape if a3 else b.shape
    d = (b if a3 else a).shape[1]
    tk = _tile(l, tk, 16)
    nk = l // tk

    def body(a_ref, b_ref, o_ref, acc_ref):
        kk = pl.program_id(1)
        part = _dot(a_ref[...], b_ref[...], ta=True)

        @pl.when(kk == 0)
        def _():
            acc_ref[...] = part

        @pl.when(kk > 0)
        def _():
            acc_ref[...] += part

        @pl.when(kk == nk - 1)
        def _():
            o_ref[...] = acc_ref[...].astype(o_ref.dtype)

    s3 = pl.BlockSpec((None, tk, c), lambda s, kk: (s, kk, 0))
    s2 = pl.BlockSpec((tk, d), lambda s, kk: (kk, 0))
    out = (c, d) if a3 else (d, c)
    return pl.pallas_call(
        body, name=name, grid=(ns, nk), in_specs=[s3, s2] if a3 else [s2, s3],
        out_specs=pl.BlockSpec((None,) + out, lambda s, kk: (s, 0, 0)),
        out_shape=jax.ShapeDtypeStruct((ns,) + out, BF16), scratch_shapes=[pltpu.VMEM(out, F32)],
        compiler_params=_params("parallel", "arbitrary"),
    )(a, b)


def _rms_fwd(x, w, *, name, tl=512):
    l, d = x.shape
    tl = _tile(l, tl, 16)

    def body(x_ref, w_ref, o_ref):
        xv = x_ref[...]
        r = lax.rsqrt(jnp.mean(xv * xv, axis=-1, keepdims=True) + NORM_EPS)
        o_ref[...] = (xv * r * w_ref[...]).astype(o_ref.dtype)

    row = pl.BlockSpec((tl, d), lambda i: (i, 0))
    return pl.pallas_call(
        body, name=name, grid=(l // tl,), in_specs=[row, pl.BlockSpec((1, d), lambda i: (0, 0))], out_specs=row,
        out_shape=jax.ShapeDtypeStruct((l, d), BF16), compiler_params=_params("parallel"),
    )(x, w)


def _rms_bwd(x, w, dy, add, *, name, tl=256):
    l, d = x.shape
    tl = _tile(l, tl, 16)

    def body(x_ref, w_ref, dy_ref, add_ref, dx_ref, dxb_ref, dw_ref):
        i = pl.program_id(0)
        xv = x_ref[...]
        r = lax.rsqrt(jnp.mean(xv * xv, axis=-1, keepdims=True) + NORM_EPS)
        xh = xv * r
        dyv = dy_ref[...].astype(F32)
        dxh = dyv * w_ref[...]
        dx = r * (dxh - xh * jnp.mean(dxh * xh, axis=-1, keepdims=True)) + add_ref[...]
        dx_ref[...] = dx
        dxb_ref[...] = dx.astype(BF16)
        part = jnp.sum(dyv * xh, axis=0, keepdims=True)

        @pl.when(i == 0)
        def _():
            dw_ref[...] = part

        @pl.when(i > 0)
        def _():
            dw_ref[...] += part

    row = pl.BlockSpec((tl, d), lambda i: (i, 0))
    vec = pl.BlockSpec((1, d), lambda i: (0, 0))
    return pl.pallas_call(
        body, name=name, grid=(l // tl,), in_specs=[row, vec, row, row], out_specs=[row, row, vec],
        out_shape=[jax.ShapeDtypeStruct((l, d), F32), jax.ShapeDtypeStruct((l, d), BF16), jax.ShapeDtypeStruct((1, d), F32)],
        compiler_params=_params("arbitrary"),
    )(x, w, dy, add)


def _loss_head(h, w, target, *, name, tl=256):
    l, d = h.shape
    tl = _tile(l, tl, 16)

    def body(x_ref, w_ref, t_ref, loss_ref, dx_ref, dxb_ref, dw_ref):
        i = pl.program_id(0)
        xv = x_ref[...]
        r = lax.rsqrt(jnp.mean(xv * xv, axis=-1, keepdims=True) + NORM_EPS)
        xh = xv * r
        wv = w_ref[...]
        diff = xh * wv - t_ref[...]
        lpart = 0.5 * jnp.sum(jnp.mean(diff * diff, axis=-1, keepdims=True), axis=0, keepdims=True)
        dyv = diff * (1.0 / d)
        dxh = dyv * wv
        dx = r * (dxh - xh * jnp.mean(dxh * xh, axis=-1, keepdims=True))
        dx_ref[...] = dx
        dxb_ref[...] = dx.astype(BF16)
        part = jnp.sum(dyv * xh, axis=0, keepdims=True)

        @pl.when(i == 0)
        def _():
            dw_ref[...] = part
            loss_ref[...] = lpart

        @pl.when(i > 0)
        def _():
            dw_ref[...] += part
            loss_ref[...] += lpart

    row = pl.BlockSpec((tl, d), lambda i: (i, 0))
    vec = pl.BlockSpec((1, d), lambda i: (0, 0))
    one = pl.BlockSpec((1, 1), lambda i: (0, 0))
    return pl.pallas_call(
        body, name=name, grid=(l // tl,), in_specs=[row, vec, row], out_specs=[one, row, row, vec],
        out_shape=[jax.ShapeDtypeStruct((1, 1), F32), jax.ShapeDtypeStruct((l, d), F32),
                   jax.ShapeDtypeStruct((l, d), BF16), jax.ShapeDtypeStruct((1, d), F32)],
        compiler_params=_params("arbitrary"),
    )(h, w, target)


def _conv_pre(ext, w_ref):
    acc = ext * w_ref[CONV_K - 1:CONV_K, :]
    for k in range(CONV_K - 1):
        acc = acc + pltpu.roll(ext, CONV_K - 1 - k, axis=0) * w_ref[k:k + 1, :]
    return acc


def _conv_fwd(proj, col0, cw, cb, *, name, tl=512, tc=512):
    l = proj.shape[0]
    c = cw.shape[1]
    tl, tc = _tile(l, tl, 16), _tile(c, tc, LANE)
    assert col0 % tc == 0
    off, hb = col0 // tc, tl // CONV_HALO

    def body(x_ref, halo_ref, w_ref, b_ref, o_ref):
        i = pl.program_id(0)
        halo = jnp.where(i > 0, halo_ref[...], 0.0)
        ext = jnp.concatenate([halo, x_ref[...]], axis=0)
        pre = _conv_pre(ext, w_ref)[CONV_HALO:] + b_ref[...]
        o_ref[...] = pre * _sigmoid(pre)

    return pl.pallas_call(
        body, name=name, grid=(l // tl, c // tc),
        in_specs=[pl.BlockSpec((tl, tc), lambda i, j: (i, off + j)),
                  pl.BlockSpec((CONV_HALO, tc), lambda i, j: (jnp.maximum(i * hb - 1, 0), off + j)),
                  pl.BlockSpec((CONV_K, tc), lambda i, j: (0, j)), pl.BlockSpec((1, tc), lambda i, j: (0, j))],
        out_specs=pl.BlockSpec((tl, tc), lambda i, j: (i, j)),
        out_shape=jax.ShapeDtypeStruct((l, c), F32), compiler_params=_params("parallel", "parallel"),
    )(proj, proj, cw, cb)


def _conv_bwd(proj, col0, cw, cb, dout, dproj, *, name, tl=512, tc=512):
    l = proj.shape[0]
    c = dout.shape[1]
    tl, tc = _tile(l, tl, 16), _tile(c, tc, LANE)
    assert col0 % tc == 0
    off, hb, nt = col0 // tc, tl // CONV_HALO, l // tl
    n_ext = tl + CONV_HALO

    def body(x_ref, prev_ref, nxt_ref, d_ref, dn_ref, w_ref, b_ref, dproj_in, dx_ref, dw_ref, db_ref):
        del dproj_in
        i = pl.program_id(1)
        last = i == nt - 1
        halo = jnp.where(i > 0, prev_ref[...], 0.0)
        ext = jnp.concatenate([halo, x_ref[...], nxt_ref[...]], axis=0)
        pre = _conv_pre(ext, w_ref)[CONV_HALO:] + b_ref[...]
        dext = jnp.concatenate([d_ref[...], jnp.where(last, 0.0, dn_ref[...])], axis=0)
        s = _sigmoid(pre)
        dpre = dext * (s * (1.0 + pre * (1.0 - s)))
        dx = dpre * w_ref[CONV_K - 1:CONV_K, :]
        for k in range(CONV_K - 1):
            dx = dx + pltpu.roll(dpre, n_ext - (CONV_K - 1 - k), axis=0) * w_ref[k:k + 1, :]
        dx_ref[...] = dx[:tl].astype(dx_ref.dtype)
        dp = dpre[:tl]
        rows = [jnp.sum(dp * pltpu.roll(ext, CONV_K - 1 - k, axis=0)[CONV_HALO:CONV_HALO + tl], axis=0, keepdims=True)
                for k in range(CONV_K - 1)]
        rows.append(jnp.sum(dp * ext[CONV_HALO:CONV_HALO + tl], axis=0, keepdims=True))
        dwp = jnp.concatenate(rows, axis=0)
        dbp = jnp.sum(dp, axis=0, keepdims=True)

        @pl.when(i == 0)
        def _():
            dw_ref[...] = dwp
            db_ref[...] = dbp

        @pl.when(i > 0)
        def _():
            dw_ref[...] += dwp
            db_ref[...] += dbp

    cur = lambda j, i: (i, j)
    nxt = lambda j, i: (jnp.minimum((i + 1) * hb, nt * hb - 1), j)
    return pl.pallas_call(
        body, name=name, grid=(c // tc, nt),
        in_specs=[pl.BlockSpec((tl, tc), lambda j, i: (i, off + j)),
                  pl.BlockSpec((CONV_HALO, tc), lambda j, i: (jnp.maximum(i * hb - 1, 0), off + j)),
                  pl.BlockSpec((CONV_HALO, tc), lambda j, i: (jnp.minimum((i + 1) * hb, nt * hb - 1), off + j)),
                  pl.BlockSpec((tl, tc), cur), pl.BlockSpec((CONV_HALO, tc), nxt),
                  pl.BlockSpec((CONV_K, tc), lambda j, i: (0, j)), pl.BlockSpec((1, tc), lambda j, i: (0, j)), ANY],
        out_specs=[pl.BlockSpec((tl, tc), lambda j, i: (i, off + j)), pl.BlockSpec((CONV_K, tc), lambda j, i: (0, j)),
                   pl.BlockSpec((1, tc), lambda j, i: (0, j))],
        out_shape=[jax.ShapeDtypeStruct(dproj.shape, dproj.dtype), jax.ShapeDtypeStruct((CONV_K, c), F32),
                   jax.ShapeDtypeStruct((1, c), F32)],
        input_output_aliases={7: 0}, compiler_params=_params("parallel", "arbitrary"),
    )(proj, proj, proj, dout, dout, cw, cb, dproj)


def _pool_fwd(proj, col0, d_pool, *, name, tl=256):
    l = proj.shape[0]
    tl = _tile(l, tl, POOL_HALO)
    assert col0 % d_pool == 0
    off, hb, pg = col0 // d_pool, tl // POOL_HALO, d_pool // GROUPS

    def body(u_ref, halo_ref, o_ref):
        i = pl.program_id(0)
        halo = jnp.where(i > 0, halo_ref[...], 0.0)
        ext = jnp.concatenate([halo, u_ref[...]], axis=0)
        t = i * tl + lax.broadcasted_iota(jnp.int32, (tl, 1), 0)
        s, width = ext, 1
        for gi, win in enumerate(POOL_WINDOWS):
            while width < win:
                s = s + pltpu.roll(s, width, axis=0)
                width *= 2
            cnt = jnp.minimum(t + 1, win).astype(F32)
            sl = slice(gi * pg, (gi + 1) * pg)
            o_ref[:, sl] = (s[POOL_HALO:, sl] / cnt - ext[POOL_HALO:, sl]).astype(o_ref.dtype)

    return pl.pallas_call(
        body, name=name, grid=(l // tl,),
        in_specs=[pl.BlockSpec((tl, d_pool), lambda i: (i, off)),
                  pl.BlockSpec((POOL_HALO, d_pool), lambda i: (jnp.maximum(i * hb - 1, 0), off))],
        out_specs=pl.BlockSpec((tl, d_pool), lambda i: (i, 0)),
        out_shape=jax.ShapeDtypeStruct((l, d_pool), BF16), compiler_params=_params("parallel"),
    )(proj, proj)


def _pool_bwd(dp, dproj, col0, *, name, tl=256):
    l, d_pool = dp.shape
    tl = _tile(l, tl, POOL_HALO)
    assert col0 % d_pool == 0
    off, hb, nt, pg = col0 // d_pool, tl // POOL_HALO, l // tl, d_pool // GROUPS
    n_ext = tl + POOL_HALO

    def body(d_ref, nxt_ref, dproj_in, o_ref):
        del dproj_in
        i = pl.program_id(0)
        ext = jnp.concatenate([d_ref[...], jnp.where(i == nt - 1, 0.0, nxt_ref[...])], axis=0)
        t = i * tl + lax.broadcasted_iota(jnp.int32, (n_ext, 1), 0)
        for gi, win in enumerate(POOL_WINDOWS):
            sl = slice(gi * pg, (gi + 1) * pg)
            dg = ext[:, sl]
            s = dg / jnp.minimum(t + 1, win).astype(F32)
            width = 1
            while width < win:
                s = s + pltpu.roll(s, n_ext - width, axis=0)
                width *= 2
            o_ref[:, sl] = (s[:tl] - dg[:tl]).astype(o_ref.dtype)

    return pl.pallas_call(
        body, name=name, grid=(nt,),
        in_specs=[pl.BlockSpec((tl, d_pool), lambda i: (i, 0)),
                  pl.BlockSpec((POOL_HALO, d_pool), lambda i: (jnp.minimum((i + 1) * hb, nt * hb - 1), 0)), ANY],
        out_specs=pl.BlockSpec((tl, d_pool), lambda i: (i, off)),
        out_shape=jax.ShapeDtypeStruct(dproj.shape, dproj.dtype), input_output_aliases={2: 0},
        compiler_params=_params("parallel"),
    )(dp, dp, dproj)


def _put_cols(src, dst, col0, *, name, tl=1024):
    l, c = src.shape
    tl = _tile(l, tl, 16)
    assert col0 % c == 0 and c % LANE == 0

    def body(s_ref, dst_in, o_ref):
        del dst_in
        o_ref[...] = s_ref[...].astype(o_ref.dtype)

    return pl.pallas_call(
        body, name=name, grid=(l // tl,), in_specs=[pl.BlockSpec((tl, c), lambda i: (i, 0)), ANY],
        out_specs=pl.BlockSpec((tl, c), lambda i: (i, col0 // c)), out_shape=jax.ShapeDtypeStruct(dst.shape, dst.dtype),
        input_output_aliases={1: 0}, compiler_params=_params("parallel"),
    )(src, dst)


def _pool_mix_fwd(pooled, pw, scale, mixed, col0, *, name, tm=1024):
    l, d_pool = pooled.shape
    pg = d_pool // GROUPS
    tm = _tile(l, tm, 16)
    assert col0 % pg == 0
    off = col0 // pg

    def body(a_ref, w_ref, s_ref, mixed_in, raw_ref, mixed_ref):
        del mixed_in
        raw = _dot(a_ref[...], w_ref[...])
        raw_ref[...] = raw
        mixed_ref[...] = (raw * s_ref[...]).astype(mixed_ref.dtype)

    blk = pl.BlockSpec((tm, pg), lambda i, g: (i, g))
    return pl.pallas_call(
        body, name=name, grid=(l // tm, GROUPS),
        in_specs=[blk, pl.BlockSpec((None, pg, pg), lambda i, g: (g, 0, 0)), pl.BlockSpec((1, pg), lambda i, g: (0, g)),
                  pl.BlockSpec(memory_space=pl.ANY)],
        out_specs=[blk, pl.BlockSpec((tm, pg), lambda i, g: (i, off + g))],
        out_shape=[jax.ShapeDtypeStruct((l, d_pool), F32), jax.ShapeDtypeStruct(mixed.shape, mixed.dtype)],
        input_output_aliases={3: 1}, compiler_params=_params("parallel", "parallel"),
    )(pooled, pw, scale, mixed)


def _pool_scale_bwd(dmixed, col0, raw, scale, *, name, tl=512):
    l, d_pool = raw.shape
    tl, tc = _tile(l, tl, 16), _tile(d_pool, 512, LANE)
    assert col0 % tc == 0
    off = col0 // tc

    def body(d_ref, raw_ref, s_ref, o_ref, ds_ref):
        i = pl.program_id(1)
        dv = d_ref[...]
        o_ref[...] = (dv * s_ref[...]).astype(o_ref.dtype)
        part = jnp.sum(dv * raw_ref[...], axis=0, keepdims=True)

        @pl.when(i == 0)
        def _():
            ds_ref[...] = part

        @pl.when(i > 0)
        def _():
            ds_ref[...] += part

    blk = pl.BlockSpec((tl, tc), lambda j, i: (i, j))
    vec = pl.BlockSpec((1, tc), lambda j, i: (0, j))
    return pl.pallas_call(
        body, name=name, grid=(d_pool // tc, l // tl),
        in_specs=[pl.BlockSpec((tl, tc), lambda j, i: (i, off + j)), blk, vec], out_specs=[blk, vec],
        out_shape=[jax.ShapeDtypeStruct((l, d_pool), BF16), jax.ShapeDtypeStruct((1, d_pool), F32)],
        compiler_params=_params("parallel", "arbitrary"),
    )(dmixed, raw, scale)


def _group_mm_nt(a, w, *, name, tm=1024):
    l, d_pool = a.shape
    pg = d_pool // GROUPS
    tm = _tile(l, tm, 16)

    def body(a_ref, w_ref, o_ref):
        o_ref[...] = _dot(a_ref[...], w_ref[...], tb=True)

    blk = pl.BlockSpec((tm, pg), lambda i, g: (i, g))
    return pl.pallas_call(
        body, name=name, grid=(l // tm, GROUPS),
        in_specs=[blk, pl.BlockSpec((None, pg, pg), lambda i, g: (g, 0, 0))], out_specs=blk,
        out_shape=jax.ShapeDtypeStruct((l, d_pool), F32), compiler_params=_params("parallel", "parallel"),
    )(a, w)


def _group_mm_tn(a, b, *, name, tk=1024):
    l, d_pool = a.shape
    pg = d_pool // GROUPS
    tk = _tile(l, tk, 16)

    def body(a_ref, b_ref, o_ref):
        kk = pl.program_id(1)
        part = _dot(a_ref[...], b_ref[...], ta=True)

        @pl.when(kk == 0)
        def _():
            o_ref[...] = part

        @pl.when(kk > 0)
        def _():
            o_ref[...] += part

    blk = pl.BlockSpec((tk, pg), lambda g, kk: (kk, g))
    return pl.pallas_call(
        body, name=name, grid=(GROUPS, l // tk), in_specs=[blk, blk],
        out_specs=pl.BlockSpec((None, pg, pg), lambda g, kk: (g, 0, 0)),
        out_shape=jax.ShapeDtypeStruct((GROUPS, pg, pg), F32), compiler_params=_params("parallel", "arbitrary"),
    )(a, b)


def _gnorm_fwd(y, proj, wn, d_mix, *, name, tl=512):
    l, d_ssm = y.shape
    gw = d_ssm // GROUPS
    tl = _tile(l, tl, 16)

    def body(y_ref, z_ref, w_ref, o_ref):
        z = z_ref[...]
        gv = y_ref[...] * (z * _sigmoid(z))
        r = lax.rsqrt(jnp.mean(gv * gv, axis=-1, keepdims=True) + NORM_EPS)
        o_ref[...] = (gv * r * w_ref[...]).astype(o_ref.dtype)

    blk = pl.BlockSpec((tl, gw), lambda i, g: (i, g))
    return pl.pallas_call(
        body, name=name, grid=(l // tl, GROUPS), in_specs=[blk, blk, pl.BlockSpec((1, gw), lambda i, g: (0, g))],
        out_specs=blk, out_shape=jax.ShapeDtypeStruct((l, d_mix), BF16), compiler_params=_params("parallel", "parallel"),
    )(y, proj, wn)


def _gnorm_bwd(dmixed, y, proj, wn, *, name, tl=512):
    l, d_ssm = y.shape
    gw = d_ssm // GROUPS
    tl = _tile(l, tl, 16)

    def body(d_ref, y_ref, z_ref, w_ref, dy_ref, dz_ref, dw_ref):
        i = pl.program_id(1)
        z, yv, dv = z_ref[...], y_ref[...], d_ref[...]
        s = _sigmoid(z)
        sz = z * s
        gv = yv * sz
        r = lax.rsqrt(jnp.mean(gv * gv, axis=-1, keepdims=True) + NORM_EPS)
        gh = gv * r
        dgh = dv * w_ref[...]
        dg = r * (dgh - gh * jnp.mean(dgh * gh, axis=-1, keepdims=True))
        dy_ref[...] = dg * sz
        dz_ref[...] = (dg * yv * (s * (1.0 + z * (1.0 - s)))).astype(dz_ref.dtype)
        part = jnp.sum(dv * gh, axis=0, keepdims=True)

        @pl.when(i == 0)
        def _():
            dw_ref[...] = part

        @pl.when(i > 0)
        def _():
            dw_ref[...] += part

    blk = pl.BlockSpec((tl, gw), lambda g, i: (i, g))
    vec = pl.BlockSpec((1, gw), lambda g, i: (0, g))
    return pl.pallas_call(
        body, name=name, grid=(GROUPS, l // tl), in_specs=[blk, blk, blk, vec], out_specs=[blk, blk, vec],
        out_shape=[jax.ShapeDtypeStruct((l, d_ssm), F32), jax.ShapeDtypeStruct(proj.shape, BF16),
                   jax.ShapeDtypeStruct((1, d_ssm), F32)],
        compiler_params=_params("parallel", "arbitrary"),
    )(dmixed, y, proj, wn)


def _split3(v):
    hi = v.astype(BF16)
    r1 = v - hi.astype(F32)
    mid = r1.astype(BF16)
    lo = (r1 - mid.astype(F32)).astype(BF16)
    return hi, mid, lo


def _tri_dot(tri, v):
    hi, mid, lo = _split3(v)
    d = lambda p: jnp.dot(tri, p, preferred_element_type=F32)
    return d(hi) + d(mid) + d(lo)


def _dot_tri(v, tri):
    hi, mid, lo = _split3(v)
    d = lambda p: jnp.dot(p, tri, preferred_element_type=F32)
    return d(hi) + d(mid) + d(lo)


def _split2(v):
    hi = v.astype(BF16)
    return hi, (v - hi.astype(F32)).astype(BF16)


def _scan_specs(hpg, nc, order):
    gw, q = hpg * HEAD_DIM, CHUNK
    b_off, c_off = (GROUPS * gw) // STATE, (GROUPS * gw) // STATE + GROUPS
    xs = pl.BlockSpec((q, gw), lambda g, c: (order(c), g))
    bm = pl.BlockSpec((q, STATE), lambda g, c: (order(c), b_off + g))
    cm = pl.BlockSpec((q, STATE), lambda g, c: (order(c), c_off + g))
    dtt = pl.BlockSpec((None, hpg, q), lambda g, c: (g, 0, order(c)))
    vec = pl.BlockSpec((1, gw), lambda g, c: (0, g))
    hcol = pl.BlockSpec((None, hpg, 1), lambda g, c: (g, 0, 0))
    st = pl.BlockSpec((None, None, STATE, gw), lambda g, c: (g, order(c), 0, 0))
    return xs, bm, cm, dtt, vec, hcol, st


def _scan_common(dtx_ref, dtrt_ref, biasx_ref, biast_ref, alogx_ref, alogt_ref):
    q = CHUNK
    dt = _softplus(dtx_ref[...] + biasx_ref[...])
    a = -jnp.exp(alogx_ref[...])
    dtt = _softplus(dtrt_ref[...] + biast_ref[...])
    at = -jnp.exp(alogt_ref[...])
    row = lax.broadcasted_iota(jnp.int32, (q, q), 0)
    col = lax.broadcasted_iota(jnp.int32, (q, q), 1)
    lower = (row >= col).astype(BF16)
    upper = (row <= col).astype(BF16)
    acum = _tri_dot(lower, dt * a)
    acum_t = _dot_tri(dtt * at, upper)
    return dt, a, acum, acum_t, row, col, upper


def _scan_fwd(xbc, dtx, dtrt, biasx, biast, alogx, alogt, dskx, *, name, ride=None):
    l = xbc.shape[0]
    hpg = dtrt.shape[1]
    gw, q, nc = hpg * HEAD_DIM, CHUNK, l // CHUNK
    assert hpg % 2 == 0
    xs_s, bm_s, cm_s, dtt_s, vec, hcol, st_s = _scan_specs(hpg, nc, lambda c: c)

    def body(xs_ref, b_ref, c_ref, dtx_ref, dtrt_ref, biasx_ref, biast_ref, alogx_ref, alogt_ref, dsk_ref,
             y_ref, st_ref, h_ref):
        @pl.when(pl.program_id(1) == 0)
        def _():
            h_ref[...] = jnp.zeros_like(h_ref)

        dt, _, acum, acum_t, row, col, _ = _scan_common(dtx_ref, dtrt_ref, biasx_ref, biast_ref, alogx_ref, alogt_ref)
        causal = row >= col
        xs = xs_ref[...]
        xdt = xs * dt
        xdtb = xdt.astype(BF16)
        bmat, cmat = b_ref[...], c_ref[...]
        cb = _dot(cmat, bmat, tb=True)
        h_all = h_ref[...]
        st_ref[...] = h_all
        rest = _dot(cmat, h_all) * jnp.exp(acum) + dsk_ref[...] * xs
        first = lax.broadcasted_iota(jnp.int32, (q, 2 * HEAD_DIM), 1) < HEAD_DIM
        for jp in range(hpg // 2):
            pair = slice(2 * jp * HEAD_DIM, (2 * jp + 2) * HEAD_DIM)
            xp = xdtb[:, pair]
            ys = []
            for j in (2 * jp, 2 * jp + 1):
                a_col, a_row = acum[:, j * HEAD_DIM:j * HEAD_DIM + 1], acum_t[j:j + 1, :]
                lm = jnp.exp(jnp.where(causal, a_col - a_row, NEG_BIG))
                ys.append(_dot(cb * lm, xp))
            y_ref[:, pair] = jnp.where(first, ys[0], ys[1]) + rest[:, pair]
        a_last = acum[q - 1:q, :]
        h_ref[...] = jnp.exp(a_last) * h_all + _dot(bmat, xdt * jnp.exp(a_last - acum), ta=True)

    return _call(
        body, name=name, grid=(GROUPS, nc),
        in_specs=[xs_s, bm_s, cm_s, xs_s, dtt_s, vec, hcol, vec, hcol, vec], out_specs=[xs_s, st_s],
        out_shape=[jax.ShapeDtypeStruct((l, GROUPS * gw), F32), jax.ShapeDtypeStruct((GROUPS, nc, STATE, gw), F32)],
        scratch=[pltpu.VMEM((STATE, gw), F32)], sem=("parallel", "arbitrary"),
        args=(xbc, xbc, xbc, dtx, dtrt, biasx, biast, alogx, alogt, dskx), ride=ride)


def _scan_bwd(xbc, dtx, dtrt, biasx, biast, alogx, alogt, dskx, states, dy, *, name, ride=None):
    l = xbc.shape[0]
    hpg = dtrt.shape[1]
    gw, q, nc = hpg * HEAD_DIM, CHUNK, l // CHUNK
    rev = lambda c: nc - 1 - c
    xs_s, bm_s, cm_s, dtt_s, vec, hcol, st_s = _scan_specs(hpg, nc, rev)
    bc_s = pl.BlockSpec((q, STATE), lambda g, c: (rev(c), g))

    def body(xs_ref, b_ref, c_ref, dtx_ref, dtrt_ref, biasx_ref, biast_ref, alogx_ref, alogt_ref, dsk_ref, st_ref, dy_ref,
             dxs_ref, db_ref, dc_ref, ddt_ref, dbias_ref, dalog_ref, ddsk_ref, dh_ref):
        step = pl.program_id(1)

        @pl.when(step == 0)
        def _():
            dh_ref[...] = jnp.zeros_like(dh_ref)

        dt, a, acum, acum_t, row, col, upper = _scan_common(dtx_ref, dtrt_ref, biasx_ref, biast_ref, alogx_ref, alogt_ref)
        causal, anti, strict = row >= col, row <= col, row > col
        xs, dyv = xs_ref[...], dy_ref[...]
        xdt = xs * dt
        xdtb, dyb = xdt.astype(BF16), dyv.astype(BF16)
        bmat, cmat = b_ref[...], c_ref[...]
        cb = _dot(cmat, bmat, tb=True)
        cbt = _dot(bmat, cmat, tb=True)
        h_all, dh = st_ref[...], dh_ref[...]
        a_last = acum[q - 1:q, :]
        ea, ee, gam = jnp.exp(acum), jnp.exp(a_last - acum), jnp.exp(a_last)
        zc = _dot(cmat, h_all)
        bdh = _dot(bmat, dh)
        dz = dyv * ea
        xe = xdt * ee
        dc_state = _dot(dz, h_all, tb=True)
        db_state = _dot(xe, dh, tb=True)
        dh_ref[...] = gam * dh + _dot(cmat, dz, ta=True)
        lane = lax.broadcasted_iota(jnp.int32, (q, 2 * HEAD_DIM), 1)
        first = lane < HEAD_DIM
        ones = jnp.ones((q // 2, 2 * HEAD_DIM), BF16)
        dcb = jnp.zeros((q, q), F32)
        dxdt_parts, da_parts = [], []
        for jp in range(hpg // 2):
            pair = slice(2 * jp * HEAD_DIM, (2 * jp + 2) * HEAD_DIM)
            xp, dyp = xdtb[:, pair], dyb[:, pair]
            dxs_r, das_r = [], []
            for r, j in enumerate((2 * jp, 2 * jp + 1)):
                seg = acum[:, j * HEAD_DIM:j * HEAD_DIM + 1] - acum_t[j:j + 1, :]
                lm = jnp.exp(jnp.where(causal, seg, NEG_BIG))
                lmt = jnp.exp(jnp.where(anti, -seg, NEG_BIG))
                mm, mmt = cb * lm, cbt * lmt
                keep = first if r == 0 else jnp.logical_not(first)
                dyk = jnp.where(keep, dyp, 0)
                dxs_r.append(_dot(mmt, dyp))
                dm = _dot(dyk, xp, tb=True)
                dcb = dcb + dm * lm
                corner = jnp.dot(upper, (dm * mm).astype(BF16), preferred_element_type=F32)
                corner = jnp.where(strict, corner, 0.0)
                chi, clo = _split2(corner[:, :q // 2] + corner[:, q // 2:])
                das_r.append(jnp.dot(chi, ones, preferred_element_type=F32) + jnp.dot(clo, ones, preferred_element_type=F32))
            dxdt_parts.append(jnp.where(first, dxs_r[0], dxs_r[1]))
            da_parts.append(jnp.where(first, das_r[0], das_r[1]))
        dxdt = jnp.concatenate(dxdt_parts, axis=1) + bdh * ee
        da_intra = jnp.concatenate(da_parts, axis=1)
        dxs_ref[...] = dxdt * dt + dsk_ref[...] * dyv
        dc_ref[...] = dc_state + _dot(dcb, bmat)
        db_ref[...] = db_state + _dot(dcb, cmat, ta=True)
        ri = lax.broadcasted_iota(jnp.int32, (gw, gw), 0) // HEAD_DIM
        ci = lax.broadcasted_iota(jnp.int32, (gw, gw), 1) // HEAD_DIM
        blockdiag = (ri == ci).astype(BF16)

        def head_sum(v):
            hi, lo = _split2(v)
            return (jnp.dot(hi, blockdiag, preferred_element_type=F32)
                    + jnp.dot(lo, blockdiag, preferred_element_type=F32))

        xt = xe * bdh
        small = jnp.concatenate([
            jnp.sum(xt, axis=0, keepdims=True) + gam * jnp.sum(dh * h_all, axis=0, keepdims=True),
            jnp.sum(dyv * xs, axis=0, keepdims=True), jnp.zeros((6, gw), F32)], axis=0)
        small = head_sum(small)
        rows = lax.broadcasted_iota(jnp.int32, (q, 1), 0)
        da_local = head_sum(dyv * zc * ea - xt) + jnp.where(rows == q - 1, small[0:1, :], 0.0)
        hi, lo = _split2(da_local)
        d_dta = (da_intra + jnp.dot(upper, hi, preferred_element_type=F32)
                 + jnp.dot(upper, lo, preferred_element_type=F32))
        d_raw = (d_dta * a + head_sum(dxdt * xs)) * _sigmoid(dtx_ref[...] + biasx_ref[...])
        ddt_ref[...] = d_raw
        p_bias = jnp.sum(d_raw, axis=0, keepdims=True)
        p_alog = jnp.sum(d_dta * dt, axis=0, keepdims=True) * a
        p_dsk = small[1:2, :]

        @pl.when(step == 0)
        def _():
            dbias_ref[...] = p_bias
            dalog_ref[...] = p_alog
            ddsk_ref[...] = p_dsk

        @pl.when(step > 0)
        def _():
            dbias_ref[...] += p_bias
            dalog_ref[...] += p_alog
            ddsk_ref[...] += p_dsk

    wide = jax.ShapeDtypeStruct((l, GROUPS * gw), F32)
    narrow = jax.ShapeDtypeStruct((l, GROUPS * STATE), F32)
    vshape = jax.ShapeDtypeStruct((1, GROUPS * gw), F32)
    return _call(
        body, name=name, grid=(GROUPS, nc),
        in_specs=[xs_s, bm_s, cm_s, xs_s, dtt_s, vec, hcol, vec, hcol, vec, st_s, xs_s],
        out_specs=[xs_s, bc_s, bc_s, xs_s, vec, vec, vec],
        out_shape=[wide, narrow, narrow, wide, vshape, vshape, vshape],
        scratch=[pltpu.VMEM((STATE, gw), F32)], sem=("parallel", "arbitrary"),
        args=(xbc, xbc, xbc, dtx, dtrt, biasx, biast, alogx, alogt, dskx, states, dy), ride=ride)


def _exchange(srcs, scatter, *, name):
    ride = _Ride(srcs, scatter)

    def body(*refs):
        src, dst, sems = refs[:ride.n], refs[ride.n:2 * ride.n], refs[2 * ride.n:]
        ride.start(src, dst, sems)
        ride.finish(src, dst, sems)

    return pl.pallas_call(body, name=name, in_specs=ride.in_specs, out_specs=ride.out_specs, out_shape=ride.out_shape,
                          scratch_shapes=ride.scratch)(*srcs)


def _adamw(parts, w, m, v, *, name, tr=256):
    r, c = w.shape
    tr = _tile(r, tr, 16)
    c1, c2 = 1.0 / (1.0 - ADAM_B1 ** ADAM_STEP), 1.0 / (1.0 - ADAM_B2 ** ADAM_STEP)

    def body(p_ref, w_ref, m_ref, v_ref, g_ref, d_ref, nm_ref, nv_ref):
        g = p_ref[0].astype(F32)
        for p in range(1, N_DEV):
            g = g + p_ref[p].astype(F32)
        nm = ADAM_B1 * m_ref[...] + (1.0 - ADAM_B1) * g
        nv = ADAM_B2 * v_ref[...] + (1.0 - ADAM_B2) * (g * g)
        g_ref[...] = g
        nm_ref[...] = nm
        nv_ref[...] = nv
        d_ref[...] = -ADAM_LR * ((nm * c1) / (jnp.sqrt(nv * c2) + ADAM_EPS) + ADAM_WD * w_ref[...])

    blk = pl.BlockSpec((tr, c), lambda i: (i, 0))
    out = jax.ShapeDtypeStruct((r, c), F32)
    return pl.pallas_call(
        body, name=name, grid=(r // tr,), in_specs=[pl.BlockSpec((N_DEV, tr, c), lambda i: (0, i, 0)), blk, blk, blk],
        out_specs=[blk, blk, blk, blk], out_shape=[out, out, out, out], compiler_params=_params("parallel"),
    )(parts, w, m, v)


def kernel(x, attn_norm_w, w_in, conv_w, conv_b, dt_bias, a_log, d_skip, ssd_norm_w, pool_w, pool_scale, w_out, ffn_norm_w, w_gate, w_up, w_down, final_norm_w, loss_target, m_attn_norm_w, m_w_in, m_conv_w, m_conv_b, m_dt_bias, m_a_log, m_d_skip, m_ssd_norm_w, m_pool_w, m_pool_scale, m_w_out, m_ffn_norm_w, m_w_gate, m_w_up, m_w_down, m_final_norm_w, v_attn_norm_w, v_w_in, v_conv_w, v_conv_b, v_dt_bias, v_a_log, v_d_skip, v_ssd_norm_w, v_pool_w, v_pool_scale, v_w_out, v_ffn_norm_w, v_w_gate, v_w_up, v_w_down, v_final_norm_w):
    l, d = x.shape[1], x.shape[2]
    heads = dt_bias.shape[1]
    hpg = heads // GROUPS
    d_ssm = heads * HEAD_DIM
    conv_ch = conv_b.shape[1]
    d_pool = pool_scale.shape[1]
    pg = d_pool // GROUPS
    d_mix = d_ssm + d_pool
    d_ff = w_gate.shape[2] * N_DEV
    d_in = w_in.shape[2] * N_DEV
    dt_pad = -(-heads // LANE) * LANE
    o_u, o_xbc, o_dt = d_ssm, d_ssm + d_pool, d_ssm + d_pool + conv_ch
    d_inp = o_dt + dt_pad
    tn_in = _tile(d_inp, 2560, LANE)
    g_dt, g_u = d_ssm + conv_ch, d_ssm + conv_ch + heads

    x2, tgt = x[0], loss_target[0]

    gi, gp, gc = _exchange([w_in[0].astype(BF16), pool_w[0].astype(BF16), conv_w[0]], False, name="gather_w_in")
    win = jnp.transpose(gi, (1, 0, 2)).reshape(d, d_in)
    wp = jnp.concatenate([win[:, :d_ssm], win[:, g_u:], win[:, d_ssm:g_dt], win[:, g_dt:g_u],
                          jnp.zeros((d, dt_pad - heads), BF16)], axis=1)
    pw = jnp.transpose(gp, (1, 0, 2, 3)).reshape(GROUPS, pg, pg)
    cw = jnp.transpose(gc, (1, 0, 2)).reshape(CONV_K, conv_ch)

    def expand(vec):
        return jnp.repeat(vec, HEAD_DIM, axis=1)

    def per_head(vec):
        return vec[:, ::HEAD_DIM]

    bias_x, alog_x, dsk_x = expand(dt_bias), expand(a_log), expand(d_skip)
    bias_c, alog_c = dt_bias.reshape(GROUPS, hpg, 1), a_log.reshape(GROUPS, hpg, 1)

    h0 = _rms_fwd(x2, attn_norm_w, name="attn_norm")
    proj, go, gg = _mm(h0, wp, name="in_proj", tm=512, tn=tn_in, tk=d,
                       ride=_Ride([w_out[0].astype(BF16), w_gate[0].astype(BF16)], False))
    wo = go.reshape(d_mix, d)
    xbc = _conv_fwd(proj, o_xbc, cw, conv_b, name="conv_fwd")
    dt_raw = proj[:, o_dt:o_dt + heads]
    dtx = expand(dt_raw)
    dtrt = jnp.transpose(dt_raw.reshape(l, GROUPS, hpg), (1, 2, 0))
    y, states, gu = _scan_fwd(xbc, dtx, dtrt, bias_x, bias_c, alog_x, alog_c, dsk_x, name="ssd_fwd",
                              ride=_Ride([w_up[0].astype(BF16)], False))
    mixed = _gnorm_fwd(y, proj, ssd_norm_w, d_mix, name="gated_norm")
    pooled = _pool_fwd(proj, o_u, d_pool, name="pool_fwd")
    pool_raw, mixed = _pool_mix_fwd(pooled, pw, pool_scale, mixed, d_ssm, name="pool_mix")
    h1 = _mm(mixed, wo, name="out_proj", tm=512, tk=d_mix, residual=x2)
    h1n = _rms_fwd(h1, ffn_norm_w, name="ffn_norm")
    gate, up, act, gd = _gate_up(h1n, gg, gu, name="gate_up", ride=_Ride([w_down[0].astype(BF16)], False))
    h2 = _mm_shards([(act, gd)], name="down_proj", per_step=4, residual=h1)

    loss11, dh2, dh2b, g_final = _loss_head(h2, final_norm_w.reshape(1, d), tgt, name="loss_head")
    dgate, dup = _gate_up_bwd(dh2b, gd, gate, up, name="gate_up_bwd")
    g_wd = _grad_shards(act, dh2b, name="grad_w_down")
    dh1n, r_wd = _mm_shards([(dgate, gg), (dup, gu)], name="dgate_dup", tb=True, ride=_Ride([g_wd], True))
    g_wg = _grad_shards(h1n, dgate, name="grad_w_gate")
    g_wu = _grad_shards(h1n, dup, name="grad_w_up")
    dh1, dh1b, g_ffn = _rms_bwd(h1, ffn_norm_w, dh1n, dh2, name="ffn_norm_bwd")
    dmixed = _mm(dh1b, wo, name="dmixed", tb=True, tk=d)
    g_wo = _mm(mixed, dh1b, name="grad_w_out", ta=True, out_dtype=BF16, tm=1024, tn=1024, tk=2048)
    draw, g_pscale = _pool_scale_bwd(dmixed, d_ssm, pool_raw, pool_scale, name="pool_scale_bwd")
    dpooled = _group_mm_nt(draw, pw, name="dpooled")
    g_pw = _group_mm_tn(pooled, draw, name="grad_pool_w")
    dy, dproj, g_ssdn = _gnorm_bwd(dmixed, y, proj, ssd_norm_w, name="gated_norm_bwd")
    dproj = _pool_bwd(dpooled, dproj, o_u, name="pool_bwd")
    s_pw = jnp.transpose(g_pw.reshape(GROUPS, N_DEV, pg // N_DEV, pg), (1, 0, 2, 3)).astype(BF16)
    dxs, dbm, dcm, ddtx, g_bias, g_alog, g_dsk, r_wg, r_wu, r_wo, r_pw = _scan_bwd(
        xbc, dtx, dtrt, bias_x, bias_c, alog_x, alog_c, dsk_x, states, dy, name="ssd_bwd",
        ride=_Ride([g_wg, g_wu, g_wo.reshape(N_DEV, d_mix // N_DEV, d), s_pw], True))
    n_bc = GROUPS * STATE
    segs = [(dxs, 0, d_ssm), (dbm, d_ssm, n_bc), (dcm, d_ssm + n_bc, n_bc)]
    g_cw, g_cb = [], []
    for si, (dseg, c0, width) in enumerate(segs):
        dproj, b, c_ = _conv_bwd(proj, o_xbc + c0, cw[:, c0:c0 + width], conv_b[:, c0:c0 + width], dseg, dproj,
                                 name=f"conv_bwd{si}")
        g_cw.append(b)
        g_cb.append(c_)
    ddt = jnp.pad(per_head(ddtx), ((0, 0), (0, dt_pad - heads)))
    dproj = _put_cols(ddt, dproj, o_dt, name="put_ddt")
    g_wp = _mm(h0, dproj, name="grad_w_in", ta=True, out_dtype=BF16, tm=512, tn=tn_in, tk=2048)
    g_win = jnp.concatenate([g_wp[:, :d_ssm], g_wp[:, o_xbc:o_dt + heads], g_wp[:, o_u:o_xbc]], axis=1)
    def cols8(g):
        return jnp.transpose(g.reshape(g.shape[0], N_DEV, g.shape[1] // N_DEV), (1, 0, 2))
    dh0, r_in, r_cw = _mm(dproj, wp, name="dh0", tb=True, tk=tn_in,
                          ride=_Ride([cols8(g_win), cols8(jnp.concatenate(g_cw, axis=1))], True))
    dx, _, g_attn = _rms_bwd(x2, attn_norm_w, dh0, dh1, name="attn_norm_bwd")

    smalls = [g_attn, jnp.concatenate(g_cb, axis=1), per_head(g_bias), per_head(g_alog), per_head(g_dsk), g_ssdn,
              g_pscale, g_ffn, g_final]
    small_w = [attn_norm_w, conv_b, dt_bias, a_log, d_skip, ssd_norm_w, pool_scale, ffn_norm_w, final_norm_w.reshape(1, d)]
    small_m = [m_attn_norm_w, m_conv_b, m_dt_bias, m_a_log, m_d_skip, m_ssd_norm_w, m_pool_scale, m_ffn_norm_w,
               m_final_norm_w.reshape(1, d)]
    small_v = [v_attn_norm_w, v_conv_b, v_dt_bias, v_a_log, v_d_skip, v_ssd_norm_w, v_pool_scale, v_ffn_norm_w,
               v_final_norm_w.reshape(1, d)]
    sizes = [s.shape[1] for s in smalls]
    n_small = sum(sizes)
    n_pad = -(-n_small // (16 * LANE)) * (16 * LANE)
    rows = n_pad // LANE
    def pack(vs):
        return jnp.pad(jnp.concatenate(vs, axis=1), ((0, 0), (0, n_pad - n_small))).reshape(rows, LANE)
    (r_small,) = _exchange([pack(smalls)], False, name="gather_small_grads")

    def big(parts, w, m, v, nm):
        shp = w.shape
        r2 = lambda t: t.reshape(-1, shp[-1])
        outs = _adamw(parts.reshape(N_DEV, -1, shp[-1]), r2(w), r2(m), r2(v), name=nm)
        return [o.reshape(shp) for o in outs]

    res = {
        "w_in": big(r_in, w_in, m_w_in, v_w_in, "adamw_w_in"),
        "conv_w": big(r_cw, conv_w, m_conv_w, v_conv_w, "adamw_conv_w"),
        "pool_w": big(r_pw, pool_w, m_pool_w, v_pool_w, "adamw_pool_w"),
        "w_out": big(r_wo, w_out, m_w_out, v_w_out, "adamw_w_out"),
        "w_gate": big(r_wg, w_gate, m_w_gate, v_w_gate, "adamw_w_gate"),
        "w_up": big(r_wu, w_up, m_w_up, v_w_up, "adamw_w_up"),
        "w_down": big(r_wd, w_down, m_w_down, v_w_down, "adamw_w_down"),
    }
    s_out = _adamw(r_small, pack(small_w), pack(small_m), pack(small_v), name="adamw_small")
    names = ["attn_norm_w", "conv_b", "dt_bias", "a_log", "d_skip", "ssd_norm_w", "pool_scale", "ffn_norm_w", "final_norm_w"]
    offs = [sum(sizes[:i]) for i in range(len(sizes))]
    for i, nm in enumerate(names):
        shp = (d,) if nm == "final_norm_w" else (1, sizes[i])
        res[nm] = [o.reshape(1, n_pad)[:, offs[i]:offs[i] + sizes[i]].reshape(shp) for o in s_out]

    loss = lax.psum(loss11[0, 0], ("x", "y", "c"))
    order = ["attn_norm_w", "w_in", "conv_w", "conv_b", "dt_bias", "a_log", "d_skip", "ssd_norm_w", "pool_w", "pool_scale",
             "w_out", "ffn_norm_w", "w_gate", "w_up", "w_down", "final_norm_w"]
    outs = [loss, dx.reshape(x.shape)]
    for part in range(4):
        outs += [res[nm][part] for nm in order]
    return tuple(outs)
```

```python
import functools

import jax
import jax.numpy as jnp
from jax import lax
from jax.experimental import pallas as pl
from jax.experimental.pallas import tpu as pltpu

F32 = jnp.float32
BF16 = jnp.bfloat16

NORM_EPS = 1e-5
HEAD_DIM = 64
STATE = 128
CHUNK = 256
GROUPS = 4
CONV_K = 4
POOL_WINDOWS = (2, 4, 8, 16)
POOL_HALO = 16
CONV_HALO = 8
LANE = 128
N_DEV = 8
ADAM_LR, ADAM_B1, ADAM_B2, ADAM_EPS, ADAM_WD, ADAM_STEP = 0.001, 0.9, 0.999, 1e-08, 0.01, 10
NEG_BIG = -1e30
EPILOGUE_ROWS = 256
MESH_ID = pl.DeviceIdType.MESH


def _tile(n, pref, mult):
    if n <= pref:
        return n
    t = (pref // mult) * mult
    while t >= mult:
        if n % t == 0:
            return t
        t -= mult
    raise ValueError(f"no tile for {n} (pref {pref}, mult {mult})")


def _params(*sem):
    return pltpu.CompilerParams(dimension_semantics=sem)


ANY = pl.BlockSpec(memory_space=pl.ANY)


def _peer(k):
    x, y, c = lax.axis_index("x"), lax.axis_index("y"), lax.axis_index("c")
    return (1 - x if k & 4 else x, 1 - y if k & 2 else y, 1 - c if k & 1 else c)


def _my_index():
    return 4 * lax.axis_index("x") + 2 * lax.axis_index("y") + lax.axis_index("c")


class _Ride:
    def __init__(self, srcs, scatter):
        self.srcs, self.scatter, self.n = list(srcs), scatter, len(srcs)
        self.two_level = not scatter
        self.in_specs = [ANY] * self.n
        self.out_specs = [ANY] * self.n
        self.out_shape = [jax.ShapeDtypeStruct((N_DEV,) + tuple(s.shape[1:] if scatter else s.shape), s.dtype)
                          for s in self.srcs]
        self.scratch = [pltpu.SemaphoreType.DMA((self.n, N_DEV - 1)), pltpu.SemaphoreType.DMA((self.n, N_DEV - 1)),
                        pltpu.SemaphoreType.DMA((self.n,))]

    def _copies(self, src, dst, sems):
        send_sems, recv_sems, local_sems = sems
        me = _my_index()
        local = [pltpu.make_async_copy(src[t].at[me] if self.scatter else src[t], dst[t].at[me], local_sems.at[t])
                 for t in range(self.n)]
        remote = {}
        for k in range(1, N_DEV):
            peer, pidx = _peer(k), me ^ k
            remote[k] = [pltpu.make_async_remote_copy(
                src_ref=src[t].at[pidx] if self.scatter else src[t], dst_ref=dst[t].at[me],
                send_sem=send_sems.at[t, k - 1], recv_sem=recv_sems.at[t, k - 1],
                device_id=peer, device_id_type=MESH_ID) for t in range(self.n)]
        return local, remote

    def _forwards(self, dst, sems):
        send_sems, recv_sems, _ = sems
        me = _my_index()
        return {k: [pltpu.make_async_remote_copy(
            src_ref=dst[t].at[me ^ k], dst_ref=dst[t].at[me ^ k],
            send_sem=send_sems.at[t, (k ^ 1) - 1], recv_sem=recv_sems.at[t, (k ^ 1) - 1],
            device_id=_peer(1), device_id_type=MESH_ID) for t in range(self.n)] for k in (2, 4, 6)}

    def start(self, src, dst, sems):
        local, remote = self._copies(src, dst, sems)
        for cp in local:
            cp.start()
        for k in (2, 4, 6, 1) if self.two_level else range(1, N_DEV):
            for cp in remote[k]:
                cp.start()

    def finish(self, src, dst, sems):
        local, remote = self._copies(src, dst, sems)
        if self.two_level:
            fwd = self._forwards(dst, sems)
            for k in (2, 4, 6):
                for t in range(self.n):
                    remote[k][t].wait_recv()
                    fwd[k][t].start()
            for k in (1, 3, 5, 7):
                for cp in remote[k]:
                    cp.wait_recv()
            for k in (1, 2, 4, 6):
                for cp in remote[k]:
                    cp.wait_send()
            for k in (2, 4, 6):
                for cp in fwd[k]:
                    cp.wait_send()
        else:
            for k in range(1, N_DEV):
                for cp in remote[k]:
                    cp.wait_send()
            for k in range(1, N_DEV):
                for cp in remote[k]:
                    cp.wait_recv()
        for lc in local:
            lc.wait()


def _call(body, *, name, grid, in_specs, out_specs, out_shape, args, sem, scratch=(), ride=None):
    in_specs, out_specs, out_shape, scratch = list(in_specs), list(out_specs), list(out_shape), list(scratch)
    if ride is None:
        return pl.pallas_call(body, name=name, grid=grid, in_specs=in_specs, out_specs=out_specs, out_shape=out_shape,
                              scratch_shapes=scratch, compiler_params=_params(*sem))(*args)
    n_in, n_out, n_sc, nr = len(in_specs), len(out_specs), len(scratch), ride.n

    def full(*refs):
        ins, csrc = refs[:n_in], refs[n_in:n_in + nr]
        o0 = n_in + nr
        outs, cdst = refs[o0:o0 + n_out], refs[o0 + n_out:o0 + n_out + nr]
        s0 = o0 + n_out + nr
        sc, csem = refs[s0:s0 + n_sc], refs[s0 + n_sc:]
        ids = [pl.program_id(ax) for ax in range(len(grid))]
        first = functools.reduce(jnp.logical_and, [i == 0 for i in ids])
        last = functools.reduce(jnp.logical_and, [i == g - 1 for i, g in zip(ids, grid)])

        @pl.when(first)
        def _():
            ride.start(csrc, cdst, csem)

        body(*ins, *outs, *sc)

        @pl.when(last)
        def _():
            ride.finish(csrc, cdst, csem)

    return pl.pallas_call(
        full, name=name, grid=grid, in_specs=in_specs + ride.in_specs, out_specs=out_specs + ride.out_specs,
        out_shape=out_shape + ride.out_shape, scratch_shapes=scratch + ride.scratch,
        compiler_params=_params(*(["arbitrary"] * len(grid))),
    )(*args, *ride.srcs)


def _sigmoid(x):
    return 1.0 / (1.0 + jnp.exp(-x))


def _softplus(x):
    return jnp.maximum(x, 0.0) + jnp.log(1.0 + jnp.exp(-jnp.abs(x)))


def _dot(a, b, ta=False, tb=False):
    dn = (((0 if ta else 1,), (1 if tb else 0,)), ((), ()))
    return lax.dot_general(a.astype(BF16), b.astype(BF16), dn, preferred_element_type=F32)


def _mm(a, b, *, name, ta=False, tb=False, out_dtype=F32, tm=1024, tn=1024, tk=2048, residual=None, ride=None):
    m = a.shape[1] if ta else a.shape[0]
    k = a.shape[0] if ta else a.shape[1]
    n = b.shape[0] if tb else b.shape[1]
    assert k == (b.shape[1] if tb else b.shape[0])
    tm, tn = _tile(m, tm, LANE if ta else 8), _tile(n, tn, LANE)
    tk = _tile(k, tk, LANE if not (ta and tb) else LANE)
    nk = k // tk
    has_res = residual is not None

    def body(*refs):
        a_ref, b_ref = refs[0], refs[1]
        r_ref = refs[2] if has_res else None
        o_ref = refs[3] if has_res else refs[2]
        part = _dot(a_ref[...], b_ref[...], ta, tb)

        def finish(acc):
            if has_res:
                acc = acc + r_ref[...]
            o_ref[...] = acc.astype(o_ref.dtype)

        if nk == 1:
            finish(part)
        else:
            acc_ref = refs[-1]
            kk = pl.program_id(2)

            @pl.when(kk == 0)
            def _():
                acc_ref[...] = part

            @pl.when(kk > 0)
            def _():
                acc_ref[...] += part

            @pl.when(kk == nk - 1)
            def _():
                finish(acc_ref[...])

    a_spec = pl.BlockSpec((tk, tm), lambda i, j, kk: (kk, i)) if ta else pl.BlockSpec((tm, tk), lambda i, j, kk: (i, kk))
    b_spec = pl.BlockSpec((tn, tk), lambda i, j, kk: (j, kk)) if tb else pl.BlockSpec((tk, tn), lambda i, j, kk: (kk, j))
    o_spec = pl.BlockSpec((tm, tn), lambda i, j, kk: (i, j))
    in_specs = [a_spec, b_spec] + ([o_spec] if has_res else [])
    args = (a, b) + ((residual,) if has_res else ())
    res = _call(body, name=name, grid=(m // tm, n // tn, nk), in_specs=in_specs, out_specs=[o_spec],
                out_shape=[jax.ShapeDtypeStruct((m, n), out_dtype)], args=args,
                scratch=[pltpu.VMEM((tm, tn), F32)] if nk > 1 else [], sem=("parallel", "parallel", "arbitrary"), ride=ride)
    return res[0] if ride is None else res


def _gate_up(h, wg, wu, *, name, tm=1024, ride=None):
    m, k = h.shape
    ns, _, c = wg.shape
    tm = _tile(m, tm, 16)

    tr = _tile(tm, EPILOGUE_ROWS, 16)

    def body(h_ref, wg_ref, wu_ref, g_ref, u_ref, a_ref):
        for r in range(tm // tr):
            rows = slice(r * tr, (r + 1) * tr)
            hb = h_ref[rows, :]
            g = _dot(hb, wg_ref[...])
            u = _dot(hb, wu_ref[...])
            g_ref[rows, :] = g
            u_ref[rows, :] = u
            a_ref[rows, :] = (g * _sigmoid(g) * u).astype(a_ref.dtype)

    o_spec = pl.BlockSpec((None, tm, c), lambda i, s: (s, i, 0))
    w_spec = pl.BlockSpec((None, k, c), lambda i, s: (s, 0, 0))
    return _call(body, name=name, grid=(m // tm, ns),
                 in_specs=[pl.BlockSpec((tm, k), lambda i, s: (i, 0)), w_spec, w_spec], out_specs=[o_spec, o_spec, o_spec],
                 out_shape=[jax.ShapeDtypeStruct((ns, m, c), F32), jax.ShapeDtypeStruct((ns, m, c), F32),
                            jax.ShapeDtypeStruct((ns, m, c), BF16)],
                 args=(h, wg, wu), sem=("parallel", "parallel"), ride=ride)


def _gate_up_bwd(dh, wd, g, u, *, name, tm=1024):
    m, k = dh.shape
    ns, c, _ = wd.shape
    tm = _tile(m, tm, 16)

    tr = _tile(tm, EPILOGUE_ROWS, 16)

    def body(dh_ref, wd_ref, g_ref, u_ref, dg_ref, du_ref):
        for r in range(tm // tr):
            rows = slice(r * tr, (r + 1) * tr)
            da = _dot(dh_ref[rows, :], wd_ref[...], tb=True)
            g = g_ref[rows, :]
            s = _sigmoid(g)
            dg_ref[rows, :] = (da * u_ref[rows, :] * (s * (1.0 + g * (1.0 - s)))).astype(dg_ref.dtype)
            du_ref[rows, :] = (da * (g * s)).astype(du_ref.dtype)

    o_spec = pl.BlockSpec((None, tm, c), lambda i, s: (s, i, 0))
    return _call(body, name=name, grid=(m // tm, ns),
                 in_specs=[pl.BlockSpec((tm, k), lambda i, s: (i, 0)), pl.BlockSpec((None, c, k), lambda i, s: (s, 0, 0)),
                           o_spec, o_spec],
                 out_specs=[o_spec, o_spec],
                 out_shape=[jax.ShapeDtypeStruct((ns, m, c), BF16), jax.ShapeDtypeStruct((ns, m, c), BF16)],
                 args=(dh, wd, g, u), sem=("parallel", "parallel"))


def _mm_shards(pairs, *, name, tb=False, residual=None, out_dtype=F32, tm=1024, tn=1024, per_step=2, ride=None):
    ns, m, c = pairs[0][0].shape
    n = pairs[0][1].shape[1] if tb else pairs[0][1].shape[2]
    tm, tn = _tile(m, tm, 16), _tile(n, tn, LANE)
    npair = len(pairs)
    has_res = residual is not None
    assert ns % per_step == 0
    nsteps = ns // per_step

    def body(*refs):
        o_ref, acc_ref = refs[-2], refs[-1]
        s = pl.program_id(2)
        part = None
        for p in range(npair):
            for q in range(per_step):
                term = _dot(refs[2 * p][q], refs[2 * p + 1][q], tb=tb)
                part = term if part is None else part + term

        @pl.when(s == 0)
        def _():
            acc_ref[...] = part

        @pl.when(s > 0)
        def _():
            acc_ref[...] += part

        @pl.when(s == nsteps - 1)
        def _():
            acc = acc_ref[...]
            if has_res:
                acc = acc + refs[2 * npair][...]
            o_ref[...] = acc.astype(o_ref.dtype)

    a_spec = pl.BlockSpec((per_step, tm, c), lambda i, j, s: (s, i, 0))
    b_spec = (pl.BlockSpec((per_step, tn, c), lambda i, j, s: (s, j, 0)) if tb
              else pl.BlockSpec((per_step, c, tn), lambda i, j, s: (s, 0, j)))
    o_spec = pl.BlockSpec((tm, tn), lambda i, j, s: (i, j))
    args = [t for pr in pairs for t in pr] + ([residual] if has_res else [])
    res = _call(body, name=name, grid=(m // tm, n // tn, nsteps),
                in_specs=[a_spec, b_spec] * npair + ([o_spec] if has_res else []), out_specs=[o_spec],
                out_shape=[jax.ShapeDtypeStruct((m, n), out_dtype)], args=args, scratch=[pltpu.VMEM((tm, tn), F32)],
                sem=("parallel", "parallel", "arbitrary"), ride=ride)
    return res[0] if ride is None else res


def _grad_shards(a, b, *, name, tk=2048):
    a3 = a.ndim == 3
    ns, l, c = a.shape if a3 else b.shape
    d = (b if a3 else a).shape[1]
    tk = _tile(l, tk, 16)
    nk = l // tk

    def body(a_ref, b_ref, o_ref, acc_ref):
        kk = pl.program_id(1)
        part = _dot(a_ref[...], b_ref[...], ta=True)

        @pl.when(kk == 0)
        def _():
            acc_ref[...] = part

        @pl.when(kk > 0)
        def _():
            acc_ref[...] += part

        @pl.when(kk == nk - 1)
        def _():
            o_ref[...] = acc_ref[...].astype(o_ref.dtype)

    s3 = pl.BlockSpec((None, tk, c), lambda s, kk: (s, kk, 0))
    s2 = pl.BlockSpec((tk, d), lambda s, kk: (kk, 0))
    out = (c, d) if a3 else (d, c)
    return pl.pallas_call(
        body, name=name, grid=(ns, nk), in_specs=[s3, s2] if a3 else [s2, s3],
        out_specs=pl.BlockSpec((None,) + out, lambda s, kk: (s, 0, 0)),
        out_shape=jax.ShapeDtypeStruct((ns,) + out, BF16), scratch_shapes=[pltpu.VMEM(out, F32)],
        compiler_params=_params("parallel", "arbitrary"),
    )(a, b)


def _rms_fwd(x, w, *, name, tl=512):
    l, d = x.shape
    tl = _tile(l, tl, 16)

    def body(x_ref, w_ref, o_ref):
        xv = x_ref[...]
        r = lax.rsqrt(jnp.mean(xv * xv, axis=-1, keepdims=True) + NORM_EPS)
        o_ref[...] = (xv * r * w_ref[...]).astype(o_ref.dtype)

    row = pl.BlockSpec((tl, d), lambda i: (i, 0))
    return pl.pallas_call(
        body, name=name, grid=(l // tl,), in_specs=[row, pl.BlockSpec((1, d), lambda i: (0, 0))], out_specs=row,
        out_shape=jax.ShapeDtypeStruct((l, d), BF16), compiler_params=_params("parallel"),
    )(x, w)


def _rms_bwd(x, w, dy, add, out_dtype, *, name, tl=256):
    l, d = x.shape
    tl = _tile(l, tl, 16)

    def body(x_ref, w_ref, dy_ref, add_ref, dx_ref, dw_ref):
        i = pl.program_id(0)
        xv = x_ref[...]
        r = lax.rsqrt(jnp.mean(xv * xv, axis=-1, keepdims=True) + NORM_EPS)
        xh = xv * r
        dyv = dy_ref[...].astype(F32)
        dxh = dyv * w_ref[...]
        dx = r * (dxh - xh * jnp.mean(dxh * xh, axis=-1, keepdims=True)) + add_ref[...].astype(F32)
        dx_ref[...] = dx.astype(dx_ref.dtype)
        part = jnp.sum(dyv * xh, axis=0, keepdims=True)

        @pl.when(i == 0)
        def _():
            dw_ref[...] = part

        @pl.when(i > 0)
        def _():
            dw_ref[...] += part

    row = pl.BlockSpec((tl, d), lambda i: (i, 0))
    vec = pl.BlockSpec((1, d), lambda i: (0, 0))
    return pl.pallas_call(
        body, name=name, grid=(l // tl,), in_specs=[row, vec, row, row], out_specs=[row, vec],
        out_shape=[jax.ShapeDtypeStruct((l, d), out_dtype), jax.ShapeDtypeStruct((1, d), F32)],
        compiler_params=_params("arbitrary"),
    )(x, w, dy, add)


def _loss_head(h, w, target, *, name, tl=256):
    l, d = h.shape
    tl = _tile(l, tl, 16)

    def body(x_ref, w_ref, t_ref, loss_ref, dxb_ref, dw_ref):
        i = pl.program_id(0)
        xv = x_ref[...]
        r = lax.rsqrt(jnp.mean(xv * xv, axis=-1, keepdims=True) + NORM_EPS)
        xh = xv * r
        wv = w_ref[...]
        diff = xh * wv - t_ref[...]
        lpart = 0.5 * jnp.sum(jnp.mean(diff * diff, axis=-1, keepdims=True), axis=0, keepdims=True)
        dyv = diff * (1.0 / d)
        dxh = dyv * wv
        dx = r * (dxh - xh * jnp.mean(dxh * xh, axis=-1, keepdims=True))
        dxb_ref[...] = dx.astype(BF16)
        part = jnp.sum(dyv * xh, axis=0, keepdims=True)

        @pl.when(i == 0)
        def _():
            dw_ref[...] = part
            loss_ref[...] = lpart

        @pl.when(i > 0)
        def _():
            dw_ref[...] += part
            loss_ref[...] += lpart

    row = pl.BlockSpec((tl, d), lambda i: (i, 0))
    vec = pl.BlockSpec((1, d), lambda i: (0, 0))
    one = pl.BlockSpec((1, 1), lambda i: (0, 0))
    return pl.pallas_call(
        body, name=name, grid=(l // tl,), in_specs=[row, vec, row], out_specs=[one, row, vec],
        out_shape=[jax.ShapeDtypeStruct((1, 1), F32), jax.ShapeDtypeStruct((l, d), BF16), jax.ShapeDtypeStruct((1, d), F32)],
        compiler_params=_params("arbitrary"),
    )(h, w, target)


def _conv_pre(ext, w_ref):
    acc = ext * w_ref[CONV_K - 1:CONV_K, :]
    for k in range(CONV_K - 1):
        acc = acc + pltpu.roll(ext, CONV_K - 1 - k, axis=0) * w_ref[k:k + 1, :]
    return acc


def _conv_fwd(proj, col0, cw, cb, *, name, tl=512, tc=512):
    l = proj.shape[0]
    c = cw.shape[1]
    tl, tc = _tile(l, tl, 16), _tile(c, tc, LANE)
    assert col0 % tc == 0
    off, hb = col0 // tc, tl // CONV_HALO

    def body(x_ref, halo_ref, w_ref, b_ref, o_ref):
        i = pl.program_id(0)
        halo = jnp.where(i > 0, halo_ref[...], 0.0)
        ext = jnp.concatenate([halo, x_ref[...]], axis=0)
        pre = _conv_pre(ext, w_ref)[CONV_HALO:] + b_ref[...]
        o_ref[...] = pre * _sigmoid(pre)

    return pl.pallas_call(
        body, name=name, grid=(l // tl, c // tc),
        in_specs=[pl.BlockSpec((tl, tc), lambda i, j: (i, off + j)),
                  pl.BlockSpec((CONV_HALO, tc), lambda i, j: (jnp.maximum(i * hb - 1, 0), off + j)),
                  pl.BlockSpec((CONV_K, tc), lambda i, j: (0, j)), pl.BlockSpec((1, tc), lambda i, j: (0, j))],
        out_specs=pl.BlockSpec((tl, tc), lambda i, j: (i, j)),
        out_shape=jax.ShapeDtypeStruct((l, c), F32), compiler_params=_params("parallel", "parallel"),
    )(proj, proj, cw, cb)


def _conv_bwd(proj, col0, cw, cb, dout, dproj, *, name, tl=512, tc=512):
    l = proj.shape[0]
    c = dout.shape[1]
    tl, tc = _tile(l, tl, 16), _tile(c, tc, LANE)
    assert col0 % tc == 0
    off, hb, nt = col0 // tc, tl // CONV_HALO, l // tl
    n_ext = tl + CONV_HALO

    def body(x_ref, prev_ref, nxt_ref, d_ref, dn_ref, w_ref, b_ref, dproj_in, dx_ref, dw_ref, db_ref):
        del dproj_in
        i = pl.program_id(1)
        last = i == nt - 1
        halo = jnp.where(i > 0, prev_ref[...], 0.0)
        ext = jnp.concatenate([halo, x_ref[...], nxt_ref[...]], axis=0)
        pre = _conv_pre(ext, w_ref)[CONV_HALO:] + b_ref[...]
        dext = jnp.concatenate([d_ref[...], jnp.where(last, 0.0, dn_ref[...])], axis=0)
        s = _sigmoid(pre)
        dpre = dext * (s * (1.0 + pre * (1.0 - s)))
        dx = dpre * w_ref[CONV_K - 1:CONV_K, :]
        for k in range(CONV_K - 1):
            dx = dx + pltpu.roll(dpre, n_ext - (CONV_K - 1 - k), axis=0) * w_ref[k:k + 1, :]
        dx_ref[...] = dx[:tl].astype(dx_ref.dtype)
        dp = dpre[:tl]
        rows = [jnp.sum(dp * pltpu.roll(ext, CONV_K - 1 - k, axis=0)[CONV_HALO:CONV_HALO + tl], axis=0, keepdims=True)
                for k in range(CONV_K - 1)]
        rows.append(jnp.sum(dp * ext[CONV_HALO:CONV_HALO + tl], axis=0, keepdims=True))
        dwp = jnp.concatenate(rows, axis=0)
        dbp = jnp.sum(dp, axis=0, keepdims=True)

        @pl.when(i == 0)
        def _():
            dw_ref[...] = dwp
            db_ref[...] = dbp

        @pl.when(i > 0)
        def _():
            dw_ref[...] += dwp
            db_ref[...] += dbp

    cur = lambda j, i: (i, j)
    nxt = lambda j, i: (jnp.minimum((i + 1) * hb, nt * hb - 1), j)
    return pl.pallas_call(
        body, name=name, grid=(c // tc, nt),
        in_specs=[pl.BlockSpec((tl, tc), lambda j, i: (i, off + j)),
                  pl.BlockSpec((CONV_HALO, tc), lambda j, i: (jnp.maximum(i * hb - 1, 0), off + j)),
                  pl.BlockSpec((CONV_HALO, tc), lambda j, i: (jnp.minimum((i + 1) * hb, nt * hb - 1), off + j)),
                  pl.BlockSpec((tl, tc), cur), pl.BlockSpec((CONV_HALO, tc), nxt),
                  pl.BlockSpec((CONV_K, tc), lambda j, i: (0, j)), pl.BlockSpec((1, tc), lambda j, i: (0, j)), ANY],
        out_specs=[pl.BlockSpec((tl, tc), lambda j, i: (i, off + j)), pl.BlockSpec((CONV_K, tc), lambda j, i: (0, j)),
                   pl.BlockSpec((1, tc), lambda j, i: (0, j))],
        out_shape=[jax.ShapeDtypeStruct(dproj.shape, dproj.dtype), jax.ShapeDtypeStruct((CONV_K, c), F32),
                   jax.ShapeDtypeStruct((1, c), F32)],
        input_output_aliases={7: 0}, compiler_params=_params("parallel", "arbitrary"),
    )(proj, proj, proj, dout, dout, cw, cb, dproj)


def _pool_fwd(proj, col0, d_pool, *, name, tl=256):
    l = proj.shape[0]
    tl = _tile(l, tl, POOL_HALO)
    assert col0 % d_pool == 0
    off, hb, pg = col0 // d_pool, tl // POOL_HALO, d_pool // GROUPS

    def body(u_ref, halo_ref, o_ref):
        i = pl.program_id(0)
        halo = jnp.where(i > 0, halo_ref[...], 0.0)
        ext = jnp.concatenate([halo, u_ref[...]], axis=0)
        t = i * tl + lax.broadcasted_iota(jnp.int32, (tl, 1), 0)
        s, width = ext, 1
        for gi, win in enumerate(POOL_WINDOWS):
            while width < win:
                s = s + pltpu.roll(s, width, axis=0)
                width *= 2
            cnt = jnp.minimum(t + 1, win).astype(F32)
            sl = slice(gi * pg, (gi + 1) * pg)
            o_ref[:, sl] = (s[POOL_HALO:, sl] / cnt - ext[POOL_HALO:, sl]).astype(o_ref.dtype)

    return pl.pallas_call(
        body, name=name, grid=(l // tl,),
        in_specs=[pl.BlockSpec((tl, d_pool), lambda i: (i, off)),
                  pl.BlockSpec((POOL_HALO, d_pool), lambda i: (jnp.maximum(i * hb - 1, 0), off))],
        out_specs=pl.BlockSpec((tl, d_pool), lambda i: (i, 0)),
        out_shape=jax.ShapeDtypeStruct((l, d_pool), BF16), compiler_params=_params("parallel"),
    )(proj, proj)


def _pool_bwd(dp, dproj, col0, *, name, tl=256):
    l, d_pool = dp.shape
    tl = _tile(l, tl, POOL_HALO)
    assert col0 % d_pool == 0
    off, hb, nt, pg = col0 // d_pool, tl // POOL_HALO, l // tl, d_pool // GROUPS
    n_ext = tl + POOL_HALO

    def body(d_ref, nxt_ref, dproj_in, o_ref):
        del dproj_in
        i = pl.program_id(0)
        ext = jnp.concatenate([d_ref[...], jnp.where(i == nt - 1, 0.0, nxt_ref[...])], axis=0)
        t = i * tl + lax.broadcasted_iota(jnp.int32, (n_ext, 1), 0)
        for gi, win in enumerate(POOL_WINDOWS):
            sl = slice(gi * pg, (gi + 1) * pg)
            dg = ext[:, sl]
            s = dg / jnp.minimum(t + 1, win).astype(F32)
            width = 1
            while width < win:
                s = s + pltpu.roll(s, n_ext - width, axis=0)
                width *= 2
            o_ref[:, sl] = (s[:tl] - dg[:tl]).astype(o_ref.dtype)

    return pl.pallas_call(
        body, name=name, grid=(nt,),
        in_specs=[pl.BlockSpec((tl, d_pool), lambda i: (i, 0)),
                  pl.BlockSpec((POOL_HALO, d_pool), lambda i: (jnp.minimum((i + 1) * hb, nt * hb - 1), 0)), ANY],
        out_specs=pl.BlockSpec((tl, d_pool), lambda i: (i, off)),
        out_shape=jax.ShapeDtypeStruct(dproj.shape, dproj.dtype), input_output_aliases={2: 0},
        compiler_params=_params("parallel"),
    )(dp, dp, dproj)


def _put_heads(src, dst, col0, width, *, name, tl=512):
    l, c = src.shape
    tl = _tile(l, tl, 16)
    assert col0 % width == 0 and width % LANE == 0 and c // HEAD_DIM <= width

    def body(s_ref, dst_in, o_ref):
        del dst_in
        pick = (lax.broadcasted_iota(jnp.int32, (c, width), 0)
                == HEAD_DIM * lax.broadcasted_iota(jnp.int32, (c, width), 1)).astype(BF16)
        hi, lo = _split2(s_ref[...])
        o_ref[...] = (jnp.dot(hi, pick, preferred_element_type=F32)
                      + jnp.dot(lo, pick, preferred_element_type=F32)).astype(o_ref.dtype)

    return pl.pallas_call(
        body, name=name, grid=(l // tl,), in_specs=[pl.BlockSpec((tl, c), lambda i: (i, 0)), ANY],
        out_specs=pl.BlockSpec((tl, width), lambda i: (i, col0 // width)),
        out_shape=jax.ShapeDtypeStruct(dst.shape, dst.dtype), input_output_aliases={1: 0},
        compiler_params=_params("parallel"),
    )(src, dst)


def _pool_mix_fwd(pooled, pw, scale, mixed, col0, *, name, tm=1024):
    l, d_pool = pooled.shape
    pg = d_pool // GROUPS
    tm = _tile(l, tm, 16)
    assert col0 % pg == 0
    off = col0 // pg

    def body(a_ref, w_ref, s_ref, mixed_in, raw_ref, mixed_ref):
        del mixed_in
        raw = _dot(a_ref[...], w_ref[...])
        raw_ref[...] = raw
        mixed_ref[...] = (raw * s_ref[...]).astype(mixed_ref.dtype)

    blk = pl.BlockSpec((tm, pg), lambda i, g: (i, g))
    return pl.pallas_call(
        body, name=name, grid=(l // tm, GROUPS),
        in_specs=[blk, pl.BlockSpec((None, pg, pg), lambda i, g: (g, 0, 0)), pl.BlockSpec((1, pg), lambda i, g: (0, g)),
                  pl.BlockSpec(memory_space=pl.ANY)],
        out_specs=[blk, pl.BlockSpec((tm, pg), lambda i, g: (i, off + g))],
        out_shape=[jax.ShapeDtypeStruct((l, d_pool), F32), jax.ShapeDtypeStruct(mixed.shape, mixed.dtype)],
        input_output_aliases={3: 1}, compiler_params=_params("parallel", "parallel"),
    )(pooled, pw, scale, mixed)


def _pool_mix_bwd(dmixed, col0, raw, scale, pw, *, name, tm=1024):
    l, d_pool = raw.shape
    pg = d_pool // GROUPS
    tm = _tile(l, tm, 16)
    assert col0 % pg == 0
    off = col0 // pg

    def body(d_ref, raw_ref, s_ref, w_ref, draw_ref, dp_ref, ds_ref):
        i = pl.program_id(1)
        dv = d_ref[...]
        draw = (dv * s_ref[...]).astype(BF16)
        draw_ref[...] = draw
        dp_ref[...] = _dot(draw, w_ref[...], tb=True)
        part = jnp.sum(dv * raw_ref[...], axis=0, keepdims=True)

        @pl.when(i == 0)
        def _():
            ds_ref[...] = part

        @pl.when(i > 0)
        def _():
            ds_ref[...] += part

    blk = pl.BlockSpec((tm, pg), lambda g, i: (i, g))
    vec = pl.BlockSpec((1, pg), lambda g, i: (0, g))
    return pl.pallas_call(
        body, name=name, grid=(GROUPS, l // tm),
        in_specs=[pl.BlockSpec((tm, pg), lambda g, i: (i, off + g)), blk, vec,
                  pl.BlockSpec((None, pg, pg), lambda g, i: (g, 0, 0))],
        out_specs=[blk, blk, vec],
        out_shape=[jax.ShapeDtypeStruct((l, d_pool), BF16), jax.ShapeDtypeStruct((l, d_pool), F32),
                   jax.ShapeDtypeStruct((1, d_pool), F32)],
        compiler_params=_params("parallel", "arbitrary"),
    )(dmixed, raw, scale, pw)


def _group_mm_tn(a, b, *, name, tk=1024):
    l, d_pool = a.shape
    pg = d_pool // GROUPS
    tk = _tile(l, tk, 16)

    def body(a_ref, b_ref, o_ref):
        kk = pl.program_id(1)
        part = _dot(a_ref[...], b_ref[...], ta=True)

        @pl.when(kk == 0)
        def _():
            o_ref[...] = part

        @pl.when(kk > 0)
        def _():
            o_ref[...] += part

    blk = pl.BlockSpec((tk, pg), lambda g, kk: (kk, g))
    return pl.pallas_call(
        body, name=name, grid=(GROUPS, l // tk), in_specs=[blk, blk],
        out_specs=pl.BlockSpec((None, pg, pg), lambda g, kk: (g, 0, 0)),
        out_shape=jax.ShapeDtypeStruct((GROUPS, pg, pg), F32), compiler_params=_params("parallel", "arbitrary"),
    )(a, b)


def _split3(v):
    hi = v.astype(BF16)
    r1 = v - hi.astype(F32)
    mid = r1.astype(BF16)
    lo = (r1 - mid.astype(F32)).astype(BF16)
    return hi, mid, lo


def _tri_dot(tri, v):
    hi, mid, lo = _split3(v)
    d = lambda p: jnp.dot(tri, p, preferred_element_type=F32)
    return d(hi) + d(mid) + d(lo)


def _dot_tri(v, tri):
    hi, mid, lo = _split3(v)
    d = lambda p: jnp.dot(p, tri, preferred_element_type=F32)
    return d(hi) + d(mid) + d(lo)


def _split2(v):
    hi = v.astype(BF16)
    return hi, (v - hi.astype(F32)).astype(BF16)


def _scan_specs(hpg, nc, order):
    gw, q = hpg * HEAD_DIM, CHUNK
    b_off, c_off = (GROUPS * gw) // STATE, (GROUPS * gw) // STATE + GROUPS
    xs = pl.BlockSpec((q, gw), lambda g, c: (order(c), g))
    bm = pl.BlockSpec((q, STATE), lambda g, c: (order(c), b_off + g))
    cm = pl.BlockSpec((q, STATE), lambda g, c: (order(c), c_off + g))
    dtt = pl.BlockSpec((None, hpg, q), lambda g, c: (g, 0, order(c)))
    vec = pl.BlockSpec((1, gw), lambda g, c: (0, g))
    hcol = pl.BlockSpec((None, hpg, 1), lambda g, c: (g, 0, 0))
    st = pl.BlockSpec((None, None, STATE, gw), lambda g, c: (g, order(c), 0, 0))
    return xs, bm, cm, dtt, vec, hcol, st


def _scan_common(dtx_ref, dtrt_ref, biasx_ref, biast_ref, alogx_ref, alogt_ref):
    q = CHUNK
    dt = _softplus(dtx_ref[...] + biasx_ref[...])
    a = -jnp.exp(alogx_ref[...])
    dtt = _softplus(dtrt_ref[...] + biast_ref[...])
    at = -jnp.exp(alogt_ref[...])
    row = lax.broadcasted_iota(jnp.int32, (q, q), 0)
    col = lax.broadcasted_iota(jnp.int32, (q, q), 1)
    lower = (row >= col).astype(BF16)
    upper = (row <= col).astype(BF16)
    acum = _tri_dot(lower, dt * a)
    acum_t = _dot_tri(dtt * at, upper)
    return dt, a, acum, acum_t, row, col, upper


def _gated(yv, z):
    s = _sigmoid(z)
    sz = z * s
    gv = yv * sz
    return gv, lax.rsqrt(jnp.mean(gv * gv, axis=-1, keepdims=True) + NORM_EPS), sz, s


def _scan_fwd(xbc, dtx, dtrt, biasx, biast, alogx, alogt, dskx, proj, wn, d_mix, *, name, ride=None):
    l = xbc.shape[0]
    hpg = dtrt.shape[1]
    gw, q, nc = hpg * HEAD_DIM, CHUNK, l // CHUNK
    assert hpg % 2 == 0
    xs_s, bm_s, cm_s, dtt_s, vec, hcol, st_s = _scan_specs(hpg, nc, lambda c: c)

    def body(xs_ref, b_ref, c_ref, dtx_ref, dtrt_ref, biasx_ref, biast_ref, alogx_ref, alogt_ref, dsk_ref, z_ref, wn_ref,
             y_ref, st_ref, mixed_ref, h_ref):
        @pl.when(pl.program_id(1) == 0)
        def _():
            h_ref[...] = jnp.zeros_like(h_ref)

        dt, _, acum, acum_t, row, col, _ = _scan_common(dtx_ref, dtrt_ref, biasx_ref, biast_ref, alogx_ref, alogt_ref)
        causal = row >= col
        xs = xs_ref[...]
        xdt = xs * dt
        xdtb = xdt.astype(BF16)
        bmat, cmat = b_ref[...], c_ref[...]
        cb = _dot(cmat, bmat, tb=True)
        h_all = h_ref[...]
        st_ref[...] = h_all
        rest = _dot(cmat, h_all) * jnp.exp(acum) + dsk_ref[...] * xs
        first = lax.broadcasted_iota(jnp.int32, (q, 2 * HEAD_DIM), 1) < HEAD_DIM
        parts = []
        for jp in range(hpg // 2):
            pair = slice(2 * jp * HEAD_DIM, (2 * jp + 2) * HEAD_DIM)
            xp = xdtb[:, pair]
            ys = []
            for j in (2 * jp, 2 * jp + 1):
                a_col, a_row = acum[:, j * HEAD_DIM:j * HEAD_DIM + 1], acum_t[j:j + 1, :]
                lm = jnp.exp(jnp.where(causal, a_col - a_row, NEG_BIG))
                ys.append(_dot(cb * lm, xp))
            parts.append(jnp.where(first, ys[0], ys[1]))
        yv = jnp.concatenate(parts, axis=1) + rest
        y_ref[...] = yv
        gv, r, _, _ = _gated(yv, z_ref[...])
        mixed_ref[...] = (gv * r * wn_ref[...]).astype(mixed_ref.dtype)
        a_last = acum[q - 1:q, :]
        h_ref[...] = jnp.exp(a_last) * h_all + _dot(bmat, xdt * jnp.exp(a_last - acum), ta=True)

    return _call(
        body, name=name, grid=(GROUPS, nc),
        in_specs=[xs_s, bm_s, cm_s, xs_s, dtt_s, vec, hcol, vec, hcol, vec, xs_s, vec], out_specs=[xs_s, st_s, xs_s],
        out_shape=[jax.ShapeDtypeStruct((l, GROUPS * gw), F32), jax.ShapeDtypeStruct((GROUPS, nc, STATE, gw), F32),
                   jax.ShapeDtypeStruct((l, d_mix), BF16)],
        scratch=[pltpu.VMEM((STATE, gw), F32)], sem=("parallel", "arbitrary"),
        args=(xbc, xbc, xbc, dtx, dtrt, biasx, biast, alogx, alogt, dskx, proj, wn), ride=ride)


def _scan_bwd(xbc, dtx, dtrt, biasx, biast, alogx, alogt, dskx, states, dmixed, y, proj, wn, *, name, ride=None):
    l = xbc.shape[0]
    hpg = dtrt.shape[1]
    gw, q, nc = hpg * HEAD_DIM, CHUNK, l // CHUNK
    rev = lambda c: nc - 1 - c
    xs_s, bm_s, cm_s, dtt_s, vec, hcol, st_s = _scan_specs(hpg, nc, rev)
    bc_s = pl.BlockSpec((q, STATE), lambda g, c: (rev(c), g))

    def body(xs_ref, b_ref, c_ref, dtx_ref, dtrt_ref, biasx_ref, biast_ref, alogx_ref, alogt_ref, dsk_ref, st_ref,
             dmix_ref, y_ref, z_ref, wn_ref,
             dxs_ref, db_ref, dc_ref, ddt_ref, dbias_ref, dalog_ref, ddsk_ref, dz_ref, dwn_ref, dh_ref):
        step = pl.program_id(1)

        @pl.when(step == 0)
        def _():
            dh_ref[...] = jnp.zeros_like(dh_ref)

        yv, z, dmix = y_ref[...], z_ref[...], dmix_ref[...]
        gv, rn, sz, sg = _gated(yv, z)
        gh = gv * rn
        dgh = dmix * wn_ref[...]
        dg = rn * (dgh - gh * jnp.mean(dgh * gh, axis=-1, keepdims=True))
        dyv = dg * sz
        dz_ref[...] = (dg * yv * (sg * (1.0 + z * (1.0 - sg)))).astype(dz_ref.dtype)
        p_wn = jnp.sum(dmix * gh, axis=0, keepdims=True)

        dt, a, acum, acum_t, row, col, upper = _scan_common(dtx_ref, dtrt_ref, biasx_ref, biast_ref, alogx_ref, alogt_ref)
        causal, anti, strict = row >= col, row <= col, row > col
        xs = xs_ref[...]
        xdt = xs * dt
        xdtb, dyb = xdt.astype(BF16), dyv.astype(BF16)
        bmat, cmat = b_ref[...], c_ref[...]
        cb = _dot(cmat, bmat, tb=True)
        cbt = _dot(bmat, cmat, tb=True)
        h_all, dh = st_ref[...], dh_ref[...]
        a_last = acum[q - 1:q, :]
        ea, ee, gam = jnp.exp(acum), jnp.exp(a_last - acum), jnp.exp(a_last)
        zc = _dot(cmat, h_all)
        bdh = _dot(bmat, dh)
        dz = dyv * ea
        xe = xdt * ee
        dc_state = _dot(dz, h_all, tb=True)
        db_state = _dot(xe, dh, tb=True)
        dh_ref[...] = gam * dh + _dot(cmat, dz, ta=True)
        lane = lax.broadcasted_iota(jnp.int32, (q, 2 * HEAD_DIM), 1)
        first = lane < HEAD_DIM
        ones = jnp.ones((q // 2, 2 * HEAD_DIM), BF16)
        dcb = jnp.zeros((q, q), F32)
        dxdt_parts, da_parts = [], []
        for jp in range(hpg // 2):
            pair = slice(2 * jp * HEAD_DIM, (2 * jp + 2) * HEAD_DIM)
            xp, dyp = xdtb[:, pair], dyb[:, pair]
            dxs_r, das_r = [], []
            for r, j in enumerate((2 * jp, 2 * jp + 1)):
                seg = acum[:, j * HEAD_DIM:j * HEAD_DIM + 1] - acum_t[j:j + 1, :]
                lm = jnp.exp(jnp.where(causal, seg, NEG_BIG))
                lmt = jnp.exp(jnp.where(anti, -seg, NEG_BIG))
                mm, mmt = cb * lm, cbt * lmt
                keep = first if r == 0 else jnp.logical_not(first)
                dyk = jnp.where(keep, dyp, 0)
                dxs_r.append(_dot(mmt, dyp))
                dm = _dot(dyk, xp, tb=True)
                dcb = dcb + dm * lm
                corner = jnp.dot(upper, (dm * mm).astype(BF16), preferred_element_type=F32)
                corner = jnp.where(strict, corner, 0.0)
                chi, clo = _split2(corner[:, :q // 2] + corner[:, q // 2:])
                das_r.append(jnp.dot(chi, ones, preferred_element_type=F32) + jnp.dot(clo, ones, preferred_element_type=F32))
            dxdt_parts.append(jnp.where(first, dxs_r[0], dxs_r[1]))
            da_parts.append(jnp.where(first, das_r[0], das_r[1]))
        dxdt = jnp.concatenate(dxdt_parts, axis=1) + bdh * ee
        da_intra = jnp.concatenate(da_parts, axis=1)
        dxs_ref[...] = dxdt * dt + dsk_ref[...] * dyv
        dc_ref[...] = dc_state + _dot(dcb, bmat)
        db_ref[...] = db_state + _dot(dcb, cmat, ta=True)
        ri = lax.broadcasted_iota(jnp.int32, (gw, gw), 0) // HEAD_DIM
        ci = lax.broadcasted_iota(jnp.int32, (gw, gw), 1) // HEAD_DIM
        blockdiag = (ri == ci).astype(BF16)

        def head_sum(v):
            hi, lo = _split2(v)
            return (jnp.dot(hi, blockdiag, preferred_element_type=F32)
                    + jnp.dot(lo, blockdiag, preferred_element_type=F32))

        xt = xe * bdh
        small = jnp.concatenate([
            jnp.sum(xt, axis=0, keepdims=True) + gam * jnp.sum(dh * h_all, axis=0, keepdims=True),
            jnp.sum(dyv * xs, axis=0, keepdims=True), jnp.zeros((6, gw), F32)], axis=0)
        small = head_sum(small)
        rows = lax.broadcasted_iota(jnp.int32, (q, 1), 0)
        da_local = head_sum(dyv * zc * ea - xt) + jnp.where(rows == q - 1, small[0:1, :], 0.0)
        hi, lo = _split2(da_local)
        d_dta = (da_intra + jnp.dot(upper, hi, preferred_element_type=F32)
                 + jnp.dot(upper, lo, preferred_element_type=F32))
        d_raw = (d_dta * a + head_sum(dxdt * xs)) * _sigmoid(dtx_ref[...] + biasx_ref[...])
        ddt_ref[...] = d_raw
        p_bias = jnp.sum(d_raw, axis=0, keepdims=True)
        p_alog = jnp.sum(d_dta * dt, axis=0, keepdims=True) * a
        p_dsk = small[1:2, :]

        @pl.when(step == 0)
        def _():
            dbias_ref[...] = p_bias
            dalog_ref[...] = p_alog
            ddsk_ref[...] = p_dsk
            dwn_ref[...] = p_wn

        @pl.when(step > 0)
        def _():
            dbias_ref[...] += p_bias
            dalog_ref[...] += p_alog
            ddsk_ref[...] += p_dsk
            dwn_ref[...] += p_wn

    wide = jax.ShapeDtypeStruct((l, GROUPS * gw), F32)
    narrow = jax.ShapeDtypeStruct((l, GROUPS * STATE), F32)
    vshape = jax.ShapeDtypeStruct((1, GROUPS * gw), F32)
    return _call(
        body, name=name, grid=(GROUPS, nc),
        in_specs=[xs_s, bm_s, cm_s, xs_s, dtt_s, vec, hcol, vec, hcol, vec, st_s, xs_s, xs_s, xs_s, vec],
        out_specs=[xs_s, bc_s, bc_s, xs_s, vec, vec, vec, xs_s, vec],
        out_shape=[wide, narrow, narrow, wide, vshape, vshape, vshape, jax.ShapeDtypeStruct(proj.shape, BF16), vshape],
        scratch=[pltpu.VMEM((STATE, gw), F32)], sem=("parallel", "arbitrary"),
        args=(xbc, xbc, xbc, dtx, dtrt, biasx, biast, alogx, alogt, dskx, states, dmixed, y, proj, wn), ride=ride)


def _exchange(srcs, scatter, *, name):
    ride = _Ride(srcs, scatter)

    def body(*refs):
        src, dst, sems = refs[:ride.n], refs[ride.n:2 * ride.n], refs[2 * ride.n:]
        ride.start(src, dst, sems)
        ride.finish(src, dst, sems)

    return pl.pallas_call(body, name=name, in_specs=ride.in_specs, out_specs=ride.out_specs, out_shape=ride.out_shape,
                          scratch_shapes=ride.scratch)(*srcs)


def _adamw(parts, w, m, v, *, name, tr=256):
    lead = w.ndim == 3
    r, c = w.shape[-2:]
    tr = _tile(r, tr, 16)
    c1, c2 = 1.0 / (1.0 - ADAM_B1 ** ADAM_STEP), 1.0 / (1.0 - ADAM_B2 ** ADAM_STEP)

    def body(p_ref, w_ref, m_ref, v_ref, g_ref, d_ref, nm_ref, nv_ref):
        g = p_ref[0].astype(F32)
        for p in range(1, N_DEV):
            g = g + p_ref[p].astype(F32)
        nm = ADAM_B1 * m_ref[...] + (1.0 - ADAM_B1) * g
        nv = ADAM_B2 * v_ref[...] + (1.0 - ADAM_B2) * (g * g)
        g_ref[...] = g
        nm_ref[...] = nm
        nv_ref[...] = nv
        d_ref[...] = -ADAM_LR * ((nm * c1) / (jnp.sqrt(nv * c2) + ADAM_EPS) + ADAM_WD * w_ref[...])

    blk = pl.BlockSpec((None, tr, c), lambda i: (0, i, 0)) if lead else pl.BlockSpec((tr, c), lambda i: (i, 0))
    out = jax.ShapeDtypeStruct(w.shape, F32)
    return pl.pallas_call(
        body, name=name, grid=(r // tr,), in_specs=[pl.BlockSpec((N_DEV, tr, c), lambda i: (0, i, 0)), blk, blk, blk],
        out_specs=[blk, blk, blk, blk], out_shape=[out, out, out, out], compiler_params=_params("parallel"),
    )(parts, w, m, v)


def kernel(x, attn_norm_w, w_in, conv_w, conv_b, dt_bias, a_log, d_skip, ssd_norm_w, pool_w, pool_scale, w_out, ffn_norm_w, w_gate, w_up, w_down, final_norm_w, loss_target, m_attn_norm_w, m_w_in, m_conv_w, m_conv_b, m_dt_bias, m_a_log, m_d_skip, m_ssd_norm_w, m_pool_w, m_pool_scale, m_w_out, m_ffn_norm_w, m_w_gate, m_w_up, m_w_down, m_final_norm_w, v_attn_norm_w, v_w_in, v_conv_w, v_conv_b, v_dt_bias, v_a_log, v_d_skip, v_ssd_norm_w, v_pool_w, v_pool_scale, v_w_out, v_ffn_norm_w, v_w_gate, v_w_up, v_w_down, v_final_norm_w):
    l, d = x.shape[1], x.shape[2]
    heads = dt_bias.shape[1]
    hpg = heads // GROUPS
    d_ssm = heads * HEAD_DIM
    conv_ch = conv_b.shape[1]
    d_pool = pool_scale.shape[1]
    pg = d_pool // GROUPS
    d_mix = d_ssm + d_pool
    d_ff = w_gate.shape[2] * N_DEV
    d_in = w_in.shape[2] * N_DEV
    dt_pad = -(-heads // LANE) * LANE
    o_u, o_xbc, o_dt = d_ssm, d_ssm + d_pool, d_ssm + d_pool + conv_ch
    d_inp = o_dt + dt_pad
    tn_in = _tile(d_inp, 2560, LANE)
    g_dt, g_u = d_ssm + conv_ch, d_ssm + conv_ch + heads

    x2, tgt = x[0], loss_target[0]

    gi, gp, gc = _exchange([w_in[0].astype(BF16), pool_w[0].astype(BF16), conv_w[0]], False, name="gather_w_in")
    win = jnp.transpose(gi, (1, 0, 2)).reshape(d, d_in)
    wp = jnp.concatenate([win[:, :d_ssm], win[:, g_u:], win[:, d_ssm:g_dt], win[:, g_dt:g_u],
                          jnp.zeros((d, dt_pad - heads), BF16)], axis=1)
    pw = jnp.transpose(gp, (1, 0, 2, 3)).reshape(GROUPS, pg, pg)
    cw = jnp.transpose(gc, (1, 0, 2)).reshape(CONV_K, conv_ch)

    def expand(vec):
        return jnp.repeat(vec, HEAD_DIM, axis=1)

    def per_head(vec):
        return vec[:, ::HEAD_DIM]

    bias_x, alog_x, dsk_x = expand(dt_bias), expand(a_log), expand(d_skip)
    bias_c, alog_c = dt_bias.reshape(GROUPS, hpg, 1), a_log.reshape(GROUPS, hpg, 1)

    h0 = _rms_fwd(x2, attn_norm_w, name="attn_norm")
    proj, go, gg = _mm(h0, wp, name="in_proj", tm=512, tn=tn_in, tk=d,
                       ride=_Ride([w_out[0].astype(BF16), w_gate[0].astype(BF16)], False))
    wo = go.reshape(d_mix, d)
    xbc = _conv_fwd(proj, o_xbc, cw, conv_b, name="conv_fwd")
    dt_raw = proj[:, o_dt:o_dt + heads]
    dtx = expand(dt_raw)
    dtrt = jnp.transpose(dt_raw.reshape(l, GROUPS, hpg), (1, 2, 0))
    y, states, mixed, gu = _scan_fwd(xbc, dtx, dtrt, bias_x, bias_c, alog_x, alog_c, dsk_x, proj, ssd_norm_w, d_mix,
                                     name="ssd_fwd", ride=_Ride([w_up[0].astype(BF16)], False))
    pooled = _pool_fwd(proj, o_u, d_pool, name="pool_fwd")
    pool_raw, mixed = _pool_mix_fwd(pooled, pw, pool_scale, mixed, d_ssm, name="pool_mix")
    h1 = _mm(mixed, wo, name="out_proj", tm=512, tk=d_mix, residual=x2)
    h1n = _rms_fwd(h1, ffn_norm_w, name="ffn_norm")
    gate, up, act, gd = _gate_up(h1n, gg, gu, name="gate_up", ride=_Ride([w_down[0].astype(BF16)], False))
    h2 = _mm_shards([(act, gd)], name="down_proj", per_step=4, residual=h1)

    loss11, dh2b, g_final = _loss_head(h2, final_norm_w.reshape(1, d), tgt, name="loss_head")
    dgate, dup = _gate_up_bwd(dh2b, gd, gate, up, name="gate_up_bwd")
    g_wd = _grad_shards(act, dh2b, name="grad_w_down")
    dh1n, r_wd = _mm_shards([(dgate, gg), (dup, gu)], name="dgate_dup", tb=True, out_dtype=BF16,
                            ride=_Ride([g_wd], True))
    g_wg = _grad_shards(h1n, dgate, name="grad_w_gate")
    g_wu = _grad_shards(h1n, dup, name="grad_w_up")
    dh1b, g_ffn = _rms_bwd(h1, ffn_norm_w, dh1n, dh2b, BF16, name="ffn_norm_bwd")
    dmixed = _mm(dh1b, wo, name="dmixed", tb=True, tk=d)
    g_wo = _mm(mixed, dh1b, name="grad_w_out", ta=True, out_dtype=BF16, tm=1024, tn=1024, tk=2048)
    draw, dpooled, g_pscale = _pool_mix_bwd(dmixed, d_ssm, pool_raw, pool_scale, pw, name="pool_mix_bwd")
    g_pw = _group_mm_tn(pooled, draw, name="grad_pool_w")
    s_pw = jnp.transpose(g_pw.reshape(GROUPS, N_DEV, pg // N_DEV, pg), (1, 0, 2, 3)).astype(BF16)
    dxs, dbm, dcm, ddtx, g_bias, g_alog, g_dsk, dproj, g_ssdn, r_wg, r_wu, r_wo, r_pw = _scan_bwd(
        xbc, dtx, dtrt, bias_x, bias_c, alog_x, alog_c, dsk_x, states, dmixed, y, proj, ssd_norm_w, name="ssd_bwd",
        ride=_Ride([g_wg, g_wu, g_wo.reshape(N_DEV, d_mix // N_DEV, d), s_pw], True))
    dproj = _pool_bwd(dpooled, dproj, o_u, name="pool_bwd")
    n_bc = GROUPS * STATE
    segs = [(dxs, 0, d_ssm), (dbm, d_ssm, n_bc), (dcm, d_ssm + n_bc, n_bc)]
    g_cw, g_cb = [], []
    for si, (dseg, c0, width) in enumerate(segs):
        dproj, b, c_ = _conv_bwd(proj, o_xbc + c0, cw[:, c0:c0 + width], conv_b[:, c0:c0 + width], dseg, dproj,
                                 name=f"conv_bwd{si}")
        g_cw.append(b)
        g_cb.append(c_)
    dproj = _put_heads(ddtx, dproj, o_dt, dt_pad, name="put_ddt")
    g_wp = _mm(h0, dproj, name="grad_w_in", ta=True, out_dtype=BF16, tm=512, tn=tn_in, tk=2048)
    g_win = jnp.concatenate([g_wp[:, :d_ssm], g_wp[:, o_xbc:o_dt + heads], g_wp[:, o_u:o_xbc]], axis=1)
    def cols8(g):
        return jnp.transpose(g.reshape(g.shape[0], N_DEV, g.shape[1] // N_DEV), (1, 0, 2))
    dh0, r_in, r_cw = _mm(dproj, wp, name="dh0", tb=True, out_dtype=BF16, tk=tn_in,
                          ride=_Ride([cols8(g_win), cols8(jnp.concatenate(g_cw, axis=1))], True))
    dx, g_attn = _rms_bwd(x2, attn_norm_w, dh0, dh1b, F32, name="attn_norm_bwd")

    smalls = [g_attn, jnp.concatenate(g_cb, axis=1), per_head(g_bias), per_head(g_alog), per_head(g_dsk), g_ssdn,
              g_pscale, g_ffn, g_final]
    small_w = [attn_norm_w, conv_b, dt_bias, a_log, d_skip, ssd_norm_w, pool_scale, ffn_norm_w, final_norm_w.reshape(1, d)]
    small_m = [m_attn_norm_w, m_conv_b, m_dt_bias, m_a_log, m_d_skip, m_ssd_norm_w, m_pool_scale, m_ffn_norm_w,
               m_final_norm_w.reshape(1, d)]
    small_v = [v_attn_norm_w, v_conv_b, v_dt_bias, v_a_log, v_d_skip, v_ssd_norm_w, v_pool_scale, v_ffn_norm_w,
               v_final_norm_w.reshape(1, d)]
    sizes = [s.shape[1] for s in smalls]
    n_small = sum(sizes)
    n_pad = -(-n_small // (16 * LANE)) * (16 * LANE)
    rows = n_pad // LANE
    def pack(vs):
        return jnp.pad(jnp.concatenate(vs, axis=1), ((0, 0), (0, n_pad - n_small))).reshape(rows, LANE)
    (r_small,) = _exchange([pack(smalls)], False, name="gather_small_grads")

    def big(parts, w, m, v, nm):
        shp = w.shape
        if w.ndim == 3:
            return _adamw(parts, w, m, v, name=nm)
        r2 = lambda t: t.reshape(-1, shp[-1])
        outs = _adamw(parts.reshape(N_DEV, -1, shp[-1]), r2(w), r2(m), r2(v), name=nm)
        return [o.reshape(shp) for o in outs]

    res = {
        "w_in": big(r_in, w_in, m_w_in, v_w_in, "adamw_w_in"),
        "conv_w": big(r_cw, conv_w, m_conv_w, v_conv_w, "adamw_conv_w"),
        "pool_w": big(r_pw, pool_w, m_pool_w, v_pool_w, "adamw_pool_w"),
        "w_out": big(r_wo, w_out, m_w_out, v_w_out, "adamw_w_out"),
        "w_gate": big(r_wg, w_gate, m_w_gate, v_w_gate, "adamw_w_gate"),
        "w_up": big(r_wu, w_up, m_w_up, v_w_up, "adamw_w_up"),
        "w_down": big(r_wd, w_down, m_w_down, v_w_down, "adamw_w_down"),
    }
    s_out = _adamw(r_small, pack(small_w), pack(small_m), pack(small_v), name="adamw_small")
    names = ["attn_norm_w", "conv_b", "dt_bias", "a_log", "d_skip", "ssd_norm_w", "pool_scale", "ffn_norm_w", "final_norm_w"]
    offs = [sum(sizes[:i]) for i in range(len(sizes))]
    for i, nm in enumerate(names):
        shp = (d,) if nm == "final_norm_w" else (1, sizes[i])
        res[nm] = [o.reshape(1, n_pad)[:, offs[i]:offs[i] + sizes[i]].reshape(shp) for o in s_out]

    loss = lax.psum(loss11[0, 0], ("x", "y", "c"))
    order = ["attn_norm_w", "w_in", "conv_w", "conv_b", "dt_bias", "a_log", "d_skip", "ssd_norm_w", "pool_w", "pool_scale",
             "w_out", "ffn_norm_w", "w_gate", "w_up", "w_down", "final_norm_w"]
    outs = [loss, dx.reshape(x.shape)]
    for part in range(4):
        outs += [res[nm][part] for nm in order]
    return tuple(outs)
```

```python
import functools

import jax
import jax.numpy as jnp
from jax import lax
from jax.experimental import pallas as pl
from jax.experimental.pallas import tpu as pltpu

F32 = jnp.float32
BF16 = jnp.bfloat16

NORM_EPS = 1e-5
HEAD_DIM = 64
STATE = 128
CHUNK = 256
GROUPS = 4
CONV_K = 4
POOL_WINDOWS = (2, 4, 8, 16)
POOL_HALO = 16
CONV_HALO = 8
LANE = 128
N_DEV = 8
ADAM_LR, ADAM_B1, ADAM_B2, ADAM_EPS, ADAM_WD, ADAM_STEP = 0.001, 0.9, 0.999, 1e-08, 0.01, 10
NEG_BIG = -1e30
EPILOGUE_ROWS = 256
MESH_ID = pl.DeviceIdType.MESH


def _tile(n, pref, mult):
    if n <= pref:
        return n
    t = (pref // mult) * mult
    while t >= mult:
        if n % t == 0:
            return t
        t -= mult
    return n


def _params(*sem):
    return pltpu.CompilerParams(dimension_semantics=sem)


ANY = pl.BlockSpec(memory_space=pl.ANY)


def _peer(k):
    x, y, c = lax.axis_index("x"), lax.axis_index("y"), lax.axis_index("c")
    return (1 - x if k & 4 else x, 1 - y if k & 2 else y, 1 - c if k & 1 else c)


def _my_index():
    return 4 * lax.axis_index("x") + 2 * lax.axis_index("y") + lax.axis_index("c")


class _Ride:
    def __init__(self, srcs, scatter):
        self.srcs, self.scatter, self.n = list(srcs), scatter, len(srcs)
        self.two_level = not scatter
        self.in_specs = [ANY] * self.n
        self.out_specs = [ANY] * self.n
        self.out_shape = [jax.ShapeDtypeStruct((N_DEV,) + tuple(s.shape[1:] if scatter else s.shape), s.dtype)
                          for s in self.srcs]
        self.scratch = [pltpu.SemaphoreType.DMA((self.n, N_DEV - 1)), pltpu.SemaphoreType.DMA((self.n, N_DEV - 1)),
                        pltpu.SemaphoreType.DMA((self.n,))]

    def _copies(self, src, dst, sems):
        send_sems, recv_sems, local_sems = sems
        me = _my_index()
        local = [pltpu.make_async_copy(src[t].at[me] if self.scatter else src[t], dst[t].at[me], local_sems.at[t])
                 for t in range(self.n)]
        remote = {}
        for k in range(1, N_DEV):
            peer, pidx = _peer(k), me ^ k
            remote[k] = [pltpu.make_async_remote_copy(
                src_ref=src[t].at[pidx] if self.scatter else src[t], dst_ref=dst[t].at[me],
                send_sem=send_sems.at[t, k - 1], recv_sem=recv_sems.at[t, k - 1],
                device_id=peer, device_id_type=MESH_ID) for t in range(self.n)]
        return local, remote

    def _forwards(self, dst, sems):
        send_sems, recv_sems, _ = sems
        me = _my_index()
        return {k: [pltpu.make_async_remote_copy(
            src_ref=dst[t].at[me ^ k], dst_ref=dst[t].at[me ^ k],
            send_sem=send_sems.at[t, (k ^ 1) - 1], recv_sem=recv_sems.at[t, (k ^ 1) - 1],
            device_id=_peer(1), device_id_type=MESH_ID) for t in range(self.n)] for k in (2, 4, 6)}

    def start(self, src, dst, sems):
        local, remote = self._copies(src, dst, sems)
        for cp in local:
            cp.start()
        for k in (2, 4, 6, 1) if self.two_level else range(1, N_DEV):
            for cp in remote[k]:
                cp.start()

    def finish(self, src, dst, sems):
        local, remote = self._copies(src, dst, sems)
        if self.two_level:
            fwd = self._forwards(dst, sems)
            for k in (2, 4, 6):
                for t in range(self.n):
                    remote[k][t].wait_recv()
                    fwd[k][t].start()
            for k in (1, 3, 5, 7):
                for cp in remote[k]:
                    cp.wait_recv()
            for k in (1, 2, 4, 6):
                for cp in remote[k]:
                    cp.wait_send()
            for k in (2, 4, 6):
                for cp in fwd[k]:
                    cp.wait_send()
        else:
            for k in range(1, N_DEV):
                for cp in remote[k]:
                    cp.wait_send()
            for k in range(1, N_DEV):
                for cp in remote[k]:
                    cp.wait_recv()
        for lc in local:
            lc.wait()


def _call(body, *, name, grid, in_specs, out_specs, out_shape, args, sem, scratch=(), ride=None):
    in_specs, out_specs, out_shape, scratch = list(in_specs), list(out_specs), list(out_shape), list(scratch)
    if ride is None:
        return pl.pallas_call(body, name=name, grid=grid, in_specs=in_specs, out_specs=out_specs, out_shape=out_shape,
                              scratch_shapes=scratch, compiler_params=_params(*sem))(*args)
    n_in, n_out, n_sc, nr = len(in_specs), len(out_specs), len(scratch), ride.n

    def full(*refs):
        ins, csrc = refs[:n_in], refs[n_in:n_in + nr]
        o0 = n_in + nr
        outs, cdst = refs[o0:o0 + n_out], refs[o0 + n_out:o0 + n_out + nr]
        s0 = o0 + n_out + nr
        sc, csem = refs[s0:s0 + n_sc], refs[s0 + n_sc:]
        ids = [pl.program_id(ax) for ax in range(len(grid))]
        first = functools.reduce(jnp.logical_and, [i == 0 for i in ids])
        last = functools.reduce(jnp.logical_and, [i == g - 1 for i, g in zip(ids, grid)])

        @pl.when(first)
        def _():
            ride.start(csrc, cdst, csem)

        body(*ins, *outs, *sc)

        @pl.when(last)
        def _():
            ride.finish(csrc, cdst, csem)

    return pl.pallas_call(
        full, name=name, grid=grid, in_specs=in_specs + ride.in_specs, out_specs=out_specs + ride.out_specs,
        out_shape=out_shape + ride.out_shape, scratch_shapes=scratch + ride.scratch,
        compiler_params=_params(*(["arbitrary"] * len(grid))),
    )(*args, *ride.srcs)


def _sigmoid(x):
    return 1.0 / (1.0 + jnp.exp(-x))


def _softplus(x):
    return jnp.maximum(x, 0.0) + jnp.log(1.0 + jnp.exp(-jnp.abs(x)))


def _dot(a, b, ta=False, tb=False):
    dn = (((0 if ta else 1,), (1 if tb else 0,)), ((), ()))
    return lax.dot_general(a.astype(BF16), b.astype(BF16), dn, preferred_element_type=F32)


def _mm(a, b, *, name, ta=False, tb=False, out_dtype=F32, tm=1024, tn=1024, tk=2048, residual=None, ride=None):
    m = a.shape[1] if ta else a.shape[0]
    k = a.shape[0] if ta else a.shape[1]
    n = b.shape[0] if tb else b.shape[1]
    assert k == (b.shape[1] if tb else b.shape[0])
    tm, tn = _tile(m, tm, LANE if ta else 8), _tile(n, tn, LANE)
    tk = _tile(k, tk, LANE if not (ta and tb) else LANE)
    nk = k // tk
    has_res = residual is not None

    def body(*refs):
        a_ref, b_ref = refs[0], refs[1]
        r_ref = refs[2] if has_res else None
        o_ref = refs[3] if has_res else refs[2]
        part = _dot(a_ref[...], b_ref[...], ta, tb)

        def finish(acc):
            if has_res:
                acc = acc + r_ref[...]
            o_ref[...] = acc.astype(o_ref.dtype)

        if nk == 1:
            finish(part)
        else:
            acc_ref = refs[-1]
            kk = pl.program_id(2)

            @pl.when(kk == 0)
            def _():
                acc_ref[...] = part

            @pl.when(kk > 0)
            def _():
                acc_ref[...] += part

            @pl.when(kk == nk - 1)
            def _():
                finish(acc_ref[...])

    a_spec = pl.BlockSpec((tk, tm), lambda i, j, kk: (kk, i)) if ta else pl.BlockSpec((tm, tk), lambda i, j, kk: (i, kk))
    b_spec = pl.BlockSpec((tn, tk), lambda i, j, kk: (j, kk)) if tb else pl.BlockSpec((tk, tn), lambda i, j, kk: (kk, j))
    o_spec = pl.BlockSpec((tm, tn), lambda i, j, kk: (i, j))
    in_specs = [a_spec, b_spec] + ([o_spec] if has_res else [])
    args = (a, b) + ((residual,) if has_res else ())
    res = _call(body, name=name, grid=(m // tm, n // tn, nk), in_specs=in_specs, out_specs=[o_spec],
                out_shape=[jax.ShapeDtypeStruct((m, n), out_dtype)], args=args,
                scratch=[pltpu.VMEM((tm, tn), F32)] if nk > 1 else [], sem=("parallel", "parallel", "arbitrary"), ride=ride)
    return res[0] if ride is None else res


def _gate_up(h, wg, wu, *, name, tm=1024, ride=None):
    m, k = h.shape
    ns, c, _ = wg.shape
    tm = _tile(m, tm, 16)

    tr = _tile(tm, EPILOGUE_ROWS, 16)

    def body(h_ref, wg_ref, wu_ref, g_ref, u_ref, a_ref):
        for r in range(tm // tr):
            rows = slice(r * tr, (r + 1) * tr)
            hb = h_ref[rows, :]
            g = _dot(hb, wg_ref[...], tb=True)
            u = _dot(hb, wu_ref[...], tb=True)
            g_ref[rows, :] = g
            u_ref[rows, :] = u
            a_ref[rows, :] = (g * _sigmoid(g) * u).astype(a_ref.dtype)

    o_spec = pl.BlockSpec((None, tm, c), lambda i, s: (s, i, 0))
    w_spec = pl.BlockSpec((None, c, k), lambda i, s: (s, 0, 0))
    return _call(body, name=name, grid=(m // tm, ns),
                 in_specs=[pl.BlockSpec((tm, k), lambda i, s: (i, 0)), w_spec, w_spec], out_specs=[o_spec, o_spec, o_spec],
                 out_shape=[jax.ShapeDtypeStruct((ns, m, c), F32), jax.ShapeDtypeStruct((ns, m, c), F32),
                            jax.ShapeDtypeStruct((ns, m, c), BF16)],
                 args=(h, wg, wu), sem=("parallel", "parallel"), ride=ride)


def _gate_up_bwd(dh, wd, g, u, *, name, tm=1024):
    m, k = dh.shape
    ns, c, _ = wd.shape
    tm = _tile(m, tm, 16)

    tr = _tile(tm, EPILOGUE_ROWS, 16)

    def body(dh_ref, wd_ref, g_ref, u_ref, dg_ref, du_ref):
        for r in range(tm // tr):
            rows = slice(r * tr, (r + 1) * tr)
            da = _dot(dh_ref[rows, :], wd_ref[...], tb=True)
            g = g_ref[rows, :]
            s = _sigmoid(g)
            dg_ref[rows, :] = (da * u_ref[rows, :] * (s * (1.0 + g * (1.0 - s)))).astype(dg_ref.dtype)
            du_ref[rows, :] = (da * (g * s)).astype(du_ref.dtype)

    o_spec = pl.BlockSpec((None, tm, c), lambda i, s: (s, i, 0))
    return _call(body, name=name, grid=(m // tm, ns),
                 in_specs=[pl.BlockSpec((tm, k), lambda i, s: (i, 0)), pl.BlockSpec((None, c, k), lambda i, s: (s, 0, 0)),
                           o_spec, o_spec],
                 out_specs=[o_spec, o_spec],
                 out_shape=[jax.ShapeDtypeStruct((ns, m, c), BF16), jax.ShapeDtypeStruct((ns, m, c), BF16)],
                 args=(dh, wd, g, u), sem=("parallel", "parallel"))


def _mm_shards(pairs, *, name, tb=False, residual=None, out_dtype=F32, tm=1024, tn=1024, per_step=2, ride=None):
    ns, m, c = pairs[0][0].shape
    n = pairs[0][1].shape[1] if tb else pairs[0][1].shape[2]
    tm, tn = _tile(m, tm, 16), _tile(n, tn, LANE)
    npair = len(pairs)
    has_res = residual is not None
    assert ns % per_step == 0
    nsteps = ns // per_step

    def body(*refs):
        o_ref, acc_ref = refs[-2], refs[-1]
        s = pl.program_id(2)
        part = None
        for p in range(npair):
            for q in range(per_step):
                term = _dot(refs[2 * p][q], refs[2 * p + 1][q], tb=tb)
                part = term if part is None else part + term

        @pl.when(s == 0)
        def _():
            acc_ref[...] = part

        @pl.when(s > 0)
        def _():
            acc_ref[...] += part

        @pl.when(s == nsteps - 1)
        def _():
            acc = acc_ref[...]
            if has_res:
                acc = acc + refs[2 * npair][...]
            o_ref[...] = acc.astype(o_ref.dtype)

    a_spec = pl.BlockSpec((per_step, tm, c), lambda i, j, s: (s, i, 0))
    b_spec = (pl.BlockSpec((per_step, tn, c), lambda i, j, s: (s, j, 0)) if tb
              else pl.BlockSpec((per_step, c, tn), lambda i, j, s: (s, 0, j)))
    o_spec = pl.BlockSpec((tm, tn), lambda i, j, s: (i, j))
    args = [t for pr in pairs for t in pr] + ([residual] if has_res else [])
    res = _call(body, name=name, grid=(m // tm, n // tn, nsteps),
                in_specs=[a_spec, b_spec] * npair + ([o_spec] if has_res else []), out_specs=[o_spec],
                out_shape=[jax.ShapeDtypeStruct((m, n), out_dtype)], args=args, scratch=[pltpu.VMEM((tm, tn), F32)],
                sem=("parallel", "parallel", "arbitrary"), ride=ride)
    return res[0] if ride is None else res


def _grad_shards(a, b, *, name, tk=2048):
    a3 = a.ndim == 3
    ns, l, c = a.shape if a3 else b.shape
    d = (b if a3 else a).shape[1]
    tk = _tile(l, tk, 16)
    nk = l // tk

    def body(a_ref, b_ref, o_ref, acc_ref):
        kk = pl.program_id(1)
        part = _dot(a_ref[...], b_ref[...], ta=True)

        @pl.when(kk == 0)
        def _():
            acc_ref[...] = part

        @pl.when(kk > 0)
        def _():
            acc_ref[...] += part

        @pl.when(kk == nk - 1)
        def _():
            o_ref[...] = acc_ref[...].astype(o_ref.dtype)

    s3 = pl.BlockSpec((None, tk, c), lambda s, kk: (s, kk, 0))
    s2 = pl.BlockSpec((tk, d), lambda s, kk: (kk, 0))
    out = (c, d) if a3 else (d, c)
    return pl.pallas_call(
        body, name=name, grid=(ns, nk), in_specs=[s3, s2] if a3 else [s2, s3],
        out_specs=pl.BlockSpec((None,) + out, lambda s, kk: (s, 0, 0)),
        out_shape=jax.ShapeDtypeStruct((ns,) + out, BF16), scratch_shapes=[pltpu.VMEM(out, F32)],
        compiler_params=_params("parallel", "arbitrary"),
    )(a, b)


def _rms_fwd(x, w, *, name, tl=512):
    l, d = x.shape
    tl = _tile(l, tl, 16)

    def body(x_ref, w_ref, o_ref):
        xv = x_ref[...]
        r = lax.rsqrt(jnp.mean(xv * xv, axis=-1, keepdims=True) + NORM_EPS)
        o_ref[...] = (xv * r * w_ref[...]).astype(o_ref.dtype)

    row = pl.BlockSpec((tl, d), lambda i: (i, 0))
    return pl.pallas_call(
        body, name=name, grid=(l // tl,), in_specs=[row, pl.BlockSpec((1, d), lambda i: (0, 0))], out_specs=row,
        out_shape=jax.ShapeDtypeStruct((l, d), BF16), compiler_params=_params("parallel"),
    )(x, w)


def _rms_bwd(x, w, dy, add, out_dtype, *, name, tl=256):
    l, d = x.shape
    tl = _tile(l, tl, 16)

    def body(x_ref, w_ref, dy_ref, add_ref, dx_ref, dw_ref):
        i = pl.program_id(0)
        xv = x_ref[...]
        r = lax.rsqrt(jnp.mean(xv * xv, axis=-1, keepdims=True) + NORM_EPS)
        xh = xv * r
        dyv = dy_ref[...].astype(F32)
        dxh = dyv * w_ref[...]
        dx = r * (dxh - xh * jnp.mean(dxh * xh, axis=-1, keepdims=True)) + add_ref[...].astype(F32)
        dx_ref[...] = dx.astype(dx_ref.dtype)
        part = jnp.sum(dyv * xh, axis=0, keepdims=True)

        @pl.when(i == 0)
        def _():
            dw_ref[...] = part

        @pl.when(i > 0)
        def _():
            dw_ref[...] += part

    row = pl.BlockSpec((tl, d), lambda i: (i, 0))
    vec = pl.BlockSpec((1, d), lambda i: (0, 0))
    return pl.pallas_call(
        body, name=name, grid=(l // tl,), in_specs=[row, vec, row, row], out_specs=[row, vec],
        out_shape=[jax.ShapeDtypeStruct((l, d), out_dtype), jax.ShapeDtypeStruct((1, d), F32)],
        compiler_params=_params("arbitrary"),
    )(x, w, dy, add)


def _loss_head(h, w, target, *, name, tl=256):
    l, d = h.shape
    tl = _tile(l, tl, 16)

    def body(x_ref, w_ref, t_ref, loss_ref, dxb_ref, dw_ref):
        i = pl.program_id(0)
        xv = x_ref[...]
        r = lax.rsqrt(jnp.mean(xv * xv, axis=-1, keepdims=True) + NORM_EPS)
        xh = xv * r
        wv = w_ref[...]
        diff = xh * wv - t_ref[...]
        lpart = 0.5 * jnp.sum(jnp.mean(diff * diff, axis=-1, keepdims=True), axis=0, keepdims=True)
        dyv = diff * (1.0 / d)
        dxh = dyv * wv
        dx = r * (dxh - xh * jnp.mean(dxh * xh, axis=-1, keepdims=True))
        dxb_ref[...] = dx.astype(BF16)
        part = jnp.sum(dyv * xh, axis=0, keepdims=True)

        @pl.when(i == 0)
        def _():
            dw_ref[...] = part
            loss_ref[...] = lpart

        @pl.when(i > 0)
        def _():
            dw_ref[...] += part
            loss_ref[...] += lpart

    row = pl.BlockSpec((tl, d), lambda i: (i, 0))
    vec = pl.BlockSpec((1, d), lambda i: (0, 0))
    one = pl.BlockSpec((1, 1), lambda i: (0, 0))
    return pl.pallas_call(
        body, name=name, grid=(l // tl,), in_specs=[row, vec, row], out_specs=[one, row, vec],
        out_shape=[jax.ShapeDtypeStruct((1, 1), F32), jax.ShapeDtypeStruct((l, d), BF16), jax.ShapeDtypeStruct((1, d), F32)],
        compiler_params=_params("arbitrary"),
    )(h, w, target)


def _conv_pre(ext, w_ref):
    acc = ext * w_ref[CONV_K - 1:CONV_K, :]
    for k in range(CONV_K - 1):
        acc = acc + pltpu.roll(ext, CONV_K - 1 - k, axis=0) * w_ref[k:k + 1, :]
    return acc


def _conv_fwd(proj, col0, cw, cb, *, name, tl=512, tc=512):
    l = proj.shape[0]
    c = cw.shape[1]
    tl, tc = _tile(l, tl, 16), _tile(c, tc, LANE)
    assert col0 % tc == 0
    off, hb = col0 // tc, tl // CONV_HALO

    def body(x_ref, halo_ref, w_ref, b_ref, o_ref):
        i = pl.program_id(0)
        halo = jnp.where(i > 0, halo_ref[...], 0.0)
        ext = jnp.concatenate([halo, x_ref[...]], axis=0)
        pre = _conv_pre(ext, w_ref)[CONV_HALO:] + b_ref[...]
        o_ref[...] = pre * _sigmoid(pre)

    return pl.pallas_call(
        body, name=name, grid=(l // tl, c // tc),
        in_specs=[pl.BlockSpec((tl, tc), lambda i, j: (i, off + j)),
                  pl.BlockSpec((CONV_HALO, tc), lambda i, j: (jnp.maximum(i * hb - 1, 0), off + j)),
                  pl.BlockSpec((CONV_K, tc), lambda i, j: (0, j)), pl.BlockSpec((1, tc), lambda i, j: (0, j))],
        out_specs=pl.BlockSpec((tl, tc), lambda i, j: (i, j)),
        out_shape=jax.ShapeDtypeStruct((l, c), F32), compiler_params=_params("parallel", "parallel"),
    )(proj, proj, cw, cb)


def _conv_bwd(proj, col0, cw, cb, dout, dproj, *, name, tl=512, tc=512):
    l = proj.shape[0]
    c = dout.shape[1]
    tl, tc = _tile(l, tl, 16), _tile(c, tc, LANE)
    assert col0 % tc == 0
    off, hb, nt = col0 // tc, tl // CONV_HALO, l // tl
    n_ext = tl + CONV_HALO

    def body(x_ref, prev_ref, nxt_ref, d_ref, dn_ref, w_ref, b_ref, dproj_in, dx_ref, dw_ref, db_ref):
        del dproj_in
        i = pl.program_id(1)
        last = i == nt - 1
        halo = jnp.where(i > 0, prev_ref[...], 0.0)
        ext = jnp.concatenate([halo, x_ref[...], nxt_ref[...]], axis=0)
        pre = _conv_pre(ext, w_ref)[CONV_HALO:] + b_ref[...]
        dext = jnp.concatenate([d_ref[...], jnp.where(last, 0.0, dn_ref[...])], axis=0)
        s = _sigmoid(pre)
        dpre = dext * (s * (1.0 + pre * (1.0 - s)))
        dx = dpre * w_ref[CONV_K - 1:CONV_K, :]
        for k in range(CONV_K - 1):
            dx = dx + pltpu.roll(dpre, n_ext - (CONV_K - 1 - k), axis=0) * w_ref[k:k + 1, :]
        dx_ref[...] = dx[:tl].astype(dx_ref.dtype)
        dp = dpre[:tl]
        rows = [jnp.sum(dp * pltpu.roll(ext, CONV_K - 1 - k, axis=0)[CONV_HALO:CONV_HALO + tl], axis=0, keepdims=True)
                for k in range(CONV_K - 1)]
        rows.append(jnp.sum(dp * ext[CONV_HALO:CONV_HALO + tl], axis=0, keepdims=True))
        dwp = jnp.concatenate(rows, axis=0)
        dbp = jnp.sum(dp, axis=0, keepdims=True)

        @pl.when(i == 0)
        def _():
            dw_ref[...] = dwp
            db_ref[...] = dbp

        @pl.when(i > 0)
        def _():
            dw_ref[...] += dwp
            db_ref[...] += dbp

    cur = lambda j, i: (i, j)
    nxt = lambda j, i: (jnp.minimum((i + 1) * hb, nt * hb - 1), j)
    return pl.pallas_call(
        body, name=name, grid=(c // tc, nt),
        in_specs=[pl.BlockSpec((tl, tc), lambda j, i: (i, off + j)),
                  pl.BlockSpec((CONV_HALO, tc), lambda j, i: (jnp.maximum(i * hb - 1, 0), off + j)),
                  pl.BlockSpec((CONV_HALO, tc), lambda j, i: (jnp.minimum((i + 1) * hb, nt * hb - 1), off + j)),
                  pl.BlockSpec((tl, tc), cur), pl.BlockSpec((CONV_HALO, tc), nxt),
                  pl.BlockSpec((CONV_K, tc), lambda j, i: (0, j)), pl.BlockSpec((1, tc), lambda j, i: (0, j)), ANY],
        out_specs=[pl.BlockSpec((tl, tc), lambda j, i: (i, off + j)), pl.BlockSpec((CONV_K, tc), lambda j, i: (0, j)),
                   pl.BlockSpec((1, tc), lambda j, i: (0, j))],
        out_shape=[jax.ShapeDtypeStruct(dproj.shape, dproj.dtype), jax.ShapeDtypeStruct((CONV_K, c), F32),
                   jax.ShapeDtypeStruct((1, c), F32)],
        input_output_aliases={7: 0}, compiler_params=_params("parallel", "arbitrary"),
    )(proj, proj, proj, dout, dout, cw, cb, dproj)


def _pool_fwd(proj, col0, d_pool, *, name, tl=256):
    l = proj.shape[0]
    tl = _tile(l, tl, POOL_HALO)
    assert col0 % d_pool == 0
    off, hb, pg = col0 // d_pool, tl // POOL_HALO, d_pool // GROUPS

    def body(u_ref, halo_ref, o_ref):
        i = pl.program_id(0)
        halo = jnp.where(i > 0, halo_ref[...], 0.0)
        ext = jnp.concatenate([halo, u_ref[...]], axis=0)
        t = i * tl + lax.broadcasted_iota(jnp.int32, (tl, 1), 0)
        s, width = ext, 1
        for gi, win in enumerate(POOL_WINDOWS):
            while width < win:
                s = s + pltpu.roll(s, width, axis=0)
                width *= 2
            cnt = jnp.minimum(t + 1, win).astype(F32)
            sl = slice(gi * pg, (gi + 1) * pg)
            o_ref[:, sl] = (s[POOL_HALO:, sl] / cnt - ext[POOL_HALO:, sl]).astype(o_ref.dtype)

    return pl.pallas_call(
        body, name=name, grid=(l // tl,),
        in_specs=[pl.BlockSpec((tl, d_pool), lambda i: (i, off)),
                  pl.BlockSpec((POOL_HALO, d_pool), lambda i: (jnp.maximum(i * hb - 1, 0), off))],
        out_specs=pl.BlockSpec((tl, d_pool), lambda i: (i, 0)),
        out_shape=jax.ShapeDtypeStruct((l, d_pool), BF16), compiler_params=_params("parallel"),
    )(proj, proj)


def _pool_bwd(dp, dproj, col0, *, name, tl=256):
    l, d_pool = dp.shape
    tl = _tile(l, tl, POOL_HALO)
    assert col0 % d_pool == 0
    off, hb, nt, pg = col0 // d_pool, tl // POOL_HALO, l // tl, d_pool // GROUPS
    n_ext = tl + POOL_HALO

    def body(d_ref, nxt_ref, dproj_in, o_ref):
        del dproj_in
        i = pl.program_id(0)
        ext = jnp.concatenate([d_ref[...], jnp.where(i == nt - 1, 0.0, nxt_ref[...])], axis=0)
        t = i * tl + lax.broadcasted_iota(jnp.int32, (n_ext, 1), 0)
        for gi, win in enumerate(POOL_WINDOWS):
            sl = slice(gi * pg, (gi + 1) * pg)
            dg = ext[:, sl]
            s = dg / jnp.minimum(t + 1, win).astype(F32)
            width = 1
            while width < win:
                s = s + pltpu.roll(s, n_ext - width, axis=0)
                width *= 2
            o_ref[:, sl] = (s[:tl] - dg[:tl]).astype(o_ref.dtype)

    return pl.pallas_call(
        body, name=name, grid=(nt,),
        in_specs=[pl.BlockSpec((tl, d_pool), lambda i: (i, 0)),
                  pl.BlockSpec((POOL_HALO, d_pool), lambda i: (jnp.minimum((i + 1) * hb, nt * hb - 1), 0)), ANY],
        out_specs=pl.BlockSpec((tl, d_pool), lambda i: (i, off)),
        out_shape=jax.ShapeDtypeStruct(dproj.shape, dproj.dtype), input_output_aliases={2: 0},
        compiler_params=_params("parallel"),
    )(dp, dp, dproj)


def _put_heads(src, dst, col0, width, *, name, tl=512):
    l, c = src.shape
    tl = _tile(l, tl, 16)
    assert col0 % width == 0 and width % LANE == 0 and c // HEAD_DIM <= width

    def body(s_ref, dst_in, o_ref):
        del dst_in
        pick = (lax.broadcasted_iota(jnp.int32, (c, width), 0)
                == HEAD_DIM * lax.broadcasted_iota(jnp.int32, (c, width), 1)).astype(BF16)
        hi, lo = _split2(s_ref[...])
        o_ref[...] = (jnp.dot(hi, pick, preferred_element_type=F32)
                      + jnp.dot(lo, pick, preferred_element_type=F32)).astype(o_ref.dtype)

    return pl.pallas_call(
        body, name=name, grid=(l // tl,), in_specs=[pl.BlockSpec((tl, c), lambda i: (i, 0)), ANY],
        out_specs=pl.BlockSpec((tl, width), lambda i: (i, col0 // width)),
        out_shape=jax.ShapeDtypeStruct(dst.shape, dst.dtype), input_output_aliases={1: 0},
        compiler_params=_params("parallel"),
    )(src, dst)


def _pool_mix_fwd(pooled, pw, scale, mixed, col0, *, name, tm=1024):
    l, d_pool = pooled.shape
    pg = d_pool // GROUPS
    tm = _tile(l, tm, 16)
    assert col0 % pg == 0
    off = col0 // pg

    def body(a_ref, w_ref, s_ref, mixed_in, raw_ref, mixed_ref):
        del mixed_in
        raw = _dot(a_ref[...], w_ref[...])
        raw_ref[...] = raw
        mixed_ref[...] = (raw * s_ref[...]).astype(mixed_ref.dtype)

    blk = pl.BlockSpec((tm, pg), lambda i, g: (i, g))
    return pl.pallas_call(
        body, name=name, grid=(l // tm, GROUPS),
        in_specs=[blk, pl.BlockSpec((None, pg, pg), lambda i, g: (g, 0, 0)), pl.BlockSpec((1, pg), lambda i, g: (0, g)),
                  pl.BlockSpec(memory_space=pl.ANY)],
        out_specs=[blk, pl.BlockSpec((tm, pg), lambda i, g: (i, off + g))],
        out_shape=[jax.ShapeDtypeStruct((l, d_pool), F32), jax.ShapeDtypeStruct(mixed.shape, mixed.dtype)],
        input_output_aliases={3: 1}, compiler_params=_params("parallel", "parallel"),
    )(pooled, pw, scale, mixed)


def _pool_mix_bwd(dmixed, col0, raw, scale, pw, *, name, tm=1024):
    l, d_pool = raw.shape
    pg = d_pool // GROUPS
    tm = _tile(l, tm, 16)
    assert col0 % pg == 0
    off = col0 // pg

    def body(d_ref, raw_ref, s_ref, w_ref, draw_ref, dp_ref, ds_ref):
        i = pl.program_id(1)
        dv = d_ref[...]
        draw = (dv * s_ref[...]).astype(BF16)
        draw_ref[...] = draw
        dp_ref[...] = _dot(draw, w_ref[...], tb=True)
        part = jnp.sum(dv * raw_ref[...], axis=0, keepdims=True)

        @pl.when(i == 0)
        def _():
            ds_ref[...] = part

        @pl.when(i > 0)
        def _():
            ds_ref[...] += part

    blk = pl.BlockSpec((tm, pg), lambda g, i: (i, g))
    vec = pl.BlockSpec((1, pg), lambda g, i: (0, g))
    return pl.pallas_call(
        body, name=name, grid=(GROUPS, l // tm),
        in_specs=[pl.BlockSpec((tm, pg), lambda g, i: (i, off + g)), blk, vec,
                  pl.BlockSpec((None, pg, pg), lambda g, i: (g, 0, 0))],
        out_specs=[blk, blk, vec],
        out_shape=[jax.ShapeDtypeStruct((l, d_pool), BF16), jax.ShapeDtypeStruct((l, d_pool), F32),
                   jax.ShapeDtypeStruct((1, d_pool), F32)],
        compiler_params=_params("parallel", "arbitrary"),
    )(dmixed, raw, scale, pw)


def _group_mm_tn(a, b, *, name, tk=1024):
    l, d_pool = a.shape
    pg = d_pool // GROUPS
    tk = _tile(l, tk, 16)

    def body(a_ref, b_ref, o_ref):
        kk = pl.program_id(1)
        part = _dot(a_ref[...], b_ref[...], ta=True)

        @pl.when(kk == 0)
        def _():
            o_ref[...] = part

        @pl.when(kk > 0)
        def _():
            o_ref[...] += part

    blk = pl.BlockSpec((tk, pg), lambda g, kk: (kk, g))
    return pl.pallas_call(
        body, name=name, grid=(GROUPS, l // tk), in_specs=[blk, blk],
        out_specs=pl.BlockSpec((None, pg, pg), lambda g, kk: (g, 0, 0)),
        out_shape=jax.ShapeDtypeStruct((GROUPS, pg, pg), F32), compiler_params=_params("parallel", "arbitrary"),
    )(a, b)


def _split3(v):
    hi = v.astype(BF16)
    r1 = v - hi.astype(F32)
    mid = r1.astype(BF16)
    lo = (r1 - mid.astype(F32)).astype(BF16)
    return hi, mid, lo


def _tri_dot(tri, v):
    hi, mid, lo = _split3(v)
    d = lambda p: jnp.dot(tri, p, preferred_element_type=F32)
    return d(hi) + d(mid) + d(lo)


def _dot_tri(v, tri):
    hi, mid, lo = _split3(v)
    d = lambda p: jnp.dot(p, tri, preferred_element_type=F32)
    return d(hi) + d(mid) + d(lo)


def _split2(v):
    hi = v.astype(BF16)
    return hi, (v - hi.astype(F32)).astype(BF16)


def _scan_specs(hpg, nc, order):
    gw, q = hpg * HEAD_DIM, CHUNK
    b_off, c_off = (GROUPS * gw) // STATE, (GROUPS * gw) // STATE + GROUPS
    xs = pl.BlockSpec((q, gw), lambda g, c: (order(c), g))
    bm = pl.BlockSpec((q, STATE), lambda g, c: (order(c), b_off + g))
    cm = pl.BlockSpec((q, STATE), lambda g, c: (order(c), c_off + g))
    dtt = pl.BlockSpec((None, hpg, q), lambda g, c: (g, 0, order(c)))
    vec = pl.BlockSpec((1, gw), lambda g, c: (0, g))
    hcol = pl.BlockSpec((None, hpg, 1), lambda g, c: (g, 0, 0))
    st = pl.BlockSpec((None, None, STATE, gw), lambda g, c: (g, order(c), 0, 0))
    return xs, bm, cm, dtt, vec, hcol, st


def _scan_common(dtx_ref, dtrt_ref, biasx_ref, biast_ref, alogx_ref, alogt_ref):
    q = CHUNK
    dt = _softplus(dtx_ref[...] + biasx_ref[...])
    a = -jnp.exp(alogx_ref[...])
    dtt = _softplus(dtrt_ref[...] + biast_ref[...])
    at = -jnp.exp(alogt_ref[...])
    row = lax.broadcasted_iota(jnp.int32, (q, q), 0)
    col = lax.broadcasted_iota(jnp.int32, (q, q), 1)
    lower = (row >= col).astype(BF16)
    upper = (row <= col).astype(BF16)
    acum = _tri_dot(lower, dt * a)
    acum_t = _dot_tri(dtt * at, upper)
    return dt, a, acum, acum_t, row, col, upper


def _gated(yv, z):
    s = _sigmoid(z)
    sz = z * s
    gv = yv * sz
    return gv, lax.rsqrt(jnp.mean(gv * gv, axis=-1, keepdims=True) + NORM_EPS), sz, s


def _scan_fwd(xbc, dtx, dtrt, biasx, biast, alogx, alogt, dskx, proj, wn, d_mix, *, name, ride=None):
    l = xbc.shape[0]
    hpg = dtrt.shape[1]
    gw, q, nc = hpg * HEAD_DIM, CHUNK, l // CHUNK
    assert hpg % 2 == 0
    xs_s, bm_s, cm_s, dtt_s, vec, hcol, st_s = _scan_specs(hpg, nc, lambda c: c)

    def body(xs_ref, b_ref, c_ref, dtx_ref, dtrt_ref, biasx_ref, biast_ref, alogx_ref, alogt_ref, dsk_ref, z_ref, wn_ref,
             y_ref, st_ref, mixed_ref, h_ref):
        @pl.when(pl.program_id(1) == 0)
        def _():
            h_ref[...] = jnp.zeros_like(h_ref)

        dt, _, acum, acum_t, row, col, _ = _scan_common(dtx_ref, dtrt_ref, biasx_ref, biast_ref, alogx_ref, alogt_ref)
        causal = row >= col
        xs = xs_ref[...]
        xdt = xs * dt
        xdtb = xdt.astype(BF16)
        bmat, cmat = b_ref[...], c_ref[...]
        cb = _dot(cmat, bmat, tb=True)
        h_all = h_ref[...]
        st_ref[...] = h_all
        rest = _dot(cmat, h_all) * jnp.exp(acum) + dsk_ref[...] * xs
        first = lax.broadcasted_iota(jnp.int32, (q, 2 * HEAD_DIM), 1) < HEAD_DIM
        parts = []
        for jp in range(hpg // 2):
            pair = slice(2 * jp * HEAD_DIM, (2 * jp + 2) * HEAD_DIM)
            xp = xdtb[:, pair]
            ys = []
            for j in (2 * jp, 2 * jp + 1):
                a_col, a_row = acum[:, j * HEAD_DIM:j * HEAD_DIM + 1], acum_t[j:j + 1, :]
                lm = jnp.exp(jnp.where(causal, a_col - a_row, NEG_BIG))
                ys.append(_dot(cb * lm, xp))
            parts.append(jnp.where(first, ys[0], ys[1]))
        yv = jnp.concatenate(parts, axis=1) + rest
        y_ref[...] = yv
        gv, r, _, _ = _gated(yv, z_ref[...])
        mixed_ref[...] = (gv * r * wn_ref[...]).astype(mixed_ref.dtype)
        a_last = acum[q - 1:q, :]
        h_ref[...] = jnp.exp(a_last) * h_all + _dot(bmat, xdt * jnp.exp(a_last - acum), ta=True)

    return _call(
        body, name=name, grid=(GROUPS, nc),
        in_specs=[xs_s, bm_s, cm_s, xs_s, dtt_s, vec, hcol, vec, hcol, vec, xs_s, vec], out_specs=[xs_s, st_s, xs_s],
        out_shape=[jax.ShapeDtypeStruct((l, GROUPS * gw), F32), jax.ShapeDtypeStruct((GROUPS, nc, STATE, gw), F32),
                   jax.ShapeDtypeStruct((l, d_mix), BF16)],
        scratch=[pltpu.VMEM((STATE, gw), F32)], sem=("parallel", "arbitrary"),
        args=(xbc, xbc, xbc, dtx, dtrt, biasx, biast, alogx, alogt, dskx, proj, wn), ride=ride)


def _scan_bwd(xbc, dtx, dtrt, biasx, biast, alogx, alogt, dskx, states, dmixed, y, proj, wn, *, name, ride=None):
    l = xbc.shape[0]
    hpg = dtrt.shape[1]
    gw, q, nc = hpg * HEAD_DIM, CHUNK, l // CHUNK
    rev = lambda c: nc - 1 - c
    xs_s, bm_s, cm_s, dtt_s, vec, hcol, st_s = _scan_specs(hpg, nc, rev)
    bc_s = pl.BlockSpec((q, STATE), lambda g, c: (rev(c), g))

    def body(xs_ref, b_ref, c_ref, dtx_ref, dtrt_ref, biasx_ref, biast_ref, alogx_ref, alogt_ref, dsk_ref, st_ref,
             dmix_ref, y_ref, z_ref, wn_ref,
             dxs_ref, db_ref, dc_ref, ddt_ref, dbias_ref, dalog_ref, ddsk_ref, dz_ref, dwn_ref, dh_ref):
        step = pl.program_id(1)

        @pl.when(step == 0)
        def _():
            dh_ref[...] = jnp.zeros_like(dh_ref)

        yv, z, dmix = y_ref[...], z_ref[...], dmix_ref[...]
        gv, rn, sz, sg = _gated(yv, z)
        gh = gv * rn
        dgh = dmix * wn_ref[...]
        dg = rn * (dgh - gh * jnp.mean(dgh * gh, axis=-1, keepdims=True))
        dyv = dg * sz
        dz_ref[...] = (dg * yv * (sg * (1.0 + z * (1.0 - sg)))).astype(dz_ref.dtype)
        p_wn = jnp.sum(dmix * gh, axis=0, keepdims=True)

        dt, a, acum, acum_t, row, col, upper = _scan_common(dtx_ref, dtrt_ref, biasx_ref, biast_ref, alogx_ref, alogt_ref)
        causal, anti, strict = row >= col, row <= col, row > col
        xs = xs_ref[...]
        xdt = xs * dt
        xdtb, dyb = xdt.astype(BF16), dyv.astype(BF16)
        bmat, cmat = b_ref[...], c_ref[...]
        cb = _dot(cmat, bmat, tb=True)
        cbt = _dot(bmat, cmat, tb=True)
        h_all, dh = st_ref[...], dh_ref[...]
        a_last = acum[q - 1:q, :]
        ea, ee, gam = jnp.exp(acum), jnp.exp(a_last - acum), jnp.exp(a_last)
        zc = _dot(cmat, h_all)
        bdh = _dot(bmat, dh)
        dz = dyv * ea
        xe = xdt * ee
        dc_state = _dot(dz, h_all, tb=True)
        db_state = _dot(xe, dh, tb=True)
        dh_ref[...] = gam * dh + _dot(cmat, dz, ta=True)
        lane = lax.broadcasted_iota(jnp.int32, (q, 2 * HEAD_DIM), 1)
        first = lane < HEAD_DIM
        ones = jnp.ones((q // 2, 2 * HEAD_DIM), BF16)
        dcb = jnp.zeros((q, q), F32)
        dxdt_parts, da_parts = [], []
        for jp in range(hpg // 2):
            pair = slice(2 * jp * HEAD_DIM, (2 * jp + 2) * HEAD_DIM)
            xp, dyp = xdtb[:, pair], dyb[:, pair]
            dxs_r, das_r = [], []
            for r, j in enumerate((2 * jp, 2 * jp + 1)):
                seg = acum[:, j * HEAD_DIM:j * HEAD_DIM + 1] - acum_t[j:j + 1, :]
                lm = jnp.exp(jnp.where(causal, seg, NEG_BIG))
                lmt = jnp.exp(jnp.where(anti, -seg, NEG_BIG))
                mm, mmt = cb * lm, cbt * lmt
                keep = first if r == 0 else jnp.logical_not(first)
                dyk = jnp.where(keep, dyp, 0)
                dxs_r.append(_dot(mmt, dyp))
                dm = _dot(dyk, xp, tb=True)
                dcb = dcb + dm * lm
                corner = jnp.dot(upper, (dm * mm).astype(BF16), preferred_element_type=F32)
                corner = jnp.where(strict, corner, 0.0)
                chi, clo = _split2(corner[:, :q // 2] + corner[:, q // 2:])
                das_r.append(jnp.dot(chi, ones, preferred_element_type=F32) + jnp.dot(clo, ones, preferred_element_type=F32))
            dxdt_parts.append(jnp.where(first, dxs_r[0], dxs_r[1]))
            da_parts.append(jnp.where(first, das_r[0], das_r[1]))
        dxdt = jnp.concatenate(dxdt_parts, axis=1) + bdh * ee
        da_intra = jnp.concatenate(da_parts, axis=1)
        dxs_ref[...] = dxdt * dt + dsk_ref[...] * dyv
        dc_ref[...] = dc_state + _dot(dcb, bmat)
        db_ref[...] = db_state + _dot(dcb, cmat, ta=True)
        ri = lax.broadcasted_iota(jnp.int32, (gw, gw), 0) // HEAD_DIM
        ci = lax.broadcasted_iota(jnp.int32, (gw, gw), 1) // HEAD_DIM
        blockdiag = (ri == ci).astype(BF16)

        def head_sum(v):
            hi, lo = _split2(v)
            return (jnp.dot(hi, blockdiag, preferred_element_type=F32)
                    + jnp.dot(lo, blockdiag, preferred_element_type=F32))

        xt = xe * bdh
        small = jnp.concatenate([
            jnp.sum(xt, axis=0, keepdims=True) + gam * jnp.sum(dh * h_all, axis=0, keepdims=True),
            jnp.sum(dyv * xs, axis=0, keepdims=True), jnp.zeros((6, gw), F32)], axis=0)
        small = head_sum(small)
        rows = lax.broadcasted_iota(jnp.int32, (q, 1), 0)
        da_local = head_sum(dyv * zc * ea - xt) + jnp.where(rows == q - 1, small[0:1, :], 0.0)
        hi, lo = _split2(da_local)
        d_dta = (da_intra + jnp.dot(upper, hi, preferred_element_type=F32)
                 + jnp.dot(upper, lo, preferred_element_type=F32))
        d_raw = (d_dta * a + head_sum(dxdt * xs)) * _sigmoid(dtx_ref[...] + biasx_ref[...])
        ddt_ref[...] = d_raw
        p_bias = jnp.sum(d_raw, axis=0, keepdims=True)
        p_alog = jnp.sum(d_dta * dt, axis=0, keepdims=True) * a
        p_dsk = small[1:2, :]

        @pl.when(step == 0)
        def _():
            dbias_ref[...] = p_bias
            dalog_ref[...] = p_alog
            ddsk_ref[...] = p_dsk
            dwn_ref[...] = p_wn

        @pl.when(step > 0)
        def _():
            dbias_ref[...] += p_bias
            dalog_ref[...] += p_alog
            ddsk_ref[...] += p_dsk
            dwn_ref[...] += p_wn

    wide = jax.ShapeDtypeStruct((l, GROUPS * gw), F32)
    narrow = jax.ShapeDtypeStruct((l, GROUPS * STATE), F32)
    vshape = jax.ShapeDtypeStruct((1, GROUPS * gw), F32)
    return _call(
        body, name=name, grid=(GROUPS, nc),
        in_specs=[xs_s, bm_s, cm_s, xs_s, dtt_s, vec, hcol, vec, hcol, vec, st_s, xs_s, xs_s, xs_s, vec],
        out_specs=[xs_s, bc_s, bc_s, xs_s, vec, vec, vec, xs_s, vec],
        out_shape=[wide, narrow, narrow, wide, vshape, vshape, vshape, jax.ShapeDtypeStruct(proj.shape, BF16), vshape],
        scratch=[pltpu.VMEM((STATE, gw), F32)], sem=("parallel", "arbitrary"),
        args=(xbc, xbc, xbc, dtx, dtrt, biasx, biast, alogx, alogt, dskx, states, dmixed, y, proj, wn), ride=ride)


def _exchange(srcs, scatter, *, name):
    ride = _Ride(srcs, scatter)

    def body(*refs):
        src, dst, sems = refs[:ride.n], refs[ride.n:2 * ride.n], refs[2 * ride.n:]
        ride.start(src, dst, sems)
        ride.finish(src, dst, sems)

    return pl.pallas_call(body, name=name, in_specs=ride.in_specs, out_specs=ride.out_specs, out_shape=ride.out_shape,
                          scratch_shapes=ride.scratch)(*srcs)


def _adamw(parts, w, m, v, *, name, tr=256):
    lead = w.ndim == 3
    r, c = w.shape[-2:]
    tr = _tile(r, tr, 16)
    c1, c2 = 1.0 / (1.0 - ADAM_B1 ** ADAM_STEP), 1.0 / (1.0 - ADAM_B2 ** ADAM_STEP)

    def body(p_ref, w_ref, m_ref, v_ref, g_ref, d_ref, nm_ref, nv_ref):
        g = p_ref[0].astype(F32)
        for p in range(1, N_DEV):
            g = g + p_ref[p].astype(F32)
        nm = ADAM_B1 * m_ref[...] + (1.0 - ADAM_B1) * g
        nv = ADAM_B2 * v_ref[...] + (1.0 - ADAM_B2) * (g * g)
        g_ref[...] = g
        nm_ref[...] = nm
        nv_ref[...] = nv
        d_ref[...] = -ADAM_LR * ((nm * c1) / (jnp.sqrt(nv * c2) + ADAM_EPS) + ADAM_WD * w_ref[...])

    blk = pl.BlockSpec((None, tr, c), lambda i: (0, i, 0)) if lead else pl.BlockSpec((tr, c), lambda i: (i, 0))
    out = jax.ShapeDtypeStruct(w.shape, F32)
    return pl.pallas_call(
        body, name=name, grid=(r // tr,), in_specs=[pl.BlockSpec((N_DEV, tr, c), lambda i: (0, i, 0)), blk, blk, blk],
        out_specs=[blk, blk, blk, blk], out_shape=[out, out, out, out], compiler_params=_params("parallel"),
    )(parts, w, m, v)


def kernel(x, attn_norm_w, w_in, conv_w, conv_b, dt_bias, a_log, d_skip, ssd_norm_w, pool_w, pool_scale, w_out, ffn_norm_w, w_gate, w_up, w_down, final_norm_w, loss_target, m_attn_norm_w, m_w_in, m_conv_w, m_conv_b, m_dt_bias, m_a_log, m_d_skip, m_ssd_norm_w, m_pool_w, m_pool_scale, m_w_out, m_ffn_norm_w, m_w_gate, m_w_up, m_w_down, m_final_norm_w, v_attn_norm_w, v_w_in, v_conv_w, v_conv_b, v_dt_bias, v_a_log, v_d_skip, v_ssd_norm_w, v_pool_w, v_pool_scale, v_w_out, v_ffn_norm_w, v_w_gate, v_w_up, v_w_down, v_final_norm_w):
    l, d = x.shape[1], x.shape[2]
    heads = dt_bias.shape[1]
    hpg = heads // GROUPS
    d_ssm = heads * HEAD_DIM
    conv_ch = conv_b.shape[1]
    d_pool = pool_scale.shape[1]
    pg = d_pool // GROUPS
    d_mix = d_ssm + d_pool
    d_ff = w_gate.shape[2] * N_DEV
    d_in = w_in.shape[2] * N_DEV
    dt_pad = -(-heads // LANE) * LANE
    o_u, o_xbc, o_dt = d_ssm, d_ssm + d_pool, d_ssm + d_pool + conv_ch
    d_inp = o_dt + dt_pad
    tn_in = _tile(d_inp, 2560, LANE)
    g_dt, g_u = d_ssm + conv_ch, d_ssm + conv_ch + heads

    x2, tgt = x[0], loss_target[0]

    tr_ = lambda t: jnp.transpose(t, (0, 2, 1))
    gi, gp, gc = _exchange([tr_(w_in)[0].astype(BF16), pool_w[0].astype(BF16), conv_w[0]], False, name="gather_w_in")
    win = gi.reshape(d_in, d)
    wp = jnp.concatenate([win[:d_ssm], win[g_u:], win[d_ssm:g_dt], win[g_dt:g_u],
                          jnp.zeros((dt_pad - heads, d), BF16)], axis=0)
    pw = jnp.transpose(gp, (1, 0, 2, 3)).reshape(GROUPS, pg, pg)
    cw = jnp.transpose(gc, (1, 0, 2)).reshape(CONV_K, conv_ch)

    def expand(vec):
        return jnp.repeat(vec, HEAD_DIM, axis=1)

    def per_head(vec):
        return vec[:, ::HEAD_DIM]

    bias_x, alog_x, dsk_x = expand(dt_bias), expand(a_log), expand(d_skip)
    bias_c, alog_c = dt_bias.reshape(GROUPS, hpg, 1), a_log.reshape(GROUPS, hpg, 1)

    h0 = _rms_fwd(x2, attn_norm_w, name="attn_norm")
    proj, go, gg = _mm(h0, wp, name="in_proj", tb=True, tm=512, tn=tn_in, tk=d,
                       ride=_Ride([w_out[0].astype(BF16), tr_(w_gate)[0].astype(BF16)], False))
    wo = go.reshape(d_mix, d)
    xbc = _conv_fwd(proj, o_xbc, cw, conv_b, name="conv_fwd")
    dt_raw = proj[:, o_dt:o_dt + heads]
    dtx = expand(dt_raw)
    dtrt = jnp.transpose(dt_raw.reshape(l, GROUPS, hpg), (1, 2, 0))
    y, states, mixed, gu = _scan_fwd(xbc, dtx, dtrt, bias_x, bias_c, alog_x, alog_c, dsk_x, proj, ssd_norm_w, d_mix,
                                     name="ssd_fwd", ride=_Ride([tr_(w_up)[0].astype(BF16)], False))
    pooled = _pool_fwd(proj, o_u, d_pool, name="pool_fwd")
    pool_raw, mixed = _pool_mix_fwd(pooled, pw, pool_scale, mixed, d_ssm, name="pool_mix")
    h1 = _mm(mixed, wo, name="out_proj", tm=512, tk=d_mix, residual=x2)
    h1n = _rms_fwd(h1, ffn_norm_w, name="ffn_norm")
    gate, up, act, gd = _gate_up(h1n, gg, gu, name="gate_up", ride=_Ride([w_down[0].astype(BF16)], False))
    h2 = _mm_shards([(act, gd)], name="down_proj", per_step=4, residual=h1)

    loss11, dh2b, g_final = _loss_head(h2, final_norm_w.reshape(1, d), tgt, name="loss_head")
    dgate, dup = _gate_up_bwd(dh2b, gd, gate, up, name="gate_up_bwd")
    g_wd = _grad_shards(act, dh2b, name="grad_w_down")
    dh1n, r_wd = _mm_shards([(dgate, gg), (dup, gu)], name="dgate_dup", out_dtype=BF16, ride=_Ride([g_wd], True))
    g_wg = _grad_shards(dgate, h1n, name="grad_w_gate")
    g_wu = _grad_shards(dup, h1n, name="grad_w_up")
    dh1b, g_ffn = _rms_bwd(h1, ffn_norm_w, dh1n, dh2b, BF16, name="ffn_norm_bwd")
    dmixed = _mm(dh1b, wo, name="dmixed", tb=True, tk=d)
    g_wo = _mm(mixed, dh1b, name="grad_w_out", ta=True, out_dtype=BF16, tm=1024, tn=1024, tk=2048)
    draw, dpooled, g_pscale = _pool_mix_bwd(dmixed, d_ssm, pool_raw, pool_scale, pw, name="pool_mix_bwd")
    g_pw = _group_mm_tn(pooled, draw, name="grad_pool_w")
    s_pw = jnp.transpose(g_pw.reshape(GROUPS, N_DEV, pg // N_DEV, pg), (1, 0, 2, 3)).astype(BF16)
    dxs, dbm, dcm, ddtx, g_bias, g_alog, g_dsk, dproj, g_ssdn, r_wg, r_wu, r_wo, r_pw = _scan_bwd(
        xbc, dtx, dtrt, bias_x, bias_c, alog_x, alog_c, dsk_x, states, dmixed, y, proj, ssd_norm_w, name="ssd_bwd",
        ride=_Ride([g_wg, g_wu, g_wo.reshape(N_DEV, d_mix // N_DEV, d), s_pw], True))
    dproj = _pool_bwd(dpooled, dproj, o_u, name="pool_bwd")
    n_bc = GROUPS * STATE
    segs = [(dxs, 0, d_ssm), (dbm, d_ssm, n_bc), (dcm, d_ssm + n_bc, n_bc)]
    g_cw, g_cb = [], []
    for si, (dseg, c0, width) in enumerate(segs):
        dproj, b, c_ = _conv_bwd(proj, o_xbc + c0, cw[:, c0:c0 + width], conv_b[:, c0:c0 + width], dseg, dproj,
                                 name=f"conv_bwd{si}")
        g_cw.append(b)
        g_cb.append(c_)
    dproj = _put_heads(ddtx, dproj, o_dt, dt_pad, name="put_ddt")
    g_wp = _mm(dproj, h0, name="grad_w_in", ta=True, out_dtype=BF16, tm=tn_in, tn=1024, tk=1024)
    g_win = jnp.concatenate([g_wp[:d_ssm], g_wp[o_xbc:o_dt + heads], g_wp[o_u:o_xbc]], axis=0)
    def cols8(g):
        return jnp.transpose(g.reshape(g.shape[0], N_DEV, g.shape[1] // N_DEV), (1, 0, 2))
    dh0, r_in, r_cw = _mm(dproj, wp, name="dh0", out_dtype=BF16, tk=tn_in,
                          ride=_Ride([g_win.reshape(N_DEV, d_in // N_DEV, d), cols8(jnp.concatenate(g_cw, axis=1))], True))
    dx, g_attn = _rms_bwd(x2, attn_norm_w, dh0, dh1b, F32, name="attn_norm_bwd")

    smalls = [g_attn, jnp.concatenate(g_cb, axis=1), per_head(g_bias), per_head(g_alog), per_head(g_dsk), g_ssdn,
              g_pscale, g_ffn, g_final]
    small_w = [attn_norm_w, conv_b, dt_bias, a_log, d_skip, ssd_norm_w, pool_scale, ffn_norm_w, final_norm_w.reshape(1, d)]
    small_m = [m_attn_norm_w, m_conv_b, m_dt_bias, m_a_log, m_d_skip, m_ssd_norm_w, m_pool_scale, m_ffn_norm_w,
               m_final_norm_w.reshape(1, d)]
    small_v = [v_attn_norm_w, v_conv_b, v_dt_bias, v_a_log, v_d_skip, v_ssd_norm_w, v_pool_scale, v_ffn_norm_w,
               v_final_norm_w.reshape(1, d)]
    sizes = [s.shape[1] for s in smalls]
    n_small = sum(sizes)
    n_pad = -(-n_small // (16 * LANE)) * (16 * LANE)
    rows = n_pad // LANE
    def pack(vs):
        return jnp.pad(jnp.concatenate(vs, axis=1), ((0, 0), (0, n_pad - n_small))).reshape(rows, LANE)
    (r_small,) = _exchange([pack(smalls)], False, name="gather_small_grads")

    def big(parts, w, m, v, nm):
        shp = w.shape
        if w.ndim == 3:
            return _adamw(parts, w, m, v, name=nm)
        r2 = lambda t: t.reshape(-1, shp[-1])
        outs = _adamw(parts.reshape(N_DEV, -1, shp[-1]), r2(w), r2(m), r2(v), name=nm)
        return [o.reshape(shp) for o in outs]

    def big_t(parts, w, m, v, nm):
        return [tr_(o) for o in _adamw(parts, tr_(w), tr_(m), tr_(v), name=nm)]

    def big_flat(parts, w, m, v, nm):
        fl = lambda t: tr_(t).reshape(-1, LANE)
        outs = _adamw(parts.reshape(N_DEV, -1, LANE), fl(w), fl(m), fl(v), name=nm, tr=2048)
        return [tr_(o.reshape(1, w.shape[2], w.shape[1])) for o in outs]

    res = {
        "w_in": big_flat(r_in, w_in, m_w_in, v_w_in, "adamw_w_in"),
        "conv_w": big(r_cw, conv_w, m_conv_w, v_conv_w, "adamw_conv_w"),
        "pool_w": big(r_pw, pool_w, m_pool_w, v_pool_w, "adamw_pool_w"),
        "w_out": big(r_wo, w_out, m_w_out, v_w_out, "adamw_w_out"),
        "w_gate": big_t(r_wg, w_gate, m_w_gate, v_w_gate, "adamw_w_gate"),
        "w_up": big_t(r_wu, w_up, m_w_up, v_w_up, "adamw_w_up"),
        "w_down": big(r_wd, w_down, m_w_down, v_w_down, "adamw_w_down"),
    }
    s_out = _adamw(r_small, pack(small_w), pack(small_m), pack(small_v), name="adamw_small")
    names = ["attn_norm_w", "conv_b", "dt_bias", "a_log", "d_skip", "ssd_norm_w", "pool_scale", "ffn_norm_w", "final_norm_w"]
    offs = [sum(sizes[:i]) for i in range(len(sizes))]
    for i, nm in enumerate(names):
        shp = (d,) if nm == "final_norm_w" else (1, sizes[i])
        res[nm] = [o.reshape(1, n_pad)[:, offs[i]:offs[i] + sizes[i]].reshape(shp) for o in s_out]

    loss = lax.psum(loss11[0, 0], ("x", "y", "c"))
    order = ["attn_norm_w", "w_in", "conv_w", "conv_b", "dt_bias", "a_log", "d_skip", "ssd_norm_w", "pool_w", "pool_scale",
             "w_out", "ffn_norm_w", "w_gate", "w_up", "w_down", "final_norm_w"]
    outs = [loss, dx.reshape(x.shape)]
    for part in range(4):
        outs += [res[nm][part] for nm in order]
    return tuple(outs)
```

```python
import functools
import math

import jax
import jax.numpy as jnp
from jax import lax
from jax.experimental import pallas as pl
from jax.experimental.pallas import tpu as pltpu

F32 = jnp.float32
BF16 = jnp.bfloat16

NORM_EPS = 1e-5
HEAD_DIM = 64
STATE = 128
CHUNK = 256
GROUPS = 4
CONV_K = 4
POOL_WINDOWS = (2, 4, 8, 16)
POOL_HALO = 16
CONV_HALO = 8
LANE = 128
N_DEV = 8
ADAM_LR, ADAM_B1, ADAM_B2, ADAM_EPS, ADAM_WD, ADAM_STEP = 0.001, 0.9, 0.999, 1e-08, 0.01, 10
NEG_BIG = -1e30
EPILOGUE_ROWS = 256
MESH_ID = pl.DeviceIdType.MESH


def _tile(n, pref, mult):
    if n <= pref:
        return n
    t = (pref // mult) * mult
    while t >= mult:
        if n % t == 0:
            return t
        t -= mult
    return n


def _params(*sem):
    return pltpu.CompilerParams(dimension_semantics=sem)


ANY = pl.BlockSpec(memory_space=pl.ANY)


def _peer(k):
    x, y, c = lax.axis_index("x"), lax.axis_index("y"), lax.axis_index("c")
    return (1 - x if k & 4 else x, 1 - y if k & 2 else y, 1 - c if k & 1 else c)


def _my_index():
    return 4 * lax.axis_index("x") + 2 * lax.axis_index("y") + lax.axis_index("c")


class _Ride:
    def __init__(self, srcs, scatter):
        self.srcs, self.scatter, self.n = list(srcs), scatter, len(srcs)
        self.two_level = not scatter
        self.in_specs = [ANY] * self.n
        self.out_specs = [ANY] * self.n
        self.out_shape = [jax.ShapeDtypeStruct((N_DEV,) + tuple(s.shape[1:] if scatter else s.shape), s.dtype)
                          for s in self.srcs]
        self.scratch = [pltpu.SemaphoreType.DMA((self.n, N_DEV - 1)), pltpu.SemaphoreType.DMA((self.n, N_DEV - 1)),
                        pltpu.SemaphoreType.DMA((self.n,))]

    def _copies(self, src, dst, sems):
        send_sems, recv_sems, local_sems = sems
        me = _my_index()
        local = [pltpu.make_async_copy(src[t].at[me] if self.scatter else src[t], dst[t].at[me], local_sems.at[t])
                 for t in range(self.n)]
        remote = {}
        for k in range(1, N_DEV):
            peer, pidx = _peer(k), me ^ k
            remote[k] = [pltpu.make_async_remote_copy(
                src_ref=src[t].at[pidx] if self.scatter else src[t], dst_ref=dst[t].at[me],
                send_sem=send_sems.at[t, k - 1], recv_sem=recv_sems.at[t, k - 1],
                device_id=peer, device_id_type=MESH_ID) for t in range(self.n)]
        return local, remote

    def _forwards(self, dst, sems):
        send_sems, recv_sems, _ = sems
        me = _my_index()
        return {k: [pltpu.make_async_remote_copy(
            src_ref=dst[t].at[me ^ k], dst_ref=dst[t].at[me ^ k],
            send_sem=send_sems.at[t, (k ^ 1) - 1], recv_sem=recv_sems.at[t, (k ^ 1) - 1],
            device_id=_peer(1), device_id_type=MESH_ID) for t in range(self.n)] for k in (2, 4, 6)}

    def start(self, src, dst, sems):
        local, remote = self._copies(src, dst, sems)
        for cp in local:
            cp.start()
        for k in (2, 4, 6, 1) if self.two_level else range(1, N_DEV):
            for cp in remote[k]:
                cp.start()

    def relay(self, src, dst, sems):
        if not self.two_level:
            return
        _, remote = self._copies(src, dst, sems)
        fwd = self._forwards(dst, sems)
        for k in (2, 4, 6):
            for t in range(self.n):
                remote[k][t].wait_recv()
                fwd[k][t].start()

    def finish(self, src, dst, sems):
        local, remote = self._copies(src, dst, sems)
        if self.two_level:
            fwd = self._forwards(dst, sems)
            for k in (1, 3, 5, 7):
                for cp in remote[k]:
                    cp.wait_recv()
            for k in (1, 2, 4, 6):
                for cp in remote[k]:
                    cp.wait_send()
            for k in (2, 4, 6):
                for cp in fwd[k]:
                    cp.wait_send()
        else:
            for k in range(1, N_DEV):
                for cp in remote[k]:
                    cp.wait_send()
            for k in range(1, N_DEV):
                for cp in remote[k]:
                    cp.wait_recv()
        for lc in local:
            lc.wait()


RELAY_AT = 0.75


def _call(body, *, name, grid, in_specs, out_specs, out_shape, args, sem, scratch=(), ride=None):
    in_specs, out_specs, out_shape, scratch = list(in_specs), list(out_specs), list(out_shape), list(scratch)
    if ride is None:
        return pl.pallas_call(body, name=name, grid=grid, in_specs=in_specs, out_specs=out_specs, out_shape=out_shape,
                              scratch_shapes=scratch, compiler_params=_params(*sem))(*args)
    n_in, n_out, n_sc, nr = len(in_specs), len(out_specs), len(scratch), ride.n

    def full(*refs):
        ins, csrc = refs[:n_in], refs[n_in:n_in + nr]
        o0 = n_in + nr
        outs, cdst = refs[o0:o0 + n_out], refs[o0 + n_out:o0 + n_out + nr]
        s0 = o0 + n_out + nr
        sc, csem = refs[s0:s0 + n_sc], refs[s0 + n_sc:]
        ids = [pl.program_id(ax) for ax in range(len(grid))]
        step = functools.reduce(lambda acc, ig: acc * ig[1] + ig[0], zip(ids, grid), 0)
        total = functools.reduce(lambda p, g: p * g, grid, 1)

        @pl.when(step == 0)
        def _():
            ride.start(csrc, cdst, csem)

        body(*ins, *outs, *sc)

        @pl.when(step == min(int(total * RELAY_AT), total - 1))
        def _():
            ride.relay(csrc, cdst, csem)

        @pl.when(step == total - 1)
        def _():
            ride.finish(csrc, cdst, csem)

    return pl.pallas_call(
        full, name=name, grid=grid, in_specs=in_specs + ride.in_specs, out_specs=out_specs + ride.out_specs,
        out_shape=out_shape + ride.out_shape, scratch_shapes=scratch + ride.scratch,
        compiler_params=_params(*(["arbitrary"] * len(grid))),
    )(*args, *ride.srcs)


def _sigmoid(x):
    return 1.0 / (1.0 + jnp.exp(-x))


def _softplus(x):
    return jnp.maximum(x, 0.0) + jnp.log(1.0 + jnp.exp(-jnp.abs(x)))


def _dot(a, b, ta=False, tb=False):
    dn = (((0 if ta else 1,), (1 if tb else 0,)), ((), ()))
    return lax.dot_general(a.astype(BF16), b.astype(BF16), dn, preferred_element_type=F32)


def _mm(a, b, *, name, ta=False, tb=False, out_dtype=F32, tm=1024, tn=1024, tk=2048, residual=None, ride=None):
    m = a.shape[1] if ta else a.shape[0]
    k = a.shape[0] if ta else a.shape[1]
    n = b.shape[0] if tb else b.shape[1]
    assert k == (b.shape[1] if tb else b.shape[0])
    tm, tn = _tile(m, tm, LANE if ta else 8), _tile(n, tn, LANE)
    tk = _tile(k, tk, LANE if not (ta and tb) else LANE)
    nk = k // tk
    has_res = residual is not None

    def body(*refs):
        a_ref, b_ref = refs[0], refs[1]
        r_ref = refs[2] if has_res else None
        o_ref = refs[3] if has_res else refs[2]
        part = _dot(a_ref[...], b_ref[...], ta, tb)

        def finish(acc):
            if has_res:
                acc = acc + r_ref[...]
            o_ref[...] = acc.astype(o_ref.dtype)

        if nk == 1:
            finish(part)
        else:
            acc_ref = refs[-1]
            kk = pl.program_id(2)

            @pl.when(kk == 0)
            def _():
                acc_ref[...] = part

            @pl.when(kk > 0)
            def _():
                acc_ref[...] += part

            @pl.when(kk == nk - 1)
            def _():
                finish(acc_ref[...])

    a_spec = pl.BlockSpec((tk, tm), lambda i, j, kk: (kk, i)) if ta else pl.BlockSpec((tm, tk), lambda i, j, kk: (i, kk))
    b_spec = pl.BlockSpec((tn, tk), lambda i, j, kk: (j, kk)) if tb else pl.BlockSpec((tk, tn), lambda i, j, kk: (kk, j))
    o_spec = pl.BlockSpec((tm, tn), lambda i, j, kk: (i, j))
    in_specs = [a_spec, b_spec] + ([o_spec] if has_res else [])
    args = (a, b) + ((residual,) if has_res else ())
    res = _call(body, name=name, grid=(m // tm, n // tn, nk), in_specs=in_specs, out_specs=[o_spec],
                out_shape=[jax.ShapeDtypeStruct((m, n), out_dtype)], args=args,
                scratch=[pltpu.VMEM((tm, tn), F32)] if nk > 1 else [], sem=("parallel", "parallel", "arbitrary"), ride=ride)
    return res[0] if ride is None else res


def _gate_up(h, wg, wu, *, name, tm=1024, ride=None):
    m, k = h.shape
    ns, c, _ = wg.shape
    tm = _tile(m, tm, 16)

    tr = _tile(tm, EPILOGUE_ROWS, 16)

    def body(h_ref, wg_ref, wu_ref, g_ref, u_ref, a_ref):
        for r in range(tm // tr):
            rows = slice(r * tr, (r + 1) * tr)
            hb = h_ref[rows, :]
            g = _dot(hb, wg_ref[...], tb=True)
            u = _dot(hb, wu_ref[...], tb=True)
            g_ref[rows, :] = g
            u_ref[rows, :] = u
            a_ref[rows, :] = (g * _sigmoid(g) * u).astype(a_ref.dtype)

    o_spec = pl.BlockSpec((None, tm, c), lambda i, s: (s, i, 0))
    w_spec = pl.BlockSpec((None, c, k), lambda i, s: (s, 0, 0))
    return _call(body, name=name, grid=(m // tm, ns),
                 in_specs=[pl.BlockSpec((tm, k), lambda i, s: (i, 0)), w_spec, w_spec], out_specs=[o_spec, o_spec, o_spec],
                 out_shape=[jax.ShapeDtypeStruct((ns, m, c), F32), jax.ShapeDtypeStruct((ns, m, c), F32),
                            jax.ShapeDtypeStruct((ns, m, c), BF16)],
                 args=(h, wg, wu), sem=("parallel", "parallel"), ride=ride)


def _gate_up_bwd(dh, wd, g, u, *, name, tm=1024):
    m, k = dh.shape
    ns, c, _ = wd.shape
    tm = _tile(m, tm, 16)

    tr = _tile(tm, EPILOGUE_ROWS, 16)

    def body(dh_ref, wd_ref, g_ref, u_ref, dg_ref, du_ref):
        for r in range(tm // tr):
            rows = slice(r * tr, (r + 1) * tr)
            da = _dot(dh_ref[rows, :], wd_ref[...], tb=True)
            g = g_ref[rows, :]
            s = _sigmoid(g)
            dg_ref[rows, :] = (da * u_ref[rows, :] * (s * (1.0 + g * (1.0 - s)))).astype(dg_ref.dtype)
            du_ref[rows, :] = (da * (g * s)).astype(du_ref.dtype)

    o_spec = pl.BlockSpec((None, tm, c), lambda i, s: (s, i, 0))
    return _call(body, name=name, grid=(m // tm, ns),
                 in_specs=[pl.BlockSpec((tm, k), lambda i, s: (i, 0)), pl.BlockSpec((None, c, k), lambda i, s: (s, 0, 0)),
                           o_spec, o_spec],
                 out_specs=[o_spec, o_spec],
                 out_shape=[jax.ShapeDtypeStruct((ns, m, c), BF16), jax.ShapeDtypeStruct((ns, m, c), BF16)],
                 args=(dh, wd, g, u), sem=("parallel", "parallel"))


def _mm_shards(pairs, *, name, tb=False, residual=None, out_dtype=F32, tm=1024, tn=1024, per_step=2, ride=None):
    ns, m, c = pairs[0][0].shape
    n = pairs[0][1].shape[1] if tb else pairs[0][1].shape[2]
    tm, tn = _tile(m, tm, 16), _tile(n, tn, LANE)
    npair = len(pairs)
    has_res = residual is not None
    assert ns % per_step == 0
    nsteps = ns // per_step

    def body(*refs):
        o_ref, acc_ref = refs[-2], refs[-1]
        s = pl.program_id(2)
        part = None
        for p in range(npair):
            for q in range(per_step):
                term = _dot(refs[2 * p][q], refs[2 * p + 1][q], tb=tb)
                part = term if part is None else part + term

        @pl.when(s == 0)
        def _():
            acc_ref[...] = part

        @pl.when(s > 0)
        def _():
            acc_ref[...] += part

        @pl.when(s == nsteps - 1)
        def _():
            acc = acc_ref[...]
            if has_res:
                acc = acc + refs[2 * npair][...]
            o_ref[...] = acc.astype(o_ref.dtype)

    a_spec = pl.BlockSpec((per_step, tm, c), lambda i, j, s: (s, i, 0))
    b_spec = (pl.BlockSpec((per_step, tn, c), lambda i, j, s: (s, j, 0)) if tb
              else pl.BlockSpec((per_step, c, tn), lambda i, j, s: (s, 0, j)))
    o_spec = pl.BlockSpec((tm, tn), lambda i, j, s: (i, j))
    args = [t for pr in pairs for t in pr] + ([residual] if has_res else [])
    res = _call(body, name=name, grid=(m // tm, n // tn, nsteps),
                in_specs=[a_spec, b_spec] * npair + ([o_spec] if has_res else []), out_specs=[o_spec],
                out_shape=[jax.ShapeDtypeStruct((m, n), out_dtype)], args=args, scratch=[pltpu.VMEM((tm, tn), F32)],
                sem=("parallel", "parallel", "arbitrary"), ride=ride)
    return res[0] if ride is None else res


def _grad_shards(a, b, *, name, tk=2048):
    a3 = a.ndim == 3
    ns, l, c = a.shape if a3 else b.shape
    d = (b if a3 else a).shape[1]
    tk = _tile(l, tk, 16)
    nk = l // tk

    def body(a_ref, b_ref, o_ref, acc_ref):
        kk = pl.program_id(1)
        part = _dot(a_ref[...], b_ref[...], ta=True)

        @pl.when(kk == 0)
        def _():
            acc_ref[...] = part

        @pl.when(kk > 0)
        def _():
            acc_ref[...] += part

        @pl.when(kk == nk - 1)
        def _():
            o_ref[...] = acc_ref[...].astype(o_ref.dtype)

    s3 = pl.BlockSpec((None, tk, c), lambda s, kk: (s, kk, 0))
    s2 = pl.BlockSpec((tk, d), lambda s, kk: (kk, 0))
    out = (c, d) if a3 else (d, c)
    return pl.pallas_call(
        body, name=name, grid=(ns, nk), in_specs=[s3, s2] if a3 else [s2, s3],
        out_specs=pl.BlockSpec((None,) + out, lambda s, kk: (s, 0, 0)),
        out_shape=jax.ShapeDtypeStruct((ns,) + out, BF16), scratch_shapes=[pltpu.VMEM(out, F32)],
        compiler_params=_params("parallel", "arbitrary"),
    )(a, b)


def _rms_fwd(x, w, *, name, tl=512, ride=None):
    l, d = x.shape
    tl = _tile(l, tl, 16)

    def body(x_ref, w_ref, o_ref):
        xv = x_ref[...]
        r = lax.rsqrt(jnp.mean(xv * xv, axis=-1, keepdims=True) + NORM_EPS)
        o_ref[...] = (xv * r * w_ref[...]).astype(o_ref.dtype)

    row = pl.BlockSpec((tl, d), lambda i: (i, 0))
    res = _call(body, name=name, grid=(l // tl,), in_specs=[row, pl.BlockSpec((1, d), lambda i: (0, 0))],
                out_specs=[row], out_shape=[jax.ShapeDtypeStruct((l, d), BF16)], args=(x, w), sem=("parallel",), ride=ride)
    return res[0] if ride is None else res


def _expand_heads(src, col0, width, heads, *, name, tl=512):
    l = src.shape[0]
    tl = _tile(l, tl, 16)
    assert col0 % width == 0 and width % LANE == 0 and heads <= width
    n = heads * HEAD_DIM

    def body(s_ref, o_ref):
        spread = (lax.broadcasted_iota(jnp.int32, (width, n), 0)
                  == lax.broadcasted_iota(jnp.int32, (width, n), 1) // HEAD_DIM).astype(BF16)
        o_ref[...] = _dot_tri(s_ref[...], spread)

    return pl.pallas_call(
        body, name=name, grid=(l // tl,), in_specs=[pl.BlockSpec((tl, width), lambda i: (i, col0 // width))],
        out_specs=pl.BlockSpec((tl, n), lambda i: (i, 0)), out_shape=jax.ShapeDtypeStruct((l, n), F32),
        compiler_params=_params("parallel"),
    )(src)


def _rms_bwd(x, w, dy, add, out_dtype, *, name, tl=256):
    l, d = x.shape
    tl = _tile(l, tl, 16)

    def body(x_ref, w_ref, dy_ref, add_ref, dx_ref, dw_ref):
        i = pl.program_id(0)
        xv = x_ref[...]
        r = lax.rsqrt(jnp.mean(xv * xv, axis=-1, keepdims=True) + NORM_EPS)
        xh = xv * r
        dyv = dy_ref[...].astype(F32)
        dxh = dyv * w_ref[...]
        dx = r * (dxh - xh * jnp.mean(dxh * xh, axis=-1, keepdims=True)) + add_ref[...].astype(F32)
        dx_ref[...] = dx.astype(dx_ref.dtype)
        part = jnp.sum(dyv * xh, axis=0, keepdims=True)

        @pl.when(i == 0)
        def _():
            dw_ref[...] = part

        @pl.when(i > 0)
        def _():
            dw_ref[...] += part

    row = pl.BlockSpec((tl, d), lambda i: (i, 0))
    vec = pl.BlockSpec((1, d), lambda i: (0, 0))
    return pl.pallas_call(
        body, name=name, grid=(l // tl,), in_specs=[row, vec, row, row], out_specs=[row, vec],
        out_shape=[jax.ShapeDtypeStruct((l, d), out_dtype), jax.ShapeDtypeStruct((1, d), F32)],
        compiler_params=_params("arbitrary"),
    )(x, w, dy, add)


def _loss_head(h, w, target, *, name, tl=256):
    l, d = h.shape
    tl = _tile(l, tl, 16)

    def body(x_ref, w_ref, t_ref, loss_ref, dxb_ref, dw_ref):
        i = pl.program_id(0)
        xv = x_ref[...]
        r = lax.rsqrt(jnp.mean(xv * xv, axis=-1, keepdims=True) + NORM_EPS)
        xh = xv * r
        wv = w_ref[...]
        diff = xh * wv - t_ref[...]
        lpart = 0.5 * jnp.sum(jnp.mean(diff * diff, axis=-1, keepdims=True), axis=0, keepdims=True)
        dyv = diff * (1.0 / d)
        dxh = dyv * wv
        dx = r * (dxh - xh * jnp.mean(dxh * xh, axis=-1, keepdims=True))
        dxb_ref[...] = dx.astype(BF16)
        part = jnp.sum(dyv * xh, axis=0, keepdims=True)

        @pl.when(i == 0)
        def _():
            dw_ref[...] = part
            loss_ref[...] = lpart

        @pl.when(i > 0)
        def _():
            dw_ref[...] += part
            loss_ref[...] += lpart

    row = pl.BlockSpec((tl, d), lambda i: (i, 0))
    vec = pl.BlockSpec((1, d), lambda i: (0, 0))
    one = pl.BlockSpec((1, 1), lambda i: (0, 0))
    return pl.pallas_call(
        body, name=name, grid=(l // tl,), in_specs=[row, vec, row], out_specs=[one, row, vec],
        out_shape=[jax.ShapeDtypeStruct((1, 1), F32), jax.ShapeDtypeStruct((l, d), BF16), jax.ShapeDtypeStruct((1, d), F32)],
        compiler_params=_params("arbitrary"),
    )(h, w, target)


def _conv_pre(ext, w_ref):
    acc = ext * w_ref[CONV_K - 1:CONV_K, :]
    for k in range(CONV_K - 1):
        acc = acc + pltpu.roll(ext, CONV_K - 1 - k, axis=0) * w_ref[k:k + 1, :]
    return acc


def _conv_fwd(proj, col0, cw, cb, *, name, tl=512, tc=1024):
    l = proj.shape[0]
    c = cw.shape[1]
    tl, tc = _tile(l, tl, 16), _tile(math.gcd(c, col0), tc, LANE)
    assert col0 % tc == 0
    off, hb = col0 // tc, tl // CONV_HALO

    def body(x_ref, halo_ref, w_ref, b_ref, o_ref):
        i = pl.program_id(0)
        halo = jnp.where(i > 0, halo_ref[...], 0.0)
        ext = jnp.concatenate([halo, x_ref[...]], axis=0)
        pre = _conv_pre(ext, w_ref)[CONV_HALO:] + b_ref[...]
        o_ref[...] = pre * _sigmoid(pre)

    return pl.pallas_call(
        body, name=name, grid=(l // tl, c // tc),
        in_specs=[pl.BlockSpec((tl, tc), lambda i, j: (i, off + j)),
                  pl.BlockSpec((CONV_HALO, tc), lambda i, j: (jnp.maximum(i * hb - 1, 0), off + j)),
                  pl.BlockSpec((CONV_K, tc), lambda i, j: (0, j)), pl.BlockSpec((1, tc), lambda i, j: (0, j))],
        out_specs=pl.BlockSpec((tl, tc), lambda i, j: (i, j)),
        out_shape=jax.ShapeDtypeStruct((l, c), F32), compiler_params=_params("parallel", "parallel"),
    )(proj, proj, cw, cb)


def _conv_bwd(proj, col0, cw, cb, dout, dproj, *, name, tl=512, tc=1024):
    l = proj.shape[0]
    c = dout.shape[1]
    tl, tc = _tile(l, tl, 16), _tile(math.gcd(c, col0), tc, LANE)
    assert col0 % tc == 0
    off, hb, nt = col0 // tc, tl // CONV_HALO, l // tl
    n_ext = tl + CONV_HALO

    def body(x_ref, prev_ref, nxt_ref, d_ref, dn_ref, w_ref, b_ref, dproj_in, dx_ref, dw_ref, db_ref):
        del dproj_in
        i = pl.program_id(1)
        last = i == nt - 1
        halo = jnp.where(i > 0, prev_ref[...], 0.0)
        ext = jnp.concatenate([halo, x_ref[...], nxt_ref[...]], axis=0)
        pre = _conv_pre(ext, w_ref)[CONV_HALO:] + b_ref[...]
        dext = jnp.concatenate([d_ref[...], jnp.where(last, 0.0, dn_ref[...])], axis=0)
        s = _sigmoid(pre)
        dpre = dext * (s * (1.0 + pre * (1.0 - s)))
        dx = dpre * w_ref[CONV_K - 1:CONV_K, :]
        for k in range(CONV_K - 1):
            dx = dx + pltpu.roll(dpre, n_ext - (CONV_K - 1 - k), axis=0) * w_ref[k:k + 1, :]
        dx_ref[...] = dx[:tl].astype(dx_ref.dtype)
        dp = dpre[:tl]
        rows = [jnp.sum(dp * pltpu.roll(ext, CONV_K - 1 - k, axis=0)[CONV_HALO:CONV_HALO + tl], axis=0, keepdims=True)
                for k in range(CONV_K - 1)]
        rows.append(jnp.sum(dp * ext[CONV_HALO:CONV_HALO + tl], axis=0, keepdims=True))
        dwp = jnp.concatenate(rows, axis=0)
        dbp = jnp.sum(dp, axis=0, keepdims=True)

        @pl.when(i == 0)
        def _():
            dw_ref[...] = dwp
            db_ref[...] = dbp

        @pl.when(i > 0)
        def _():
            dw_ref[...] += dwp
            db_ref[...] += dbp

    cur = lambda j, i: (i, j)
    nxt = lambda j, i: (jnp.minimum((i + 1) * hb, nt * hb - 1), j)
    return pl.pallas_call(
        body, name=name, grid=(c // tc, nt),
        in_specs=[pl.BlockSpec((tl, tc), lambda j, i: (i, off + j)),
                  pl.BlockSpec((CONV_HALO, tc), lambda j, i: (jnp.maximum(i * hb - 1, 0), off + j)),
                  pl.BlockSpec((CONV_HALO, tc), lambda j, i: (jnp.minimum((i + 1) * hb, nt * hb - 1), off + j)),
                  pl.BlockSpec((tl, tc), cur), pl.BlockSpec((CONV_HALO, tc), nxt),
                  pl.BlockSpec((CONV_K, tc), lambda j, i: (0, j)), pl.BlockSpec((1, tc), lambda j, i: (0, j)), ANY],
        out_specs=[pl.BlockSpec((tl, tc), lambda j, i: (i, off + j)), pl.BlockSpec((CONV_K, tc), lambda j, i: (0, j)),
                   pl.BlockSpec((1, tc), lambda j, i: (0, j))],
        out_shape=[jax.ShapeDtypeStruct(dproj.shape, dproj.dtype), jax.ShapeDtypeStruct((CONV_K, c), F32),
                   jax.ShapeDtypeStruct((1, c), F32)],
        input_output_aliases={7: 0}, compiler_params=_params("parallel", "arbitrary"),
    )(proj, proj, proj, dout, dout, cw, cb, dproj)


def _pool_fwd(proj, col0, d_pool, *, name, tl=256):
    l = proj.shape[0]
    tl = _tile(l, tl, POOL_HALO)
    assert col0 % d_pool == 0
    off, hb, pg = col0 // d_pool, tl // POOL_HALO, d_pool // GROUPS

    def body(u_ref, halo_ref, o_ref):
        i = pl.program_id(0)
        halo = jnp.where(i > 0, halo_ref[...], 0.0)
        ext = jnp.concatenate([halo, u_ref[...]], axis=0)
        t = i * tl + lax.broadcasted_iota(jnp.int32, (tl, 1), 0)
        s, width = ext, 1
        for gi, win in enumerate(POOL_WINDOWS):
            while width < win:
                s = s + pltpu.roll(s, width, axis=0)
                width *= 2
            cnt = jnp.minimum(t + 1, win).astype(F32)
            sl = slice(gi * pg, (gi + 1) * pg)
            o_ref[:, sl] = (s[POOL_HALO:, sl] / cnt - ext[POOL_HALO:, sl]).astype(o_ref.dtype)

    return pl.pallas_call(
        body, name=name, grid=(l // tl,),
        in_specs=[pl.BlockSpec((tl, d_pool), lambda i: (i, off)),
                  pl.BlockSpec((POOL_HALO, d_pool), lambda i: (jnp.maximum(i * hb - 1, 0), off))],
        out_specs=pl.BlockSpec((tl, d_pool), lambda i: (i, 0)),
        out_shape=jax.ShapeDtypeStruct((l, d_pool), BF16), compiler_params=_params("parallel"),
    )(proj, proj)


def _pool_bwd(dp, dproj, col0, *, name, tl=256):
    l, d_pool = dp.shape
    tl = _tile(l, tl, POOL_HALO)
    assert col0 % d_pool == 0
    off, hb, nt, pg = col0 // d_pool, tl // POOL_HALO, l // tl, d_pool // GROUPS
    n_ext = tl + POOL_HALO

    def body(d_ref, nxt_ref, dproj_in, o_ref):
        del dproj_in
        i = pl.program_id(0)
        ext = jnp.concatenate([d_ref[...], jnp.where(i == nt - 1, 0.0, nxt_ref[...])], axis=0)
        t = i * tl + lax.broadcasted_iota(jnp.int32, (n_ext, 1), 0)
        for gi, win in enumerate(POOL_WINDOWS):
            sl = slice(gi * pg, (gi + 1) * pg)
            dg = ext[:, sl]
            s = dg / jnp.minimum(t + 1, win).astype(F32)
            width = 1
            while width < win:
                s = s + pltpu.roll(s, n_ext - width, axis=0)
                width *= 2
            o_ref[:, sl] = (s[:tl] - dg[:tl]).astype(o_ref.dtype)

    return pl.pallas_call(
        body, name=name, grid=(nt,),
        in_specs=[pl.BlockSpec((tl, d_pool), lambda i: (i, 0)),
                  pl.BlockSpec((POOL_HALO, d_pool), lambda i: (jnp.minimum((i + 1) * hb, nt * hb - 1), 0)), ANY],
        out_specs=pl.BlockSpec((tl, d_pool), lambda i: (i, off)),
        out_shape=jax.ShapeDtypeStruct(dproj.shape, dproj.dtype), input_output_aliases={2: 0},
        compiler_params=_params("parallel"),
    )(dp, dp, dproj)


def _put_heads(src, dst, col0, width, *, name, tl=512):
    l, c = src.shape
    tl = _tile(l, tl, 16)
    assert col0 % width == 0 and width % LANE == 0 and c // HEAD_DIM <= width

    def body(s_ref, dst_in, o_ref):
        del dst_in
        pick = (lax.broadcasted_iota(jnp.int32, (c, width), 0)
                == HEAD_DIM * lax.broadcasted_iota(jnp.int32, (c, width), 1)).astype(BF16)
        hi, lo = _split2(s_ref[...])
        o_ref[...] = (jnp.dot(hi, pick, preferred_element_type=F32)
                      + jnp.dot(lo, pick, preferred_element_type=F32)).astype(o_ref.dtype)

    return pl.pallas_call(
        body, name=name, grid=(l // tl,), in_specs=[pl.BlockSpec((tl, c), lambda i: (i, 0)), ANY],
        out_specs=pl.BlockSpec((tl, width), lambda i: (i, col0 // width)),
        out_shape=jax.ShapeDtypeStruct(dst.shape, dst.dtype), input_output_aliases={1: 0},
        compiler_params=_params("parallel"),
    )(src, dst)


def _pool_mix_fwd(pooled, pw, scale, mixed, col0, *, name, tm=1024):
    l, d_pool = pooled.shape
    pg = d_pool // GROUPS
    tm = _tile(l, tm, 16)
    assert col0 % pg == 0
    off = col0 // pg

    def body(a_ref, w_ref, s_ref, mixed_in, raw_ref, mixed_ref):
        del mixed_in
        raw = _dot(a_ref[...], w_ref[...])
        raw_ref[...] = raw
        mixed_ref[...] = (raw * s_ref[...]).astype(mixed_ref.dtype)

    blk = pl.BlockSpec((tm, pg), lambda i, g: (i, g))
    return pl.pallas_call(
        body, name=name, grid=(l // tm, GROUPS),
        in_specs=[blk, pl.BlockSpec((None, pg, pg), lambda i, g: (g, 0, 0)), pl.BlockSpec((1, pg), lambda i, g: (0, g)),
                  pl.BlockSpec(memory_space=pl.ANY)],
        out_specs=[blk, pl.BlockSpec((tm, pg), lambda i, g: (i, off + g))],
        out_shape=[jax.ShapeDtypeStruct((l, d_pool), F32), jax.ShapeDtypeStruct(mixed.shape, mixed.dtype)],
        input_output_aliases={3: 1}, compiler_params=_params("parallel", "parallel"),
    )(pooled, pw, scale, mixed)


def _pool_mix_bwd(dmixed, col0, raw, scale, pw, *, name, tm=1024):
    l, d_pool = raw.shape
    pg = d_pool // GROUPS
    tm = _tile(l, tm, 16)
    assert col0 % pg == 0
    off = col0 // pg

    def body(d_ref, raw_ref, s_ref, w_ref, draw_ref, dp_ref, ds_ref):
        i = pl.program_id(1)
        dv = d_ref[...]
        draw = (dv * s_ref[...]).astype(BF16)
        draw_ref[...] = draw
        dp_ref[...] = _dot(draw, w_ref[...], tb=True)
        part = jnp.sum(dv * raw_ref[...], axis=0, keepdims=True)

        @pl.when(i == 0)
        def _():
            ds_ref[...] = part

        @pl.when(i > 0)
        def _():
            ds_ref[...] += part

    blk = pl.BlockSpec((tm, pg), lambda g, i: (i, g))
    vec = pl.BlockSpec((1, pg), lambda g, i: (0, g))
    return pl.pallas_call(
        body, name=name, grid=(GROUPS, l // tm),
        in_specs=[pl.BlockSpec((tm, pg), lambda g, i: (i, off + g)), blk, vec,
                  pl.BlockSpec((None, pg, pg), lambda g, i: (g, 0, 0))],
        out_specs=[blk, blk, vec],
        out_shape=[jax.ShapeDtypeStruct((l, d_pool), BF16), jax.ShapeDtypeStruct((l, d_pool), F32),
                   jax.ShapeDtypeStruct((1, d_pool), F32)],
        compiler_params=_params("parallel", "arbitrary"),
    )(dmixed, raw, scale, pw)


def _group_mm_tn(a, b, *, name, tk=1024):
    l, d_pool = a.shape
    pg = d_pool // GROUPS
    tk = _tile(l, tk, 16)

    def body(a_ref, b_ref, o_ref):
        kk = pl.program_id(1)
        part = _dot(a_ref[...], b_ref[...], ta=True)

        @pl.when(kk == 0)
        def _():
            o_ref[...] = part

        @pl.when(kk > 0)
        def _():
            o_ref[...] += part

    blk = pl.BlockSpec((tk, pg), lambda g, kk: (kk, g))
    return pl.pallas_call(
        body, name=name, grid=(GROUPS, l // tk), in_specs=[blk, blk],
        out_specs=pl.BlockSpec((None, pg, pg), lambda g, kk: (g, 0, 0)),
        out_shape=jax.ShapeDtypeStruct((GROUPS, pg, pg), F32), compiler_params=_params("parallel", "arbitrary"),
    )(a, b)


def _split3(v):
    hi = v.astype(BF16)
    r1 = v - hi.astype(F32)
    mid = r1.astype(BF16)
    lo = (r1 - mid.astype(F32)).astype(BF16)
    return hi, mid, lo


def _tri_dot(tri, v):
    hi, mid, lo = _split3(v)
    d = lambda p: jnp.dot(tri, p, preferred_element_type=F32)
    return d(hi) + d(mid) + d(lo)


def _dot_tri(v, tri):
    hi, mid, lo = _split3(v)
    d = lambda p: jnp.dot(p, tri, preferred_element_type=F32)
    return d(hi) + d(mid) + d(lo)


def _split2(v):
    hi = v.astype(BF16)
    return hi, (v - hi.astype(F32)).astype(BF16)


def _scan_specs(hpg, nc, order):
    gw, q = hpg * HEAD_DIM, CHUNK
    b_off, c_off = (GROUPS * gw) // STATE, (GROUPS * gw) // STATE + GROUPS
    xs = pl.BlockSpec((q, gw), lambda g, c: (order(c), g))
    bm = pl.BlockSpec((q, STATE), lambda g, c: (order(c), b_off + g))
    cm = pl.BlockSpec((q, STATE), lambda g, c: (order(c), c_off + g))
    dtt = pl.BlockSpec((None, hpg, q), lambda g, c: (g, 0, order(c)))
    vec = pl.BlockSpec((1, gw), lambda g, c: (0, g))
    hcol = pl.BlockSpec((None, hpg, 1), lambda g, c: (g, 0, 0))
    st = pl.BlockSpec((None, None, STATE, gw), lambda g, c: (g, order(c), 0, 0))
    return xs, bm, cm, dtt, vec, hcol, st


def _scan_common(dtx_ref, dtrt_ref, biasx_ref, biast_ref, alogx_ref, alogt_ref):
    q = CHUNK
    dt = _softplus(dtx_ref[...] + biasx_ref[...])
    a = -jnp.exp(alogx_ref[...])
    dtt = _softplus(dtrt_ref[...] + biast_ref[...])
    at = -jnp.exp(alogt_ref[...])
    row = lax.broadcasted_iota(jnp.int32, (q, q), 0)
    col = lax.broadcasted_iota(jnp.int32, (q, q), 1)
    lower = (row >= col).astype(BF16)
    upper = (row <= col).astype(BF16)
    acum = _tri_dot(lower, dt * a)
    acum_t = _dot_tri(dtt * at, upper)
    return dt, a, acum, acum_t, row, col, upper


def _gated(yv, z):
    s = _sigmoid(z)
    sz = z * s
    gv = yv * sz
    return gv, lax.rsqrt(jnp.mean(gv * gv, axis=-1, keepdims=True) + NORM_EPS), sz, s


def _scan_fwd(xbc, dtx, dtrt, biasx, biast, alogx, alogt, dskx, proj, wn, d_mix, *, name, ride=None):
    l = xbc.shape[0]
    hpg = dtrt.shape[1]
    gw, q, nc = hpg * HEAD_DIM, CHUNK, l // CHUNK
    assert hpg % 2 == 0
    xs_s, bm_s, cm_s, dtt_s, vec, hcol, st_s = _scan_specs(hpg, nc, lambda c: c)

    def body(xs_ref, b_ref, c_ref, dtx_ref, dtrt_ref, biasx_ref, biast_ref, alogx_ref, alogt_ref, dsk_ref, z_ref, wn_ref,
             y_ref, st_ref, mixed_ref, h_ref):
        @pl.when(pl.program_id(1) == 0)
        def _():
            h_ref[...] = jnp.zeros_like(h_ref)

        dt, _, acum, acum_t, row, col, _ = _scan_common(dtx_ref, dtrt_ref, biasx_ref, biast_ref, alogx_ref, alogt_ref)
        causal = row >= col
        xs = xs_ref[...]
        xdt = xs * dt
        xdtb = xdt.astype(BF16)
        bmat, cmat = b_ref[...], c_ref[...]
        cb = _dot(cmat, bmat, tb=True)
        h_all = h_ref[...]
        st_ref[...] = h_all
        rest = _dot(cmat, h_all) * jnp.exp(acum) + dsk_ref[...] * xs
        first = lax.broadcasted_iota(jnp.int32, (q, 2 * HEAD_DIM), 1) < HEAD_DIM
        parts = []
        for jp in range(hpg // 2):
            pair = slice(2 * jp * HEAD_DIM, (2 * jp + 2) * HEAD_DIM)
            xp = xdtb[:, pair]
            ys = []
            for j in (2 * jp, 2 * jp + 1):
                a_col, a_row = acum[:, j * HEAD_DIM:j * HEAD_DIM + 1], acum_t[j:j + 1, :]
                lm = jnp.exp(jnp.where(causal, a_col - a_row, NEG_BIG))
                ys.append(_dot(cb * lm, xp))
            parts.append(jnp.where(first, ys[0], ys[1]))
        yv = jnp.concatenate(parts, axis=1) + rest
        y_ref[...] = yv
        gv, r, _, _ = _gated(yv, z_ref[...])
        mixed_ref[...] = (gv * r * wn_ref[...]).astype(mixed_ref.dtype)
        a_last = acum[q - 1:q, :]
        h_ref[...] = jnp.exp(a_last) * h_all + _dot(bmat, xdt * jnp.exp(a_last - acum), ta=True)

    return _call(
        body, name=name, grid=(GROUPS, nc),
        in_specs=[xs_s, bm_s, cm_s, xs_s, dtt_s, vec, hcol, vec, hcol, vec, xs_s, vec], out_specs=[xs_s, st_s, xs_s],
        out_shape=[jax.ShapeDtypeStruct((l, GROUPS * gw), F32), jax.ShapeDtypeStruct((GROUPS, nc, STATE, gw), F32),
                   jax.ShapeDtypeStruct((l, d_mix), BF16)],
        scratch=[pltpu.VMEM((STATE, gw), F32)], sem=("parallel", "arbitrary"),
        args=(xbc, xbc, xbc, dtx, dtrt, biasx, biast, alogx, alogt, dskx, proj, wn), ride=ride)


def _scan_bwd(xbc, dtx, dtrt, biasx, biast, alogx, alogt, dskx, states, dmixed, y, proj, wn, *, name, ride=None):
    l = xbc.shape[0]
    hpg = dtrt.shape[1]
    gw, q, nc = hpg * HEAD_DIM, CHUNK, l // CHUNK
    rev = lambda c: nc - 1 - c
    xs_s, bm_s, cm_s, dtt_s, vec, hcol, st_s = _scan_specs(hpg, nc, rev)
    bc_s = pl.BlockSpec((q, STATE), lambda g, c: (rev(c), g))

    def body(xs_ref, b_ref, c_ref, dtx_ref, dtrt_ref, biasx_ref, biast_ref, alogx_ref, alogt_ref, dsk_ref, st_ref,
             dmix_ref, y_ref, z_ref, wn_ref,
             dxs_ref, db_ref, dc_ref, ddt_ref, dbias_ref, dalog_ref, ddsk_ref, dz_ref, dwn_ref, dh_ref):
        step = pl.program_id(1)

        @pl.when(step == 0)
        def _():
            dh_ref[...] = jnp.zeros_like(dh_ref)

        yv, z, dmix = y_ref[...], z_ref[...], dmix_ref[...]
        gv, rn, sz, sg = _gated(yv, z)
        gh = gv * rn
        dgh = dmix * wn_ref[...]
        dg = rn * (dgh - gh * jnp.mean(dgh * gh, axis=-1, keepdims=True))
        dyv = dg * sz
        dz_ref[...] = (dg * yv * (sg * (1.0 + z * (1.0 - sg)))).astype(dz_ref.dtype)
        p_wn = jnp.sum(dmix * gh, axis=0, keepdims=True)

        dt, a, acum, acum_t, row, col, upper = _scan_common(dtx_ref, dtrt_ref, biasx_ref, biast_ref, alogx_ref, alogt_ref)
        causal, anti, strict = row >= col, row <= col, row > col
        xs = xs_ref[...]
        xdt = xs * dt
        xdtb, dyb = xdt.astype(BF16), dyv.astype(BF16)
        bmat, cmat = b_ref[...], c_ref[...]
        cb = _dot(cmat, bmat, tb=True)
        cbt = _dot(bmat, cmat, tb=True)
        h_all, dh = st_ref[...], dh_ref[...]
        a_last = acum[q - 1:q, :]
        ea, ee, gam = jnp.exp(acum), jnp.exp(a_last - acum), jnp.exp(a_last)
        zc = _dot(cmat, h_all)
        bdh = _dot(bmat, dh)
        dz = dyv * ea
        xe = xdt * ee
        dc_state = _dot(dz, h_all, tb=True)
        db_state = _dot(xe, dh, tb=True)
        dh_ref[...] = gam * dh + _dot(cmat, dz, ta=True)
        lane = lax.broadcasted_iota(jnp.int32, (q, 2 * HEAD_DIM), 1)
        first = lane < HEAD_DIM
        ones = jnp.ones((q // 2, 2 * HEAD_DIM), BF16)
        dcb = jnp.zeros((q, q), F32)
        dxdt_parts, da_parts = [], []
        for jp in range(hpg // 2):
            pair = slice(2 * jp * HEAD_DIM, (2 * jp + 2) * HEAD_DIM)
            xp, dyp = xdtb[:, pair], dyb[:, pair]
            dxs_r, das_r = [], []
            for r, j in enumerate((2 * jp, 2 * jp + 1)):
                seg = acum[:, j * HEAD_DIM:j * HEAD_DIM + 1] - acum_t[j:j + 1, :]
                lm = jnp.exp(jnp.where(causal, seg, NEG_BIG))
                lmt = jnp.exp(jnp.where(anti, -seg, NEG_BIG))
                mm, mmt = cb * lm, cbt * lmt
                keep = first if r == 0 else jnp.logical_not(first)
                dyk = jnp.where(keep, dyp, 0)
                dxs_r.append(_dot(mmt, dyp))
                dm = _dot(dyk, xp, tb=True)
                dcb = dcb + dm * lm
                corner = jnp.dot(upper, (dm * mm).astype(BF16), preferred_element_type=F32)
                corner = jnp.where(strict, corner, 0.0)
                chi, clo = _split2(corner[:, :q // 2] + corner[:, q // 2:])
                das_r.append(jnp.dot(chi, ones, preferred_element_type=F32) + jnp.dot(clo, ones, preferred_element_type=F32))
            dxdt_parts.append(jnp.where(first, dxs_r[0], dxs_r[1]))
            da_parts.append(jnp.where(first, das_r[0], das_r[1]))
        dxdt = jnp.concatenate(dxdt_parts, axis=1) + bdh * ee
        da_intra = jnp.concatenate(da_parts, axis=1)
        dxs_ref[...] = dxdt * dt + dsk_ref[...] * dyv
        dc_ref[...] = dc_state + _dot(dcb, bmat)
        db_ref[...] = db_state + _dot(dcb, cmat, ta=True)
        ri = lax.broadcasted_iota(jnp.int32, (gw, gw), 0) // HEAD_DIM
        ci = lax.broadcasted_iota(jnp.int32, (gw, gw), 1) // HEAD_DIM
        blockdiag = (ri == ci).astype(BF16)

        def head_sum(v):
            hi, lo = _split2(v)
            return (jnp.dot(hi, blockdiag, preferred_element_type=F32)
                    + jnp.dot(lo, blockdiag, preferred_element_type=F32))

        xt = xe * bdh
        small = jnp.concatenate([
            jnp.sum(xt, axis=0, keepdims=True) + gam * jnp.sum(dh * h_all, axis=0, keepdims=True),
            jnp.sum(dyv * xs, axis=0, keepdims=True), jnp.zeros((6, gw), F32)], axis=0)
        small = head_sum(small)
        rows = lax.broadcasted_iota(jnp.int32, (q, 1), 0)
        da_local = head_sum(dyv * zc * ea - xt) + jnp.where(rows == q - 1, small[0:1, :], 0.0)
        hi, lo = _split2(da_local)
        d_dta = (da_intra + jnp.dot(upper, hi, preferred_element_type=F32)
                 + jnp.dot(upper, lo, preferred_element_type=F32))
        d_raw = (d_dta * a + head_sum(dxdt * xs)) * _sigmoid(dtx_ref[...] + biasx_ref[...])
        ddt_ref[...] = d_raw
        p_bias = jnp.sum(d_raw, axis=0, keepdims=True)
        p_alog = jnp.sum(d_dta * dt, axis=0, keepdims=True) * a
        p_dsk = small[1:2, :]

        @pl.when(step == 0)
        def _():
            dbias_ref[...] = p_bias
            dalog_ref[...] = p_alog
            ddsk_ref[...] = p_dsk
            dwn_ref[...] = p_wn

        @pl.when(step > 0)
        def _():
            dbias_ref[...] += p_bias
            dalog_ref[...] += p_alog
            ddsk_ref[...] += p_dsk
            dwn_ref[...] += p_wn

    wide = jax.ShapeDtypeStruct((l, GROUPS * gw), F32)
    narrow = jax.ShapeDtypeStruct((l, GROUPS * STATE), F32)
    vshape = jax.ShapeDtypeStruct((1, GROUPS * gw), F32)
    return _call(
        body, name=name, grid=(GROUPS, nc),
        in_specs=[xs_s, bm_s, cm_s, xs_s, dtt_s, vec, hcol, vec, hcol, vec, st_s, xs_s, xs_s, xs_s, vec],
        out_specs=[xs_s, bc_s, bc_s, xs_s, vec, vec, vec, xs_s, vec],
        out_shape=[wide, narrow, narrow, wide, vshape, vshape, vshape, jax.ShapeDtypeStruct(proj.shape, BF16), vshape],
        scratch=[pltpu.VMEM((STATE, gw), F32)], sem=("parallel", "arbitrary"),
        args=(xbc, xbc, xbc, dtx, dtrt, biasx, biast, alogx, alogt, dskx, states, dmixed, y, proj, wn), ride=ride)


def _exchange(srcs, scatter, *, name):
    ride = _Ride(srcs, scatter)

    def body(*refs):
        src, dst, sems = refs[:ride.n], refs[ride.n:2 * ride.n], refs[2 * ride.n:]
        ride.start(src, dst, sems)
        ride.relay(src, dst, sems)
        ride.finish(src, dst, sems)

    return pl.pallas_call(body, name=name, in_specs=ride.in_specs, out_specs=ride.out_specs, out_shape=ride.out_shape,
                          scratch_shapes=ride.scratch)(*srcs)


def _adamw(parts, w, m, v, *, name, tr=256):
    lead = w.ndim == 3
    r, c = w.shape[-2:]
    tr = _tile(r, tr, 16)
    c1, c2 = 1.0 / (1.0 - ADAM_B1 ** ADAM_STEP), 1.0 / (1.0 - ADAM_B2 ** ADAM_STEP)

    def body(p_ref, w_ref, m_ref, v_ref, g_ref, d_ref, nm_ref, nv_ref):
        g = p_ref[0].astype(F32)
        for p in range(1, N_DEV):
            g = g + p_ref[p].astype(F32)
        nm = ADAM_B1 * m_ref[...] + (1.0 - ADAM_B1) * g
        nv = ADAM_B2 * v_ref[...] + (1.0 - ADAM_B2) * (g * g)
        g_ref[...] = g
        nm_ref[...] = nm
        nv_ref[...] = nv
        d_ref[...] = -ADAM_LR * ((nm * c1) / (jnp.sqrt(nv * c2) + ADAM_EPS) + ADAM_WD * w_ref[...])

    blk = pl.BlockSpec((None, tr, c), lambda i: (0, i, 0)) if lead else pl.BlockSpec((tr, c), lambda i: (i, 0))
    out = jax.ShapeDtypeStruct(w.shape, F32)
    return pl.pallas_call(
        body, name=name, grid=(r // tr,), in_specs=[pl.BlockSpec((N_DEV, tr, c), lambda i: (0, i, 0)), blk, blk, blk],
        out_specs=[blk, blk, blk, blk], out_shape=[out, out, out, out], compiler_params=_params("parallel"),
    )(parts, w, m, v)


def kernel(x, attn_norm_w, w_in, conv_w, conv_b, dt_bias, a_log, d_skip, ssd_norm_w, pool_w, pool_scale, w_out, ffn_norm_w, w_gate, w_up, w_down, final_norm_w, loss_target, m_attn_norm_w, m_w_in, m_conv_w, m_conv_b, m_dt_bias, m_a_log, m_d_skip, m_ssd_norm_w, m_pool_w, m_pool_scale, m_w_out, m_ffn_norm_w, m_w_gate, m_w_up, m_w_down, m_final_norm_w, v_attn_norm_w, v_w_in, v_conv_w, v_conv_b, v_dt_bias, v_a_log, v_d_skip, v_ssd_norm_w, v_pool_w, v_pool_scale, v_w_out, v_ffn_norm_w, v_w_gate, v_w_up, v_w_down, v_final_norm_w):
    l, d = x.shape[1], x.shape[2]
    heads = dt_bias.shape[1]
    hpg = heads // GROUPS
    d_ssm = heads * HEAD_DIM
    conv_ch = conv_b.shape[1]
    d_pool = pool_scale.shape[1]
    pg = d_pool // GROUPS
    d_mix = d_ssm + d_pool
    d_ff = w_gate.shape[2] * N_DEV
    d_in = w_in.shape[2] * N_DEV
    dt_pad = -(-heads // LANE) * LANE
    o_u, o_xbc, o_dt = d_ssm, d_ssm + d_pool, d_ssm + d_pool + conv_ch
    d_inp = o_dt + dt_pad
    tn_in = _tile(d_inp, 2560, LANE)
    g_dt, g_u = d_ssm + conv_ch, d_ssm + conv_ch + heads

    x2, tgt = x[0], loss_target[0]

    tr_ = lambda t: jnp.transpose(t, (0, 2, 1))
    h0, gi, gp, gc = _rms_fwd(x2, attn_norm_w, name="attn_norm",
                              ride=_Ride([tr_(w_in)[0].astype(BF16), pool_w[0].astype(BF16), conv_w[0]], False))
    win = gi.reshape(d_in, d)
    wp = jnp.concatenate([win[:d_ssm], win[g_u:], win[d_ssm:g_dt], win[g_dt:g_u],
                          jnp.zeros((dt_pad - heads, d), BF16)], axis=0)
    pw = jnp.transpose(gp, (1, 0, 2, 3)).reshape(GROUPS, pg, pg)
    cw = jnp.transpose(gc, (1, 0, 2)).reshape(CONV_K, conv_ch)

    def expand(vec):
        return jnp.repeat(vec, HEAD_DIM, axis=1)

    def per_head(vec):
        return vec[:, ::HEAD_DIM]

    bias_x, alog_x, dsk_x = expand(dt_bias), expand(a_log), expand(d_skip)
    bias_c, alog_c = dt_bias.reshape(GROUPS, hpg, 1), a_log.reshape(GROUPS, hpg, 1)

    proj, go, gg = _mm(h0, wp, name="in_proj", tb=True, tm=512, tn=tn_in, tk=d,
                       ride=_Ride([w_out[0].astype(BF16), tr_(w_gate)[0].astype(BF16)], False))
    wo = go.reshape(d_mix, d)
    xbc = _conv_fwd(proj, o_xbc, cw, conv_b, name="conv_fwd")
    dt_raw = proj[:, o_dt:o_dt + heads]
    dtx = _expand_heads(proj, o_dt, dt_pad, heads, name="expand_dt")
    dtrt = jnp.transpose(dt_raw.reshape(l, GROUPS, hpg), (1, 2, 0))
    y, states, mixed, gu = _scan_fwd(xbc, dtx, dtrt, bias_x, bias_c, alog_x, alog_c, dsk_x, proj, ssd_norm_w, d_mix,
                                     name="ssd_fwd", ride=_Ride([tr_(w_up)[0].astype(BF16)], False))
    pooled = _pool_fwd(proj, o_u, d_pool, name="pool_fwd")
    pool_raw, mixed = _pool_mix_fwd(pooled, pw, pool_scale, mixed, d_ssm, name="pool_mix")
    h1 = _mm(mixed, wo, name="out_proj", tm=512, tk=d_mix, residual=x2)
    h1n = _rms_fwd(h1, ffn_norm_w, name="ffn_norm")
    gate, up, act, gd = _gate_up(h1n, gg, gu, name="gate_up", ride=_Ride([w_down[0].astype(BF16)], False))
    h2 = _mm_shards([(act, gd)], name="down_proj", per_step=4, residual=h1)

    loss11, dh2b, g_final = _loss_head(h2, final_norm_w.reshape(1, d), tgt, name="loss_head")
    dgate, dup = _gate_up_bwd(dh2b, gd, gate, up, name="gate_up_bwd")
    g_wd = _grad_shards(act, dh2b, name="grad_w_down")
    dh1n, r_wd = _mm_shards([(dgate, gg), (dup, gu)], name="dgate_dup", out_dtype=BF16, ride=_Ride([g_wd], True))
    g_wg = _grad_shards(dgate, h1n, name="grad_w_gate")
    g_wu = _grad_shards(dup, h1n, name="grad_w_up")
    dh1b, g_ffn = _rms_bwd(h1, ffn_norm_w, dh1n, dh2b, BF16, name="ffn_norm_bwd")
    dmixed = _mm(dh1b, wo, name="dmixed", tb=True, tk=d)
    g_wo = _mm(mixed, dh1b, name="grad_w_out", ta=True, out_dtype=BF16, tm=1024, tn=1024, tk=2048)
    draw, dpooled, g_pscale = _pool_mix_bwd(dmixed, d_ssm, pool_raw, pool_scale, pw, name="pool_mix_bwd")
    g_pw = _group_mm_tn(pooled, draw, name="grad_pool_w")
    s_pw = jnp.transpose(g_pw.reshape(GROUPS, N_DEV, pg // N_DEV, pg), (1, 0, 2, 3)).astype(BF16)
    dxs, dbm, dcm, ddtx, g_bias, g_alog, g_dsk, dproj, g_ssdn, r_wg, r_wu, r_wo, r_pw = _scan_bwd(
        xbc, dtx, dtrt, bias_x, bias_c, alog_x, alog_c, dsk_x, states, dmixed, y, proj, ssd_norm_w, name="ssd_bwd",
        ride=_Ride([g_wg, g_wu, g_wo.reshape(N_DEV, d_mix // N_DEV, d), s_pw], True))
    dproj = _pool_bwd(dpooled, dproj, o_u, name="pool_bwd")
    n_bc = GROUPS * STATE
    segs = [(dxs, 0, d_ssm), (dbm, d_ssm, n_bc), (dcm, d_ssm + n_bc, n_bc)]
    g_cw, g_cb = [], []
    for si, (dseg, c0, width) in enumerate(segs):
        dproj, b, c_ = _conv_bwd(proj, o_xbc + c0, cw[:, c0:c0 + width], conv_b[:, c0:c0 + width], dseg, dproj,
                                 name=f"conv_bwd{si}")
        g_cw.append(b)
        g_cb.append(c_)
    dproj = _put_heads(ddtx, dproj, o_dt, dt_pad, name="put_ddt")
    g_wp = _mm(dproj, h0, name="grad_w_in", ta=True, out_dtype=BF16, tm=tn_in, tn=1024, tk=1024)
    g_win = jnp.concatenate([g_wp[:d_ssm], g_wp[o_xbc:o_dt + heads], g_wp[o_u:o_xbc]], axis=0)
    def cols8(g):
        return jnp.transpose(g.reshape(g.shape[0], N_DEV, g.shape[1] // N_DEV), (1, 0, 2))
    dh0, r_in, r_cw = _mm(dproj, wp, name="dh0", out_dtype=BF16, tk=tn_in,
                          ride=_Ride([g_win.reshape(N_DEV, d_in // N_DEV, d), cols8(jnp.concatenate(g_cw, axis=1))], True))
    dx, g_attn = _rms_bwd(x2, attn_norm_w, dh0, dh1b, F32, name="attn_norm_bwd")

    smalls = [g_attn, jnp.concatenate(g_cb, axis=1), per_head(g_bias), per_head(g_alog), per_head(g_dsk), g_ssdn,
              g_pscale, g_ffn, g_final]
    small_w = [attn_norm_w, conv_b, dt_bias, a_log, d_skip, ssd_norm_w, pool_scale, ffn_norm_w, final_norm_w.reshape(1, d)]
    small_m = [m_attn_norm_w, m_conv_b, m_dt_bias, m_a_log, m_d_skip, m_ssd_norm_w, m_pool_scale, m_ffn_norm_w,
               m_final_norm_w.reshape(1, d)]
    small_v = [v_attn_norm_w, v_conv_b, v_dt_bias, v_a_log, v_d_skip, v_ssd_norm_w, v_pool_scale, v_ffn_norm_w,
               v_final_norm_w.reshape(1, d)]
    sizes = [s.shape[1] for s in smalls]
    n_small = sum(sizes)
    n_pad = -(-n_small // (16 * LANE)) * (16 * LANE)
    rows = n_pad // LANE
    def pack(vs):
        return jnp.pad(jnp.concatenate(vs, axis=1), ((0, 0), (0, n_pad - n_small))).reshape(rows, LANE)
    (r_small,) = _exchange([pack(smalls)], False, name="gather_small_grads")

    def big(parts, w, m, v, nm):
        shp = w.shape
        if w.ndim == 3:
            return _adamw(parts, w, m, v, name=nm)
        r2 = lambda t: t.reshape(-1, shp[-1])
        outs = _adamw(parts.reshape(N_DEV, -1, shp[-1]), r2(w), r2(m), r2(v), name=nm)
        return [o.reshape(shp) for o in outs]

    def big_t(parts, w, m, v, nm):
        return [tr_(o) for o in _adamw(parts, tr_(w), tr_(m), tr_(v), name=nm)]

    def big_flat(parts, w, m, v, nm):
        fl = lambda t: tr_(t).reshape(-1, LANE)
        outs = _adamw(parts.reshape(N_DEV, -1, LANE), fl(w), fl(m), fl(v), name=nm, tr=2048)
        return [tr_(o.reshape(1, w.shape[2], w.shape[1])) for o in outs]

    res = {
        "w_in": big_flat(r_in, w_in, m_w_in, v_w_in, "adamw_w_in"),
        "conv_w": big(r_cw, conv_w, m_conv_w, v_conv_w, "adamw_conv_w"),
        "pool_w": big(r_pw, pool_w, m_pool_w, v_pool_w, "adamw_pool_w"),
        "w_out": big(r_wo, w_out, m_w_out, v_w_out, "adamw_w_out"),
        "w_gate": big_t(r_wg, w_gate, m_w_gate, v_w_gate, "adamw_w_gate"),
        "w_up": big_t(r_wu, w_up, m_w_up, v_w_up, "adamw_w_up"),
        "w_down": big(r_wd, w_down, m_w_down, v_w_down, "adamw_w_down"),
    }
    s_out = _adamw(r_small, pack(small_w), pack(small_m), pack(small_v), name="adamw_small")
    names = ["attn_norm_w", "conv_b", "dt_bias", "a_log", "d_skip", "ssd_norm_w", "pool_scale", "ffn_norm_w", "final_norm_w"]
    offs = [sum(sizes[:i]) for i in range(len(sizes))]
    for i, nm in enumerate(names):
        shp = (d,) if nm == "final_norm_w" else (1, sizes[i])
        res[nm] = [o.reshape(1, n_pad)[:, offs[i]:offs[i] + sizes[i]].reshape(shp) for o in s_out]

    loss = lax.psum(loss11[0, 0], ("x", "y", "c"))
    order = ["attn_norm_w", "w_in", "conv_w", "conv_b", "dt_bias", "a_log", "d_skip", "ssd_norm_w", "pool_w", "pool_scale",
             "w_out", "ffn_norm_w", "w_gate", "w_up", "w_down", "final_norm_w"]
    outs = [loss, dx.reshape(x.shape)]
    for part in range(4):
        outs += [res[nm][part] for nm in order]
    return tuple(outs)
```

```python
import functools
import math

import jax
import jax.numpy as jnp
from jax import lax
from jax.experimental import pallas as pl
from jax.experimental.pallas import tpu as pltpu

F32 = jnp.float32
BF16 = jnp.bfloat16

NORM_EPS = 1e-5
HEAD_DIM = 64
STATE = 128
CHUNK = 256
GROUPS = 4
CONV_K = 4
POOL_WINDOWS = (2, 4, 8, 16)
POOL_HALO = 16
CONV_HALO = 8
LANE = 128
N_DEV = 8
ADAM_LR, ADAM_B1, ADAM_B2, ADAM_EPS, ADAM_WD, ADAM_STEP = 0.001, 0.9, 0.999, 1e-08, 0.01, 10
NEG_BIG = -1e30
FFN_GROUP = 4
MESH_ID = pl.DeviceIdType.MESH


def _tile(n, pref, mult):
    if n <= pref:
        return n
    t = (pref // mult) * mult
    while t >= mult:
        if n % t == 0:
            return t
        t -= mult
    return n


def _params(*sem):
    return pltpu.CompilerParams(dimension_semantics=sem)


ANY = pl.BlockSpec(memory_space=pl.ANY)


def _peer(k):
    x, y, c = lax.axis_index("x"), lax.axis_index("y"), lax.axis_index("c")
    return (1 - x if k & 4 else x, 1 - y if k & 2 else y, 1 - c if k & 1 else c)


def _my_index():
    return 4 * lax.axis_index("x") + 2 * lax.axis_index("y") + lax.axis_index("c")


class _Ride:
    def __init__(self, srcs, scatter):
        self.srcs, self.scatter, self.n = list(srcs), scatter, len(srcs)
        self.two_level = not scatter
        self.in_specs = [ANY] * self.n
        self.out_specs = [ANY] * self.n
        self.out_shape = [jax.ShapeDtypeStruct((N_DEV,) + tuple(s.shape[1:] if scatter else s.shape), s.dtype)
                          for s in self.srcs]
        self.scratch = [pltpu.SemaphoreType.DMA((self.n, N_DEV - 1)), pltpu.SemaphoreType.DMA((self.n, N_DEV - 1)),
                        pltpu.SemaphoreType.DMA((self.n,))]

    def _copies(self, src, dst, sems):
        send_sems, recv_sems, local_sems = sems
        me = _my_index()
        local = [pltpu.make_async_copy(src[t].at[me] if self.scatter else src[t], dst[t].at[me], local_sems.at[t])
                 for t in range(self.n)]
        remote = {}
        for k in range(1, N_DEV):
            peer, pidx = _peer(k), me ^ k
            remote[k] = [pltpu.make_async_remote_copy(
                src_ref=src[t].at[pidx] if self.scatter else src[t], dst_ref=dst[t].at[me],
                send_sem=send_sems.at[t, k - 1], recv_sem=recv_sems.at[t, k - 1],
                device_id=peer, device_id_type=MESH_ID) for t in range(self.n)]
        return local, remote

    def _forwards(self, dst, sems):
        send_sems, recv_sems, _ = sems
        me = _my_index()
        return {k: [pltpu.make_async_remote_copy(
            src_ref=dst[t].at[me ^ k], dst_ref=dst[t].at[me ^ k],
            send_sem=send_sems.at[t, (k ^ 1) - 1], recv_sem=recv_sems.at[t, (k ^ 1) - 1],
            device_id=_peer(1), device_id_type=MESH_ID) for t in range(self.n)] for k in (2, 4, 6)}

    def start(self, src, dst, sems):
        local, remote = self._copies(src, dst, sems)
        for cp in local:
            cp.start()
        for k in (2, 4, 6, 1) if self.two_level else range(1, N_DEV):
            for cp in remote[k]:
                cp.start()

    def relay(self, src, dst, sems):
        if not self.two_level:
            return
        _, remote = self._copies(src, dst, sems)
        fwd = self._forwards(dst, sems)
        for k in (2, 4, 6):
            for t in range(self.n):
                remote[k][t].wait_recv()
                fwd[k][t].start()

    def finish(self, src, dst, sems):
        local, remote = self._copies(src, dst, sems)
        if self.two_level:
            fwd = self._forwards(dst, sems)
            for k in (1, 3, 5, 7):
                for cp in remote[k]:
                    cp.wait_recv()
            for k in (1, 2, 4, 6):
                for cp in remote[k]:
                    cp.wait_send()
            for k in (2, 4, 6):
                for cp in fwd[k]:
                    cp.wait_send()
        else:
            for k in range(1, N_DEV):
                for cp in remote[k]:
                    cp.wait_send()
            for k in range(1, N_DEV):
                for cp in remote[k]:
                    cp.wait_recv()
        for lc in local:
            lc.wait()


RELAY_AT = 0.75


def _call(body, *, name, grid, in_specs, out_specs, out_shape, args, sem, scratch=(), ride=None):
    in_specs, out_specs, out_shape, scratch = list(in_specs), list(out_specs), list(out_shape), list(scratch)
    if ride is None:
        return pl.pallas_call(body, name=name, grid=grid, in_specs=in_specs, out_specs=out_specs, out_shape=out_shape,
                              scratch_shapes=scratch, compiler_params=_params(*sem))(*args)
    n_in, n_out, n_sc, nr = len(in_specs), len(out_specs), len(scratch), ride.n

    def full(*refs):
        ins, csrc = refs[:n_in], refs[n_in:n_in + nr]
        o0 = n_in + nr
        outs, cdst = refs[o0:o0 + n_out], refs[o0 + n_out:o0 + n_out + nr]
        s0 = o0 + n_out + nr
        sc, csem = refs[s0:s0 + n_sc], refs[s0 + n_sc:]
        ids = [pl.program_id(ax) for ax in range(len(grid))]
        step = functools.reduce(lambda acc, ig: acc * ig[1] + ig[0], zip(ids, grid), 0)
        total = functools.reduce(lambda p, g: p * g, grid, 1)

        @pl.when(step == 0)
        def _():
            ride.start(csrc, cdst, csem)

        body(*ins, *outs, *sc)

        @pl.when(step == min(int(total * RELAY_AT), total - 1))
        def _():
            ride.relay(csrc, cdst, csem)

        @pl.when(step == total - 1)
        def _():
            ride.finish(csrc, cdst, csem)

    return pl.pallas_call(
        full, name=name, grid=grid, in_specs=in_specs + ride.in_specs, out_specs=out_specs + ride.out_specs,
        out_shape=out_shape + ride.out_shape, scratch_shapes=scratch + ride.scratch,
        compiler_params=_params(*(["arbitrary"] * len(grid))),
    )(*args, *ride.srcs)


def _sigmoid(x):
    return 1.0 / (1.0 + jnp.exp(-x))


def _softplus(x):
    return jnp.maximum(x, 0.0) + jnp.log(1.0 + jnp.exp(-jnp.abs(x)))


def _dot(a, b, ta=False, tb=False):
    dn = (((0 if ta else 1,), (1 if tb else 0,)), ((), ()))
    return lax.dot_general(a.astype(BF16), b.astype(BF16), dn, preferred_element_type=F32)


def _mm(a, b, *, name, ta=False, tb=False, out_dtype=F32, tm=1024, tn=1024, tk=2048, residual=None, ride=None):
    m = a.shape[1] if ta else a.shape[0]
    k = a.shape[0] if ta else a.shape[1]
    n = b.shape[0] if tb else b.shape[1]
    assert k == (b.shape[1] if tb else b.shape[0])
    tm, tn = _tile(m, tm, LANE if ta else 8), _tile(n, tn, LANE)
    tk = _tile(k, tk, LANE if not (ta and tb) else LANE)
    nk = k // tk
    has_res = residual is not None

    def body(*refs):
        a_ref, b_ref = refs[0], refs[1]
        r_ref = refs[2] if has_res else None
        o_ref = refs[3] if has_res else refs[2]
        part = _dot(a_ref[...], b_ref[...], ta, tb)

        def finish(acc):
            if has_res:
                acc = acc + r_ref[...]
            o_ref[...] = acc.astype(o_ref.dtype)

        if nk == 1:
            finish(part)
        else:
            acc_ref = refs[-1]
            kk = pl.program_id(2)

            @pl.when(kk == 0)
            def _():
                acc_ref[...] = part

            @pl.when(kk > 0)
            def _():
                acc_ref[...] += part

            @pl.when(kk == nk - 1)
            def _():
                finish(acc_ref[...])

    a_spec = pl.BlockSpec((tk, tm), lambda i, j, kk: (kk, i)) if ta else pl.BlockSpec((tm, tk), lambda i, j, kk: (i, kk))
    b_spec = pl.BlockSpec((tn, tk), lambda i, j, kk: (j, kk)) if tb else pl.BlockSpec((tk, tn), lambda i, j, kk: (kk, j))
    o_spec = pl.BlockSpec((tm, tn), lambda i, j, kk: (i, j))
    in_specs = [a_spec, b_spec] + ([o_spec] if has_res else [])
    args = (a, b) + ((residual,) if has_res else ())
    res = _call(body, name=name, grid=(m // tm, n // tn, nk), in_specs=in_specs, out_specs=[o_spec],
                out_shape=[jax.ShapeDtypeStruct((m, n), out_dtype)], args=args,
                scratch=[pltpu.VMEM((tm, tn), F32)] if nk > 1 else [], sem=("parallel", "parallel", "arbitrary"), ride=ride)
    return res[0] if ride is None else res


def _gate_up(h, wg, wu, *, name, tm=256, ride=None):
    m, k = h.shape
    ns, c, _ = wg.shape
    tm = _tile(m, tm, 16)

    def body(h_ref, wg_ref, wu_ref, g_ref, u_ref, a_ref):
        hb = h_ref[...]
        g = _dot(hb, wg_ref[...], tb=True)
        u = _dot(hb, wu_ref[...], tb=True)
        g_ref[...] = g
        u_ref[...] = u
        a_ref[...] = (g * _sigmoid(g) * u).astype(a_ref.dtype)

    o_spec = pl.BlockSpec((None, tm, c), lambda s, i: (s, i, 0))
    w_spec = pl.BlockSpec((None, c, k), lambda s, i: (s, 0, 0), pipeline_mode=pl.Buffered(1))
    return _call(body, name=name, grid=(ns, m // tm),
                 in_specs=[pl.BlockSpec((tm, k), lambda s, i: (i, 0)), w_spec, w_spec], out_specs=[o_spec, o_spec, o_spec],
                 out_shape=[jax.ShapeDtypeStruct((ns, m, c), F32), jax.ShapeDtypeStruct((ns, m, c), F32),
                            jax.ShapeDtypeStruct((ns, m, c), BF16)],
                 args=(h, wg, wu), sem=("parallel", "parallel"), ride=ride)


def _gate_up_bwd(dh, wd, g, u, *, name, tm=256):
    m, k = dh.shape
    ns, c, _ = wd.shape
    tm = _tile(m, tm, 16)

    def body(dh_ref, wd_ref, g_ref, u_ref, dg_ref, du_ref):
        da = _dot(dh_ref[...], wd_ref[...], tb=True)
        g = g_ref[...]
        s = _sigmoid(g)
        dg_ref[...] = (da * u_ref[...] * (s * (1.0 + g * (1.0 - s)))).astype(dg_ref.dtype)
        du_ref[...] = (da * (g * s)).astype(du_ref.dtype)

    o_spec = pl.BlockSpec((None, tm, c), lambda s, i: (s, i, 0))
    w_spec = pl.BlockSpec((None, c, k), lambda s, i: (s, 0, 0), pipeline_mode=pl.Buffered(1))
    return _call(body, name=name, grid=(ns, m // tm),
                 in_specs=[pl.BlockSpec((tm, k), lambda s, i: (i, 0)), w_spec, o_spec, o_spec],
                 out_specs=[o_spec, o_spec],
                 out_shape=[jax.ShapeDtypeStruct((ns, m, c), BF16), jax.ShapeDtypeStruct((ns, m, c), BF16)],
                 args=(dh, wd, g, u), sem=("parallel", "parallel"))


def _mm_shards(pairs, *, name, tb=False, residual=None, out_dtype=F32, tm=1024, tn=1024, per_step=2, ride=None):
    ns, m, c = pairs[0][0].shape
    n = pairs[0][1].shape[1] if tb else pairs[0][1].shape[2]
    tm, tn = _tile(m, tm, 16), _tile(n, tn, LANE)
    npair = len(pairs)
    has_res = residual is not None
    assert ns % per_step == 0
    nsteps = ns // per_step

    def body(*refs):
        o_ref, acc_ref = refs[-2], refs[-1]
        s = pl.program_id(2)
        part = None
        for p in range(npair):
            for q in range(per_step):
                term = _dot(refs[2 * p][q], refs[2 * p + 1][q], tb=tb)
                part = term if part is None else part + term

        @pl.when(s == 0)
        def _():
            acc_ref[...] = part

        @pl.when(s > 0)
        def _():
            acc_ref[...] += part

        @pl.when(s == nsteps - 1)
        def _():
            acc = acc_ref[...]
            if has_res:
                acc = acc + refs[2 * npair][...]
            o_ref[...] = acc.astype(o_ref.dtype)

    a_spec = pl.BlockSpec((per_step, tm, c), lambda i, j, s: (s, i, 0))
    b_spec = (pl.BlockSpec((per_step, tn, c), lambda i, j, s: (s, j, 0)) if tb
              else pl.BlockSpec((per_step, c, tn), lambda i, j, s: (s, 0, j)))
    o_spec = pl.BlockSpec((tm, tn), lambda i, j, s: (i, j))
    args = [t for pr in pairs for t in pr] + ([residual] if has_res else [])
    res = _call(body, name=name, grid=(m // tm, n // tn, nsteps),
                in_specs=[a_spec, b_spec] * npair + ([o_spec] if has_res else []), out_specs=[o_spec],
                out_shape=[jax.ShapeDtypeStruct((m, n), out_dtype)], args=args, scratch=[pltpu.VMEM((tm, tn), F32)],
                sem=("parallel", "parallel", "arbitrary"), ride=ride)
    return res[0] if ride is None else res


def _grad_shards(a, b, *, name, tk=2048, tn=512):
    ns, l, c = a.shape
    d = b.shape[1]
    tk, tn = _tile(l, tk, 16), _tile(d, tn, LANE)
    nk = l // tk

    def body(a_ref, b_ref, o_ref, acc_ref):
        kk = pl.program_id(2)
        part = _dot(a_ref[...], b_ref[...], ta=True)

        @pl.when(kk == 0)
        def _():
            acc_ref[...] = part

        @pl.when(kk > 0)
        def _():
            acc_ref[...] += part

        @pl.when(kk == nk - 1)
        def _():
            o_ref[...] = acc_ref[...].astype(o_ref.dtype)

    return pl.pallas_call(
        body, name=name, grid=(ns, d // tn, nk),
        in_specs=[pl.BlockSpec((None, tk, c), lambda s, j, kk: (s, kk, 0)), pl.BlockSpec((tk, tn), lambda s, j, kk: (kk, j))],
        out_specs=pl.BlockSpec((None, c, tn), lambda s, j, kk: (s, 0, j)),
        out_shape=jax.ShapeDtypeStruct((ns, c, d), BF16), scratch_shapes=[pltpu.VMEM((c, tn), F32)],
        compiler_params=_params("parallel", "parallel", "arbitrary"),
    )(a, b)


def _rms_fwd(x, w, *, name, tl=512, ride=None):
    l, d = x.shape
    tl = _tile(l, tl, 16)

    def body(x_ref, w_ref, o_ref):
        xv = x_ref[...]
        r = lax.rsqrt(jnp.mean(xv * xv, axis=-1, keepdims=True) + NORM_EPS)
        o_ref[...] = (xv * r * w_ref[...]).astype(o_ref.dtype)

    row = pl.BlockSpec((tl, d), lambda i: (i, 0))
    res = _call(body, name=name, grid=(l // tl,), in_specs=[row, pl.BlockSpec((1, d), lambda i: (0, 0))],
                out_specs=[row], out_shape=[jax.ShapeDtypeStruct((l, d), BF16)], args=(x, w), sem=("parallel",), ride=ride)
    return res[0] if ride is None else res


def _expand_heads(src, col0, width, heads, *, name, tl=512):
    l = src.shape[0]
    tl = _tile(l, tl, 16)
    assert col0 % width == 0 and width % LANE == 0 and heads <= width
    n = heads * HEAD_DIM

    def body(s_ref, o_ref):
        spread = (lax.broadcasted_iota(jnp.int32, (width, n), 0)
                  == lax.broadcasted_iota(jnp.int32, (width, n), 1) // HEAD_DIM).astype(BF16)
        o_ref[...] = _dot_tri(s_ref[...], spread)

    return pl.pallas_call(
        body, name=name, grid=(l // tl,), in_specs=[pl.BlockSpec((tl, width), lambda i: (i, col0 // width))],
        out_specs=pl.BlockSpec((tl, n), lambda i: (i, 0)), out_shape=jax.ShapeDtypeStruct((l, n), F32),
        compiler_params=_params("parallel"),
    )(src)


def _rms_bwd(x, w, dy, add, out_dtype, *, name, tl=256):
    l, d = x.shape
    tl = _tile(l, tl, 16)

    def body(x_ref, w_ref, dy_ref, add_ref, dx_ref, dw_ref):
        i = pl.program_id(0)
        xv = x_ref[...]
        r = lax.rsqrt(jnp.mean(xv * xv, axis=-1, keepdims=True) + NORM_EPS)
        xh = xv * r
        dyv = dy_ref[...].astype(F32)
        dxh = dyv * w_ref[...]
        dx = r * (dxh - xh * jnp.mean(dxh * xh, axis=-1, keepdims=True)) + add_ref[...].astype(F32)
        dx_ref[...] = dx.astype(dx_ref.dtype)
        part = jnp.sum(dyv * xh, axis=0, keepdims=True)

        @pl.when(i == 0)
        def _():
            dw_ref[...] = part

        @pl.when(i > 0)
        def _():
            dw_ref[...] += part

    row = pl.BlockSpec((tl, d), lambda i: (i, 0))
    vec = pl.BlockSpec((1, d), lambda i: (0, 0))
    return pl.pallas_call(
        body, name=name, grid=(l // tl,), in_specs=[row, vec, row, row], out_specs=[row, vec],
        out_shape=[jax.ShapeDtypeStruct((l, d), out_dtype), jax.ShapeDtypeStruct((1, d), F32)],
        compiler_params=_params("arbitrary"),
    )(x, w, dy, add)


def _loss_head(h, w, target, *, name, tl=256):
    l, d = h.shape
    tl = _tile(l, tl, 16)

    def body(x_ref, w_ref, t_ref, loss_ref, dxb_ref, dw_ref):
        i = pl.program_id(0)
        xv = x_ref[...]
        r = lax.rsqrt(jnp.mean(xv * xv, axis=-1, keepdims=True) + NORM_EPS)
        xh = xv * r
        wv = w_ref[...]
        diff = xh * wv - t_ref[...]
        lpart = 0.5 * jnp.sum(jnp.mean(diff * diff, axis=-1, keepdims=True), axis=0, keepdims=True)
        dyv = diff * (1.0 / d)
        dxh = dyv * wv
        dx = r * (dxh - xh * jnp.mean(dxh * xh, axis=-1, keepdims=True))
        dxb_ref[...] = dx.astype(BF16)
        part = jnp.sum(dyv * xh, axis=0, keepdims=True)

        @pl.when(i == 0)
        def _():
            dw_ref[...] = part
            loss_ref[...] = lpart

        @pl.when(i > 0)
        def _():
            dw_ref[...] += part
            loss_ref[...] += lpart

    row = pl.BlockSpec((tl, d), lambda i: (i, 0))
    vec = pl.BlockSpec((1, d), lambda i: (0, 0))
    one = pl.BlockSpec((1, 1), lambda i: (0, 0))
    return pl.pallas_call(
        body, name=name, grid=(l // tl,), in_specs=[row, vec, row], out_specs=[one, row, vec],
        out_shape=[jax.ShapeDtypeStruct((1, 1), F32), jax.ShapeDtypeStruct((l, d), BF16), jax.ShapeDtypeStruct((1, d), F32)],
        compiler_params=_params("arbitrary"),
    )(h, w, target)


def _conv_pre(ext, w_ref):
    acc = ext * w_ref[CONV_K - 1:CONV_K, :]
    for k in range(CONV_K - 1):
        acc = acc + pltpu.roll(ext, CONV_K - 1 - k, axis=0) * w_ref[k:k + 1, :]
    return acc


def _conv_fwd(proj, col0, cw, cb, *, name, tl=512, tc=1024):
    l = proj.shape[0]
    c = cw.shape[1]
    tl, tc = _tile(l, tl, 16), _tile(math.gcd(c, col0), tc, LANE)
    assert col0 % tc == 0
    off, hb = col0 // tc, tl // CONV_HALO

    def body(x_ref, halo_ref, w_ref, b_ref, o_ref):
        i = pl.program_id(0)
        halo = jnp.where(i > 0, halo_ref[...], 0.0)
        ext = jnp.concatenate([halo, x_ref[...]], axis=0)
        pre = _conv_pre(ext, w_ref)[CONV_HALO:] + b_ref[...]
        o_ref[...] = pre * _sigmoid(pre)

    return pl.pallas_call(
        body, name=name, grid=(l // tl, c // tc),
        in_specs=[pl.BlockSpec((tl, tc), lambda i, j: (i, off + j)),
                  pl.BlockSpec((CONV_HALO, tc), lambda i, j: (jnp.maximum(i * hb - 1, 0), off + j)),
                  pl.BlockSpec((CONV_K, tc), lambda i, j: (0, j)), pl.BlockSpec((1, tc), lambda i, j: (0, j))],
        out_specs=pl.BlockSpec((tl, tc), lambda i, j: (i, j)),
        out_shape=jax.ShapeDtypeStruct((l, c), F32), compiler_params=_params("parallel", "parallel"),
    )(proj, proj, cw, cb)


def _conv_bwd(proj, col0, cw, cb, dout, dproj, *, name, tl=512, tc=1024):
    l = proj.shape[0]
    c = dout.shape[1]
    tl, tc = _tile(l, tl, 16), _tile(math.gcd(c, col0), tc, LANE)
    assert col0 % tc == 0
    off, hb, nt = col0 // tc, tl // CONV_HALO, l // tl
    n_ext = tl + CONV_HALO

    def body(x_ref, prev_ref, nxt_ref, d_ref, dn_ref, w_ref, b_ref, dproj_in, dx_ref, dw_ref, db_ref):
        del dproj_in
        i = pl.program_id(1)
        last = i == nt - 1
        halo = jnp.where(i > 0, prev_ref[...], 0.0)
        ext = jnp.concatenate([halo, x_ref[...], nxt_ref[...]], axis=0)
        pre = _conv_pre(ext, w_ref)[CONV_HALO:] + b_ref[...]
        dext = jnp.concatenate([d_ref[...], jnp.where(last, 0.0, dn_ref[...])], axis=0)
        s = _sigmoid(pre)
        dpre = dext * (s * (1.0 + pre * (1.0 - s)))
        dx = dpre * w_ref[CONV_K - 1:CONV_K, :]
        for k in range(CONV_K - 1):
            dx = dx + pltpu.roll(dpre, n_ext - (CONV_K - 1 - k), axis=0) * w_ref[k:k + 1, :]
        dx_ref[...] = dx[:tl].astype(dx_ref.dtype)
        dp = dpre[:tl]
        rows = [jnp.sum(dp * pltpu.roll(ext, CONV_K - 1 - k, axis=0)[CONV_HALO:CONV_HALO + tl], axis=0, keepdims=True)
                for k in range(CONV_K - 1)]
        rows.append(jnp.sum(dp * ext[CONV_HALO:CONV_HALO + tl], axis=0, keepdims=True))
        dwp = jnp.concatenate(rows, axis=0)
        dbp = jnp.sum(dp, axis=0, keepdims=True)

        @pl.when(i == 0)
        def _():
            dw_ref[...] = dwp
            db_ref[...] = dbp

        @pl.when(i > 0)
        def _():
            dw_ref[...] += dwp
            db_ref[...] += dbp

    cur = lambda j, i: (i, j)
    nxt = lambda j, i: (jnp.minimum((i + 1) * hb, nt * hb - 1), j)
    return pl.pallas_call(
        body, name=name, grid=(c // tc, nt),
        in_specs=[pl.BlockSpec((tl, tc), lambda j, i: (i, off + j)),
                  pl.BlockSpec((CONV_HALO, tc), lambda j, i: (jnp.maximum(i * hb - 1, 0), off + j)),
                  pl.BlockSpec((CONV_HALO, tc), lambda j, i: (jnp.minimum((i + 1) * hb, nt * hb - 1), off + j)),
                  pl.BlockSpec((tl, tc), cur), pl.BlockSpec((CONV_HALO, tc), nxt),
                  pl.BlockSpec((CONV_K, tc), lambda j, i: (0, j)), pl.BlockSpec((1, tc), lambda j, i: (0, j)), ANY],
        out_specs=[pl.BlockSpec((tl, tc), lambda j, i: (i, off + j)), pl.BlockSpec((CONV_K, tc), lambda j, i: (0, j)),
                   pl.BlockSpec((1, tc), lambda j, i: (0, j))],
        out_shape=[jax.ShapeDtypeStruct(dproj.shape, dproj.dtype), jax.ShapeDtypeStruct((CONV_K, c), F32),
                   jax.ShapeDtypeStruct((1, c), F32)],
        input_output_aliases={7: 0}, compiler_params=_params("parallel", "arbitrary"),
    )(proj, proj, proj, dout, dout, cw, cb, dproj)


def _pool_fwd(proj, col0, d_pool, *, name, tl=256):
    l = proj.shape[0]
    tl = _tile(l, tl, POOL_HALO)
    assert col0 % d_pool == 0
    off, hb, pg = col0 // d_pool, tl // POOL_HALO, d_pool // GROUPS

    def body(u_ref, halo_ref, o_ref):
        i = pl.program_id(0)
        halo = jnp.where(i > 0, halo_ref[...], 0.0)
        ext = jnp.concatenate([halo, u_ref[...]], axis=0)
        t = i * tl + lax.broadcasted_iota(jnp.int32, (tl, 1), 0)
        s, width = ext, 1
        for gi, win in enumerate(POOL_WINDOWS):
            while width < win:
                s = s + pltpu.roll(s, width, axis=0)
                width *= 2
            cnt = jnp.minimum(t + 1, win).astype(F32)
            sl = slice(gi * pg, (gi + 1) * pg)
            o_ref[:, sl] = (s[POOL_HALO:, sl] / cnt - ext[POOL_HALO:, sl]).astype(o_ref.dtype)

    return pl.pallas_call(
        body, name=name, grid=(l // tl,),
        in_specs=[pl.BlockSpec((tl, d_pool), lambda i: (i, off)),
                  pl.BlockSpec((POOL_HALO, d_pool), lambda i: (jnp.maximum(i * hb - 1, 0), off))],
        out_specs=pl.BlockSpec((tl, d_pool), lambda i: (i, 0)),
        out_shape=jax.ShapeDtypeStruct((l, d_pool), BF16), compiler_params=_params("parallel"),
    )(proj, proj)


def _pool_bwd(dp, dproj, col0, *, name, tl=256):
    l, d_pool = dp.shape
    tl = _tile(l, tl, POOL_HALO)
    assert col0 % d_pool == 0
    off, hb, nt, pg = col0 // d_pool, tl // POOL_HALO, l // tl, d_pool // GROUPS
    n_ext = tl + POOL_HALO

    def body(d_ref, nxt_ref, dproj_in, o_ref):
        del dproj_in
        i = pl.program_id(0)
        ext = jnp.concatenate([d_ref[...], jnp.where(i == nt - 1, 0.0, nxt_ref[...])], axis=0)
        t = i * tl + lax.broadcasted_iota(jnp.int32, (n_ext, 1), 0)
        for gi, win in enumerate(POOL_WINDOWS):
            sl = slice(gi * pg, (gi + 1) * pg)
            dg = ext[:, sl]
            s = dg / jnp.minimum(t + 1, win).astype(F32)
            width = 1
            while width < win:
                s = s + pltpu.roll(s, n_ext - width, axis=0)
                width *= 2
            o_ref[:, sl] = (s[:tl] - dg[:tl]).astype(o_ref.dtype)

    return pl.pallas_call(
        body, name=name, grid=(nt,),
        in_specs=[pl.BlockSpec((tl, d_pool), lambda i: (i, 0)),
                  pl.BlockSpec((POOL_HALO, d_pool), lambda i: (jnp.minimum((i + 1) * hb, nt * hb - 1), 0)), ANY],
        out_specs=pl.BlockSpec((tl, d_pool), lambda i: (i, off)),
        out_shape=jax.ShapeDtypeStruct(dproj.shape, dproj.dtype), input_output_aliases={2: 0},
        compiler_params=_params("parallel"),
    )(dp, dp, dproj)


def _put_heads(src, dst, col0, width, *, name, tl=512):
    l, c = src.shape
    tl = _tile(l, tl, 16)
    assert col0 % width == 0 and width % LANE == 0 and c // HEAD_DIM <= width

    def body(s_ref, dst_in, o_ref):
        del dst_in
        pick = (lax.broadcasted_iota(jnp.int32, (c, width), 0)
                == HEAD_DIM * lax.broadcasted_iota(jnp.int32, (c, width), 1)).astype(BF16)
        hi, lo = _split2(s_ref[...])
        o_ref[...] = (jnp.dot(hi, pick, preferred_element_type=F32)
                      + jnp.dot(lo, pick, preferred_element_type=F32)).astype(o_ref.dtype)

    return pl.pallas_call(
        body, name=name, grid=(l // tl,), in_specs=[pl.BlockSpec((tl, c), lambda i: (i, 0)), ANY],
        out_specs=pl.BlockSpec((tl, width), lambda i: (i, col0 // width)),
        out_shape=jax.ShapeDtypeStruct(dst.shape, dst.dtype), input_output_aliases={1: 0},
        compiler_params=_params("parallel"),
    )(src, dst)


def _pool_mix_fwd(pooled, pw, scale, mixed, col0, *, name, tm=1024):
    l, d_pool = pooled.shape
    pg = d_pool // GROUPS
    tm = _tile(l, tm, 16)
    assert col0 % pg == 0
    off = col0 // pg

    def body(a_ref, w_ref, s_ref, mixed_in, raw_ref, mixed_ref):
        del mixed_in
        raw = _dot(a_ref[...], w_ref[...])
        raw_ref[...] = raw
        mixed_ref[...] = (raw * s_ref[...]).astype(mixed_ref.dtype)

    blk = pl.BlockSpec((tm, pg), lambda i, g: (i, g))
    return pl.pallas_call(
        body, name=name, grid=(l // tm, GROUPS),
        in_specs=[blk, pl.BlockSpec((None, pg, pg), lambda i, g: (g, 0, 0)), pl.BlockSpec((1, pg), lambda i, g: (0, g)),
                  pl.BlockSpec(memory_space=pl.ANY)],
        out_specs=[blk, pl.BlockSpec((tm, pg), lambda i, g: (i, off + g))],
        out_shape=[jax.ShapeDtypeStruct((l, d_pool), F32), jax.ShapeDtypeStruct(mixed.shape, mixed.dtype)],
        input_output_aliases={3: 1}, compiler_params=_params("parallel", "parallel"),
    )(pooled, pw, scale, mixed)


def _pool_mix_bwd(dmixed, col0, raw, scale, pw, *, name, tm=1024):
    l, d_pool = raw.shape
    pg = d_pool // GROUPS
    tm = _tile(l, tm, 16)
    assert col0 % pg == 0
    off = col0 // pg

    def body(d_ref, raw_ref, s_ref, w_ref, draw_ref, dp_ref, ds_ref):
        i = pl.program_id(1)
        dv = d_ref[...]
        draw = (dv * s_ref[...]).astype(BF16)
        draw_ref[...] = draw
        dp_ref[...] = _dot(draw, w_ref[...], tb=True)
        part = jnp.sum(dv * raw_ref[...], axis=0, keepdims=True)

        @pl.when(i == 0)
        def _():
            ds_ref[...] = part

        @pl.when(i > 0)
        def _():
            ds_ref[...] += part

    blk = pl.BlockSpec((tm, pg), lambda g, i: (i, g))
    vec = pl.BlockSpec((1, pg), lambda g, i: (0, g))
    return pl.pallas_call(
        body, name=name, grid=(GROUPS, l // tm),
        in_specs=[pl.BlockSpec((tm, pg), lambda g, i: (i, off + g)), blk, vec,
                  pl.BlockSpec((None, pg, pg), lambda g, i: (g, 0, 0))],
        out_specs=[blk, blk, vec],
        out_shape=[jax.ShapeDtypeStruct((l, d_pool), BF16), jax.ShapeDtypeStruct((l, d_pool), F32),
                   jax.ShapeDtypeStruct((1, d_pool), F32)],
        compiler_params=_params("parallel", "arbitrary"),
    )(dmixed, raw, scale, pw)


def _group_mm_tn(a, b, *, name, tk=1024):
    l, d_pool = a.shape
    pg = d_pool // GROUPS
    tk = _tile(l, tk, 16)

    def body(a_ref, b_ref, o_ref):
        kk = pl.program_id(1)
        part = _dot(a_ref[...], b_ref[...], ta=True)

        @pl.when(kk == 0)
        def _():
            o_ref[...] = part

        @pl.when(kk > 0)
        def _():
            o_ref[...] += part

    blk = pl.BlockSpec((tk, pg), lambda g, kk: (kk, g))
    return pl.pallas_call(
        body, name=name, grid=(GROUPS, l // tk), in_specs=[blk, blk],
        out_specs=pl.BlockSpec((None, pg, pg), lambda g, kk: (g, 0, 0)),
        out_shape=jax.ShapeDtypeStruct((GROUPS, pg, pg), F32), compiler_params=_params("parallel", "arbitrary"),
    )(a, b)


def _split3(v):
    hi = v.astype(BF16)
    r1 = v - hi.astype(F32)
    mid = r1.astype(BF16)
    lo = (r1 - mid.astype(F32)).astype(BF16)
    return hi, mid, lo


def _tri_dot(tri, v):
    hi, mid, lo = _split3(v)
    d = lambda p: jnp.dot(tri, p, preferred_element_type=F32)
    return d(hi) + d(mid) + d(lo)


def _dot_tri(v, tri):
    hi, mid, lo = _split3(v)
    d = lambda p: jnp.dot(p, tri, preferred_element_type=F32)
    return d(hi) + d(mid) + d(lo)


def _split2(v):
    hi = v.astype(BF16)
    return hi, (v - hi.astype(F32)).astype(BF16)


def _scan_specs(hpg, nc, order):
    gw, q = hpg * HEAD_DIM, CHUNK
    b_off, c_off = (GROUPS * gw) // STATE, (GROUPS * gw) // STATE + GROUPS
    xs = pl.BlockSpec((q, gw), lambda g, c: (order(c), g))
    bm = pl.BlockSpec((q, STATE), lambda g, c: (order(c), b_off + g))
    cm = pl.BlockSpec((q, STATE), lambda g, c: (order(c), c_off + g))
    dtt = pl.BlockSpec((None, hpg, q), lambda g, c: (g, 0, order(c)))
    vec = pl.BlockSpec((1, gw), lambda g, c: (0, g))
    hcol = pl.BlockSpec((None, hpg, 1), lambda g, c: (g, 0, 0))
    st = pl.BlockSpec((None, None, STATE, gw), lambda g, c: (g, order(c), 0, 0))
    return xs, bm, cm, dtt, vec, hcol, st


def _scan_common(dtx_ref, dtrt_ref, biasx_ref, biast_ref, alogx_ref, alogt_ref):
    q = CHUNK
    dt = _softplus(dtx_ref[...] + biasx_ref[...])
    a = -jnp.exp(alogx_ref[...])
    dtt = _softplus(dtrt_ref[...] + biast_ref[...])
    at = -jnp.exp(alogt_ref[...])
    row = lax.broadcasted_iota(jnp.int32, (q, q), 0)
    col = lax.broadcasted_iota(jnp.int32, (q, q), 1)
    lower = (row >= col).astype(BF16)
    upper = (row <= col).astype(BF16)
    acum = _tri_dot(lower, dt * a)
    acum_t = _dot_tri(dtt * at, upper)
    return dt, a, acum, acum_t, row, col, upper


def _gated(yv, z):
    s = _sigmoid(z)
    sz = z * s
    gv = yv * sz
    return gv, lax.rsqrt(jnp.mean(gv * gv, axis=-1, keepdims=True) + NORM_EPS), sz, s


def _scan_fwd(xbc, dtx, dtrt, biasx, biast, alogx, alogt, dskx, proj, wn, d_mix, *, name, ride=None):
    l = xbc.shape[0]
    hpg = dtrt.shape[1]
    gw, q, nc = hpg * HEAD_DIM, CHUNK, l // CHUNK
    assert hpg % 2 == 0
    xs_s, bm_s, cm_s, dtt_s, vec, hcol, st_s = _scan_specs(hpg, nc, lambda c: c)

    def body(xs_ref, b_ref, c_ref, dtx_ref, dtrt_ref, biasx_ref, biast_ref, alogx_ref, alogt_ref, dsk_ref, z_ref, wn_ref,
             y_ref, st_ref, mixed_ref, h_ref):
        @pl.when(pl.program_id(1) == 0)
        def _():
            h_ref[...] = jnp.zeros_like(h_ref)

        dt, _, acum, acum_t, row, col, _ = _scan_common(dtx_ref, dtrt_ref, biasx_ref, biast_ref, alogx_ref, alogt_ref)
        causal = row >= col
        xs = xs_ref[...]
        xdt = xs * dt
        xdtb = xdt.astype(BF16)
        bmat, cmat = b_ref[...], c_ref[...]
        cb = _dot(cmat, bmat, tb=True)
        h_all = h_ref[...]
        st_ref[...] = h_all
        rest = _dot(cmat, h_all) * jnp.exp(acum) + dsk_ref[...] * xs
        first = lax.broadcasted_iota(jnp.int32, (q, 2 * HEAD_DIM), 1) < HEAD_DIM
        parts = []
        for jp in range(hpg // 2):
            pair = slice(2 * jp * HEAD_DIM, (2 * jp + 2) * HEAD_DIM)
            xp = xdtb[:, pair]
            ys = []
            for j in (2 * jp, 2 * jp + 1):
                a_col, a_row = acum[:, j * HEAD_DIM:j * HEAD_DIM + 1], acum_t[j:j + 1, :]
                lm = jnp.exp(jnp.where(causal, a_col - a_row, NEG_BIG))
                ys.append(_dot(cb * lm, xp))
            parts.append(jnp.where(first, ys[0], ys[1]))
        yv = jnp.concatenate(parts, axis=1) + rest
        y_ref[...] = yv
        gv, r, _, _ = _gated(yv, z_ref[...])
        mixed_ref[...] = (gv * r * wn_ref[...]).astype(mixed_ref.dtype)
        a_last = acum[q - 1:q, :]
        h_ref[...] = jnp.exp(a_last) * h_all + _dot(bmat, xdt * jnp.exp(a_last - acum), ta=True)

    return _call(
        body, name=name, grid=(GROUPS, nc),
        in_specs=[xs_s, bm_s, cm_s, xs_s, dtt_s, vec, hcol, vec, hcol, vec, xs_s, vec], out_specs=[xs_s, st_s, xs_s],
        out_shape=[jax.ShapeDtypeStruct((l, GROUPS * gw), F32), jax.ShapeDtypeStruct((GROUPS, nc, STATE, gw), F32),
                   jax.ShapeDtypeStruct((l, d_mix), BF16)],
        scratch=[pltpu.VMEM((STATE, gw), F32)], sem=("parallel", "arbitrary"),
        args=(xbc, xbc, xbc, dtx, dtrt, biasx, biast, alogx, alogt, dskx, proj, wn), ride=ride)


def _scan_bwd(xbc, dtx, dtrt, biasx, biast, alogx, alogt, dskx, states, dmixed, y, proj, wn, *, name, ride=None):
    l = xbc.shape[0]
    hpg = dtrt.shape[1]
    gw, q, nc = hpg * HEAD_DIM, CHUNK, l // CHUNK
    rev = lambda c: nc - 1 - c
    xs_s, bm_s, cm_s, dtt_s, vec, hcol, st_s = _scan_specs(hpg, nc, rev)
    bc_s = pl.BlockSpec((q, STATE), lambda g, c: (rev(c), g))

    def body(xs_ref, b_ref, c_ref, dtx_ref, dtrt_ref, biasx_ref, biast_ref, alogx_ref, alogt_ref, dsk_ref, st_ref,
             dmix_ref, y_ref, z_ref, wn_ref,
             dxs_ref, db_ref, dc_ref, ddt_ref, dbias_ref, dalog_ref, ddsk_ref, dz_ref, dwn_ref, dh_ref):
        step = pl.program_id(1)

        @pl.when(step == 0)
        def _():
            dh_ref[...] = jnp.zeros_like(dh_ref)

        yv, z, dmix = y_ref[...], z_ref[...], dmix_ref[...]
        gv, rn, sz, sg = _gated(yv, z)
        gh = gv * rn
        dgh = dmix * wn_ref[...]
        dg = rn * (dgh - gh * jnp.mean(dgh * gh, axis=-1, keepdims=True))
        dyv = dg * sz
        dz_ref[...] = (dg * yv * (sg * (1.0 + z * (1.0 - sg)))).astype(dz_ref.dtype)
        p_wn = jnp.sum(dmix * gh, axis=0, keepdims=True)

        dt, a, acum, acum_t, row, col, upper = _scan_common(dtx_ref, dtrt_ref, biasx_ref, biast_ref, alogx_ref, alogt_ref)
        causal, anti, strict = row >= col, row <= col, row > col
        xs = xs_ref[...]
        xdt = xs * dt
        xdtb, dyb = xdt.astype(BF16), dyv.astype(BF16)
        bmat, cmat = b_ref[...], c_ref[...]
        cb = _dot(cmat, bmat, tb=True)
        cbt = _dot(bmat, cmat, tb=True)
        h_all, dh = st_ref[...], dh_ref[...]
        a_last = acum[q - 1:q, :]
        ea, ee, gam = jnp.exp(acum), jnp.exp(a_last - acum), jnp.exp(a_last)
        zc = _dot(cmat, h_all)
        bdh = _dot(bmat, dh)
        dz = dyv * ea
        xe = xdt * ee
        dc_state = _dot(dz, h_all, tb=True)
        db_state = _dot(xe, dh, tb=True)
        dh_ref[...] = gam * dh + _dot(cmat, dz, ta=True)
        lane = lax.broadcasted_iota(jnp.int32, (q, 2 * HEAD_DIM), 1)
        first = lane < HEAD_DIM
        ones = jnp.ones((q // 2, 2 * HEAD_DIM), BF16)
        dcb = jnp.zeros((q, q), F32)
        dxdt_parts, da_parts = [], []
        for jp in range(hpg // 2):
            pair = slice(2 * jp * HEAD_DIM, (2 * jp + 2) * HEAD_DIM)
            xp, dyp = xdtb[:, pair], dyb[:, pair]
            dxs_r, das_r = [], []
            for r, j in enumerate((2 * jp, 2 * jp + 1)):
                seg = acum[:, j * HEAD_DIM:j * HEAD_DIM + 1] - acum_t[j:j + 1, :]
                lm = jnp.exp(jnp.where(causal, seg, NEG_BIG))
                lmt = jnp.exp(jnp.where(anti, -seg, NEG_BIG))
                mm, mmt = cb * lm, cbt * lmt
                keep = first if r == 0 else jnp.logical_not(first)
                dyk = jnp.where(keep, dyp, 0)
                dxs_r.append(_dot(mmt, dyp))
                dm = _dot(dyk, xp, tb=True)
                dcb = dcb + dm * lm
                corner = jnp.dot(upper, (dm * mm).astype(BF16), preferred_element_type=F32)
                corner = jnp.where(strict, corner, 0.0)
                chi, clo = _split2(corner[:, :q // 2] + corner[:, q // 2:])
                das_r.append(jnp.dot(chi, ones, preferred_element_type=F32) + jnp.dot(clo, ones, preferred_element_type=F32))
            dxdt_parts.append(jnp.where(first, dxs_r[0], dxs_r[1]))
            da_parts.append(jnp.where(first, das_r[0], das_r[1]))
        dxdt = jnp.concatenate(dxdt_parts, axis=1) + bdh * ee
        da_intra = jnp.concatenate(da_parts, axis=1)
        dxs_ref[...] = dxdt * dt + dsk_ref[...] * dyv
        dc_ref[...] = dc_state + _dot(dcb, bmat)
        db_ref[...] = db_state + _dot(dcb, cmat, ta=True)
        ri = lax.broadcasted_iota(jnp.int32, (gw, gw), 0) // HEAD_DIM
        ci = lax.broadcasted_iota(jnp.int32, (gw, gw), 1) // HEAD_DIM
        blockdiag = (ri == ci).astype(BF16)

        def head_sum(v):
            hi, lo = _split2(v)
            return (jnp.dot(hi, blockdiag, preferred_element_type=F32)
                    + jnp.dot(lo, blockdiag, preferred_element_type=F32))

        xt = xe * bdh
        small = jnp.concatenate([
            jnp.sum(xt, axis=0, keepdims=True) + gam * jnp.sum(dh * h_all, axis=0, keepdims=True),
            jnp.sum(dyv * xs, axis=0, keepdims=True), jnp.zeros((6, gw), F32)], axis=0)
        small = head_sum(small)
        rows = lax.broadcasted_iota(jnp.int32, (q, 1), 0)
        da_local = head_sum(dyv * zc * ea - xt) + jnp.where(rows == q - 1, small[0:1, :], 0.0)
        hi, lo = _split2(da_local)
        d_dta = (da_intra + jnp.dot(upper, hi, preferred_element_type=F32)
                 + jnp.dot(upper, lo, preferred_element_type=F32))
        d_raw = (d_dta * a + head_sum(dxdt * xs)) * _sigmoid(dtx_ref[...] + biasx_ref[...])
        ddt_ref[...] = d_raw
        p_bias = jnp.sum(d_raw, axis=0, keepdims=True)
        p_alog = jnp.sum(d_dta * dt, axis=0, keepdims=True) * a
        p_dsk = small[1:2, :]

        @pl.when(step == 0)
        def _():
            dbias_ref[...] = p_bias
            dalog_ref[...] = p_alog
            ddsk_ref[...] = p_dsk
            dwn_ref[...] = p_wn

        @pl.when(step > 0)
        def _():
            dbias_ref[...] += p_bias
            dalog_ref[...] += p_alog
            ddsk_ref[...] += p_dsk
            dwn_ref[...] += p_wn

    wide = jax.ShapeDtypeStruct((l, GROUPS * gw), F32)
    narrow = jax.ShapeDtypeStruct((l, GROUPS * STATE), F32)
    vshape = jax.ShapeDtypeStruct((1, GROUPS * gw), F32)
    return _call(
        body, name=name, grid=(GROUPS, nc),
        in_specs=[xs_s, bm_s, cm_s, xs_s, dtt_s, vec, hcol, vec, hcol, vec, st_s, xs_s, xs_s, xs_s, vec],
        out_specs=[xs_s, bc_s, bc_s, xs_s, vec, vec, vec, xs_s, vec],
        out_shape=[wide, narrow, narrow, wide, vshape, vshape, vshape, jax.ShapeDtypeStruct(proj.shape, BF16), vshape],
        scratch=[pltpu.VMEM((STATE, gw), F32)], sem=("parallel", "arbitrary"),
        args=(xbc, xbc, xbc, dtx, dtrt, biasx, biast, alogx, alogt, dskx, states, dmixed, y, proj, wn), ride=ride)


def _exchange(srcs, scatter, *, name):
    ride = _Ride(srcs, scatter)

    def body(*refs):
        src, dst, sems = refs[:ride.n], refs[ride.n:2 * ride.n], refs[2 * ride.n:]
        ride.start(src, dst, sems)
        ride.relay(src, dst, sems)
        ride.finish(src, dst, sems)

    return pl.pallas_call(body, name=name, in_specs=ride.in_specs, out_specs=ride.out_specs, out_shape=ride.out_shape,
                          scratch_shapes=ride.scratch)(*srcs)


def _adamw(parts, w, m, v, *, name, tr=256):
    lead = w.ndim == 3
    r, c = w.shape[-2:]
    tr = _tile(r, tr, 16)
    c1, c2 = 1.0 / (1.0 - ADAM_B1 ** ADAM_STEP), 1.0 / (1.0 - ADAM_B2 ** ADAM_STEP)

    def body(p_ref, w_ref, m_ref, v_ref, g_ref, d_ref, nm_ref, nv_ref):
        g = p_ref[0].astype(F32)
        for p in range(1, N_DEV):
            g = g + p_ref[p].astype(F32)
        nm = ADAM_B1 * m_ref[...] + (1.0 - ADAM_B1) * g
        nv = ADAM_B2 * v_ref[...] + (1.0 - ADAM_B2) * (g * g)
        g_ref[...] = g
        nm_ref[...] = nm
        nv_ref[...] = nv
        d_ref[...] = -ADAM_LR * ((nm * c1) / (jnp.sqrt(nv * c2) + ADAM_EPS) + ADAM_WD * w_ref[...])

    blk = pl.BlockSpec((None, tr, c), lambda i: (0, i, 0)) if lead else pl.BlockSpec((tr, c), lambda i: (i, 0))
    out = jax.ShapeDtypeStruct(w.shape, F32)
    return pl.pallas_call(
        body, name=name, grid=(r // tr,), in_specs=[pl.BlockSpec((N_DEV, tr, c), lambda i: (0, i, 0)), blk, blk, blk],
        out_specs=[blk, blk, blk, blk], out_shape=[out, out, out, out], compiler_params=_params("parallel"),
    )(parts, w, m, v)


def kernel(x, attn_norm_w, w_in, conv_w, conv_b, dt_bias, a_log, d_skip, ssd_norm_w, pool_w, pool_scale, w_out, ffn_norm_w, w_gate, w_up, w_down, final_norm_w, loss_target, m_attn_norm_w, m_w_in, m_conv_w, m_conv_b, m_dt_bias, m_a_log, m_d_skip, m_ssd_norm_w, m_pool_w, m_pool_scale, m_w_out, m_ffn_norm_w, m_w_gate, m_w_up, m_w_down, m_final_norm_w, v_attn_norm_w, v_w_in, v_conv_w, v_conv_b, v_dt_bias, v_a_log, v_d_skip, v_ssd_norm_w, v_pool_w, v_pool_scale, v_w_out, v_ffn_norm_w, v_w_gate, v_w_up, v_w_down, v_final_norm_w):
    l, d = x.shape[1], x.shape[2]
    heads = dt_bias.shape[1]
    hpg = heads // GROUPS
    d_ssm = heads * HEAD_DIM
    conv_ch = conv_b.shape[1]
    d_pool = pool_scale.shape[1]
    pg = d_pool // GROUPS
    d_mix = d_ssm + d_pool
    d_ff = w_gate.shape[2] * N_DEV
    d_in = w_in.shape[2] * N_DEV
    dt_pad = -(-heads // LANE) * LANE
    o_u, o_xbc, o_dt = d_ssm, d_ssm + d_pool, d_ssm + d_pool + conv_ch
    d_inp = o_dt + dt_pad
    tn_in = _tile(d_inp, 2560, LANE)
    g_dt, g_u = d_ssm + conv_ch, d_ssm + conv_ch + heads

    x2, tgt = x[0], loss_target[0]

    tr_ = lambda t: jnp.transpose(t, (0, 2, 1))
    h0, gi, gp, gc = _rms_fwd(x2, attn_norm_w, name="attn_norm",
                              ride=_Ride([tr_(w_in)[0].astype(BF16), pool_w[0].astype(BF16), conv_w[0]], False))
    win = gi.reshape(d_in, d)
    wp = jnp.concatenate([win[:d_ssm], win[g_u:], win[d_ssm:g_dt], win[g_dt:g_u],
                          jnp.zeros((dt_pad - heads, d), BF16)], axis=0)
    pw = jnp.transpose(gp, (1, 0, 2, 3)).reshape(GROUPS, pg, pg)
    cw = jnp.transpose(gc, (1, 0, 2)).reshape(CONV_K, conv_ch)

    def expand(vec):
        return jnp.repeat(vec, HEAD_DIM, axis=1)

    def per_head(vec):
        return vec[:, ::HEAD_DIM]

    bias_x, alog_x, dsk_x = expand(dt_bias), expand(a_log), expand(d_skip)
    bias_c, alog_c = dt_bias.reshape(GROUPS, hpg, 1), a_log.reshape(GROUPS, hpg, 1)

    proj, go, gg = _mm(h0, wp, name="in_proj", tb=True, tm=512, tn=tn_in, tk=d,
                       ride=_Ride([w_out[0].astype(BF16), tr_(w_gate)[0].astype(BF16)], False))
    wo = go.reshape(d_mix, d)
    xbc = _conv_fwd(proj, o_xbc, cw, conv_b, name="conv_fwd")
    dt_raw = proj[:, o_dt:o_dt + heads]
    dtx = _expand_heads(proj, o_dt, dt_pad, heads, name="expand_dt")
    dtrt = jnp.transpose(dt_raw.reshape(l, GROUPS, hpg), (1, 2, 0))
    y, states, mixed, gu = _scan_fwd(xbc, dtx, dtrt, bias_x, bias_c, alog_x, alog_c, dsk_x, proj, ssd_norm_w, d_mix,
                                     name="ssd_fwd", ride=_Ride([tr_(w_up)[0].astype(BF16)], False))
    pooled = _pool_fwd(proj, o_u, d_pool, name="pool_fwd")
    pool_raw, mixed = _pool_mix_fwd(pooled, pw, pool_scale, mixed, d_ssm, name="pool_mix")
    h1 = _mm(mixed, wo, name="out_proj", tm=512, tk=d_mix, residual=x2)
    h1n = _rms_fwd(h1, ffn_norm_w, name="ffn_norm")
    wide = lambda t: t.reshape(N_DEV // FFN_GROUP, -1, d)
    gg, gu = wide(gg), wide(gu)
    gate, up, act, gd = _gate_up(h1n, gg, gu, name="gate_up", ride=_Ride([w_down[0].astype(BF16)], False))
    gd = wide(gd)
    h2 = _mm_shards([(act, gd)], name="down_proj", per_step=1, residual=h1)

    loss11, dh2b, g_final = _loss_head(h2, final_norm_w.reshape(1, d), tgt, name="loss_head")
    dgate, dup = _gate_up_bwd(dh2b, gd, gate, up, name="gate_up_bwd")
    shards = lambda t: t.reshape(N_DEV, -1, d)
    g_wd = shards(_grad_shards(act, dh2b, name="grad_w_down"))
    dh1n, r_wd = _mm_shards([(dgate, gg), (dup, gu)], name="dgate_dup", out_dtype=BF16, tm=512, per_step=1,
                            ride=_Ride([g_wd], True))
    g_wg = shards(_grad_shards(dgate, h1n, name="grad_w_gate"))
    g_wu = shards(_grad_shards(dup, h1n, name="grad_w_up"))
    dh1b, g_ffn = _rms_bwd(h1, ffn_norm_w, dh1n, dh2b, BF16, name="ffn_norm_bwd")
    dmixed = _mm(dh1b, wo, name="dmixed", tb=True, tk=d)
    g_wo = _mm(mixed, dh1b, name="grad_w_out", ta=True, out_dtype=BF16, tm=1024, tn=1024, tk=2048)
    draw, dpooled, g_pscale = _pool_mix_bwd(dmixed, d_ssm, pool_raw, pool_scale, pw, name="pool_mix_bwd")
    g_pw = _group_mm_tn(pooled, draw, name="grad_pool_w")
    s_pw = jnp.transpose(g_pw.reshape(GROUPS, N_DEV, pg // N_DEV, pg), (1, 0, 2, 3)).astype(BF16)
    dxs, dbm, dcm, ddtx, g_bias, g_alog, g_dsk, dproj, g_ssdn, r_wg, r_wu, r_wo, r_pw = _scan_bwd(
        xbc, dtx, dtrt, bias_x, bias_c, alog_x, alog_c, dsk_x, states, dmixed, y, proj, ssd_norm_w, name="ssd_bwd",
        ride=_Ride([g_wg, g_wu, g_wo.reshape(N_DEV, d_mix // N_DEV, d), s_pw], True))
    dproj = _pool_bwd(dpooled, dproj, o_u, name="pool_bwd")
    n_bc = GROUPS * STATE
    segs = [(dxs, 0, d_ssm), (dbm, d_ssm, n_bc), (dcm, d_ssm + n_bc, n_bc)]
    g_cw, g_cb = [], []
    for si, (dseg, c0, width) in enumerate(segs):
        dproj, b, c_ = _conv_bwd(proj, o_xbc + c0, cw[:, c0:c0 + width], conv_b[:, c0:c0 + width], dseg, dproj,
                                 name=f"conv_bwd{si}")
        g_cw.append(b)
        g_cb.append(c_)
    dproj = _put_heads(ddtx, dproj, o_dt, dt_pad, name="put_ddt")
    g_wp = _mm(dproj, h0, name="grad_w_in", ta=True, out_dtype=BF16, tm=tn_in, tn=1024, tk=1024)
    g_win = jnp.concatenate([g_wp[:d_ssm], g_wp[o_xbc:o_dt + heads], g_wp[o_u:o_xbc]], axis=0)
    def cols8(g):
        return jnp.transpose(g.reshape(g.shape[0], N_DEV, g.shape[1] // N_DEV), (1, 0, 2))
    dh0, r_in, r_cw = _mm(dproj, wp, name="dh0", out_dtype=BF16, tk=tn_in,
                          ride=_Ride([g_win.reshape(N_DEV, d_in // N_DEV, d), cols8(jnp.concatenate(g_cw, axis=1))], True))
    dx, g_attn = _rms_bwd(x2, attn_norm_w, dh0, dh1b, F32, name="attn_norm_bwd")

    smalls = [g_attn, jnp.concatenate(g_cb, axis=1), per_head(g_bias), per_head(g_alog), per_head(g_dsk), g_ssdn,
              g_pscale, g_ffn, g_final]
    small_w = [attn_norm_w, conv_b, dt_bias, a_log, d_skip, ssd_norm_w, pool_scale, ffn_norm_w, final_norm_w.reshape(1, d)]
    small_m = [m_attn_norm_w, m_conv_b, m_dt_bias, m_a_log, m_d_skip, m_ssd_norm_w, m_pool_scale, m_ffn_norm_w,
               m_final_norm_w.reshape(1, d)]
    small_v = [v_attn_norm_w, v_conv_b, v_dt_bias, v_a_log, v_d_skip, v_ssd_norm_w, v_pool_scale, v_ffn_norm_w,
               v_final_norm_w.reshape(1, d)]
    sizes = [s.shape[1] for s in smalls]
    n_small = sum(sizes)
    n_pad = -(-n_small // (16 * LANE)) * (16 * LANE)
    rows = n_pad // LANE
    def pack(vs):
        return jnp.pad(jnp.concatenate(vs, axis=1), ((0, 0), (0, n_pad - n_small))).reshape(rows, LANE)
    (r_small,) = _exchange([pack(smalls)], False, name="gather_small_grads")

    def big(parts, w, m, v, nm):
        shp = w.shape
        if w.ndim == 3:
            return _adamw(parts, w, m, v, name=nm)
        r2 = lambda t: t.reshape(-1, shp[-1])
        outs = _adamw(parts.reshape(N_DEV, -1, shp[-1]), r2(w), r2(m), r2(v), name=nm)
        return [o.reshape(shp) for o in outs]

    def big_t(parts, w, m, v, nm):
        return [tr_(o) for o in _adamw(parts, tr_(w), tr_(m), tr_(v), name=nm)]

    def big_flat(parts, w, m, v, nm):
        fl = lambda t: tr_(t).reshape(-1, LANE)
        outs = _adamw(parts.reshape(N_DEV, -1, LANE), fl(w), fl(m), fl(v), name=nm, tr=2048)
        return [tr_(o.reshape(1, w.shape[2], w.shape[1])) for o in outs]

    res = {
        "w_in": big_flat(r_in, w_in, m_w_in, v_w_in, "adamw_w_in"),
        "conv_w": big(r_cw, conv_w, m_conv_w, v_conv_w, "adamw_conv_w"),
        "pool_w": big(r_pw, pool_w, m_pool_w, v_pool_w, "adamw_pool_w"),
        "w_out": big(r_wo, w_out, m_w_out, v_w_out, "adamw_w_out"),
        "w_gate": big_t(r_wg, w_gate, m_w_gate, v_w_gate, "adamw_w_gate"),
        "w_up": big_t(r_wu, w_up, m_w_up, v_w_up, "adamw_w_up"),
        "w_down": big(r_wd, w_down, m_w_down, v_w_down, "adamw_w_down"),
    }
    s_out = _adamw(r_small, pack(small_w), pack(small_m), pack(small_v), name="adamw_small")
    names = ["attn_norm_w", "conv_b", "dt_bias", "a_log", "d_skip", "ssd_norm_w", "pool_scale", "ffn_norm_w", "final_norm_w"]
    offs = [sum(sizes[:i]) for i in range(len(sizes))]
    for i, nm in enumerate(names):
        shp = (d,) if nm == "final_norm_w" else (1, sizes[i])
        res[nm] = [o.reshape(1, n_pad)[:, offs[i]:offs[i] + sizes[i]].reshape(shp) for o in s_out]

    loss = lax.psum(loss11[0, 0], ("x", "y", "c"))
    order = ["attn_norm_w", "w_in", "conv_w", "conv_b", "dt_bias", "a_log", "d_skip", "ssd_norm_w", "pool_w", "pool_scale",
             "w_out", "ffn_norm_w", "w_gate", "w_up", "w_down", "final_norm_w"]
    outs = [loss, dx.reshape(x.shape)]
    for part in range(4):
        outs += [res[nm][part] for nm in order]
    return tuple(outs)
```

```python
import functools
import math

import jax
import jax.numpy as jnp
from jax import lax
from jax.experimental import pallas as pl
from jax.experimental.pallas import tpu as pltpu

F32 = jnp.float32
BF16 = jnp.bfloat16

NORM_EPS = 1e-5
HEAD_DIM = 64
STATE = 128
CHUNK = 256
GROUPS = 4
CONV_K = 4
POOL_WINDOWS = (2, 4, 8, 16)
POOL_HALO = 16
CONV_HALO = 8
LANE = 128
N_DEV = 8
ADAM_LR, ADAM_B1, ADAM_B2, ADAM_EPS, ADAM_WD, ADAM_STEP = 0.001, 0.9, 0.999, 1e-08, 0.01, 10
NEG_BIG = -1e30
FFN_GROUP = 4
MESH_ID = pl.DeviceIdType.MESH


def _tile(n, pref, mult):
    if n <= pref:
        return n
    t = (pref // mult) * mult
    while t >= mult:
        if n % t == 0:
            return t
        t -= mult
    return n


def _params(*sem):
    return pltpu.CompilerParams(dimension_semantics=sem)


ANY = pl.BlockSpec(memory_space=pl.ANY)


def _peer(k):
    x, y, c = lax.axis_index("x"), lax.axis_index("y"), lax.axis_index("c")
    return (1 - x if k & 4 else x, 1 - y if k & 2 else y, 1 - c if k & 1 else c)


def _my_index():
    return 4 * lax.axis_index("x") + 2 * lax.axis_index("y") + lax.axis_index("c")


class _Ride:
    def __init__(self, srcs, scatter):
        self.srcs, self.scatter, self.n = list(srcs), scatter, len(srcs)
        self.two_level = not scatter
        self.in_specs = [ANY] * self.n
        self.out_specs = [ANY] * self.n
        self.out_shape = [jax.ShapeDtypeStruct((N_DEV,) + tuple(s.shape[1:] if scatter else s.shape), s.dtype)
                          for s in self.srcs]
        self.scratch = [pltpu.SemaphoreType.DMA((self.n, N_DEV - 1)), pltpu.SemaphoreType.DMA((self.n, N_DEV - 1)),
                        pltpu.SemaphoreType.DMA((self.n,))]

    def _copies(self, src, dst, sems):
        send_sems, recv_sems, local_sems = sems
        me = _my_index()
        local = [pltpu.make_async_copy(src[t].at[me] if self.scatter else src[t], dst[t].at[me], local_sems.at[t])
                 for t in range(self.n)]
        remote = {}
        for k in range(1, N_DEV):
            peer, pidx = _peer(k), me ^ k
            remote[k] = [pltpu.make_async_remote_copy(
                src_ref=src[t].at[pidx] if self.scatter else src[t], dst_ref=dst[t].at[me],
                send_sem=send_sems.at[t, k - 1], recv_sem=recv_sems.at[t, k - 1],
                device_id=peer, device_id_type=MESH_ID) for t in range(self.n)]
        return local, remote

    def _forwards(self, dst, sems):
        send_sems, recv_sems, _ = sems
        me = _my_index()
        return {k: [pltpu.make_async_remote_copy(
            src_ref=dst[t].at[me ^ k], dst_ref=dst[t].at[me ^ k],
            send_sem=send_sems.at[t, (k ^ 1) - 1], recv_sem=recv_sems.at[t, (k ^ 1) - 1],
            device_id=_peer(1), device_id_type=MESH_ID) for t in range(self.n)] for k in (2, 4, 6)}

    def start(self, src, dst, sems):
        local, remote = self._copies(src, dst, sems)
        for cp in local:
            cp.start()
        for k in (2, 4, 6, 1) if self.two_level else range(1, N_DEV):
            for cp in remote[k]:
                cp.start()

    def relay(self, src, dst, sems):
        if not self.two_level:
            return
        _, remote = self._copies(src, dst, sems)
        fwd = self._forwards(dst, sems)
        for k in (2, 4, 6):
            for t in range(self.n):
                remote[k][t].wait_recv()
                fwd[k][t].start()

    def finish(self, src, dst, sems):
        local, remote = self._copies(src, dst, sems)
        if self.two_level:
            fwd = self._forwards(dst, sems)
            for k in (1, 3, 5, 7):
                for cp in remote[k]:
                    cp.wait_recv()
            for k in (1, 2, 4, 6):
                for cp in remote[k]:
                    cp.wait_send()
            for k in (2, 4, 6):
                for cp in fwd[k]:
                    cp.wait_send()
        else:
            for k in range(1, N_DEV):
                for cp in remote[k]:
                    cp.wait_send()
            for k in range(1, N_DEV):
                for cp in remote[k]:
                    cp.wait_recv()
        for lc in local:
            lc.wait()


RELAY_AT = 0.75


def _call(body, *, name, grid, in_specs, out_specs, out_shape, args, sem, scratch=(), ride=None):
    in_specs, out_specs, out_shape, scratch = list(in_specs), list(out_specs), list(out_shape), list(scratch)
    if ride is None:
        return pl.pallas_call(body, name=name, grid=grid, in_specs=in_specs, out_specs=out_specs, out_shape=out_shape,
                              scratch_shapes=scratch, compiler_params=_params(*sem))(*args)
    n_in, n_out, n_sc, nr = len(in_specs), len(out_specs), len(scratch), ride.n

    def full(*refs):
        ins, csrc = refs[:n_in], refs[n_in:n_in + nr]
        o0 = n_in + nr
        outs, cdst = refs[o0:o0 + n_out], refs[o0 + n_out:o0 + n_out + nr]
        s0 = o0 + n_out + nr
        sc, csem = refs[s0:s0 + n_sc], refs[s0 + n_sc:]
        ids = [pl.program_id(ax) for ax in range(len(grid))]
        step = functools.reduce(lambda acc, ig: acc * ig[1] + ig[0], zip(ids, grid), 0)
        total = functools.reduce(lambda p, g: p * g, grid, 1)

        @pl.when(step == 0)
        def _():
            ride.start(csrc, cdst, csem)

        body(*ins, *outs, *sc)

        @pl.when(step == min(int(total * RELAY_AT), total - 1))
        def _():
            ride.relay(csrc, cdst, csem)

        @pl.when(step == total - 1)
        def _():
            ride.finish(csrc, cdst, csem)

    return pl.pallas_call(
        full, name=name, grid=grid, in_specs=in_specs + ride.in_specs, out_specs=out_specs + ride.out_specs,
        out_shape=out_shape + ride.out_shape, scratch_shapes=scratch + ride.scratch,
        compiler_params=_params(*(["arbitrary"] * len(grid))),
    )(*args, *ride.srcs)


def _sigmoid(x):
    return 1.0 / (1.0 + jnp.exp(-x))


def _softplus(x):
    return jnp.maximum(x, 0.0) + jnp.log(1.0 + jnp.exp(-jnp.abs(x)))


def _dot(a, b, ta=False, tb=False):
    dn = (((0 if ta else 1,), (1 if tb else 0,)), ((), ()))
    return lax.dot_general(a.astype(BF16), b.astype(BF16), dn, preferred_element_type=F32)


def _mm(a, b, *, name, ta=False, tb=False, out_dtype=F32, tm=1024, tn=1024, tk=2048, residual=None, ride=None):
    m = a.shape[1] if ta else a.shape[0]
    k = a.shape[0] if ta else a.shape[1]
    n = b.shape[0] if tb else b.shape[1]
    assert k == (b.shape[1] if tb else b.shape[0])
    tm, tn = _tile(m, tm, LANE if ta else 8), _tile(n, tn, LANE)
    tk = _tile(k, tk, LANE if not (ta and tb) else LANE)
    nk = k // tk
    has_res = residual is not None

    def body(*refs):
        a_ref, b_ref = refs[0], refs[1]
        r_ref = refs[2] if has_res else None
        o_ref = refs[3] if has_res else refs[2]
        part = _dot(a_ref[...], b_ref[...], ta, tb)

        def finish(acc):
            if has_res:
                acc = acc + r_ref[...]
            o_ref[...] = acc.astype(o_ref.dtype)

        if nk == 1:
            finish(part)
        else:
            acc_ref = refs[-1]
            kk = pl.program_id(2)

            @pl.when(kk == 0)
            def _():
                acc_ref[...] = part

            @pl.when(kk > 0)
            def _():
                acc_ref[...] += part

            @pl.when(kk == nk - 1)
            def _():
                finish(acc_ref[...])

    a_spec = pl.BlockSpec((tk, tm), lambda i, j, kk: (kk, i)) if ta else pl.BlockSpec((tm, tk), lambda i, j, kk: (i, kk))
    b_spec = pl.BlockSpec((tn, tk), lambda i, j, kk: (j, kk)) if tb else pl.BlockSpec((tk, tn), lambda i, j, kk: (kk, j))
    o_spec = pl.BlockSpec((tm, tn), lambda i, j, kk: (i, j))
    in_specs = [a_spec, b_spec] + ([o_spec] if has_res else [])
    args = (a, b) + ((residual,) if has_res else ())
    res = _call(body, name=name, grid=(m // tm, n // tn, nk), in_specs=in_specs, out_specs=[o_spec],
                out_shape=[jax.ShapeDtypeStruct((m, n), out_dtype)], args=args,
                scratch=[pltpu.VMEM((tm, tn), F32)] if nk > 1 else [], sem=("parallel", "parallel", "arbitrary"), ride=ride)
    return res[0] if ride is None else res


def _gate_up(h, wg, wu, *, name, tm=256, ride=None):
    m, k = h.shape
    ns, c, _ = wg.shape
    tm = _tile(m, tm, 16)

    def body(h_ref, wg_ref, wu_ref, g_ref, u_ref, a_ref):
        hb = h_ref[...]
        g = _dot(hb, wg_ref[...], tb=True)
        u = _dot(hb, wu_ref[...], tb=True)
        g_ref[...] = g
        u_ref[...] = u
        a_ref[...] = (g * _sigmoid(g) * u).astype(a_ref.dtype)

    o_spec = pl.BlockSpec((None, tm, c), lambda s, i: (s, i, 0))
    w_spec = pl.BlockSpec((None, c, k), lambda s, i: (s, 0, 0), pipeline_mode=pl.Buffered(1))
    return _call(body, name=name, grid=(ns, m // tm),
                 in_specs=[pl.BlockSpec((tm, k), lambda s, i: (i, 0)), w_spec, w_spec], out_specs=[o_spec, o_spec, o_spec],
                 out_shape=[jax.ShapeDtypeStruct((ns, m, c), F32), jax.ShapeDtypeStruct((ns, m, c), F32),
                            jax.ShapeDtypeStruct((ns, m, c), BF16)],
                 args=(h, wg, wu), sem=("parallel", "parallel"), ride=ride)


def _gate_up_bwd(dh, wd, g, u, *, name, tm=256):
    m, k = dh.shape
    ns, c, _ = wd.shape
    tm = _tile(m, tm, 16)

    def body(dh_ref, wd_ref, g_ref, u_ref, dg_ref, du_ref):
        da = _dot(dh_ref[...], wd_ref[...], tb=True)
        g = g_ref[...]
        s = _sigmoid(g)
        dg_ref[...] = (da * u_ref[...] * (s * (1.0 + g * (1.0 - s)))).astype(dg_ref.dtype)
        du_ref[...] = (da * (g * s)).astype(du_ref.dtype)

    o_spec = pl.BlockSpec((None, tm, c), lambda s, i: (s, i, 0))
    w_spec = pl.BlockSpec((None, c, k), lambda s, i: (s, 0, 0), pipeline_mode=pl.Buffered(1))
    return _call(body, name=name, grid=(ns, m // tm),
                 in_specs=[pl.BlockSpec((tm, k), lambda s, i: (i, 0)), w_spec, o_spec, o_spec],
                 out_specs=[o_spec, o_spec],
                 out_shape=[jax.ShapeDtypeStruct((ns, m, c), BF16), jax.ShapeDtypeStruct((ns, m, c), BF16)],
                 args=(dh, wd, g, u), sem=("parallel", "parallel"))


def _mm_shards(pairs, *, name, tb=False, residual=None, out_dtype=F32, tm=1024, tn=1024, per_step=2, ride=None):
    ns, m, c = pairs[0][0].shape
    n = pairs[0][1].shape[1] if tb else pairs[0][1].shape[2]
    tm, tn = _tile(m, tm, 16), _tile(n, tn, LANE)
    npair = len(pairs)
    has_res = residual is not None
    assert ns % per_step == 0
    nsteps = ns // per_step

    def body(*refs):
        o_ref, acc_ref = refs[-2], refs[-1]
        s = pl.program_id(2)
        part = None
        for p in range(npair):
            for q in range(per_step):
                term = _dot(refs[2 * p][q], refs[2 * p + 1][q], tb=tb)
                part = term if part is None else part + term

        @pl.when(s == 0)
        def _():
            acc_ref[...] = part

        @pl.when(s > 0)
        def _():
            acc_ref[...] += part

        @pl.when(s == nsteps - 1)
        def _():
            acc = acc_ref[...]
            if has_res:
                acc = acc + refs[2 * npair][...]
            o_ref[...] = acc.astype(o_ref.dtype)

    a_spec = pl.BlockSpec((per_step, tm, c), lambda i, j, s: (s, i, 0))
    b_spec = (pl.BlockSpec((per_step, tn, c), lambda i, j, s: (s, j, 0)) if tb
              else pl.BlockSpec((per_step, c, tn), lambda i, j, s: (s, 0, j)))
    o_spec = pl.BlockSpec((tm, tn), lambda i, j, s: (i, j))
    args = [t for pr in pairs for t in pr] + ([residual] if has_res else [])
    res = _call(body, name=name, grid=(m // tm, n // tn, nsteps),
                in_specs=[a_spec, b_spec] * npair + ([o_spec] if has_res else []), out_specs=[o_spec],
                out_shape=[jax.ShapeDtypeStruct((m, n), out_dtype)], args=args, scratch=[pltpu.VMEM((tm, tn), F32)],
                sem=("parallel", "parallel", "arbitrary"), ride=ride)
    return res[0] if ride is None else res


def _grad_shards(a, b, *, name, tk=2048, tc=1408, tn=1024):
    ns, l, c = a.shape
    d = b.shape[1]
    tk, tc, tn = _tile(l, tk, 16), _tile(c, tc, LANE), _tile(d, tn, LANE)
    nk = l // tk

    def body(a_ref, b_ref, o_ref, acc_ref):
        kk = pl.program_id(3)
        part = _dot(a_ref[...], b_ref[...], ta=True)

        @pl.when(kk == 0)
        def _():
            acc_ref[...] = part

        @pl.when(kk > 0)
        def _():
            acc_ref[...] += part

        @pl.when(kk == nk - 1)
        def _():
            o_ref[...] = acc_ref[...].astype(o_ref.dtype)

    return pl.pallas_call(
        body, name=name, grid=(ns, c // tc, d // tn, nk),
        in_specs=[pl.BlockSpec((None, tk, tc), lambda s, i, j, kk: (s, kk, i)),
                  pl.BlockSpec((tk, tn), lambda s, i, j, kk: (kk, j))],
        out_specs=pl.BlockSpec((None, tc, tn), lambda s, i, j, kk: (s, i, j)),
        out_shape=jax.ShapeDtypeStruct((ns, c, d), BF16), scratch_shapes=[pltpu.VMEM((tc, tn), F32)],
        compiler_params=_params("parallel", "parallel", "parallel", "arbitrary"),
    )(a, b)


def _rms_fwd(x, w, *, name, tl=512, ride=None):
    l, d = x.shape
    tl = _tile(l, tl, 16)

    def body(x_ref, w_ref, o_ref):
        xv = x_ref[...]
        r = lax.rsqrt(jnp.mean(xv * xv, axis=-1, keepdims=True) + NORM_EPS)
        o_ref[...] = (xv * r * w_ref[...]).astype(o_ref.dtype)

    row = pl.BlockSpec((tl, d), lambda i: (i, 0))
    res = _call(body, name=name, grid=(l // tl,), in_specs=[row, pl.BlockSpec((1, d), lambda i: (0, 0))],
                out_specs=[row], out_shape=[jax.ShapeDtypeStruct((l, d), BF16)], args=(x, w), sem=("parallel",), ride=ride)
    return res[0] if ride is None else res


def _rms_bwd(x, w, dy, add, out_dtype, *, name, tl=256):
    l, d = x.shape
    tl = _tile(l, tl, 16)

    def body(x_ref, w_ref, dy_ref, add_ref, dx_ref, dw_ref):
        i = pl.program_id(0)
        xv = x_ref[...]
        r = lax.rsqrt(jnp.mean(xv * xv, axis=-1, keepdims=True) + NORM_EPS)
        xh = xv * r
        dyv = dy_ref[...].astype(F32)
        dxh = dyv * w_ref[...]
        dx = r * (dxh - xh * jnp.mean(dxh * xh, axis=-1, keepdims=True)) + add_ref[...].astype(F32)
        dx_ref[...] = dx.astype(dx_ref.dtype)
        part = jnp.sum(dyv * xh, axis=0, keepdims=True)

        @pl.when(i == 0)
        def _():
            dw_ref[...] = part

        @pl.when(i > 0)
        def _():
            dw_ref[...] += part

    row = pl.BlockSpec((tl, d), lambda i: (i, 0))
    vec = pl.BlockSpec((1, d), lambda i: (0, 0))
    return pl.pallas_call(
        body, name=name, grid=(l // tl,), in_specs=[row, vec, row, row], out_specs=[row, vec],
        out_shape=[jax.ShapeDtypeStruct((l, d), out_dtype), jax.ShapeDtypeStruct((1, d), F32)],
        compiler_params=_params("arbitrary"),
    )(x, w, dy, add)


def _loss_head(h, w, target, *, name, tl=256):
    l, d = h.shape
    tl = _tile(l, tl, 16)

    def body(x_ref, w_ref, t_ref, loss_ref, dxb_ref, dw_ref):
        i = pl.program_id(0)
        xv = x_ref[...]
        r = lax.rsqrt(jnp.mean(xv * xv, axis=-1, keepdims=True) + NORM_EPS)
        xh = xv * r
        wv = w_ref[...]
        diff = xh * wv - t_ref[...]
        lpart = 0.5 * jnp.sum(jnp.mean(diff * diff, axis=-1, keepdims=True), axis=0, keepdims=True)
        dyv = diff * (1.0 / d)
        dxh = dyv * wv
        dx = r * (dxh - xh * jnp.mean(dxh * xh, axis=-1, keepdims=True))
        dxb_ref[...] = dx.astype(BF16)
        part = jnp.sum(dyv * xh, axis=0, keepdims=True)

        @pl.when(i == 0)
        def _():
            dw_ref[...] = part
            loss_ref[...] = lpart

        @pl.when(i > 0)
        def _():
            dw_ref[...] += part
            loss_ref[...] += lpart

    row = pl.BlockSpec((tl, d), lambda i: (i, 0))
    vec = pl.BlockSpec((1, d), lambda i: (0, 0))
    one = pl.BlockSpec((1, 1), lambda i: (0, 0))
    return pl.pallas_call(
        body, name=name, grid=(l // tl,), in_specs=[row, vec, row], out_specs=[one, row, vec],
        out_shape=[jax.ShapeDtypeStruct((1, 1), F32), jax.ShapeDtypeStruct((l, d), BF16), jax.ShapeDtypeStruct((1, d), F32)],
        compiler_params=_params("arbitrary"),
    )(h, w, target)


def _conv_pre(ext, w_ref):
    acc = ext * w_ref[CONV_K - 1:CONV_K, :]
    for k in range(CONV_K - 1):
        acc = acc + pltpu.roll(ext, CONV_K - 1 - k, axis=0) * w_ref[k:k + 1, :]
    return acc


def _conv_fwd(proj, col0, cw, cb, *, name, tl=512, tc=1024):
    l = proj.shape[0]
    c = cw.shape[1]
    tl, tc = _tile(l, tl, 16), _tile(math.gcd(c, col0), tc, LANE)
    assert col0 % tc == 0
    off, hb = col0 // tc, tl // CONV_HALO

    def body(x_ref, halo_ref, w_ref, b_ref, o_ref):
        i = pl.program_id(0)
        halo = jnp.where(i > 0, halo_ref[...], 0.0)
        ext = jnp.concatenate([halo, x_ref[...]], axis=0)
        pre = _conv_pre(ext, w_ref)[CONV_HALO:] + b_ref[...]
        o_ref[...] = pre * _sigmoid(pre)

    return pl.pallas_call(
        body, name=name, grid=(l // tl, c // tc),
        in_specs=[pl.BlockSpec((tl, tc), lambda i, j: (i, off + j)),
                  pl.BlockSpec((CONV_HALO, tc), lambda i, j: (jnp.maximum(i * hb - 1, 0), off + j)),
                  pl.BlockSpec((CONV_K, tc), lambda i, j: (0, j)), pl.BlockSpec((1, tc), lambda i, j: (0, j))],
        out_specs=pl.BlockSpec((tl, tc), lambda i, j: (i, j)),
        out_shape=jax.ShapeDtypeStruct((l, c), F32), compiler_params=_params("parallel", "parallel"),
    )(proj, proj, cw, cb)


def _conv_bwd(proj, col0, cw, cb, dout, dproj, *, name, tl=512, tc=1024):
    l = proj.shape[0]
    c = dout.shape[1]
    tl, tc = _tile(l, tl, 16), _tile(math.gcd(c, col0), tc, LANE)
    assert col0 % tc == 0
    off, hb, nt = col0 // tc, tl // CONV_HALO, l // tl
    n_ext = tl + CONV_HALO

    def body(x_ref, prev_ref, nxt_ref, d_ref, dn_ref, w_ref, b_ref, dproj_in, dx_ref, dw_ref, db_ref):
        del dproj_in
        i = pl.program_id(1)
        last = i == nt - 1
        halo = jnp.where(i > 0, prev_ref[...], 0.0)
        ext = jnp.concatenate([halo, x_ref[...], nxt_ref[...]], axis=0)
        pre = _conv_pre(ext, w_ref)[CONV_HALO:] + b_ref[...]
        dext = jnp.concatenate([d_ref[...], jnp.where(last, 0.0, dn_ref[...])], axis=0)
        s = _sigmoid(pre)
        dpre = dext * (s * (1.0 + pre * (1.0 - s)))
        dx = dpre * w_ref[CONV_K - 1:CONV_K, :]
        for k in range(CONV_K - 1):
            dx = dx + pltpu.roll(dpre, n_ext - (CONV_K - 1 - k), axis=0) * w_ref[k:k + 1, :]
        dx_ref[...] = dx[:tl].astype(dx_ref.dtype)
        dp = dpre[:tl]
        rows = [jnp.sum(dp * pltpu.roll(ext, CONV_K - 1 - k, axis=0)[CONV_HALO:CONV_HALO + tl], axis=0, keepdims=True)
                for k in range(CONV_K - 1)]
        rows.append(jnp.sum(dp * ext[CONV_HALO:CONV_HALO + tl], axis=0, keepdims=True))
        dwp = jnp.concatenate(rows, axis=0)
        dbp = jnp.sum(dp, axis=0, keepdims=True)

        @pl.when(i == 0)
        def _():
            dw_ref[...] = dwp
            db_ref[...] = dbp

        @pl.when(i > 0)
        def _():
            dw_ref[...] += dwp
            db_ref[...] += dbp

    cur = lambda j, i: (i, j)
    nxt = lambda j, i: (jnp.minimum((i + 1) * hb, nt * hb - 1), j)
    return pl.pallas_call(
        body, name=name, grid=(c // tc, nt),
        in_specs=[pl.BlockSpec((tl, tc), lambda j, i: (i, off + j)),
                  pl.BlockSpec((CONV_HALO, tc), lambda j, i: (jnp.maximum(i * hb - 1, 0), off + j)),
                  pl.BlockSpec((CONV_HALO, tc), lambda j, i: (jnp.minimum((i + 1) * hb, nt * hb - 1), off + j)),
                  pl.BlockSpec((tl, tc), cur), pl.BlockSpec((CONV_HALO, tc), nxt),
                  pl.BlockSpec((CONV_K, tc), lambda j, i: (0, j)), pl.BlockSpec((1, tc), lambda j, i: (0, j)), ANY],
        out_specs=[pl.BlockSpec((tl, tc), lambda j, i: (i, off + j)), pl.BlockSpec((CONV_K, tc), lambda j, i: (0, j)),
                   pl.BlockSpec((1, tc), lambda j, i: (0, j))],
        out_shape=[jax.ShapeDtypeStruct(dproj.shape, dproj.dtype), jax.ShapeDtypeStruct((CONV_K, c), F32),
                   jax.ShapeDtypeStruct((1, c), F32)],
        input_output_aliases={7: 0}, compiler_params=_params("parallel", "arbitrary"),
    )(proj, proj, proj, dout, dout, cw, cb, dproj)


def _pool_fwd(proj, col0, d_pool, *, name, tl=256):
    l = proj.shape[0]
    tl = _tile(l, tl, POOL_HALO)
    assert col0 % d_pool == 0
    off, hb, pg = col0 // d_pool, tl // POOL_HALO, d_pool // GROUPS

    def body(u_ref, halo_ref, o_ref):
        i = pl.program_id(0)
        halo = jnp.where(i > 0, halo_ref[...], 0.0)
        ext = jnp.concatenate([halo, u_ref[...]], axis=0)
        t = i * tl + lax.broadcasted_iota(jnp.int32, (tl, 1), 0)
        s, width = ext, 1
        for gi, win in enumerate(POOL_WINDOWS):
            while width < win:
                s = s + pltpu.roll(s, width, axis=0)
                width *= 2
            cnt = jnp.minimum(t + 1, win).astype(F32)
            sl = slice(gi * pg, (gi + 1) * pg)
            o_ref[:, sl] = (s[POOL_HALO:, sl] / cnt - ext[POOL_HALO:, sl]).astype(o_ref.dtype)

    return pl.pallas_call(
        body, name=name, grid=(l // tl,),
        in_specs=[pl.BlockSpec((tl, d_pool), lambda i: (i, off)),
                  pl.BlockSpec((POOL_HALO, d_pool), lambda i: (jnp.maximum(i * hb - 1, 0), off))],
        out_specs=pl.BlockSpec((tl, d_pool), lambda i: (i, 0)),
        out_shape=jax.ShapeDtypeStruct((l, d_pool), BF16), compiler_params=_params("parallel"),
    )(proj, proj)


def _pool_bwd(dp, dproj, col0, *, name, tl=256):
    l, d_pool = dp.shape
    tl = _tile(l, tl, POOL_HALO)
    assert col0 % d_pool == 0
    off, hb, nt, pg = col0 // d_pool, tl // POOL_HALO, l // tl, d_pool // GROUPS
    n_ext = tl + POOL_HALO

    def body(d_ref, nxt_ref, dproj_in, o_ref):
        del dproj_in
        i = pl.program_id(0)
        ext = jnp.concatenate([d_ref[...], jnp.where(i == nt - 1, 0.0, nxt_ref[...])], axis=0)
        t = i * tl + lax.broadcasted_iota(jnp.int32, (n_ext, 1), 0)
        for gi, win in enumerate(POOL_WINDOWS):
            sl = slice(gi * pg, (gi + 1) * pg)
            dg = ext[:, sl]
            s = dg / jnp.minimum(t + 1, win).astype(F32)
            width = 1
            while width < win:
                s = s + pltpu.roll(s, n_ext - width, axis=0)
                width *= 2
            o_ref[:, sl] = (s[:tl] - dg[:tl]).astype(o_ref.dtype)

    return pl.pallas_call(
        body, name=name, grid=(nt,),
        in_specs=[pl.BlockSpec((tl, d_pool), lambda i: (i, 0)),
                  pl.BlockSpec((POOL_HALO, d_pool), lambda i: (jnp.minimum((i + 1) * hb, nt * hb - 1), 0)), ANY],
        out_specs=pl.BlockSpec((tl, d_pool), lambda i: (i, off)),
        out_shape=jax.ShapeDtypeStruct(dproj.shape, dproj.dtype), input_output_aliases={2: 0},
        compiler_params=_params("parallel"),
    )(dp, dp, dproj)


def _put_heads(src, dst, col0, width, *, name, tl=512):
    l, c = src.shape
    tl = _tile(l, tl, 16)
    assert col0 % width == 0 and width % LANE == 0 and c // HEAD_DIM <= width

    def body(s_ref, dst_in, o_ref):
        del dst_in
        pick = (lax.broadcasted_iota(jnp.int32, (c, width), 0)
                == HEAD_DIM * lax.broadcasted_iota(jnp.int32, (c, width), 1)).astype(BF16)
        hi, lo = _split2(s_ref[...])
        o_ref[...] = (jnp.dot(hi, pick, preferred_element_type=F32)
                      + jnp.dot(lo, pick, preferred_element_type=F32)).astype(o_ref.dtype)

    return pl.pallas_call(
        body, name=name, grid=(l // tl,), in_specs=[pl.BlockSpec((tl, c), lambda i: (i, 0)), ANY],
        out_specs=pl.BlockSpec((tl, width), lambda i: (i, col0 // width)),
        out_shape=jax.ShapeDtypeStruct(dst.shape, dst.dtype), input_output_aliases={1: 0},
        compiler_params=_params("parallel"),
    )(src, dst)


def _pool_mix_fwd(pooled, pw, scale, mixed, col0, *, name, tm=1024):
    l, d_pool = pooled.shape
    pg = d_pool // GROUPS
    tm = _tile(l, tm, 16)
    assert col0 % pg == 0
    off = col0 // pg

    def body(a_ref, w_ref, s_ref, mixed_in, raw_ref, mixed_ref):
        del mixed_in
        raw = _dot(a_ref[...], w_ref[...])
        raw_ref[...] = raw
        mixed_ref[...] = (raw * s_ref[...]).astype(mixed_ref.dtype)

    blk = pl.BlockSpec((tm, pg), lambda i, g: (i, g))
    return pl.pallas_call(
        body, name=name, grid=(l // tm, GROUPS),
        in_specs=[blk, pl.BlockSpec((None, pg, pg), lambda i, g: (g, 0, 0)), pl.BlockSpec((1, pg), lambda i, g: (0, g)),
                  pl.BlockSpec(memory_space=pl.ANY)],
        out_specs=[blk, pl.BlockSpec((tm, pg), lambda i, g: (i, off + g))],
        out_shape=[jax.ShapeDtypeStruct((l, d_pool), F32), jax.ShapeDtypeStruct(mixed.shape, mixed.dtype)],
        input_output_aliases={3: 1}, compiler_params=_params("parallel", "parallel"),
    )(pooled, pw, scale, mixed)


def _pool_mix_bwd(dmixed, col0, raw, scale, pw, *, name, tm=1024):
    l, d_pool = raw.shape
    pg = d_pool // GROUPS
    tm = _tile(l, tm, 16)
    assert col0 % pg == 0
    off = col0 // pg

    def body(d_ref, raw_ref, s_ref, w_ref, draw_ref, dp_ref, ds_ref):
        i = pl.program_id(1)
        dv = d_ref[...]
        draw = (dv * s_ref[...]).astype(BF16)
        draw_ref[...] = draw
        dp_ref[...] = _dot(draw, w_ref[...], tb=True)
        part = jnp.sum(dv * raw_ref[...], axis=0, keepdims=True)

        @pl.when(i == 0)
        def _():
            ds_ref[...] = part

        @pl.when(i > 0)
        def _():
            ds_ref[...] += part

    blk = pl.BlockSpec((tm, pg), lambda g, i: (i, g))
    vec = pl.BlockSpec((1, pg), lambda g, i: (0, g))
    return pl.pallas_call(
        body, name=name, grid=(GROUPS, l // tm),
        in_specs=[pl.BlockSpec((tm, pg), lambda g, i: (i, off + g)), blk, vec,
                  pl.BlockSpec((None, pg, pg), lambda g, i: (g, 0, 0))],
        out_specs=[blk, blk, vec],
        out_shape=[jax.ShapeDtypeStruct((l, d_pool), BF16), jax.ShapeDtypeStruct((l, d_pool), F32),
                   jax.ShapeDtypeStruct((1, d_pool), F32)],
        compiler_params=_params("parallel", "arbitrary"),
    )(dmixed, raw, scale, pw)


def _group_mm_tn(a, b, *, name, tk=1024):
    l, d_pool = a.shape
    pg = d_pool // GROUPS
    tk = _tile(l, tk, 16)

    def body(a_ref, b_ref, o_ref):
        kk = pl.program_id(1)
        part = _dot(a_ref[...], b_ref[...], ta=True)

        @pl.when(kk == 0)
        def _():
            o_ref[...] = part

        @pl.when(kk > 0)
        def _():
            o_ref[...] += part

    blk = pl.BlockSpec((tk, pg), lambda g, kk: (kk, g))
    return pl.pallas_call(
        body, name=name, grid=(GROUPS, l // tk), in_specs=[blk, blk],
        out_specs=pl.BlockSpec((None, pg, pg), lambda g, kk: (g, 0, 0)),
        out_shape=jax.ShapeDtypeStruct((GROUPS, pg, pg), F32), compiler_params=_params("parallel", "arbitrary"),
    )(a, b)


def _split3(v):
    hi = v.astype(BF16)
    r1 = v - hi.astype(F32)
    mid = r1.astype(BF16)
    lo = (r1 - mid.astype(F32)).astype(BF16)
    return hi, mid, lo


def _tri_dot(tri, v):
    hi, mid, lo = _split3(v)
    d = lambda p: jnp.dot(tri, p, preferred_element_type=F32)
    return d(hi) + d(mid) + d(lo)


def _dot_tri(v, tri):
    hi, mid, lo = _split3(v)
    d = lambda p: jnp.dot(p, tri, preferred_element_type=F32)
    return d(hi) + d(mid) + d(lo)


def _split2(v):
    hi = v.astype(BF16)
    return hi, (v - hi.astype(F32)).astype(BF16)


def _scan_specs(hpg, nc, order):
    gw, q = hpg * HEAD_DIM, CHUNK
    b_off, c_off = (GROUPS * gw) // STATE, (GROUPS * gw) // STATE + GROUPS
    xs = pl.BlockSpec((q, gw), lambda g, c: (order(c), g))
    bm = pl.BlockSpec((q, STATE), lambda g, c: (order(c), b_off + g))
    cm = pl.BlockSpec((q, STATE), lambda g, c: (order(c), c_off + g))
    dtt = pl.BlockSpec((None, hpg, q), lambda g, c: (g, 0, order(c)))
    vec = pl.BlockSpec((1, gw), lambda g, c: (0, g))
    hcol = pl.BlockSpec((None, hpg, 1), lambda g, c: (g, 0, 0))
    st = pl.BlockSpec((None, None, STATE, gw), lambda g, c: (g, order(c), 0, 0))
    return xs, bm, cm, dtt, vec, hcol, st


def _scan_common(dtc_ref, dtrt_ref, biasc_ref, biast_ref, alogx_ref, alogt_ref, hpg, want_sigmoid=False):
    q, gw, width = CHUNK, hpg * HEAD_DIM, dtc_ref.shape[1]
    raw = dtc_ref[...] + biasc_ref[...]
    spread = (lax.broadcasted_iota(jnp.int32, (width, gw), 0)
              == hpg * pl.program_id(0) + lax.broadcasted_iota(jnp.int32, (width, gw), 1) // HEAD_DIM).astype(BF16)
    dt = _dot_tri(_softplus(raw), spread)
    sg = None
    if want_sigmoid:
        hi, lo = _split2(_sigmoid(raw))
        sg = jnp.dot(hi, spread, preferred_element_type=F32) + jnp.dot(lo, spread, preferred_element_type=F32)
    a = -jnp.exp(alogx_ref[...])
    dtt = _softplus(dtrt_ref[...] + biast_ref[...])
    at = -jnp.exp(alogt_ref[...])
    row = lax.broadcasted_iota(jnp.int32, (q, q), 0)
    col = lax.broadcasted_iota(jnp.int32, (q, q), 1)
    lower = (row >= col).astype(BF16)
    upper = (row <= col).astype(BF16)
    acum = _tri_dot(lower, dt * a)
    acum_t = _dot_tri(dtt * at, upper)
    return dt, sg, a, acum, acum_t, row, col, upper


def _gated(yv, z):
    s = _sigmoid(z)
    sz = z * s
    gv = yv * sz
    return gv, lax.rsqrt(jnp.mean(gv * gv, axis=-1, keepdims=True) + NORM_EPS), sz, s


def _scan_fwd(xbc, proj, dt_col, dtrt, biasc, biast, alogx, alogt, dskx, wn, d_mix, *, name, ride=None):
    l = xbc.shape[0]
    hpg = dtrt.shape[1]
    gw, q, nc = hpg * HEAD_DIM, CHUNK, l // CHUNK
    assert hpg % 2 == 0
    xs_s, bm_s, cm_s, dtt_s, vec, hcol, st_s = _scan_specs(hpg, nc, lambda c: c)
    width = biasc.shape[1]
    dtc_s = pl.BlockSpec((q, width), lambda g, c: (c, dt_col // width))
    biasc_s = pl.BlockSpec((1, width), lambda g, c: (0, 0))

    def body(xs_ref, b_ref, c_ref, dtc_ref, dtrt_ref, biasc_ref, biast_ref, alogx_ref, alogt_ref, dsk_ref, z_ref, wn_ref,
             y_ref, st_ref, mixed_ref, h_ref):
        @pl.when(pl.program_id(1) == 0)
        def _():
            h_ref[...] = jnp.zeros_like(h_ref)

        dt, _, _, acum, acum_t, row, col, _ = _scan_common(dtc_ref, dtrt_ref, biasc_ref, biast_ref, alogx_ref, alogt_ref, hpg)
        causal = row >= col
        xs = xs_ref[...]
        xdt = xs * dt
        xdtb = xdt.astype(BF16)
        bmat, cmat = b_ref[...], c_ref[...]
        cb = _dot(cmat, bmat, tb=True)
        h_all = h_ref[...]
        st_ref[...] = h_all
        rest = _dot(cmat, h_all) * jnp.exp(acum) + dsk_ref[...] * xs
        first = lax.broadcasted_iota(jnp.int32, (q, 2 * HEAD_DIM), 1) < HEAD_DIM
        parts = []
        for jp in range(hpg // 2):
            pair = slice(2 * jp * HEAD_DIM, (2 * jp + 2) * HEAD_DIM)
            xp = xdtb[:, pair]
            ys = []
            for j in (2 * jp, 2 * jp + 1):
                a_col, a_row = acum[:, j * HEAD_DIM:j * HEAD_DIM + 1], acum_t[j:j + 1, :]
                lm = jnp.exp(jnp.where(causal, a_col - a_row, NEG_BIG))
                ys.append(_dot(cb * lm, xp))
            parts.append(jnp.where(first, ys[0], ys[1]))
        yv = jnp.concatenate(parts, axis=1) + rest
        y_ref[...] = yv
        gv, r, _, _ = _gated(yv, z_ref[...])
        mixed_ref[...] = (gv * r * wn_ref[...]).astype(mixed_ref.dtype)
        a_last = acum[q - 1:q, :]
        h_ref[...] = jnp.exp(a_last) * h_all + _dot(bmat, xdt * jnp.exp(a_last - acum), ta=True)

    return _call(
        body, name=name, grid=(GROUPS, nc),
        in_specs=[xs_s, bm_s, cm_s, dtc_s, dtt_s, biasc_s, hcol, vec, hcol, vec, xs_s, vec], out_specs=[xs_s, st_s, xs_s],
        out_shape=[jax.ShapeDtypeStruct((l, GROUPS * gw), F32), jax.ShapeDtypeStruct((GROUPS, nc, STATE, gw), F32),
                   jax.ShapeDtypeStruct((l, d_mix), BF16)],
        scratch=[pltpu.VMEM((STATE, gw), F32)], sem=("parallel", "arbitrary"),
        args=(xbc, xbc, xbc, proj, dtrt, biasc, biast, alogx, alogt, dskx, proj, wn), ride=ride)


def _scan_bwd(xbc, proj, dt_col, dtrt, biasc, biast, alogx, alogt, dskx, states, dmixed, y, wn, *, name, ride=None):
    l = xbc.shape[0]
    hpg = dtrt.shape[1]
    gw, q, nc = hpg * HEAD_DIM, CHUNK, l // CHUNK
    rev = lambda c: nc - 1 - c
    xs_s, bm_s, cm_s, dtt_s, vec, hcol, st_s = _scan_specs(hpg, nc, rev)
    bc_s = pl.BlockSpec((q, STATE), lambda g, c: (rev(c), g))
    width = biasc.shape[1]
    dtc_s = pl.BlockSpec((q, width), lambda g, c: (rev(c), dt_col // width))
    biasc_s = pl.BlockSpec((1, width), lambda g, c: (0, 0))

    def body(xs_ref, b_ref, c_ref, dtc_ref, dtrt_ref, biasc_ref, biast_ref, alogx_ref, alogt_ref, dsk_ref, st_ref,
             dmix_ref, y_ref, z_ref, wn_ref,
             dxs_ref, db_ref, dc_ref, ddt_ref, dbias_ref, dalog_ref, ddsk_ref, dz_ref, dwn_ref, dh_ref):
        step = pl.program_id(1)

        @pl.when(step == 0)
        def _():
            dh_ref[...] = jnp.zeros_like(dh_ref)

        yv, z, dmix = y_ref[...], z_ref[...], dmix_ref[...]
        gv, rn, sz, sg = _gated(yv, z)
        gh = gv * rn
        dgh = dmix * wn_ref[...]
        dg = rn * (dgh - gh * jnp.mean(dgh * gh, axis=-1, keepdims=True))
        dyv = dg * sz
        dz_ref[...] = (dg * yv * (sg * (1.0 + z * (1.0 - sg)))).astype(dz_ref.dtype)
        p_wn = jnp.sum(dmix * gh, axis=0, keepdims=True)

        dt, sg_dt, a, acum, acum_t, row, col, upper = _scan_common(
            dtc_ref, dtrt_ref, biasc_ref, biast_ref, alogx_ref, alogt_ref, hpg, want_sigmoid=True)
        causal, anti, strict = row >= col, row <= col, row > col
        xs = xs_ref[...]
        xdt = xs * dt
        xdtb, dyb = xdt.astype(BF16), dyv.astype(BF16)
        bmat, cmat = b_ref[...], c_ref[...]
        cb = _dot(cmat, bmat, tb=True)
        cbt = _dot(bmat, cmat, tb=True)
        h_all, dh = st_ref[...], dh_ref[...]
        a_last = acum[q - 1:q, :]
        ea, ee, gam = jnp.exp(acum), jnp.exp(a_last - acum), jnp.exp(a_last)
        zc = _dot(cmat, h_all)
        bdh = _dot(bmat, dh)
        dz = dyv * ea
        xe = xdt * ee
        dc_state = _dot(dz, h_all, tb=True)
        db_state = _dot(xe, dh, tb=True)
        dh_ref[...] = gam * dh + _dot(cmat, dz, ta=True)
        lane = lax.broadcasted_iota(jnp.int32, (q, 2 * HEAD_DIM), 1)
        first = lane < HEAD_DIM
        ones = jnp.ones((q // 2, 2 * HEAD_DIM), BF16)
        dcb = jnp.zeros((q, q), F32)
        dxdt_parts, da_parts = [], []
        for jp in range(hpg // 2):
            pair = slice(2 * jp * HEAD_DIM, (2 * jp + 2) * HEAD_DIM)
            xp, dyp = xdtb[:, pair], dyb[:, pair]
            dxs_r, das_r = [], []
            for r, j in enumerate((2 * jp, 2 * jp + 1)):
                seg = acum[:, j * HEAD_DIM:j * HEAD_DIM + 1] - acum_t[j:j + 1, :]
                lm = jnp.exp(jnp.where(causal, seg, NEG_BIG))
                lmt = jnp.exp(jnp.where(anti, -seg, NEG_BIG))
                mm, mmt = cb * lm, cbt * lmt
                keep = first if r == 0 else jnp.logical_not(first)
                dyk = jnp.where(keep, dyp, 0)
                dxs_r.append(_dot(mmt, dyp))
                dm = _dot(dyk, xp, tb=True)
                dcb = dcb + dm * lm
                corner = jnp.dot(upper, (dm * mm).astype(BF16), preferred_element_type=F32)
                corner = jnp.where(strict, corner, 0.0)
                chi, clo = _split2(corner[:, :q // 2] + corner[:, q // 2:])
                das_r.append(jnp.dot(chi, ones, preferred_element_type=F32) + jnp.dot(clo, ones, preferred_element_type=F32))
            dxdt_parts.append(jnp.where(first, dxs_r[0], dxs_r[1]))
            da_parts.append(jnp.where(first, das_r[0], das_r[1]))
        dxdt = jnp.concatenate(dxdt_parts, axis=1) + bdh * ee
        da_intra = jnp.concatenate(da_parts, axis=1)
        dxs_ref[...] = dxdt * dt + dsk_ref[...] * dyv
        dc_ref[...] = dc_state + _dot(dcb, bmat)
        db_ref[...] = db_state + _dot(dcb, cmat, ta=True)
        ri = lax.broadcasted_iota(jnp.int32, (gw, gw), 0) // HEAD_DIM
        ci = lax.broadcasted_iota(jnp.int32, (gw, gw), 1) // HEAD_DIM
        blockdiag = (ri == ci).astype(BF16)

        def head_sum(v):
            hi, lo = _split2(v)
            return (jnp.dot(hi, blockdiag, preferred_element_type=F32)
                    + jnp.dot(lo, blockdiag, preferred_element_type=F32))

        xt = xe * bdh
        small = jnp.concatenate([
            jnp.sum(xt, axis=0, keepdims=True) + gam * jnp.sum(dh * h_all, axis=0, keepdims=True),
            jnp.sum(dyv * xs, axis=0, keepdims=True), jnp.zeros((6, gw), F32)], axis=0)
        small = head_sum(small)
        rows = lax.broadcasted_iota(jnp.int32, (q, 1), 0)
        da_local = head_sum(dyv * zc * ea - xt) + jnp.where(rows == q - 1, small[0:1, :], 0.0)
        hi, lo = _split2(da_local)
        d_dta = (da_intra + jnp.dot(upper, hi, preferred_element_type=F32)
                 + jnp.dot(upper, lo, preferred_element_type=F32))
        d_raw = (d_dta * a + head_sum(dxdt * xs)) * sg_dt
        ddt_ref[...] = d_raw
        p_bias = jnp.sum(d_raw, axis=0, keepdims=True)
        p_alog = jnp.sum(d_dta * dt, axis=0, keepdims=True) * a
        p_dsk = small[1:2, :]

        @pl.when(step == 0)
        def _():
            dbias_ref[...] = p_bias
            dalog_ref[...] = p_alog
            ddsk_ref[...] = p_dsk
            dwn_ref[...] = p_wn

        @pl.when(step > 0)
        def _():
            dbias_ref[...] += p_bias
            dalog_ref[...] += p_alog
            ddsk_ref[...] += p_dsk
            dwn_ref[...] += p_wn

    wide = jax.ShapeDtypeStruct((l, GROUPS * gw), F32)
    narrow = jax.ShapeDtypeStruct((l, GROUPS * STATE), F32)
    vshape = jax.ShapeDtypeStruct((1, GROUPS * gw), F32)
    return _call(
        body, name=name, grid=(GROUPS, nc),
        in_specs=[xs_s, bm_s, cm_s, dtc_s, dtt_s, biasc_s, hcol, vec, hcol, vec, st_s, xs_s, xs_s, xs_s, vec],
        out_specs=[xs_s, bc_s, bc_s, xs_s, vec, vec, vec, xs_s, vec],
        out_shape=[wide, narrow, narrow, wide, vshape, vshape, vshape, jax.ShapeDtypeStruct(proj.shape, BF16), vshape],
        scratch=[pltpu.VMEM((STATE, gw), F32)], sem=("parallel", "arbitrary"),
        args=(xbc, xbc, xbc, proj, dtrt, biasc, biast, alogx, alogt, dskx, states, dmixed, y, proj, wn), ride=ride)


def _exchange(srcs, scatter, *, name):
    ride = _Ride(srcs, scatter)

    def body(*refs):
        src, dst, sems = refs[:ride.n], refs[ride.n:2 * ride.n], refs[2 * ride.n:]
        ride.start(src, dst, sems)
        ride.relay(src, dst, sems)
        ride.finish(src, dst, sems)

    return pl.pallas_call(body, name=name, in_specs=ride.in_specs, out_specs=ride.out_specs, out_shape=ride.out_shape,
                          scratch_shapes=ride.scratch)(*srcs)


def _adamw(parts, w, m, v, *, name, tr=256):
    lead = w.ndim == 3
    r, c = w.shape[-2:]
    tr = _tile(r, tr, 16)
    c1, c2 = 1.0 / (1.0 - ADAM_B1 ** ADAM_STEP), 1.0 / (1.0 - ADAM_B2 ** ADAM_STEP)

    def body(p_ref, w_ref, m_ref, v_ref, g_ref, d_ref, nm_ref, nv_ref):
        g = p_ref[0].astype(F32)
        for p in range(1, N_DEV):
            g = g + p_ref[p].astype(F32)
        nm = ADAM_B1 * m_ref[...] + (1.0 - ADAM_B1) * g
        nv = ADAM_B2 * v_ref[...] + (1.0 - ADAM_B2) * (g * g)
        g_ref[...] = g
        nm_ref[...] = nm
        nv_ref[...] = nv
        d_ref[...] = -ADAM_LR * ((nm * c1) / (jnp.sqrt(nv * c2) + ADAM_EPS) + ADAM_WD * w_ref[...])

    blk = pl.BlockSpec((None, tr, c), lambda i: (0, i, 0)) if lead else pl.BlockSpec((tr, c), lambda i: (i, 0))
    out = jax.ShapeDtypeStruct(w.shape, F32)
    return pl.pallas_call(
        body, name=name, grid=(r // tr,), in_specs=[pl.BlockSpec((N_DEV, tr, c), lambda i: (0, i, 0)), blk, blk, blk],
        out_specs=[blk, blk, blk, blk], out_shape=[out, out, out, out], compiler_params=_params("parallel"),
    )(parts, w, m, v)


def kernel(x, attn_norm_w, w_in, conv_w, conv_b, dt_bias, a_log, d_skip, ssd_norm_w, pool_w, pool_scale, w_out, ffn_norm_w, w_gate, w_up, w_down, final_norm_w, loss_target, m_attn_norm_w, m_w_in, m_conv_w, m_conv_b, m_dt_bias, m_a_log, m_d_skip, m_ssd_norm_w, m_pool_w, m_pool_scale, m_w_out, m_ffn_norm_w, m_w_gate, m_w_up, m_w_down, m_final_norm_w, v_attn_norm_w, v_w_in, v_conv_w, v_conv_b, v_dt_bias, v_a_log, v_d_skip, v_ssd_norm_w, v_pool_w, v_pool_scale, v_w_out, v_ffn_norm_w, v_w_gate, v_w_up, v_w_down, v_final_norm_w):
    l, d = x.shape[1], x.shape[2]
    heads = dt_bias.shape[1]
    hpg = heads // GROUPS
    d_ssm = heads * HEAD_DIM
    conv_ch = conv_b.shape[1]
    d_pool = pool_scale.shape[1]
    pg = d_pool // GROUPS
    d_mix = d_ssm + d_pool
    d_ff = w_gate.shape[2] * N_DEV
    d_in = w_in.shape[2] * N_DEV
    dt_pad = -(-heads // LANE) * LANE
    o_u, o_xbc, o_dt = d_ssm, d_ssm + d_pool, d_ssm + d_pool + conv_ch
    d_inp = o_dt + dt_pad
    tn_in = _tile(d_inp, 2560, LANE)
    g_dt, g_u = d_ssm + conv_ch, d_ssm + conv_ch + heads

    x2, tgt = x[0], loss_target[0]

    tr_ = lambda t: jnp.transpose(t, (0, 2, 1))
    c_in = d_in // N_DEV
    c_pad = -(-c_in // 16) * 16
    segs_in = [(0, d_ssm, 0), (g_u, d_in, o_u), (d_ssm, g_dt, o_xbc), (g_dt, g_u, o_dt)]

    def stacked_rows(lo, hi):
        cuts = [(max(lo, p * c_in), min(hi, (p + 1) * c_in), p * (c_pad - c_in)) for p in range(N_DEV)]
        return [(a + shift, b + shift) for a, b, shift in cuts if a < b]

    h0, gi, gp, gc = _rms_fwd(x2, attn_norm_w, name="attn_norm", ride=_Ride(
        [jnp.pad(tr_(w_in)[0].astype(BF16), ((0, c_pad - c_in), (0, 0))), pool_w[0].astype(BF16), conv_w[0]], False))
    win = gi.reshape(N_DEV * c_pad, d)
    wp = jnp.concatenate([win[a:b] for lo, hi, _ in segs_in for a, b in stacked_rows(lo, hi)]
                         + [jnp.zeros((dt_pad - heads, d), BF16)], axis=0)
    pw = jnp.transpose(gp, (1, 0, 2, 3)).reshape(GROUPS, pg, pg)
    cw = jnp.transpose(gc, (1, 0, 2)).reshape(CONV_K, conv_ch)

    def expand(vec):
        return jnp.repeat(vec, HEAD_DIM, axis=1)

    def per_head(vec):
        return vec[:, ::HEAD_DIM]

    alog_x, dsk_x = expand(a_log), expand(d_skip)
    bias_w = jnp.pad(dt_bias, ((0, 0), (0, dt_pad - heads)))
    bias_c, alog_c = dt_bias.reshape(GROUPS, hpg, 1), a_log.reshape(GROUPS, hpg, 1)

    proj, go, gg = _mm(h0, wp, name="in_proj", tb=True, tm=512, tn=tn_in, tk=d,
                       ride=_Ride([w_out[0].astype(BF16), tr_(w_gate)[0].astype(BF16)], False))
    wo = go.reshape(d_mix, d)
    xbc = _conv_fwd(proj, o_xbc, cw, conv_b, name="conv_fwd")
    dt_raw = proj[:, o_dt:o_dt + heads]
    dtrt = jnp.transpose(dt_raw.reshape(l, GROUPS, hpg), (1, 2, 0))
    y, states, mixed, gu = _scan_fwd(xbc, proj, o_dt, dtrt, bias_w, bias_c, alog_x, alog_c, dsk_x, ssd_norm_w, d_mix,
                                     name="ssd_fwd", ride=_Ride([tr_(w_up)[0].astype(BF16)], False))
    pooled = _pool_fwd(proj, o_u, d_pool, name="pool_fwd")
    pool_raw, mixed = _pool_mix_fwd(pooled, pw, pool_scale, mixed, d_ssm, name="pool_mix")
    h1 = _mm(mixed, wo, name="out_proj", tm=512, tk=d_mix, residual=x2)
    h1n = _rms_fwd(h1, ffn_norm_w, name="ffn_norm")
    wide = lambda t: t.reshape(N_DEV // FFN_GROUP, -1, d)
    gg, gu = wide(gg), wide(gu)
    gate, up, act, gd = _gate_up(h1n, gg, gu, name="gate_up", ride=_Ride([w_down[0].astype(BF16)], False))
    gd = wide(gd)
    h2 = _mm_shards([(act, gd)], name="down_proj", per_step=1, residual=h1)

    loss11, dh2b, g_final = _loss_head(h2, final_norm_w.reshape(1, d), tgt, name="loss_head")
    dgate, dup = _gate_up_bwd(dh2b, gd, gate, up, name="gate_up_bwd")
    shards = lambda t: t.reshape(N_DEV, -1, d)
    g_wd = shards(_grad_shards(act, dh2b, name="grad_w_down"))
    dh1n, r_wd = _mm_shards([(dgate, gg), (dup, gu)], name="dgate_dup", out_dtype=BF16, tm=512, per_step=1,
                            ride=_Ride([g_wd], True))
    g_wg = shards(_grad_shards(dgate, h1n, name="grad_w_gate"))
    g_wu = shards(_grad_shards(dup, h1n, name="grad_w_up"))
    dh1b, g_ffn = _rms_bwd(h1, ffn_norm_w, dh1n, dh2b, BF16, name="ffn_norm_bwd")
    dmixed = _mm(dh1b, wo, name="dmixed", tb=True, tk=d)
    g_wo = _mm(mixed, dh1b, name="grad_w_out", ta=True, out_dtype=BF16, tm=1024, tn=1024, tk=2048)
    draw, dpooled, g_pscale = _pool_mix_bwd(dmixed, d_ssm, pool_raw, pool_scale, pw, name="pool_mix_bwd")
    g_pw = _group_mm_tn(pooled, draw, name="grad_pool_w")
    s_pw = jnp.transpose(g_pw.reshape(GROUPS, N_DEV, pg // N_DEV, pg), (1, 0, 2, 3)).astype(BF16)
    dxs, dbm, dcm, ddtx, g_bias, g_alog, g_dsk, dproj, g_ssdn, r_wg, r_wu, r_wo, r_pw = _scan_bwd(
        xbc, proj, o_dt, dtrt, bias_w, bias_c, alog_x, alog_c, dsk_x, states, dmixed, y, ssd_norm_w, name="ssd_bwd",
        ride=_Ride([g_wg, g_wu, g_wo.reshape(N_DEV, d_mix // N_DEV, d), s_pw], True))
    dproj = _pool_bwd(dpooled, dproj, o_u, name="pool_bwd")
    n_bc = GROUPS * STATE
    segs = [(dxs, 0, d_ssm), (dbm, d_ssm, n_bc), (dcm, d_ssm + n_bc, n_bc)]
    g_cw, g_cb = [], []
    for si, (dseg, c0, width) in enumerate(segs):
        dproj, b, c_ = _conv_bwd(proj, o_xbc + c0, cw[:, c0:c0 + width], conv_b[:, c0:c0 + width], dseg, dproj,
                                 name=f"conv_bwd{si}")
        g_cw.append(b)
        g_cb.append(c_)
    dproj = _put_heads(ddtx, dproj, o_dt, dt_pad, name="put_ddt")
    g_wp = _mm(dproj, h0, name="grad_w_in", ta=True, out_dtype=BF16, tm=384, tn=d, tk=2048)
    g_win = jnp.concatenate([g_wp[:d_ssm], g_wp[o_xbc:o_dt + heads], g_wp[o_u:o_xbc]], axis=0).reshape(N_DEV, c_in, d)
    def cols8(g):
        return jnp.transpose(g.reshape(g.shape[0], N_DEV, g.shape[1] // N_DEV), (1, 0, 2))
    dh0, r_in, r_cw = _mm(dproj, wp, name="dh0", out_dtype=BF16, tk=tn_in,
                          ride=_Ride([g_win, cols8(jnp.concatenate(g_cw, axis=1))], True))
    dx, g_attn = _rms_bwd(x2, attn_norm_w, dh0, dh1b, F32, name="attn_norm_bwd")

    smalls = [g_attn, jnp.concatenate(g_cb, axis=1), per_head(g_bias), per_head(g_alog), per_head(g_dsk), g_ssdn,
              g_pscale, g_ffn, g_final]
    small_w = [attn_norm_w, conv_b, dt_bias, a_log, d_skip, ssd_norm_w, pool_scale, ffn_norm_w, final_norm_w.reshape(1, d)]
    small_m = [m_attn_norm_w, m_conv_b, m_dt_bias, m_a_log, m_d_skip, m_ssd_norm_w, m_pool_scale, m_ffn_norm_w,
               m_final_norm_w.reshape(1, d)]
    small_v = [v_attn_norm_w, v_conv_b, v_dt_bias, v_a_log, v_d_skip, v_ssd_norm_w, v_pool_scale, v_ffn_norm_w,
               v_final_norm_w.reshape(1, d)]
    sizes = [s.shape[1] for s in smalls]
    n_small = sum(sizes)
    n_pad = -(-n_small // (16 * LANE)) * (16 * LANE)
    rows = n_pad // LANE
    def pack(vs):
        return jnp.pad(jnp.concatenate(vs, axis=1), ((0, 0), (0, n_pad - n_small))).reshape(rows, LANE)
    (r_small,) = _exchange([pack(smalls)], False, name="gather_small_grads")

    def big(parts, w, m, v, nm):
        shp = w.shape
        if w.ndim == 3:
            return _adamw(parts, w, m, v, name=nm)
        r2 = lambda t: t.reshape(-1, shp[-1])
        outs = _adamw(parts.reshape(N_DEV, -1, shp[-1]), r2(w), r2(m), r2(v), name=nm)
        return [o.reshape(shp) for o in outs]

    def big_t(parts, w, m, v, nm):
        return [tr_(o) for o in _adamw(parts, tr_(w), tr_(m), tr_(v), name=nm)]

    def big_flat(parts, w, m, v, nm):
        fl = lambda t: tr_(t).reshape(-1, LANE)
        outs = _adamw(parts.reshape(N_DEV, -1, LANE), fl(w), fl(m), fl(v), name=nm, tr=2048)
        return [tr_(o.reshape(1, w.shape[2], w.shape[1])) for o in outs]

    res = {
        "w_in": big_flat(r_in, w_in, m_w_in, v_w_in, "adamw_w_in"),
        "conv_w": big(r_cw, conv_w, m_conv_w, v_conv_w, "adamw_conv_w"),
        "pool_w": big(r_pw, pool_w, m_pool_w, v_pool_w, "adamw_pool_w"),
        "w_out": big(r_wo, w_out, m_w_out, v_w_out, "adamw_w_out"),
        "w_gate": big_t(r_wg, w_gate, m_w_gate, v_w_gate, "adamw_w_gate"),
        "w_up": big_t(r_wu, w_up, m_w_up, v_w_up, "adamw_w_up"),
        "w_down": big(r_wd, w_down, m_w_down, v_w_down, "adamw_w_down"),
    }
    s_out = _adamw(r_small, pack(small_w), pack(small_m), pack(small_v), name="adamw_small")
    names = ["attn_norm_w", "conv_b", "dt_bias", "a_log", "d_skip", "ssd_norm_w", "pool_scale", "ffn_norm_w", "final_norm_w"]
    offs = [sum(sizes[:i]) for i in range(len(sizes))]
    for i, nm in enumerate(names):
        shp = (d,) if nm == "final_norm_w" else (1, sizes[i])
        res[nm] = [o.reshape(1, n_pad)[:, offs[i]:offs[i] + sizes[i]].reshape(shp) for o in s_out]

    loss = lax.psum(loss11[0, 0], ("x", "y", "c"))
    order = ["attn_norm_w", "w_in", "conv_w", "conv_b", "dt_bias", "a_log", "d_skip", "ssd_norm_w", "pool_w", "pool_scale",
             "w_out", "ffn_norm_w", "w_gate", "w_up", "w_down", "final_norm_w"]
    outs = [loss, dx.reshape(x.shape)]
    for part in range(4):
        outs += [res[nm][part] for nm in order]
    return tuple(outs)
```

```python
import functools
import math

import jax
import jax.numpy as jnp
from jax import lax
from jax.experimental import pallas as pl
from jax.experimental.pallas import tpu as pltpu

F32 = jnp.float32
BF16 = jnp.bfloat16

NORM_EPS = 1e-5
HEAD_DIM = 64
STATE = 128
CHUNK = 256
GROUPS = 4
CONV_K = 4
POOL_WINDOWS = (2, 4, 8, 16)
POOL_HALO = 16
CONV_HALO = 8
LANE = 128
N_DEV = 8
ADAM_LR, ADAM_B1, ADAM_B2, ADAM_EPS, ADAM_WD, ADAM_STEP = 0.001, 0.9, 0.999, 1e-08, 0.01, 10
NEG_BIG = -1e30
FFN_GROUP = 4
MESH_ID = pl.DeviceIdType.MESH


def _tile(n, pref, mult):
    if n <= pref:
        return n
    t = (pref // mult) * mult
    while t >= mult:
        if n % t == 0:
            return t
        t -= mult
    return n


def _params(*sem):
    return pltpu.CompilerParams(dimension_semantics=sem)


ANY = pl.BlockSpec(memory_space=pl.ANY)


def _peer(k):
    x, y, c = lax.axis_index("x"), lax.axis_index("y"), lax.axis_index("c")
    return (1 - x if k & 4 else x, 1 - y if k & 2 else y, 1 - c if k & 1 else c)


def _my_index():
    return 4 * lax.axis_index("x") + 2 * lax.axis_index("y") + lax.axis_index("c")


class _Ride:
    def __init__(self, srcs, scatter):
        self.srcs, self.scatter, self.n = list(srcs), scatter, len(srcs)
        self.two_level = not scatter
        self.in_specs = [ANY] * self.n
        self.out_specs = [ANY] * self.n
        self.out_shape = [jax.ShapeDtypeStruct((N_DEV,) + tuple(s.shape[1:] if scatter else s.shape), s.dtype)
                          for s in self.srcs]
        self.scratch = [pltpu.SemaphoreType.DMA((self.n, N_DEV - 1)), pltpu.SemaphoreType.DMA((self.n, N_DEV - 1)),
                        pltpu.SemaphoreType.DMA((self.n,))]

    def _copies(self, src, dst, sems):
        send_sems, recv_sems, local_sems = sems
        me = _my_index()
        local = [pltpu.make_async_copy(src[t].at[me] if self.scatter else src[t], dst[t].at[me], local_sems.at[t])
                 for t in range(self.n)]
        remote = {}
        for k in range(1, N_DEV):
            peer, pidx = _peer(k), me ^ k
            remote[k] = [pltpu.make_async_remote_copy(
                src_ref=src[t].at[pidx] if self.scatter else src[t], dst_ref=dst[t].at[me],
                send_sem=send_sems.at[t, k - 1], recv_sem=recv_sems.at[t, k - 1],
                device_id=peer, device_id_type=MESH_ID) for t in range(self.n)]
        return local, remote

    def _forwards(self, dst, sems):
        send_sems, recv_sems, _ = sems
        me = _my_index()
        return {k: [pltpu.make_async_remote_copy(
            src_ref=dst[t].at[me ^ k], dst_ref=dst[t].at[me ^ k],
            send_sem=send_sems.at[t, (k ^ 1) - 1], recv_sem=recv_sems.at[t, (k ^ 1) - 1],
            device_id=_peer(1), device_id_type=MESH_ID) for t in range(self.n)] for k in (2, 4, 6)}

    def start(self, src, dst, sems):
        local, remote = self._copies(src, dst, sems)
        for cp in local:
            cp.start()
        for k in (2, 4, 6, 1) if self.two_level else range(1, N_DEV):
            for cp in remote[k]:
                cp.start()

    def relay(self, src, dst, sems):
        if not self.two_level:
            return
        _, remote = self._copies(src, dst, sems)
        fwd = self._forwards(dst, sems)
        for k in (2, 4, 6):
            for t in range(self.n):
                remote[k][t].wait_recv()
                fwd[k][t].start()

    def finish(self, src, dst, sems):
        local, remote = self._copies(src, dst, sems)
        if self.two_level:
            fwd = self._forwards(dst, sems)
            for k in (1, 3, 5, 7):
                for cp in remote[k]:
                    cp.wait_recv()
            for k in (1, 2, 4, 6):
                for cp in remote[k]:
                    cp.wait_send()
            for k in (2, 4, 6):
                for cp in fwd[k]:
                    cp.wait_send()
        else:
            for k in range(1, N_DEV):
                for cp in remote[k]:
                    cp.wait_send()
            for k in range(1, N_DEV):
                for cp in remote[k]:
                    cp.wait_recv()
        for lc in local:
            lc.wait()


RELAY_AT = 0.75


def _call(body, *, name, grid, in_specs, out_specs, out_shape, args, sem, scratch=(), ride=None):
    in_specs, out_specs, out_shape, scratch = list(in_specs), list(out_specs), list(out_shape), list(scratch)
    if ride is None:
        return pl.pallas_call(body, name=name, grid=grid, in_specs=in_specs, out_specs=out_specs, out_shape=out_shape,
                              scratch_shapes=scratch, compiler_params=_params(*sem))(*args)
    n_in, n_out, n_sc, nr = len(in_specs), len(out_specs), len(scratch), ride.n

    def full(*refs):
        ins, csrc = refs[:n_in], refs[n_in:n_in + nr]
        o0 = n_in + nr
        outs, cdst = refs[o0:o0 + n_out], refs[o0 + n_out:o0 + n_out + nr]
        s0 = o0 + n_out + nr
        sc, csem = refs[s0:s0 + n_sc], refs[s0 + n_sc:]
        ids = [pl.program_id(ax) for ax in range(len(grid))]
        step = functools.reduce(lambda acc, ig: acc * ig[1] + ig[0], zip(ids, grid), 0)
        total = functools.reduce(lambda p, g: p * g, grid, 1)

        @pl.when(step == 0)
        def _():
            ride.start(csrc, cdst, csem)

        body(*ins, *outs, *sc)

        @pl.when(step == min(int(total * RELAY_AT), total - 1))
        def _():
            ride.relay(csrc, cdst, csem)

        @pl.when(step == total - 1)
        def _():
            ride.finish(csrc, cdst, csem)

    return pl.pallas_call(
        full, name=name, grid=grid, in_specs=in_specs + ride.in_specs, out_specs=out_specs + ride.out_specs,
        out_shape=out_shape + ride.out_shape, scratch_shapes=scratch + ride.scratch,
        compiler_params=_params(*(["arbitrary"] * len(grid))),
    )(*args, *ride.srcs)


def _sigmoid(x):
    return 1.0 / (1.0 + jnp.exp(-x))


def _softplus(x):
    return jnp.maximum(x, 0.0) + jnp.log(1.0 + jnp.exp(-jnp.abs(x)))


def _dot(a, b, ta=False, tb=False):
    dn = (((0 if ta else 1,), (1 if tb else 0,)), ((), ()))
    return lax.dot_general(a.astype(BF16), b.astype(BF16), dn, preferred_element_type=F32)


def _mm(a, b, *, name, ta=False, tb=False, out_dtype=F32, tm=1024, tn=1024, tk=2048, residual=None, ride=None):
    m = a.shape[1] if ta else a.shape[0]
    k = a.shape[0] if ta else a.shape[1]
    n = b.shape[0] if tb else b.shape[1]
    assert k == (b.shape[1] if tb else b.shape[0])
    tm, tn = _tile(m, tm, LANE if ta else 8), _tile(n, tn, LANE)
    tk = _tile(k, tk, LANE if not (ta and tb) else LANE)
    nk = k // tk
    has_res = residual is not None

    def body(*refs):
        a_ref, b_ref = refs[0], refs[1]
        r_ref = refs[2] if has_res else None
        o_ref = refs[3] if has_res else refs[2]
        part = _dot(a_ref[...], b_ref[...], ta, tb)

        def finish(acc):
            if has_res:
                acc = acc + r_ref[...]
            o_ref[...] = acc.astype(o_ref.dtype)

        if nk == 1:
            finish(part)
        else:
            acc_ref = refs[-1]
            kk = pl.program_id(2)

            @pl.when(kk == 0)
            def _():
                acc_ref[...] = part

            @pl.when(kk > 0)
            def _():
                acc_ref[...] += part

            @pl.when(kk == nk - 1)
            def _():
                finish(acc_ref[...])

    a_spec = pl.BlockSpec((tk, tm), lambda i, j, kk: (kk, i)) if ta else pl.BlockSpec((tm, tk), lambda i, j, kk: (i, kk))
    b_spec = pl.BlockSpec((tn, tk), lambda i, j, kk: (j, kk)) if tb else pl.BlockSpec((tk, tn), lambda i, j, kk: (kk, j))
    o_spec = pl.BlockSpec((tm, tn), lambda i, j, kk: (i, j))
    in_specs = [a_spec, b_spec] + ([o_spec] if has_res else [])
    args = (a, b) + ((residual,) if has_res else ())
    res = _call(body, name=name, grid=(m // tm, n // tn, nk), in_specs=in_specs, out_specs=[o_spec],
                out_shape=[jax.ShapeDtypeStruct((m, n), out_dtype)], args=args,
                scratch=[pltpu.VMEM((tm, tn), F32)] if nk > 1 else [], sem=("parallel", "parallel", "arbitrary"), ride=ride)
    return res[0] if ride is None else res


def _gate_up(h, wg, wu, *, name, tm=256, ride=None):
    m, k = h.shape
    ns, c, _ = wg.shape
    tm = _tile(m, tm, 16)

    def body(h_ref, wg_ref, wu_ref, g_ref, u_ref, a_ref):
        hb = h_ref[...]
        g = _dot(hb, wg_ref[...], tb=True)
        u = _dot(hb, wu_ref[...], tb=True)
        g_ref[...] = g
        u_ref[...] = u
        a_ref[...] = (g * _sigmoid(g) * u).astype(a_ref.dtype)

    o_spec = pl.BlockSpec((None, tm, c), lambda s, i: (s, i, 0))
    w_spec = pl.BlockSpec((None, c, k), lambda s, i: (s, 0, 0), pipeline_mode=pl.Buffered(1))
    return _call(body, name=name, grid=(ns, m // tm),
                 in_specs=[pl.BlockSpec((tm, k), lambda s, i: (i, 0)), w_spec, w_spec], out_specs=[o_spec, o_spec, o_spec],
                 out_shape=[jax.ShapeDtypeStruct((ns, m, c), F32), jax.ShapeDtypeStruct((ns, m, c), F32),
                            jax.ShapeDtypeStruct((ns, m, c), BF16)],
                 args=(h, wg, wu), sem=("parallel", "parallel"), ride=ride)


def _gate_up_bwd(dh, wd, g, u, *, name, tm=256):
    m, k = dh.shape
    ns, c, _ = wd.shape
    tm = _tile(m, tm, 16)

    def body(dh_ref, wd_ref, g_ref, u_ref, dg_ref, du_ref):
        da = _dot(dh_ref[...], wd_ref[...], tb=True)
        g = g_ref[...]
        s = _sigmoid(g)
        dg_ref[...] = (da * u_ref[...] * (s * (1.0 + g * (1.0 - s)))).astype(dg_ref.dtype)
        du_ref[...] = (da * (g * s)).astype(du_ref.dtype)

    o_spec = pl.BlockSpec((None, tm, c), lambda s, i: (s, i, 0))
    w_spec = pl.BlockSpec((None, c, k), lambda s, i: (s, 0, 0), pipeline_mode=pl.Buffered(1))
    return _call(body, name=name, grid=(ns, m // tm),
                 in_specs=[pl.BlockSpec((tm, k), lambda s, i: (i, 0)), w_spec, o_spec, o_spec],
                 out_specs=[o_spec, o_spec],
                 out_shape=[jax.ShapeDtypeStruct((ns, m, c), BF16), jax.ShapeDtypeStruct((ns, m, c), BF16)],
                 args=(dh, wd, g, u), sem=("parallel", "parallel"))


def _mm_shards(pairs, *, name, tb=False, residual=None, out_dtype=F32, tm=1024, tn=1024, per_step=2, ride=None):
    ns, m, c = pairs[0][0].shape
    n = pairs[0][1].shape[1] if tb else pairs[0][1].shape[2]
    tm, tn = _tile(m, tm, 16), _tile(n, tn, LANE)
    npair = len(pairs)
    has_res = residual is not None
    assert ns % per_step == 0
    nsteps = ns // per_step

    def body(*refs):
        o_ref, acc_ref = refs[-2], refs[-1]
        s = pl.program_id(2)
        part = None
        for p in range(npair):
            for q in range(per_step):
                term = _dot(refs[2 * p][q], refs[2 * p + 1][q], tb=tb)
                part = term if part is None else part + term

        @pl.when(s == 0)
        def _():
            acc_ref[...] = part

        @pl.when(s > 0)
        def _():
            acc_ref[...] += part

        @pl.when(s == nsteps - 1)
        def _():
            acc = acc_ref[...]
            if has_res:
                acc = acc + refs[2 * npair][...]
            o_ref[...] = acc.astype(o_ref.dtype)

    a_spec = pl.BlockSpec((per_step, tm, c), lambda i, j, s: (s, i, 0))
    b_spec = (pl.BlockSpec((per_step, tn, c), lambda i, j, s: (s, j, 0)) if tb
              else pl.BlockSpec((per_step, c, tn), lambda i, j, s: (s, 0, j)))
    o_spec = pl.BlockSpec((tm, tn), lambda i, j, s: (i, j))
    args = [t for pr in pairs for t in pr] + ([residual] if has_res else [])
    res = _call(body, name=name, grid=(m // tm, n // tn, nsteps),
                in_specs=[a_spec, b_spec] * npair + ([o_spec] if has_res else []), out_specs=[o_spec],
                out_shape=[jax.ShapeDtypeStruct((m, n), out_dtype)], args=args, scratch=[pltpu.VMEM((tm, tn), F32)],
                sem=("parallel", "parallel", "arbitrary"), ride=ride)
    return res[0] if ride is None else res


def _grad_shards(a, b, *, name, tk=2048, tc=1408, tn=1024):
    ns, l, c = a.shape
    d = b.shape[1]
    tk, tc, tn = _tile(l, tk, 16), _tile(c, tc, LANE), _tile(d, tn, LANE)
    nk = l // tk

    def body(a_ref, b_ref, o_ref, acc_ref):
        kk = pl.program_id(3)
        part = _dot(a_ref[...], b_ref[...], ta=True)

        @pl.when(kk == 0)
        def _():
            acc_ref[...] = part

        @pl.when(kk > 0)
        def _():
            acc_ref[...] += part

        @pl.when(kk == nk - 1)
        def _():
            o_ref[...] = acc_ref[...].astype(o_ref.dtype)

    return pl.pallas_call(
        body, name=name, grid=(ns, c // tc, d // tn, nk),
        in_specs=[pl.BlockSpec((None, tk, tc), lambda s, i, j, kk: (s, kk, i)),
                  pl.BlockSpec((tk, tn), lambda s, i, j, kk: (kk, j))],
        out_specs=pl.BlockSpec((None, tc, tn), lambda s, i, j, kk: (s, i, j)),
        out_shape=jax.ShapeDtypeStruct((ns, c, d), BF16), scratch_shapes=[pltpu.VMEM((tc, tn), F32)],
        compiler_params=_params("parallel", "parallel", "parallel", "arbitrary"),
    )(a, b)


def _rms_fwd(x, w, *, name, tl=512, ride=None):
    l, d = x.shape
    tl = _tile(l, tl, 16)

    def body(x_ref, w_ref, o_ref):
        xv = x_ref[...]
        r = lax.rsqrt(jnp.mean(xv * xv, axis=-1, keepdims=True) + NORM_EPS)
        o_ref[...] = (xv * r * w_ref[...]).astype(o_ref.dtype)

    row = pl.BlockSpec((tl, d), lambda i: (i, 0))
    res = _call(body, name=name, grid=(l // tl,), in_specs=[row, pl.BlockSpec((1, d), lambda i: (0, 0))],
                out_specs=[row], out_shape=[jax.ShapeDtypeStruct((l, d), BF16)], args=(x, w), sem=("parallel",), ride=ride)
    return res[0] if ride is None else res


def _rms_bwd(x, w, dy, add, out_dtype, *, name, tl=256):
    l, d = x.shape
    tl = _tile(l, tl, 16)

    def body(x_ref, w_ref, dy_ref, add_ref, dx_ref, dw_ref):
        i = pl.program_id(0)
        xv = x_ref[...]
        r = lax.rsqrt(jnp.mean(xv * xv, axis=-1, keepdims=True) + NORM_EPS)
        xh = xv * r
        dyv = dy_ref[...].astype(F32)
        dxh = dyv * w_ref[...]
        dx = r * (dxh - xh * jnp.mean(dxh * xh, axis=-1, keepdims=True)) + add_ref[...].astype(F32)
        dx_ref[...] = dx.astype(dx_ref.dtype)
        part = jnp.sum(dyv * xh, axis=0, keepdims=True)

        @pl.when(i == 0)
        def _():
            dw_ref[...] = part

        @pl.when(i > 0)
        def _():
            dw_ref[...] += part

    row = pl.BlockSpec((tl, d), lambda i: (i, 0))
    vec = pl.BlockSpec((1, d), lambda i: (0, 0))
    return pl.pallas_call(
        body, name=name, grid=(l // tl,), in_specs=[row, vec, row, row], out_specs=[row, vec],
        out_shape=[jax.ShapeDtypeStruct((l, d), out_dtype), jax.ShapeDtypeStruct((1, d), F32)],
        compiler_params=_params("arbitrary"),
    )(x, w, dy, add)


def _loss_head(h, w, target, *, name, tl=256):
    l, d = h.shape
    tl = _tile(l, tl, 16)

    def body(x_ref, w_ref, t_ref, loss_ref, dxb_ref, dw_ref):
        i = pl.program_id(0)
        xv = x_ref[...]
        r = lax.rsqrt(jnp.mean(xv * xv, axis=-1, keepdims=True) + NORM_EPS)
        xh = xv * r
        wv = w_ref[...]
        diff = xh * wv - t_ref[...]
        lpart = 0.5 * jnp.sum(jnp.mean(diff * diff, axis=-1, keepdims=True), axis=0, keepdims=True)
        dyv = diff * (1.0 / d)
        dxh = dyv * wv
        dx = r * (dxh - xh * jnp.mean(dxh * xh, axis=-1, keepdims=True))
        dxb_ref[...] = dx.astype(BF16)
        part = jnp.sum(dyv * xh, axis=0, keepdims=True)

        @pl.when(i == 0)
        def _():
            dw_ref[...] = part
            loss_ref[...] = lpart

        @pl.when(i > 0)
        def _():
            dw_ref[...] += part
            loss_ref[...] += lpart

    row = pl.BlockSpec((tl, d), lambda i: (i, 0))
    vec = pl.BlockSpec((1, d), lambda i: (0, 0))
    one = pl.BlockSpec((1, 1), lambda i: (0, 0))
    return pl.pallas_call(
        body, name=name, grid=(l // tl,), in_specs=[row, vec, row], out_specs=[one, row, vec],
        out_shape=[jax.ShapeDtypeStruct((1, 1), F32), jax.ShapeDtypeStruct((l, d), BF16), jax.ShapeDtypeStruct((1, d), F32)],
        compiler_params=_params("arbitrary"),
    )(h, w, target)


def _conv_pre(ext, w_ref):
    acc = ext * w_ref[CONV_K - 1:CONV_K, :]
    for k in range(CONV_K - 1):
        acc = acc + pltpu.roll(ext, CONV_K - 1 - k, axis=0) * w_ref[k:k + 1, :]
    return acc


def _conv_fwd(proj, col0, cw, cb, *, name, tl=512, tc=1024):
    l = proj.shape[0]
    c = cw.shape[1]
    tl, tc = _tile(l, tl, 16), _tile(math.gcd(c, col0), tc, LANE)
    assert col0 % tc == 0
    off, hb = col0 // tc, tl // CONV_HALO

    def body(x_ref, halo_ref, w_ref, b_ref, o_ref):
        i = pl.program_id(0)
        halo = jnp.where(i > 0, halo_ref[...], 0.0)
        ext = jnp.concatenate([halo, x_ref[...]], axis=0)
        pre = _conv_pre(ext, w_ref)[CONV_HALO:] + b_ref[...]
        o_ref[...] = pre * _sigmoid(pre)

    return pl.pallas_call(
        body, name=name, grid=(l // tl, c // tc),
        in_specs=[pl.BlockSpec((tl, tc), lambda i, j: (i, off + j)),
                  pl.BlockSpec((CONV_HALO, tc), lambda i, j: (jnp.maximum(i * hb - 1, 0), off + j)),
                  pl.BlockSpec((CONV_K, tc), lambda i, j: (0, j)), pl.BlockSpec((1, tc), lambda i, j: (0, j))],
        out_specs=pl.BlockSpec((tl, tc), lambda i, j: (i, j)),
        out_shape=jax.ShapeDtypeStruct((l, c), F32), compiler_params=_params("parallel", "parallel"),
    )(proj, proj, cw, cb)


def _conv_bwd(proj, col0, cw, cb, dout, dproj, *, name, tl=512, tc=1024):
    l = proj.shape[0]
    c = dout.shape[1]
    tl, tc = _tile(l, tl, 16), _tile(math.gcd(c, col0), tc, LANE)
    assert col0 % tc == 0
    off, hb, nt = col0 // tc, tl // CONV_HALO, l // tl
    n_ext = tl + CONV_HALO

    def body(x_ref, prev_ref, nxt_ref, d_ref, dn_ref, w_ref, b_ref, dproj_in, dx_ref, dw_ref, db_ref):
        del dproj_in
        i = pl.program_id(1)
        last = i == nt - 1
        halo = jnp.where(i > 0, prev_ref[...], 0.0)
        ext = jnp.concatenate([halo, x_ref[...], nxt_ref[...]], axis=0)
        pre = _conv_pre(ext, w_ref)[CONV_HALO:] + b_ref[...]
        dext = jnp.concatenate([d_ref[...], jnp.where(last, 0.0, dn_ref[...])], axis=0)
        s = _sigmoid(pre)
        dpre = dext * (s * (1.0 + pre * (1.0 - s)))
        dx = dpre * w_ref[CONV_K - 1:CONV_K, :]
        for k in range(CONV_K - 1):
            dx = dx + pltpu.roll(dpre, n_ext - (CONV_K - 1 - k), axis=0) * w_ref[k:k + 1, :]
        dx_ref[...] = dx[:tl].astype(dx_ref.dtype)
        dp = dpre[:tl]
        rows = [jnp.sum(dp * pltpu.roll(ext, CONV_K - 1 - k, axis=0)[CONV_HALO:CONV_HALO + tl], axis=0, keepdims=True)
                for k in range(CONV_K - 1)]
        rows.append(jnp.sum(dp * ext[CONV_HALO:CONV_HALO + tl], axis=0, keepdims=True))
        dwp = jnp.concatenate(rows, axis=0)
        dbp = jnp.sum(dp, axis=0, keepdims=True)

        @pl.when(i == 0)
        def _():
            dw_ref[...] = dwp
            db_ref[...] = dbp

        @pl.when(i > 0)
        def _():
            dw_ref[...] += dwp
            db_ref[...] += dbp

    cur = lambda j, i: (i, j)
    nxt = lambda j, i: (jnp.minimum((i + 1) * hb, nt * hb - 1), j)
    return pl.pallas_call(
        body, name=name, grid=(c // tc, nt),
        in_specs=[pl.BlockSpec((tl, tc), lambda j, i: (i, off + j)),
                  pl.BlockSpec((CONV_HALO, tc), lambda j, i: (jnp.maximum(i * hb - 1, 0), off + j)),
                  pl.BlockSpec((CONV_HALO, tc), lambda j, i: (jnp.minimum((i + 1) * hb, nt * hb - 1), off + j)),
                  pl.BlockSpec((tl, tc), cur), pl.BlockSpec((CONV_HALO, tc), nxt),
                  pl.BlockSpec((CONV_K, tc), lambda j, i: (0, j)), pl.BlockSpec((1, tc), lambda j, i: (0, j)), ANY],
        out_specs=[pl.BlockSpec((tl, tc), lambda j, i: (i, off + j)), pl.BlockSpec((CONV_K, tc), lambda j, i: (0, j)),
                   pl.BlockSpec((1, tc), lambda j, i: (0, j))],
        out_shape=[jax.ShapeDtypeStruct(dproj.shape, dproj.dtype), jax.ShapeDtypeStruct((CONV_K, c), F32),
                   jax.ShapeDtypeStruct((1, c), F32)],
        input_output_aliases={7: 0}, compiler_params=_params("parallel", "arbitrary"),
    )(proj, proj, proj, dout, dout, cw, cb, dproj)


def _pool_fwd(proj, col0, d_pool, *, name, tl=256):
    l = proj.shape[0]
    tl = _tile(l, tl, POOL_HALO)
    assert col0 % d_pool == 0
    off, hb, pg = col0 // d_pool, tl // POOL_HALO, d_pool // GROUPS

    def body(u_ref, halo_ref, o_ref):
        i = pl.program_id(0)
        halo = jnp.where(i > 0, halo_ref[...], 0.0)
        ext = jnp.concatenate([halo, u_ref[...]], axis=0)
        t = i * tl + lax.broadcasted_iota(jnp.int32, (tl, 1), 0)
        s, width = ext, 1
        for gi, win in enumerate(POOL_WINDOWS):
            while width < win:
                s = s + pltpu.roll(s, width, axis=0)
                width *= 2
            cnt = jnp.minimum(t + 1, win).astype(F32)
            sl = slice(gi * pg, (gi + 1) * pg)
            o_ref[:, sl] = (s[POOL_HALO:, sl] / cnt - ext[POOL_HALO:, sl]).astype(o_ref.dtype)

    return pl.pallas_call(
        body, name=name, grid=(l // tl,),
        in_specs=[pl.BlockSpec((tl, d_pool), lambda i: (i, off)),
                  pl.BlockSpec((POOL_HALO, d_pool), lambda i: (jnp.maximum(i * hb - 1, 0), off))],
        out_specs=pl.BlockSpec((tl, d_pool), lambda i: (i, 0)),
        out_shape=jax.ShapeDtypeStruct((l, d_pool), BF16), compiler_params=_params("parallel"),
    )(proj, proj)


def _pool_bwd(dp, dproj, col0, *, name, tl=256):
    l, d_pool = dp.shape
    tl = _tile(l, tl, POOL_HALO)
    assert col0 % d_pool == 0
    off, hb, nt, pg = col0 // d_pool, tl // POOL_HALO, l // tl, d_pool // GROUPS
    n_ext = tl + POOL_HALO

    def body(d_ref, nxt_ref, dproj_in, o_ref):
        del dproj_in
        i = pl.program_id(0)
        ext = jnp.concatenate([d_ref[...], jnp.where(i == nt - 1, 0.0, nxt_ref[...])], axis=0)
        t = i * tl + lax.broadcasted_iota(jnp.int32, (n_ext, 1), 0)
        for gi, win in enumerate(POOL_WINDOWS):
            sl = slice(gi * pg, (gi + 1) * pg)
            dg = ext[:, sl]
            s = dg / jnp.minimum(t + 1, win).astype(F32)
            width = 1
            while width < win:
                s = s + pltpu.roll(s, n_ext - width, axis=0)
                width *= 2
            o_ref[:, sl] = (s[:tl] - dg[:tl]).astype(o_ref.dtype)

    return pl.pallas_call(
        body, name=name, grid=(nt,),
        in_specs=[pl.BlockSpec((tl, d_pool), lambda i: (i, 0)),
                  pl.BlockSpec((POOL_HALO, d_pool), lambda i: (jnp.minimum((i + 1) * hb, nt * hb - 1), 0)), ANY],
        out_specs=pl.BlockSpec((tl, d_pool), lambda i: (i, off)),
        out_shape=jax.ShapeDtypeStruct(dproj.shape, dproj.dtype), input_output_aliases={2: 0},
        compiler_params=_params("parallel"),
    )(dp, dp, dproj)


def _put_heads(src, dst, col0, width, *, name, tl=512):
    l, c = src.shape
    tl = _tile(l, tl, 16)
    assert col0 % width == 0 and width % LANE == 0 and c // HEAD_DIM <= width

    def body(s_ref, dst_in, o_ref):
        del dst_in
        pick = (lax.broadcasted_iota(jnp.int32, (c, width), 0)
                == HEAD_DIM * lax.broadcasted_iota(jnp.int32, (c, width), 1)).astype(BF16)
        hi, lo = _split2(s_ref[...])
        o_ref[...] = (jnp.dot(hi, pick, preferred_element_type=F32)
                      + jnp.dot(lo, pick, preferred_element_type=F32)).astype(o_ref.dtype)

    return pl.pallas_call(
        body, name=name, grid=(l // tl,), in_specs=[pl.BlockSpec((tl, c), lambda i: (i, 0)), ANY],
        out_specs=pl.BlockSpec((tl, width), lambda i: (i, col0 // width)),
        out_shape=jax.ShapeDtypeStruct(dst.shape, dst.dtype), input_output_aliases={1: 0},
        compiler_params=_params("parallel"),
    )(src, dst)


def _pool_mix_fwd(pooled, pw, scale, mixed, col0, *, name, tm=1024):
    l, d_pool = pooled.shape
    pg = d_pool // GROUPS
    tm = _tile(l, tm, 16)
    assert col0 % pg == 0
    off = col0 // pg

    def body(a_ref, w_ref, s_ref, mixed_in, raw_ref, mixed_ref):
        del mixed_in
        raw = _dot(a_ref[...], w_ref[...])
        raw_ref[...] = raw
        mixed_ref[...] = (raw * s_ref[...]).astype(mixed_ref.dtype)

    blk = pl.BlockSpec((tm, pg), lambda i, g: (i, g))
    return pl.pallas_call(
        body, name=name, grid=(l // tm, GROUPS),
        in_specs=[blk, pl.BlockSpec((None, pg, pg), lambda i, g: (g, 0, 0)), pl.BlockSpec((1, pg), lambda i, g: (0, g)),
                  pl.BlockSpec(memory_space=pl.ANY)],
        out_specs=[blk, pl.BlockSpec((tm, pg), lambda i, g: (i, off + g))],
        out_shape=[jax.ShapeDtypeStruct((l, d_pool), F32), jax.ShapeDtypeStruct(mixed.shape, mixed.dtype)],
        input_output_aliases={3: 1}, compiler_params=_params("parallel", "parallel"),
    )(pooled, pw, scale, mixed)


def _pool_mix_bwd(dmixed, col0, raw, scale, pw, *, name, tm=1024):
    l, d_pool = raw.shape
    pg = d_pool // GROUPS
    tm = _tile(l, tm, 16)
    assert col0 % pg == 0
    off = col0 // pg

    def body(d_ref, raw_ref, s_ref, w_ref, draw_ref, dp_ref, ds_ref):
        i = pl.program_id(1)
        dv = d_ref[...]
        draw = (dv * s_ref[...]).astype(BF16)
        draw_ref[...] = draw
        dp_ref[...] = _dot(draw, w_ref[...], tb=True)
        part = jnp.sum(dv * raw_ref[...], axis=0, keepdims=True)

        @pl.when(i == 0)
        def _():
            ds_ref[...] = part

        @pl.when(i > 0)
        def _():
            ds_ref[...] += part

    blk = pl.BlockSpec((tm, pg), lambda g, i: (i, g))
    vec = pl.BlockSpec((1, pg), lambda g, i: (0, g))
    return pl.pallas_call(
        body, name=name, grid=(GROUPS, l // tm),
        in_specs=[pl.BlockSpec((tm, pg), lambda g, i: (i, off + g)), blk, vec,
                  pl.BlockSpec((None, pg, pg), lambda g, i: (g, 0, 0))],
        out_specs=[blk, blk, vec],
        out_shape=[jax.ShapeDtypeStruct((l, d_pool), BF16), jax.ShapeDtypeStruct((l, d_pool), F32),
                   jax.ShapeDtypeStruct((1, d_pool), F32)],
        compiler_params=_params("parallel", "arbitrary"),
    )(dmixed, raw, scale, pw)


def _group_mm_tn(a, b, *, name, tk=1024):
    l, d_pool = a.shape
    pg = d_pool // GROUPS
    tk = _tile(l, tk, 16)

    def body(a_ref, b_ref, o_ref):
        kk = pl.program_id(1)
        part = _dot(a_ref[...], b_ref[...], ta=True)

        @pl.when(kk == 0)
        def _():
            o_ref[...] = part

        @pl.when(kk > 0)
        def _():
            o_ref[...] += part

    blk = pl.BlockSpec((tk, pg), lambda g, kk: (kk, g))
    return pl.pallas_call(
        body, name=name, grid=(GROUPS, l // tk), in_specs=[blk, blk],
        out_specs=pl.BlockSpec((None, pg, pg), lambda g, kk: (g, 0, 0)),
        out_shape=jax.ShapeDtypeStruct((GROUPS, pg, pg), F32), compiler_params=_params("parallel", "arbitrary"),
    )(a, b)


def _split3(v):
    hi = v.astype(BF16)
    r1 = v - hi.astype(F32)
    mid = r1.astype(BF16)
    lo = (r1 - mid.astype(F32)).astype(BF16)
    return hi, mid, lo


def _tri_dot(tri, v):
    hi, mid, lo = _split3(v)
    d = lambda p: jnp.dot(tri, p, preferred_element_type=F32)
    return d(hi) + d(mid) + d(lo)


def _dot_tri(v, tri):
    hi, mid, lo = _split3(v)
    d = lambda p: jnp.dot(p, tri, preferred_element_type=F32)
    return d(hi) + d(mid) + d(lo)


def _split2(v):
    hi = v.astype(BF16)
    return hi, (v - hi.astype(F32)).astype(BF16)


def _scan_specs(hpg, nc, order):
    gw, q = hpg * HEAD_DIM, CHUNK
    b_off, c_off = (GROUPS * gw) // STATE, (GROUPS * gw) // STATE + GROUPS
    xs = pl.BlockSpec((q, gw), lambda g, c: (order(c), g))
    bm = pl.BlockSpec((q, STATE), lambda g, c: (order(c), b_off + g))
    cm = pl.BlockSpec((q, STATE), lambda g, c: (order(c), c_off + g))
    dtt = pl.BlockSpec((None, hpg, q), lambda g, c: (g, 0, order(c)))
    vec = pl.BlockSpec((1, gw), lambda g, c: (0, g))
    hcol = pl.BlockSpec((None, hpg, 1), lambda g, c: (g, 0, 0))
    st = pl.BlockSpec((None, None, STATE, gw), lambda g, c: (g, order(c), 0, 0))
    return xs, bm, cm, dtt, vec, hcol, st


def _scan_common(dtc_ref, dtrt_ref, biasc_ref, biast_ref, alogx_ref, alogt_ref, hpg, want_sigmoid=False):
    q, gw, width = CHUNK, hpg * HEAD_DIM, dtc_ref.shape[1]
    raw = dtc_ref[...] + biasc_ref[...]
    spread = (lax.broadcasted_iota(jnp.int32, (width, gw), 0)
              == hpg * pl.program_id(0) + lax.broadcasted_iota(jnp.int32, (width, gw), 1) // HEAD_DIM).astype(BF16)
    dt = _dot_tri(_softplus(raw), spread)
    sg = None
    if want_sigmoid:
        hi, lo = _split2(_sigmoid(raw))
        sg = jnp.dot(hi, spread, preferred_element_type=F32) + jnp.dot(lo, spread, preferred_element_type=F32)
    a = -jnp.exp(alogx_ref[...])
    dtt = _softplus(dtrt_ref[...] + biast_ref[...])
    at = -jnp.exp(alogt_ref[...])
    row = lax.broadcasted_iota(jnp.int32, (q, q), 0)
    col = lax.broadcasted_iota(jnp.int32, (q, q), 1)
    lower = (row >= col).astype(BF16)
    upper = (row <= col).astype(BF16)
    acum = _tri_dot(lower, dt * a)
    acum_t = _dot_tri(dtt * at, upper)
    return dt, sg, a, acum, acum_t, row, col, upper


def _gated(yv, z):
    s = _sigmoid(z)
    sz = z * s
    gv = yv * sz
    return gv, lax.rsqrt(jnp.mean(gv * gv, axis=-1, keepdims=True) + NORM_EPS), sz, s


def _scan_fwd(xbc, proj, dt_col, dtrt, biasc, biast, alogx, alogt, dskx, wn, d_mix, *, name, ride=None):
    l = xbc.shape[0]
    hpg = dtrt.shape[1]
    gw, q, nc = hpg * HEAD_DIM, CHUNK, l // CHUNK
    assert hpg % 2 == 0
    xs_s, bm_s, cm_s, dtt_s, vec, hcol, st_s = _scan_specs(hpg, nc, lambda c: c)
    width = biasc.shape[1]
    dtc_s = pl.BlockSpec((q, width), lambda g, c: (c, dt_col // width))
    biasc_s = pl.BlockSpec((1, width), lambda g, c: (0, 0))

    def body(xs_ref, b_ref, c_ref, dtc_ref, dtrt_ref, biasc_ref, biast_ref, alogx_ref, alogt_ref, dsk_ref, z_ref, wn_ref,
             y_ref, st_ref, mixed_ref, h_ref):
        @pl.when(pl.program_id(1) == 0)
        def _():
            h_ref[...] = jnp.zeros_like(h_ref)

        dt, _, _, acum, acum_t, row, col, _ = _scan_common(dtc_ref, dtrt_ref, biasc_ref, biast_ref, alogx_ref, alogt_ref, hpg)
        causal = row >= col
        xs = xs_ref[...]
        xdt = xs * dt
        xdtb = xdt.astype(BF16)
        bmat, cmat = b_ref[...], c_ref[...]
        cb = _dot(cmat, bmat, tb=True)
        h_all = h_ref[...]
        st_ref[...] = h_all
        rest = _dot(cmat, h_all) * jnp.exp(acum) + dsk_ref[...] * xs
        first = lax.broadcasted_iota(jnp.int32, (q, 2 * HEAD_DIM), 1) < HEAD_DIM
        parts = []
        for jp in range(hpg // 2):
            pair = slice(2 * jp * HEAD_DIM, (2 * jp + 2) * HEAD_DIM)
            xp = xdtb[:, pair]
            ys = []
            for j in (2 * jp, 2 * jp + 1):
                a_col, a_row = acum[:, j * HEAD_DIM:j * HEAD_DIM + 1], acum_t[j:j + 1, :]
                lm = jnp.exp(jnp.where(causal, a_col - a_row, NEG_BIG))
                ys.append(_dot(cb * lm, xp))
            parts.append(jnp.where(first, ys[0], ys[1]))
        yv = jnp.concatenate(parts, axis=1) + rest
        y_ref[...] = yv
        gv, r, _, _ = _gated(yv, z_ref[...])
        mixed_ref[...] = (gv * r * wn_ref[...]).astype(mixed_ref.dtype)
        a_last = acum[q - 1:q, :]
        h_ref[...] = jnp.exp(a_last) * h_all + _dot(bmat, xdt * jnp.exp(a_last - acum), ta=True)

    return _call(
        body, name=name, grid=(GROUPS, nc),
        in_specs=[xs_s, bm_s, cm_s, dtc_s, dtt_s, biasc_s, hcol, vec, hcol, vec, xs_s, vec], out_specs=[xs_s, st_s, xs_s],
        out_shape=[jax.ShapeDtypeStruct((l, GROUPS * gw), F32), jax.ShapeDtypeStruct((GROUPS, nc, STATE, gw), F32),
                   jax.ShapeDtypeStruct((l, d_mix), BF16)],
        scratch=[pltpu.VMEM((STATE, gw), F32)], sem=("parallel", "arbitrary"),
        args=(xbc, xbc, xbc, proj, dtrt, biasc, biast, alogx, alogt, dskx, proj, wn), ride=ride)


def _scan_bwd(xbc, proj, dt_col, dtrt, biasc, biast, alogx, alogt, dskx, states, dmixed, y, wn, *, name, ride=None):
    l = xbc.shape[0]
    hpg = dtrt.shape[1]
    gw, q, nc = hpg * HEAD_DIM, CHUNK, l // CHUNK
    rev = lambda c: nc - 1 - c
    xs_s, bm_s, cm_s, dtt_s, vec, hcol, st_s = _scan_specs(hpg, nc, rev)
    bc_s = pl.BlockSpec((q, STATE), lambda g, c: (rev(c), g))
    width = biasc.shape[1]
    dtc_s = pl.BlockSpec((q, width), lambda g, c: (rev(c), dt_col // width))
    biasc_s = pl.BlockSpec((1, width), lambda g, c: (0, 0))

    def body(xs_ref, b_ref, c_ref, dtc_ref, dtrt_ref, biasc_ref, biast_ref, alogx_ref, alogt_ref, dsk_ref, st_ref,
             dmix_ref, y_ref, z_ref, wn_ref,
             dxs_ref, db_ref, dc_ref, ddt_ref, dbias_ref, dalog_ref, ddsk_ref, dz_ref, dwn_ref, dh_ref):
        step = pl.program_id(1)

        @pl.when(step == 0)
        def _():
            dh_ref[...] = jnp.zeros_like(dh_ref)

        yv, z, dmix = y_ref[...], z_ref[...], dmix_ref[...]
        gv, rn, sz, sg = _gated(yv, z)
        gh = gv * rn
        dgh = dmix * wn_ref[...]
        dg = rn * (dgh - gh * jnp.mean(dgh * gh, axis=-1, keepdims=True))
        dyv = dg * sz
        dz_ref[...] = (dg * yv * (sg * (1.0 + z * (1.0 - sg)))).astype(dz_ref.dtype)
        p_wn = jnp.sum(dmix * gh, axis=0, keepdims=True)

        dt, sg_dt, a, acum, acum_t, row, col, upper = _scan_common(
            dtc_ref, dtrt_ref, biasc_ref, biast_ref, alogx_ref, alogt_ref, hpg, want_sigmoid=True)
        causal, anti, strict = row >= col, row <= col, row > col
        xs = xs_ref[...]
        xdt = xs * dt
        xdtb, dyb = xdt.astype(BF16), dyv.astype(BF16)
        bmat, cmat = b_ref[...], c_ref[...]
        cb = _dot(cmat, bmat, tb=True)
        cbt = _dot(bmat, cmat, tb=True)
        h_all, dh = st_ref[...], dh_ref[...]
        a_last = acum[q - 1:q, :]
        ea, ee, gam = jnp.exp(acum), jnp.exp(a_last - acum), jnp.exp(a_last)
        zc = _dot(cmat, h_all)
        bdh = _dot(bmat, dh)
        dz = dyv * ea
        xe = xdt * ee
        dc_state = _dot(dz, h_all, tb=True)
        db_state = _dot(xe, dh, tb=True)
        dh_ref[...] = gam * dh + _dot(cmat, dz, ta=True)
        lane = lax.broadcasted_iota(jnp.int32, (q, 2 * HEAD_DIM), 1)
        first = lane < HEAD_DIM
        ones = jnp.ones((q // 2, 2 * HEAD_DIM), BF16)
        dcb = jnp.zeros((q, q), F32)
        dxdt_parts, da_parts = [], []
        for jp in range(hpg // 2):
            pair = slice(2 * jp * HEAD_DIM, (2 * jp + 2) * HEAD_DIM)
            xp, dyp = xdtb[:, pair], dyb[:, pair]
            dxs_r, das_r = [], []
            for r, j in enumerate((2 * jp, 2 * jp + 1)):
                seg = acum[:, j * HEAD_DIM:j * HEAD_DIM + 1] - acum_t[j:j + 1, :]
                lm = jnp.exp(jnp.where(causal, seg, NEG_BIG))
                lmt = jnp.exp(jnp.where(anti, -seg, NEG_BIG))
                mm, mmt = cb * lm, cbt * lmt
                keep = first if r == 0 else jnp.logical_not(first)
                dyk = jnp.where(keep, dyp, 0)
                dxs_r.append(_dot(mmt, dyp))
                dm = _dot(dyk, xp, tb=True)
                dcb = dcb + dm * lm
                corner = jnp.dot(upper, (dm * mm).astype(BF16), preferred_element_type=F32)
                corner = jnp.where(strict, corner, 0.0)
                das_r.append(_dot(corner[:, :q // 2] + corner[:, q // 2:], ones))
            dxdt_parts.append(jnp.where(first, dxs_r[0], dxs_r[1]))
            da_parts.append(jnp.where(first, das_r[0], das_r[1]))
        dxdt = jnp.concatenate(dxdt_parts, axis=1) + bdh * ee
        da_intra = jnp.concatenate(da_parts, axis=1)
        dxs_ref[...] = dxdt * dt + dsk_ref[...] * dyv
        dc_ref[...] = dc_state + _dot(dcb, bmat)
        db_ref[...] = db_state + _dot(dcb, cmat, ta=True)
        ri = lax.broadcasted_iota(jnp.int32, (gw, gw), 0) // HEAD_DIM
        ci = lax.broadcasted_iota(jnp.int32, (gw, gw), 1) // HEAD_DIM
        blockdiag = (ri == ci).astype(BF16)

        def head_sum(v):
            hi, lo = _split2(v)
            return (jnp.dot(hi, blockdiag, preferred_element_type=F32)
                    + jnp.dot(lo, blockdiag, preferred_element_type=F32))

        xt = xe * bdh
        small = jnp.concatenate([
            jnp.sum(xt, axis=0, keepdims=True) + gam * jnp.sum(dh * h_all, axis=0, keepdims=True),
            jnp.sum(dyv * xs, axis=0, keepdims=True), jnp.zeros((6, gw), F32)], axis=0)
        small = head_sum(small)
        rows = lax.broadcasted_iota(jnp.int32, (q, 1), 0)
        da_local = head_sum(dyv * zc * ea - xt) + jnp.where(rows == q - 1, small[0:1, :], 0.0)
        hi, lo = _split2(da_local)
        d_dta = (da_intra + jnp.dot(upper, hi, preferred_element_type=F32)
                 + jnp.dot(upper, lo, preferred_element_type=F32))
        d_raw = (d_dta * a + head_sum(dxdt * xs)) * sg_dt
        ddt_ref[...] = d_raw
        p_bias = jnp.sum(d_raw, axis=0, keepdims=True)
        p_alog = jnp.sum(d_dta * dt, axis=0, keepdims=True) * a
        p_dsk = small[1:2, :]

        @pl.when(step == 0)
        def _():
            dbias_ref[...] = p_bias
            dalog_ref[...] = p_alog
            ddsk_ref[...] = p_dsk
            dwn_ref[...] = p_wn

        @pl.when(step > 0)
        def _():
            dbias_ref[...] += p_bias
            dalog_ref[...] += p_alog
            ddsk_ref[...] += p_dsk
            dwn_ref[...] += p_wn

    wide = jax.ShapeDtypeStruct((l, GROUPS * gw), F32)
    narrow = jax.ShapeDtypeStruct((l, GROUPS * STATE), F32)
    vshape = jax.ShapeDtypeStruct((1, GROUPS * gw), F32)
    return _call(
        body, name=name, grid=(GROUPS, nc),
        in_specs=[xs_s, bm_s, cm_s, dtc_s, dtt_s, biasc_s, hcol, vec, hcol, vec, st_s, xs_s, xs_s, xs_s, vec],
        out_specs=[xs_s, bc_s, bc_s, xs_s, vec, vec, vec, xs_s, vec],
        out_shape=[wide, narrow, narrow, wide, vshape, vshape, vshape, jax.ShapeDtypeStruct(proj.shape, BF16), vshape],
        scratch=[pltpu.VMEM((STATE, gw), F32)], sem=("parallel", "arbitrary"),
        args=(xbc, xbc, xbc, proj, dtrt, biasc, biast, alogx, alogt, dskx, states, dmixed, y, proj, wn), ride=ride)


def _exchange(srcs, scatter, *, name):
    ride = _Ride(srcs, scatter)

    def body(*refs):
        src, dst, sems = refs[:ride.n], refs[ride.n:2 * ride.n], refs[2 * ride.n:]
        ride.start(src, dst, sems)
        ride.relay(src, dst, sems)
        ride.finish(src, dst, sems)

    return pl.pallas_call(body, name=name, in_specs=ride.in_specs, out_specs=ride.out_specs, out_shape=ride.out_shape,
                          scratch_shapes=ride.scratch)(*srcs)


def _adamw(parts, w, m, v, *, name, tr=256):
    lead = w.ndim == 3
    r, c = w.shape[-2:]
    tr = _tile(r, tr, 16)
    c1, c2 = 1.0 / (1.0 - ADAM_B1 ** ADAM_STEP), 1.0 / (1.0 - ADAM_B2 ** ADAM_STEP)

    def body(p_ref, w_ref, m_ref, v_ref, g_ref, d_ref, nm_ref, nv_ref):
        g = p_ref[0].astype(F32)
        for p in range(1, N_DEV):
            g = g + p_ref[p].astype(F32)
        nm = ADAM_B1 * m_ref[...] + (1.0 - ADAM_B1) * g
        nv = ADAM_B2 * v_ref[...] + (1.0 - ADAM_B2) * (g * g)
        g_ref[...] = g
        nm_ref[...] = nm
        nv_ref[...] = nv
        d_ref[...] = -ADAM_LR * ((nm * c1) / (jnp.sqrt(nv * c2) + ADAM_EPS) + ADAM_WD * w_ref[...])

    blk = pl.BlockSpec((None, tr, c), lambda i: (0, i, 0)) if lead else pl.BlockSpec((tr, c), lambda i: (i, 0))
    out = jax.ShapeDtypeStruct(w.shape, F32)
    return pl.pallas_call(
        body, name=name, grid=(r // tr,), in_specs=[pl.BlockSpec((N_DEV, tr, c), lambda i: (0, i, 0)), blk, blk, blk],
        out_specs=[blk, blk, blk, blk], out_shape=[out, out, out, out], compiler_params=_params("parallel"),
    )(parts, w, m, v)


def kernel(x, attn_norm_w, w_in, conv_w, conv_b, dt_bias, a_log, d_skip, ssd_norm_w, pool_w, pool_scale, w_out, ffn_norm_w, w_gate, w_up, w_down, final_norm_w, loss_target, m_attn_norm_w, m_w_in, m_conv_w, m_conv_b, m_dt_bias, m_a_log, m_d_skip, m_ssd_norm_w, m_pool_w, m_pool_scale, m_w_out, m_ffn_norm_w, m_w_gate, m_w_up, m_w_down, m_final_norm_w, v_attn_norm_w, v_w_in, v_conv_w, v_conv_b, v_dt_bias, v_a_log, v_d_skip, v_ssd_norm_w, v_pool_w, v_pool_scale, v_w_out, v_ffn_norm_w, v_w_gate, v_w_up, v_w_down, v_final_norm_w):
    l, d = x.shape[1], x.shape[2]
    heads = dt_bias.shape[1]
    hpg = heads // GROUPS
    d_ssm = heads * HEAD_DIM
    conv_ch = conv_b.shape[1]
    d_pool = pool_scale.shape[1]
    pg = d_pool // GROUPS
    d_mix = d_ssm + d_pool
    d_ff = w_gate.shape[2] * N_DEV
    d_in = w_in.shape[2] * N_DEV
    dt_pad = -(-heads // LANE) * LANE
    o_u, o_xbc, o_dt = d_ssm, d_ssm + d_pool, d_ssm + d_pool + conv_ch
    d_inp = o_dt + dt_pad
    tn_in = _tile(d_inp, 2560, LANE)
    g_dt, g_u = d_ssm + conv_ch, d_ssm + conv_ch + heads

    x2, tgt = x[0], loss_target[0]

    tr_ = lambda t: jnp.transpose(t, (0, 2, 1))
    c_in = d_in // N_DEV
    h0, gi, gp, gc = _rms_fwd(x2, attn_norm_w, name="attn_norm",
                              ride=_Ride([tr_(w_in)[0].astype(BF16), pool_w[0].astype(BF16), conv_w[0]], False))
    win = gi.reshape(d_in, d)
    wp = jnp.concatenate([win[:d_ssm], win[g_u:], win[d_ssm:g_dt], win[g_dt:g_u],
                          jnp.zeros((dt_pad - heads, d), BF16)], axis=0)
    pw = jnp.transpose(gp, (1, 0, 2, 3)).reshape(GROUPS, pg, pg)
    cw = jnp.transpose(gc, (1, 0, 2)).reshape(CONV_K, conv_ch)

    def expand(vec):
        return jnp.repeat(vec, HEAD_DIM, axis=1)

    def per_head(vec):
        return vec[:, ::HEAD_DIM]

    alog_x, dsk_x = expand(a_log), expand(d_skip)
    bias_w = jnp.pad(dt_bias, ((0, 0), (0, dt_pad - heads)))
    bias_c, alog_c = dt_bias.reshape(GROUPS, hpg, 1), a_log.reshape(GROUPS, hpg, 1)

    proj, go, gg = _mm(h0, wp, name="in_proj", tb=True, tm=512, tn=tn_in, tk=d,
                       ride=_Ride([w_out[0].astype(BF16), tr_(w_gate)[0].astype(BF16)], False))
    wo = go.reshape(d_mix, d)
    xbc = _conv_fwd(proj, o_xbc, cw, conv_b, name="conv_fwd")
    dt_raw = proj[:, o_dt:o_dt + heads]
    dtrt = jnp.transpose(dt_raw.reshape(l, GROUPS, hpg), (1, 2, 0))
    y, states, mixed, gu = _scan_fwd(xbc, proj, o_dt, dtrt, bias_w, bias_c, alog_x, alog_c, dsk_x, ssd_norm_w, d_mix,
                                     name="ssd_fwd", ride=_Ride([tr_(w_up)[0].astype(BF16)], False))
    pooled = _pool_fwd(proj, o_u, d_pool, name="pool_fwd")
    pool_raw, mixed = _pool_mix_fwd(pooled, pw, pool_scale, mixed, d_ssm, name="pool_mix")
    h1 = _mm(mixed, wo, name="out_proj", tm=512, tk=d_mix, residual=x2)
    h1n = _rms_fwd(h1, ffn_norm_w, name="ffn_norm")
    wide = lambda t: t.reshape(N_DEV // FFN_GROUP, -1, d)
    gg, gu = wide(gg), wide(gu)
    gate, up, act, gd = _gate_up(h1n, gg, gu, name="gate_up", ride=_Ride([w_down[0].astype(BF16)], False))
    gd = wide(gd)
    h2 = _mm_shards([(act, gd)], name="down_proj", per_step=1, residual=h1)

    loss11, dh2b, g_final = _loss_head(h2, final_norm_w.reshape(1, d), tgt, name="loss_head")
    dgate, dup = _gate_up_bwd(dh2b, gd, gate, up, name="gate_up_bwd")
    shards = lambda t: t.reshape(N_DEV, -1, d)
    g_wd = shards(_grad_shards(act, dh2b, name="grad_w_down"))
    dh1n, r_wd = _mm_shards([(dgate, gg), (dup, gu)], name="dgate_dup", out_dtype=BF16, tm=512, per_step=1,
                            ride=_Ride([g_wd], True))
    g_wg = shards(_grad_shards(dgate, h1n, name="grad_w_gate"))
    g_wu = shards(_grad_shards(dup, h1n, name="grad_w_up"))
    dh1b, g_ffn = _rms_bwd(h1, ffn_norm_w, dh1n, dh2b, BF16, name="ffn_norm_bwd")
    dmixed = _mm(dh1b, wo, name="dmixed", tb=True, tk=d)
    g_wo = _mm(mixed, dh1b, name="grad_w_out", ta=True, out_dtype=BF16, tm=1024, tn=1024, tk=2048)
    draw, dpooled, g_pscale = _pool_mix_bwd(dmixed, d_ssm, pool_raw, pool_scale, pw, name="pool_mix_bwd")
    g_pw = _group_mm_tn(pooled, draw, name="grad_pool_w")
    s_pw = jnp.transpose(g_pw.reshape(GROUPS, N_DEV, pg // N_DEV, pg), (1, 0, 2, 3)).astype(BF16)
    dxs, dbm, dcm, ddtx, g_bias, g_alog, g_dsk, dproj, g_ssdn, r_wg, r_wu, r_wo, r_pw = _scan_bwd(
        xbc, proj, o_dt, dtrt, bias_w, bias_c, alog_x, alog_c, dsk_x, states, dmixed, y, ssd_norm_w, name="ssd_bwd",
        ride=_Ride([g_wg, g_wu, g_wo.reshape(N_DEV, d_mix // N_DEV, d), s_pw], True))
    dproj = _pool_bwd(dpooled, dproj, o_u, name="pool_bwd")
    n_bc = GROUPS * STATE
    segs = [(dxs, 0, d_ssm), (dbm, d_ssm, n_bc), (dcm, d_ssm + n_bc, n_bc)]
    g_cw, g_cb = [], []
    for si, (dseg, c0, width) in enumerate(segs):
        dproj, b, c_ = _conv_bwd(proj, o_xbc + c0, cw[:, c0:c0 + width], conv_b[:, c0:c0 + width], dseg, dproj,
                                 name=f"conv_bwd{si}")
        g_cw.append(b)
        g_cb.append(c_)
    dproj = _put_heads(ddtx, dproj, o_dt, dt_pad, name="put_ddt")
    g_wp = _mm(dproj, h0, name="grad_w_in", ta=True, out_dtype=BF16, tm=384, tn=d, tk=2048)
    g_win = jnp.concatenate([g_wp[:d_ssm], g_wp[o_xbc:o_dt + heads], g_wp[o_u:o_xbc]], axis=0).reshape(N_DEV, c_in, d)
    def cols8(g):
        return jnp.transpose(g.reshape(g.shape[0], N_DEV, g.shape[1] // N_DEV), (1, 0, 2))
    dh0, r_in, r_cw = _mm(dproj, wp, name="dh0", out_dtype=BF16, tk=tn_in,
                          ride=_Ride([g_win, cols8(jnp.concatenate(g_cw, axis=1))], True))
    dx, g_attn = _rms_bwd(x2, attn_norm_w, dh0, dh1b, F32, name="attn_norm_bwd")

    smalls = [g_attn, jnp.concatenate(g_cb, axis=1), per_head(g_bias), per_head(g_alog), per_head(g_dsk), g_ssdn,
              g_pscale, g_ffn, g_final]
    small_w = [attn_norm_w, conv_b, dt_bias, a_log, d_skip, ssd_norm_w, pool_scale, ffn_norm_w, final_norm_w.reshape(1, d)]
    small_m = [m_attn_norm_w, m_conv_b, m_dt_bias, m_a_log, m_d_skip, m_ssd_norm_w, m_pool_scale, m_ffn_norm_w,
               m_final_norm_w.reshape(1, d)]
    small_v = [v_attn_norm_w, v_conv_b, v_dt_bias, v_a_log, v_d_skip, v_ssd_norm_w, v_pool_scale, v_ffn_norm_w,
               v_final_norm_w.reshape(1, d)]
    sizes = [s.shape[1] for s in smalls]
    n_small = sum(sizes)
    n_pad = -(-n_small // (16 * LANE)) * (16 * LANE)
    rows = n_pad // LANE
    def pack(vs):
        return jnp.pad(jnp.concatenate(vs, axis=1), ((0, 0), (0, n_pad - n_small))).reshape(rows, LANE)
    (r_small,) = _exchange([pack(smalls)], False, name="gather_small_grads")

    def big(parts, w, m, v, nm):
        shp = w.shape
        if w.ndim == 3:
            return _adamw(parts, w, m, v, name=nm)
        r2 = lambda t: t.reshape(-1, shp[-1])
        outs = _adamw(parts.reshape(N_DEV, -1, shp[-1]), r2(w), r2(m), r2(v), name=nm)
        return [o.reshape(shp) for o in outs]

    def big_t(parts, w, m, v, nm):
        return [tr_(o) for o in _adamw(parts, tr_(w), tr_(m), tr_(v), name=nm)]

    def big_flat(parts, w, m, v, nm):
        fl = lambda t: tr_(t).reshape(-1, LANE)
        outs = _adamw(parts.reshape(N_DEV, -1, LANE), fl(w), fl(m), fl(v), name=nm, tr=2048)
        return [tr_(o.reshape(1, w.shape[2], w.shape[1])) for o in outs]

    res = {
        "w_in": big_flat(r_in, w_in, m_w_in, v_w_in, "adamw_w_in"),
        "conv_w": big(r_cw, conv_w, m_conv_w, v_conv_w, "adamw_conv_w"),
        "pool_w": big(r_pw, pool_w, m_pool_w, v_pool_w, "adamw_pool_w"),
        "w_out": big(r_wo, w_out, m_w_out, v_w_out, "adamw_w_out"),
        "w_gate": big_t(r_wg, w_gate, m_w_gate, v_w_gate, "adamw_w_gate"),
        "w_up": big_t(r_wu, w_up, m_w_up, v_w_up, "adamw_w_up"),
        "w_down": big(r_wd, w_down, m_w_down, v_w_down, "adamw_w_down"),
    }
    s_out = _adamw(r_small, pack(small_w), pack(small_m), pack(small_v), name="adamw_small")
    names = ["attn_norm_w", "conv_b", "dt_bias", "a_log", "d_skip", "ssd_norm_w", "pool_scale", "ffn_norm_w", "final_norm_w"]
    offs = [sum(sizes[:i]) for i in range(len(sizes))]
    for i, nm in enumerate(names):
        shp = (d,) if nm == "final_norm_w" else (1, sizes[i])
        res[nm] = [o.reshape(1, n_pad)[:, offs[i]:offs[i] + sizes[i]].reshape(shp) for o in s_out]

    loss = lax.psum(loss11[0, 0], ("x", "y", "c"))
    order = ["attn_norm_w", "w_in", "conv_w", "conv_b", "dt_bias", "a_log", "d_skip", "ssd_norm_w", "pool_w", "pool_scale",
             "w_out", "ffn_norm_w", "w_gate", "w_up", "w_down", "final_norm_w"]
    outs = [loss, dx.reshape(x.shape)]
    for part in range(4):
        outs += [res[nm][part] for nm in order]
    return tuple(outs)
```

```python
import functools
import math

import jax
import jax.numpy as jnp
from jax import lax
from jax.experimental import pallas as pl
from jax.experimental.pallas import tpu as pltpu

F32 = jnp.float32
BF16 = jnp.bfloat16

NORM_EPS = 1e-5
HEAD_DIM = 64
STATE = 128
CHUNK = 256
GROUPS = 4
CONV_K = 4
POOL_WINDOWS = (2, 4, 8, 16)
POOL_HALO = 16
CONV_HALO = 8
LANE = 128
N_DEV = 8
ADAM_LR, ADAM_B1, ADAM_B2, ADAM_EPS, ADAM_WD, ADAM_STEP = 0.001, 0.9, 0.999, 1e-08, 0.01, 10
NEG_BIG = -1e30
FFN_GROUP = 4
MESH_ID = pl.DeviceIdType.MESH


def _tile(n, pref, mult):
    if n <= pref:
        return n
    t = (pref // mult) * mult
    while t >= mult:
        if n % t == 0:
            return t
        t -= mult
    return n


def _params(*sem):
    return pltpu.CompilerParams(dimension_semantics=sem)


ANY = pl.BlockSpec(memory_space=pl.ANY)


def _peer(k):
    x, y, c = lax.axis_index("x"), lax.axis_index("y"), lax.axis_index("c")
    return (1 - x if k & 4 else x, 1 - y if k & 2 else y, 1 - c if k & 1 else c)


def _my_index():
    return 4 * lax.axis_index("x") + 2 * lax.axis_index("y") + lax.axis_index("c")


class _Ride:
    def __init__(self, srcs, scatter):
        self.srcs, self.scatter, self.n = list(srcs), scatter, len(srcs)
        self.two_level = not scatter
        self.in_specs = [ANY] * self.n
        self.out_specs = [ANY] * self.n
        self.out_shape = [jax.ShapeDtypeStruct((N_DEV,) + tuple(s.shape[1:] if scatter else s.shape), s.dtype)
                          for s in self.srcs]
        self.scratch = [pltpu.SemaphoreType.DMA((self.n, N_DEV - 1)), pltpu.SemaphoreType.DMA((self.n, N_DEV - 1)),
                        pltpu.SemaphoreType.DMA((self.n,))]

    def _copies(self, src, dst, sems):
        send_sems, recv_sems, local_sems = sems
        me = _my_index()
        local = [pltpu.make_async_copy(src[t].at[me] if self.scatter else src[t], dst[t].at[me], local_sems.at[t])
                 for t in range(self.n)]
        remote = {}
        for k in range(1, N_DEV):
            peer, pidx = _peer(k), me ^ k
            remote[k] = [pltpu.make_async_remote_copy(
                src_ref=src[t].at[pidx] if self.scatter else src[t], dst_ref=dst[t].at[me],
                send_sem=send_sems.at[t, k - 1], recv_sem=recv_sems.at[t, k - 1],
                device_id=peer, device_id_type=MESH_ID) for t in range(self.n)]
        return local, remote

    def _forwards(self, dst, sems):
        send_sems, recv_sems, _ = sems
        me = _my_index()
        return {k: [pltpu.make_async_remote_copy(
            src_ref=dst[t].at[me ^ k], dst_ref=dst[t].at[me ^ k],
            send_sem=send_sems.at[t, (k ^ 1) - 1], recv_sem=recv_sems.at[t, (k ^ 1) - 1],
            device_id=_peer(1), device_id_type=MESH_ID) for t in range(self.n)] for k in (2, 4, 6)}

    def start(self, src, dst, sems):
        local, remote = self._copies(src, dst, sems)
        for cp in local:
            cp.start()
        for k in (2, 4, 6, 1) if self.two_level else range(1, N_DEV):
            for cp in remote[k]:
                cp.start()

    def relay(self, src, dst, sems):
        if not self.two_level:
            return
        _, remote = self._copies(src, dst, sems)
        fwd = self._forwards(dst, sems)
        for k in (2, 4, 6):
            for t in range(self.n):
                remote[k][t].wait_recv()
                fwd[k][t].start()

    def finish(self, src, dst, sems):
        local, remote = self._copies(src, dst, sems)
        if self.two_level:
            fwd = self._forwards(dst, sems)
            for k in (1, 3, 5, 7):
                for cp in remote[k]:
                    cp.wait_recv()
            for k in (1, 2, 4, 6):
                for cp in remote[k]:
                    cp.wait_send()
            for k in (2, 4, 6):
                for cp in fwd[k]:
                    cp.wait_send()
        else:
            for k in range(1, N_DEV):
                for cp in remote[k]:
                    cp.wait_send()
            for k in range(1, N_DEV):
                for cp in remote[k]:
                    cp.wait_recv()
        for lc in local:
            lc.wait()


RELAY_AT = 0.75


def _call(body, *, name, grid, in_specs, out_specs, out_shape, args, sem, scratch=(), ride=None):
    in_specs, out_specs, out_shape, scratch = list(in_specs), list(out_specs), list(out_shape), list(scratch)
    if ride is None:
        return pl.pallas_call(body, name=name, grid=grid, in_specs=in_specs, out_specs=out_specs, out_shape=out_shape,
                              scratch_shapes=scratch, compiler_params=_params(*sem))(*args)
    n_in, n_out, n_sc, nr = len(in_specs), len(out_specs), len(scratch), ride.n

    def full(*refs):
        ins, csrc = refs[:n_in], refs[n_in:n_in + nr]
        o0 = n_in + nr
        outs, cdst = refs[o0:o0 + n_out], refs[o0 + n_out:o0 + n_out + nr]
        s0 = o0 + n_out + nr
        sc, csem = refs[s0:s0 + n_sc], refs[s0 + n_sc:]
        ids = [pl.program_id(ax) for ax in range(len(grid))]
        step = functools.reduce(lambda acc, ig: acc * ig[1] + ig[0], zip(ids, grid), 0)
        total = functools.reduce(lambda p, g: p * g, grid, 1)

        @pl.when(step == 0)
        def _():
            ride.start(csrc, cdst, csem)

        body(*ins, *outs, *sc)

        @pl.when(step == min(int(total * RELAY_AT), total - 1))
        def _():
            ride.relay(csrc, cdst, csem)

        @pl.when(step == total - 1)
        def _():
            ride.finish(csrc, cdst, csem)

    return pl.pallas_call(
        full, name=name, grid=grid, in_specs=in_specs + ride.in_specs, out_specs=out_specs + ride.out_specs,
        out_shape=out_shape + ride.out_shape, scratch_shapes=scratch + ride.scratch,
        compiler_params=_params(*(["arbitrary"] * len(grid))),
    )(*args, *ride.srcs)


def _sigmoid(x):
    return 1.0 / (1.0 + jnp.exp(-x))


def _softplus(x):
    return jnp.maximum(x, 0.0) + jnp.log(1.0 + jnp.exp(-jnp.abs(x)))


def _dot(a, b, ta=False, tb=False):
    dn = (((0 if ta else 1,), (1 if tb else 0,)), ((), ()))
    return lax.dot_general(a.astype(BF16), b.astype(BF16), dn, preferred_element_type=F32)


def _mm(a, b, *, name, ta=False, tb=False, out_dtype=F32, tm=1024, tn=1024, tk=2048, residual=None, ride=None):
    m = a.shape[1] if ta else a.shape[0]
    k = a.shape[0] if ta else a.shape[1]
    n = b.shape[0] if tb else b.shape[1]
    assert k == (b.shape[1] if tb else b.shape[0])
    tm, tn = _tile(m, tm, LANE if ta else 8), _tile(n, tn, LANE)
    tk = _tile(k, tk, LANE if not (ta and tb) else LANE)
    nk = k // tk
    has_res = residual is not None

    def body(*refs):
        a_ref, b_ref = refs[0], refs[1]
        r_ref = refs[2] if has_res else None
        o_ref = refs[3] if has_res else refs[2]
        part = _dot(a_ref[...], b_ref[...], ta, tb)

        def finish(acc):
            if has_res:
                acc = acc + r_ref[...]
            o_ref[...] = acc.astype(o_ref.dtype)

        if nk == 1:
            finish(part)
        else:
            acc_ref = refs[-1]
            kk = pl.program_id(2)

            @pl.when(kk == 0)
            def _():
                acc_ref[...] = part

            @pl.when(kk > 0)
            def _():
                acc_ref[...] += part

            @pl.when(kk == nk - 1)
            def _():
                finish(acc_ref[...])

    a_spec = pl.BlockSpec((tk, tm), lambda i, j, kk: (kk, i)) if ta else pl.BlockSpec((tm, tk), lambda i, j, kk: (i, kk))
    b_spec = pl.BlockSpec((tn, tk), lambda i, j, kk: (j, kk)) if tb else pl.BlockSpec((tk, tn), lambda i, j, kk: (kk, j))
    o_spec = pl.BlockSpec((tm, tn), lambda i, j, kk: (i, j))
    in_specs = [a_spec, b_spec] + ([o_spec] if has_res else [])
    args = (a, b) + ((residual,) if has_res else ())
    res = _call(body, name=name, grid=(m // tm, n // tn, nk), in_specs=in_specs, out_specs=[o_spec],
                out_shape=[jax.ShapeDtypeStruct((m, n), out_dtype)], args=args,
                scratch=[pltpu.VMEM((tm, tn), F32)] if nk > 1 else [], sem=("parallel", "parallel", "arbitrary"), ride=ride)
    return res[0] if ride is None else res


def _gate_up(h, wg, wu, *, name, tm=256, ride=None):
    m, k = h.shape
    ns, c, _ = wg.shape
    tm = _tile(m, tm, 16)

    def body(h_ref, wg_ref, wu_ref, g_ref, u_ref, a_ref):
        hb = h_ref[...]
        g = _dot(hb, wg_ref[...], tb=True)
        u = _dot(hb, wu_ref[...], tb=True)
        g_ref[...] = g
        u_ref[...] = u
        a_ref[...] = (g * _sigmoid(g) * u).astype(a_ref.dtype)

    o_spec = pl.BlockSpec((None, tm, c), lambda s, i: (s, i, 0))
    w_spec = pl.BlockSpec((None, c, k), lambda s, i: (s, 0, 0), pipeline_mode=pl.Buffered(1))
    return _call(body, name=name, grid=(ns, m // tm),
                 in_specs=[pl.BlockSpec((tm, k), lambda s, i: (i, 0)), w_spec, w_spec], out_specs=[o_spec, o_spec, o_spec],
                 out_shape=[jax.ShapeDtypeStruct((ns, m, c), F32), jax.ShapeDtypeStruct((ns, m, c), F32),
                            jax.ShapeDtypeStruct((ns, m, c), BF16)],
                 args=(h, wg, wu), sem=("parallel", "parallel"), ride=ride)


def _gate_up_bwd(dh, wd, g, u, *, name, tm=256):
    m, k = dh.shape
    ns, c, _ = wd.shape
    tm = _tile(m, tm, 16)

    def body(dh_ref, wd_ref, g_ref, u_ref, dg_ref, du_ref):
        da = _dot(dh_ref[...], wd_ref[...], tb=True)
        g = g_ref[...]
        s = _sigmoid(g)
        dg_ref[...] = (da * u_ref[...] * (s * (1.0 + g * (1.0 - s)))).astype(dg_ref.dtype)
        du_ref[...] = (da * (g * s)).astype(du_ref.dtype)

    o_spec = pl.BlockSpec((None, tm, c), lambda s, i: (s, i, 0))
    w_spec = pl.BlockSpec((None, c, k), lambda s, i: (s, 0, 0), pipeline_mode=pl.Buffered(1))
    return _call(body, name=name, grid=(ns, m // tm),
                 in_specs=[pl.BlockSpec((tm, k), lambda s, i: (i, 0)), w_spec, o_spec, o_spec],
                 out_specs=[o_spec, o_spec],
                 out_shape=[jax.ShapeDtypeStruct((ns, m, c), BF16), jax.ShapeDtypeStruct((ns, m, c), BF16)],
                 args=(dh, wd, g, u), sem=("parallel", "parallel"))


def _mm_shards(pairs, *, name, tb=False, residual=None, out_dtype=F32, tm=1024, tn=1024, per_step=2, ride=None):
    ns, m, c = pairs[0][0].shape
    n = pairs[0][1].shape[1] if tb else pairs[0][1].shape[2]
    tm, tn = _tile(m, tm, 16), _tile(n, tn, LANE)
    npair = len(pairs)
    has_res = residual is not None
    assert ns % per_step == 0
    nsteps = ns // per_step

    def body(*refs):
        o_ref, acc_ref = refs[-2], refs[-1]
        s = pl.program_id(2)
        part = None
        for p in range(npair):
            for q in range(per_step):
                term = _dot(refs[2 * p][q], refs[2 * p + 1][q], tb=tb)
                part = term if part is None else part + term

        @pl.when(s == 0)
        def _():
            acc_ref[...] = part

        @pl.when(s > 0)
        def _():
            acc_ref[...] += part

        @pl.when(s == nsteps - 1)
        def _():
            acc = acc_ref[...]
            if has_res:
                acc = acc + refs[2 * npair][...]
            o_ref[...] = acc.astype(o_ref.dtype)

    a_spec = pl.BlockSpec((per_step, tm, c), lambda i, j, s: (s, i, 0))
    b_spec = (pl.BlockSpec((per_step, tn, c), lambda i, j, s: (s, j, 0)) if tb
              else pl.BlockSpec((per_step, c, tn), lambda i, j, s: (s, 0, j)))
    o_spec = pl.BlockSpec((tm, tn), lambda i, j, s: (i, j))
    args = [t for pr in pairs for t in pr] + ([residual] if has_res else [])
    res = _call(body, name=name, grid=(m // tm, n // tn, nsteps),
                in_specs=[a_spec, b_spec] * npair + ([o_spec] if has_res else []), out_specs=[o_spec],
                out_shape=[jax.ShapeDtypeStruct((m, n), out_dtype)], args=args, scratch=[pltpu.VMEM((tm, tn), F32)],
                sem=("parallel", "parallel", "arbitrary"), ride=ride)
    return res[0] if ride is None else res


def _grad_shards(a, b, *, name, tk=2048, tc=1408, tn=1024):
    ns, l, c = a.shape
    d = b.shape[1]
    tk, tc, tn = _tile(l, tk, 16), _tile(c, tc, LANE), _tile(d, tn, LANE)
    nk = l // tk

    def body(a_ref, b_ref, o_ref, acc_ref):
        kk = pl.program_id(3)
        part = _dot(a_ref[...], b_ref[...], ta=True)

        @pl.when(kk == 0)
        def _():
            acc_ref[...] = part

        @pl.when(kk > 0)
        def _():
            acc_ref[...] += part

        @pl.when(kk == nk - 1)
        def _():
            o_ref[...] = acc_ref[...].astype(o_ref.dtype)

    return pl.pallas_call(
        body, name=name, grid=(ns, c // tc, d // tn, nk),
        in_specs=[pl.BlockSpec((None, tk, tc), lambda s, i, j, kk: (s, kk, i)),
                  pl.BlockSpec((tk, tn), lambda s, i, j, kk: (kk, j))],
        out_specs=pl.BlockSpec((None, tc, tn), lambda s, i, j, kk: (s, i, j)),
        out_shape=jax.ShapeDtypeStruct((ns, c, d), BF16), scratch_shapes=[pltpu.VMEM((tc, tn), F32)],
        compiler_params=_params("parallel", "parallel", "parallel", "arbitrary"),
    )(a, b)


def _rms_fwd(x, w, *, name, tl=512, ride=None):
    l, d = x.shape
    tl = _tile(l, tl, 16)

    def body(x_ref, w_ref, o_ref):
        xv = x_ref[...]
        r = lax.rsqrt(jnp.mean(xv * xv, axis=-1, keepdims=True) + NORM_EPS)
        o_ref[...] = (xv * r * w_ref[...]).astype(o_ref.dtype)

    row = pl.BlockSpec((tl, d), lambda i: (i, 0))
    res = _call(body, name=name, grid=(l // tl,), in_specs=[row, pl.BlockSpec((1, d), lambda i: (0, 0))],
                out_specs=[row], out_shape=[jax.ShapeDtypeStruct((l, d), BF16)], args=(x, w), sem=("parallel",), ride=ride)
    return res[0] if ride is None else res


def _rms_bwd(x, w, dy, add, out_dtype, *, name, tl=256):
    l, d = x.shape
    tl = _tile(l, tl, 16)

    def body(x_ref, w_ref, dy_ref, add_ref, dx_ref, dw_ref):
        i = pl.program_id(0)
        xv = x_ref[...]
        r = lax.rsqrt(jnp.mean(xv * xv, axis=-1, keepdims=True) + NORM_EPS)
        xh = xv * r
        dyv = dy_ref[...].astype(F32)
        dxh = dyv * w_ref[...]
        dx = r * (dxh - xh * jnp.mean(dxh * xh, axis=-1, keepdims=True)) + add_ref[...].astype(F32)
        dx_ref[...] = dx.astype(dx_ref.dtype)
        part = jnp.sum(dyv * xh, axis=0, keepdims=True)

        @pl.when(i == 0)
        def _():
            dw_ref[...] = part

        @pl.when(i > 0)
        def _():
            dw_ref[...] += part

    row = pl.BlockSpec((tl, d), lambda i: (i, 0))
    vec = pl.BlockSpec((1, d), lambda i: (0, 0))
    return pl.pallas_call(
        body, name=name, grid=(l // tl,), in_specs=[row, vec, row, row], out_specs=[row, vec],
        out_shape=[jax.ShapeDtypeStruct((l, d), out_dtype), jax.ShapeDtypeStruct((1, d), F32)],
        compiler_params=_params("arbitrary"),
    )(x, w, dy, add)


def _loss_head(h, w, target, *, name, tl=256):
    l, d = h.shape
    tl = _tile(l, tl, 16)

    def body(x_ref, w_ref, t_ref, loss_ref, dxb_ref, dw_ref):
        i = pl.program_id(0)
        xv = x_ref[...]
        r = lax.rsqrt(jnp.mean(xv * xv, axis=-1, keepdims=True) + NORM_EPS)
        xh = xv * r
        wv = w_ref[...]
        diff = xh * wv - t_ref[...]
        lpart = 0.5 * jnp.sum(jnp.mean(diff * diff, axis=-1, keepdims=True), axis=0, keepdims=True)
        dyv = diff * (1.0 / d)
        dxh = dyv * wv
        dx = r * (dxh - xh * jnp.mean(dxh * xh, axis=-1, keepdims=True))
        dxb_ref[...] = dx.astype(BF16)
        part = jnp.sum(dyv * xh, axis=0, keepdims=True)

        @pl.when(i == 0)
        def _():
            dw_ref[...] = part
            loss_ref[...] = lpart

        @pl.when(i > 0)
        def _():
            dw_ref[...] += part
            loss_ref[...] += lpart

    row = pl.BlockSpec((tl, d), lambda i: (i, 0))
    vec = pl.BlockSpec((1, d), lambda i: (0, 0))
    one = pl.BlockSpec((1, 1), lambda i: (0, 0))
    return pl.pallas_call(
        body, name=name, grid=(l // tl,), in_specs=[row, vec, row], out_specs=[one, row, vec],
        out_shape=[jax.ShapeDtypeStruct((1, 1), F32), jax.ShapeDtypeStruct((l, d), BF16), jax.ShapeDtypeStruct((1, d), F32)],
        compiler_params=_params("arbitrary"),
    )(h, w, target)


def _conv_pre(ext, w_ref):
    acc = ext * w_ref[CONV_K - 1:CONV_K, :]
    for k in range(CONV_K - 1):
        acc = acc + pltpu.roll(ext, CONV_K - 1 - k, axis=0) * w_ref[k:k + 1, :]
    return acc


def _conv_fwd(proj, col0, cw, cb, *, name, tl=512, tc=1024):
    l = proj.shape[0]
    c = cw.shape[1]
    tl, tc = _tile(l, tl, 16), _tile(math.gcd(c, col0), tc, LANE)
    assert col0 % tc == 0
    off, hb = col0 // tc, tl // CONV_HALO

    def body(x_ref, halo_ref, w_ref, b_ref, o_ref):
        i = pl.program_id(0)
        halo = jnp.where(i > 0, halo_ref[...], 0.0)
        ext = jnp.concatenate([halo, x_ref[...]], axis=0)
        pre = _conv_pre(ext, w_ref)[CONV_HALO:] + b_ref[...]
        o_ref[...] = pre * _sigmoid(pre)

    return pl.pallas_call(
        body, name=name, grid=(l // tl, c // tc),
        in_specs=[pl.BlockSpec((tl, tc), lambda i, j: (i, off + j)),
                  pl.BlockSpec((CONV_HALO, tc), lambda i, j: (jnp.maximum(i * hb - 1, 0), off + j)),
                  pl.BlockSpec((CONV_K, tc), lambda i, j: (0, j)), pl.BlockSpec((1, tc), lambda i, j: (0, j))],
        out_specs=pl.BlockSpec((tl, tc), lambda i, j: (i, j)),
        out_shape=jax.ShapeDtypeStruct((l, c), F32), compiler_params=_params("parallel", "parallel"),
    )(proj, proj, cw, cb)


def _conv_bwd(proj, col0, cw, cb, dout, dproj, *, name, tl=512, tc=1024):
    l = proj.shape[0]
    c = dout.shape[1]
    tl, tc = _tile(l, tl, 16), _tile(math.gcd(c, col0), tc, LANE)
    assert col0 % tc == 0
    off, hb, nt = col0 // tc, tl // CONV_HALO, l // tl
    n_ext = tl + CONV_HALO

    def body(x_ref, prev_ref, nxt_ref, d_ref, dn_ref, w_ref, b_ref, dproj_in, dx_ref, dw_ref, db_ref):
        del dproj_in
        i = pl.program_id(1)
        last = i == nt - 1
        halo = jnp.where(i > 0, prev_ref[...], 0.0)
        ext = jnp.concatenate([halo, x_ref[...], nxt_ref[...]], axis=0)
        pre = _conv_pre(ext, w_ref)[CONV_HALO:] + b_ref[...]
        dext = jnp.concatenate([d_ref[...], jnp.where(last, 0.0, dn_ref[...])], axis=0)
        s = _sigmoid(pre)
        dpre = dext * (s * (1.0 + pre * (1.0 - s)))
        dx = dpre * w_ref[CONV_K - 1:CONV_K, :]
        for k in range(CONV_K - 1):
            dx = dx + pltpu.roll(dpre, n_ext - (CONV_K - 1 - k), axis=0) * w_ref[k:k + 1, :]
        dx_ref[...] = dx[:tl].astype(dx_ref.dtype)
        dp = dpre[:tl]
        rows = [jnp.sum(dp * pltpu.roll(ext, CONV_K - 1 - k, axis=0)[CONV_HALO:CONV_HALO + tl], axis=0, keepdims=True)
                for k in range(CONV_K - 1)]
        rows.append(jnp.sum(dp * ext[CONV_HALO:CONV_HALO + tl], axis=0, keepdims=True))
        dwp = jnp.concatenate(rows, axis=0)
        dbp = jnp.sum(dp, axis=0, keepdims=True)

        @pl.when(i == 0)
        def _():
            dw_ref[...] = dwp
            db_ref[...] = dbp

        @pl.when(i > 0)
        def _():
            dw_ref[...] += dwp
            db_ref[...] += dbp

    cur = lambda j, i: (i, j)
    nxt = lambda j, i: (jnp.minimum((i + 1) * hb, nt * hb - 1), j)
    return pl.pallas_call(
        body, name=name, grid=(c // tc, nt),
        in_specs=[pl.BlockSpec((tl, tc), lambda j, i: (i, off + j)),
                  pl.BlockSpec((CONV_HALO, tc), lambda j, i: (jnp.maximum(i * hb - 1, 0), off + j)),
                  pl.BlockSpec((CONV_HALO, tc), lambda j, i: (jnp.minimum((i + 1) * hb, nt * hb - 1), off + j)),
                  pl.BlockSpec((tl, tc), cur), pl.BlockSpec((CONV_HALO, tc), nxt),
                  pl.BlockSpec((CONV_K, tc), lambda j, i: (0, j)), pl.BlockSpec((1, tc), lambda j, i: (0, j)), ANY],
        out_specs=[pl.BlockSpec((tl, tc), lambda j, i: (i, off + j)), pl.BlockSpec((CONV_K, tc), lambda j, i: (0, j)),
                   pl.BlockSpec((1, tc), lambda j, i: (0, j))],
        out_shape=[jax.ShapeDtypeStruct(dproj.shape, dproj.dtype), jax.ShapeDtypeStruct((CONV_K, c), F32),
                   jax.ShapeDtypeStruct((1, c), F32)],
        input_output_aliases={7: 0}, compiler_params=_params("parallel", "arbitrary"),
    )(proj, proj, proj, dout, dout, cw, cb, dproj)


def _pool_fwd(proj, col0, d_pool, *, name, tl=256):
    l = proj.shape[0]
    tl = _tile(l, tl, POOL_HALO)
    assert col0 % d_pool == 0
    off, hb, pg = col0 // d_pool, tl // POOL_HALO, d_pool // GROUPS

    def body(u_ref, halo_ref, o_ref):
        i = pl.program_id(0)
        halo = jnp.where(i > 0, halo_ref[...], 0.0)
        ext = jnp.concatenate([halo, u_ref[...]], axis=0)
        t = i * tl + lax.broadcasted_iota(jnp.int32, (tl, 1), 0)
        s, width = ext, 1
        for gi, win in enumerate(POOL_WINDOWS):
            while width < win:
                s = s + pltpu.roll(s, width, axis=0)
                width *= 2
            cnt = jnp.minimum(t + 1, win).astype(F32)
            sl = slice(gi * pg, (gi + 1) * pg)
            o_ref[:, sl] = (s[POOL_HALO:, sl] / cnt - ext[POOL_HALO:, sl]).astype(o_ref.dtype)

    return pl.pallas_call(
        body, name=name, grid=(l // tl,),
        in_specs=[pl.BlockSpec((tl, d_pool), lambda i: (i, off)),
                  pl.BlockSpec((POOL_HALO, d_pool), lambda i: (jnp.maximum(i * hb - 1, 0), off))],
        out_specs=pl.BlockSpec((tl, d_pool), lambda i: (i, 0)),
        out_shape=jax.ShapeDtypeStruct((l, d_pool), BF16), compiler_params=_params("parallel"),
    )(proj, proj)


def _pool_bwd(dp, dproj, col0, *, name, tl=256):
    l, d_pool = dp.shape
    tl = _tile(l, tl, POOL_HALO)
    assert col0 % d_pool == 0
    off, hb, nt, pg = col0 // d_pool, tl // POOL_HALO, l // tl, d_pool // GROUPS
    n_ext = tl + POOL_HALO

    def body(d_ref, nxt_ref, dproj_in, o_ref):
        del dproj_in
        i = pl.program_id(0)
        ext = jnp.concatenate([d_ref[...], jnp.where(i == nt - 1, 0.0, nxt_ref[...])], axis=0)
        t = i * tl + lax.broadcasted_iota(jnp.int32, (n_ext, 1), 0)
        for gi, win in enumerate(POOL_WINDOWS):
            sl = slice(gi * pg, (gi + 1) * pg)
            dg = ext[:, sl]
            s = dg / jnp.minimum(t + 1, win).astype(F32)
            width = 1
            while width < win:
                s = s + pltpu.roll(s, n_ext - width, axis=0)
                width *= 2
            o_ref[:, sl] = (s[:tl] - dg[:tl]).astype(o_ref.dtype)

    return pl.pallas_call(
        body, name=name, grid=(nt,),
        in_specs=[pl.BlockSpec((tl, d_pool), lambda i: (i, 0)),
                  pl.BlockSpec((POOL_HALO, d_pool), lambda i: (jnp.minimum((i + 1) * hb, nt * hb - 1), 0)), ANY],
        out_specs=pl.BlockSpec((tl, d_pool), lambda i: (i, off)),
        out_shape=jax.ShapeDtypeStruct(dproj.shape, dproj.dtype), input_output_aliases={2: 0},
        compiler_params=_params("parallel"),
    )(dp, dp, dproj)


def _put_heads(src, dst, col0, width, *, name, tl=512):
    l, c = src.shape
    tl = _tile(l, tl, 16)
    assert col0 % width == 0 and width % LANE == 0 and c // HEAD_DIM <= width

    def body(s_ref, dst_in, o_ref):
        del dst_in
        pick = (lax.broadcasted_iota(jnp.int32, (c, width), 0)
                == HEAD_DIM * lax.broadcasted_iota(jnp.int32, (c, width), 1)).astype(BF16)
        hi, lo = _split2(s_ref[...])
        o_ref[...] = (jnp.dot(hi, pick, preferred_element_type=F32)
                      + jnp.dot(lo, pick, preferred_element_type=F32)).astype(o_ref.dtype)

    return pl.pallas_call(
        body, name=name, grid=(l // tl,), in_specs=[pl.BlockSpec((tl, c), lambda i: (i, 0)), ANY],
        out_specs=pl.BlockSpec((tl, width), lambda i: (i, col0 // width)),
        out_shape=jax.ShapeDtypeStruct(dst.shape, dst.dtype), input_output_aliases={1: 0},
        compiler_params=_params("parallel"),
    )(src, dst)


def _pool_mix_fwd(pooled, pw, scale, mixed, col0, *, name, tm=1024):
    l, d_pool = pooled.shape
    pg = d_pool // GROUPS
    tm = _tile(l, tm, 16)
    assert col0 % pg == 0
    off = col0 // pg

    def body(a_ref, w_ref, s_ref, mixed_in, raw_ref, mixed_ref):
        del mixed_in
        raw = _dot(a_ref[...], w_ref[...])
        raw_ref[...] = raw
        mixed_ref[...] = (raw * s_ref[...]).astype(mixed_ref.dtype)

    blk = pl.BlockSpec((tm, pg), lambda i, g: (i, g))
    return pl.pallas_call(
        body, name=name, grid=(l // tm, GROUPS),
        in_specs=[blk, pl.BlockSpec((None, pg, pg), lambda i, g: (g, 0, 0)), pl.BlockSpec((1, pg), lambda i, g: (0, g)),
                  pl.BlockSpec(memory_space=pl.ANY)],
        out_specs=[blk, pl.BlockSpec((tm, pg), lambda i, g: (i, off + g))],
        out_shape=[jax.ShapeDtypeStruct((l, d_pool), F32), jax.ShapeDtypeStruct(mixed.shape, mixed.dtype)],
        input_output_aliases={3: 1}, compiler_params=_params("parallel", "parallel"),
    )(pooled, pw, scale, mixed)


def _pool_mix_bwd(dmixed, col0, raw, scale, pw, *, name, tm=1024):
    l, d_pool = raw.shape
    pg = d_pool // GROUPS
    tm = _tile(l, tm, 16)
    assert col0 % pg == 0
    off = col0 // pg

    def body(d_ref, raw_ref, s_ref, w_ref, draw_ref, dp_ref, ds_ref):
        i = pl.program_id(1)
        dv = d_ref[...]
        draw = (dv * s_ref[...]).astype(BF16)
        draw_ref[...] = draw
        dp_ref[...] = _dot(draw, w_ref[...], tb=True)
        part = jnp.sum(dv * raw_ref[...], axis=0, keepdims=True)

        @pl.when(i == 0)
        def _():
            ds_ref[...] = part

        @pl.when(i > 0)
        def _():
            ds_ref[...] += part

    blk = pl.BlockSpec((tm, pg), lambda g, i: (i, g))
    vec = pl.BlockSpec((1, pg), lambda g, i: (0, g))
    return pl.pallas_call(
        body, name=name, grid=(GROUPS, l // tm),
        in_specs=[pl.BlockSpec((tm, pg), lambda g, i: (i, off + g)), blk, vec,
                  pl.BlockSpec((None, pg, pg), lambda g, i: (g, 0, 0))],
        out_specs=[blk, blk, vec],
        out_shape=[jax.ShapeDtypeStruct((l, d_pool), BF16), jax.ShapeDtypeStruct((l, d_pool), F32),
                   jax.ShapeDtypeStruct((1, d_pool), F32)],
        compiler_params=_params("parallel", "arbitrary"),
    )(dmixed, raw, scale, pw)


def _group_mm_tn(a, b, *, name, tk=1024):
    l, d_pool = a.shape
    pg = d_pool // GROUPS
    tk = _tile(l, tk, 16)

    def body(a_ref, b_ref, o_ref):
        kk = pl.program_id(1)
        part = _dot(a_ref[...], b_ref[...], ta=True)

        @pl.when(kk == 0)
        def _():
            o_ref[...] = part

        @pl.when(kk > 0)
        def _():
            o_ref[...] += part

    blk = pl.BlockSpec((tk, pg), lambda g, kk: (kk, g))
    return pl.pallas_call(
        body, name=name, grid=(GROUPS, l // tk), in_specs=[blk, blk],
        out_specs=pl.BlockSpec((None, pg, pg), lambda g, kk: (g, 0, 0)),
        out_shape=jax.ShapeDtypeStruct((GROUPS, pg, pg), F32), compiler_params=_params("parallel", "arbitrary"),
    )(a, b)


def _split3(v):
    hi = v.astype(BF16)
    r1 = v - hi.astype(F32)
    mid = r1.astype(BF16)
    lo = (r1 - mid.astype(F32)).astype(BF16)
    return hi, mid, lo


def _tri_dot(tri, v):
    hi, mid, lo = _split3(v)
    d = lambda p: jnp.dot(tri, p, preferred_element_type=F32)
    return d(hi) + d(mid) + d(lo)


def _dot_tri(v, tri):
    hi, mid, lo = _split3(v)
    d = lambda p: jnp.dot(p, tri, preferred_element_type=F32)
    return d(hi) + d(mid) + d(lo)


def _split2(v):
    hi = v.astype(BF16)
    return hi, (v - hi.astype(F32)).astype(BF16)


SCAN_CHUNKS = 4


def _scan_specs(hpg, nc, order):
    gw, rows = hpg * HEAD_DIM, SCAN_CHUNKS * CHUNK
    b_off, c_off = (GROUPS * gw) // STATE, (GROUPS * gw) // STATE + GROUPS
    xs = pl.BlockSpec((rows, gw), lambda g, c: (order(c), g))
    bm = pl.BlockSpec((rows, STATE), lambda g, c: (order(c), b_off + g))
    cm = pl.BlockSpec((rows, STATE), lambda g, c: (order(c), c_off + g))
    dtt = pl.BlockSpec((None, hpg, rows), lambda g, c: (g, 0, order(c)))
    vec = pl.BlockSpec((1, gw), lambda g, c: (0, g))
    hcol = pl.BlockSpec((None, hpg, 1), lambda g, c: (g, 0, 0))
    st = pl.BlockSpec((None, SCAN_CHUNKS, STATE, gw), lambda g, c: (g, order(c), 0, 0))
    return xs, bm, cm, dtt, vec, hcol, st


def _scan_common(raw, raw_t, alogx, alogt, hpg, want_sigmoid=False):
    q, gw, width = CHUNK, hpg * HEAD_DIM, raw.shape[1]
    spread = (lax.broadcasted_iota(jnp.int32, (width, gw), 0)
              == hpg * pl.program_id(0) + lax.broadcasted_iota(jnp.int32, (width, gw), 1) // HEAD_DIM).astype(BF16)
    dt = _dot_tri(_softplus(raw), spread)
    sg = None
    if want_sigmoid:
        hi, lo = _split2(_sigmoid(raw))
        sg = jnp.dot(hi, spread, preferred_element_type=F32) + jnp.dot(lo, spread, preferred_element_type=F32)
    a = -jnp.exp(alogx)
    dtt = _softplus(raw_t)
    at = -jnp.exp(alogt)
    row = lax.broadcasted_iota(jnp.int32, (q, q), 0)
    col = lax.broadcasted_iota(jnp.int32, (q, q), 1)
    lower = (row >= col).astype(BF16)
    upper = (row <= col).astype(BF16)
    acum = _tri_dot(lower, dt * a)
    acum_t = _dot_tri(dtt * at, upper)
    return dt, sg, a, acum, acum_t, row, col, upper


def _gated(yv, z):
    s = _sigmoid(z)
    sz = z * s
    gv = yv * sz
    return gv, lax.rsqrt(jnp.mean(gv * gv, axis=-1, keepdims=True) + NORM_EPS), sz, s


def _scan_fwd(xbc, proj, dt_col, dtrt, biasc, biast, alogx, alogt, dskx, wn, d_mix, *, name, ride=None):
    l = xbc.shape[0]
    hpg = dtrt.shape[1]
    gw, q, nc = hpg * HEAD_DIM, CHUNK, l // CHUNK
    assert hpg % 2 == 0 and nc % SCAN_CHUNKS == 0
    xs_s, bm_s, cm_s, dtt_s, vec, hcol, st_s = _scan_specs(hpg, nc, lambda c: c)
    width = biasc.shape[1]
    dtc_s = pl.BlockSpec((SCAN_CHUNKS * q, width), lambda g, c: (c, dt_col // width))
    biasc_s = pl.BlockSpec((1, width), lambda g, c: (0, 0))

    def body(xs_ref, b_ref, c_ref, dtc_ref, dtrt_ref, biasc_ref, biast_ref, alogx_ref, alogt_ref, dsk_ref, z_ref, wn_ref,
             y_ref, st_ref, mixed_ref, h_ref):
        @pl.when(pl.program_id(1) == 0)
        def _():
            h_ref[...] = jnp.zeros_like(h_ref)

        def chunk(sub, h_all):
            rows = slice(sub * q, (sub + 1) * q)
            dt, _, _, acum, acum_t, row, col, _ = _scan_common(
                dtc_ref[rows, :] + biasc_ref[...], dtrt_ref[:, rows] + biast_ref[...], alogx_ref[...], alogt_ref[...], hpg)
            causal = row >= col
            xs = xs_ref[rows, :]
            xdt = xs * dt
            xdtb = xdt.astype(BF16)
            bmat, cmat = b_ref[rows, :], c_ref[rows, :]
            cb = _dot(cmat, bmat, tb=True)
            st_ref[sub] = h_all
            rest = _dot(cmat, h_all) * jnp.exp(acum) + dsk_ref[...] * xs
            first = lax.broadcasted_iota(jnp.int32, (q, 2 * HEAD_DIM), 1) < HEAD_DIM
            parts = []
            for jp in range(hpg // 2):
                pair = slice(2 * jp * HEAD_DIM, (2 * jp + 2) * HEAD_DIM)
                xp = xdtb[:, pair]
                ys = []
                for j in (2 * jp, 2 * jp + 1):
                    a_col, a_row = acum[:, j * HEAD_DIM:j * HEAD_DIM + 1], acum_t[j:j + 1, :]
                    lm = jnp.exp(jnp.where(causal, a_col - a_row, NEG_BIG))
                    ys.append(_dot(cb * lm, xp))
                parts.append(jnp.where(first, ys[0], ys[1]))
            yv = jnp.concatenate(parts, axis=1) + rest
            y_ref[rows, :] = yv
            gv, r, _, _ = _gated(yv, z_ref[rows, :])
            mixed_ref[rows, :] = (gv * r * wn_ref[...]).astype(mixed_ref.dtype)
            a_last = acum[q - 1:q, :]
            return jnp.exp(a_last) * h_all + _dot(bmat, xdt * jnp.exp(a_last - acum), ta=True)

        h_all = h_ref[...]
        for sub in range(SCAN_CHUNKS):
            h_all = chunk(sub, h_all)
        h_ref[...] = h_all

    return _call(
        body, name=name, grid=(GROUPS, nc // SCAN_CHUNKS),
        in_specs=[xs_s, bm_s, cm_s, dtc_s, dtt_s, biasc_s, hcol, vec, hcol, vec, xs_s, vec], out_specs=[xs_s, st_s, xs_s],
        out_shape=[jax.ShapeDtypeStruct((l, GROUPS * gw), F32), jax.ShapeDtypeStruct((GROUPS, nc, STATE, gw), F32),
                   jax.ShapeDtypeStruct((l, d_mix), BF16)],
        scratch=[pltpu.VMEM((STATE, gw), F32)], sem=("parallel", "arbitrary"),
        args=(xbc, xbc, xbc, proj, dtrt, biasc, biast, alogx, alogt, dskx, proj, wn), ride=ride)


def _scan_bwd(xbc, proj, dt_col, dtrt, biasc, biast, alogx, alogt, dskx, states, dmixed, y, wn, *, name, ride=None):
    l = xbc.shape[0]
    hpg = dtrt.shape[1]
    gw, q, nc = hpg * HEAD_DIM, CHUNK, l // CHUNK
    nsteps = nc // SCAN_CHUNKS
    rev = lambda c: nsteps - 1 - c
    xs_s, bm_s, cm_s, dtt_s, vec, hcol, st_s = _scan_specs(hpg, nc, rev)
    bc_s = pl.BlockSpec((SCAN_CHUNKS * q, STATE), lambda g, c: (rev(c), g))
    width = biasc.shape[1]
    dtc_s = pl.BlockSpec((SCAN_CHUNKS * q, width), lambda g, c: (rev(c), dt_col // width))
    biasc_s = pl.BlockSpec((1, width), lambda g, c: (0, 0))

    def body(xs_ref, b_ref, c_ref, dtc_ref, dtrt_ref, biasc_ref, biast_ref, alogx_ref, alogt_ref, dsk_ref, st_ref,
             dmix_ref, y_ref, z_ref, wn_ref,
             dxs_ref, db_ref, dc_ref, ddt_ref, dbias_ref, dalog_ref, ddsk_ref, dz_ref, dwn_ref, dh_ref):
        step = pl.program_id(1)

        @pl.when(step == 0)
        def _():
            dh_ref[...] = jnp.zeros_like(dh_ref)

        def chunk(sub, dh):
            rws = slice(sub * q, (sub + 1) * q)
            yv, z, dmix = y_ref[rws, :], z_ref[rws, :], dmix_ref[rws, :]
            gv, rn, sz, sg = _gated(yv, z)
            gh = gv * rn
            dgh = dmix * wn_ref[...]
            dg = rn * (dgh - gh * jnp.mean(dgh * gh, axis=-1, keepdims=True))
            dyv = dg * sz
            dz_ref[rws, :] = (dg * yv * (sg * (1.0 + z * (1.0 - sg)))).astype(dz_ref.dtype)
            p_wn = jnp.sum(dmix * gh, axis=0, keepdims=True)

            dt, sg_dt, a, acum, acum_t, row, col, upper = _scan_common(
                dtc_ref[rws, :] + biasc_ref[...], dtrt_ref[:, rws] + biast_ref[...], alogx_ref[...], alogt_ref[...], hpg,
                want_sigmoid=True)
            causal, anti, strict = row >= col, row <= col, row > col
            xs = xs_ref[rws, :]
            xdt = xs * dt
            xdtb, dyb = xdt.astype(BF16), dyv.astype(BF16)
            bmat, cmat = b_ref[rws, :], c_ref[rws, :]
            cb = _dot(cmat, bmat, tb=True)
            cbt = _dot(bmat, cmat, tb=True)
            h_all = st_ref[sub]
            a_last = acum[q - 1:q, :]
            ea, ee, gam = jnp.exp(acum), jnp.exp(a_last - acum), jnp.exp(a_last)
            zc = _dot(cmat, h_all)
            bdh = _dot(bmat, dh)
            dz = dyv * ea
            xe = xdt * ee
            dc_state = _dot(dz, h_all, tb=True)
            db_state = _dot(xe, dh, tb=True)
            dh_in = gam * dh + _dot(cmat, dz, ta=True)
            lane = lax.broadcasted_iota(jnp.int32, (q, 2 * HEAD_DIM), 1)
            first = lane < HEAD_DIM
            ones = jnp.ones((q // 2, 2 * HEAD_DIM), BF16)
            dcb = jnp.zeros((q, q), F32)
            dxdt_parts, da_parts = [], []
            for jp in range(hpg // 2):
                pair = slice(2 * jp * HEAD_DIM, (2 * jp + 2) * HEAD_DIM)
                xp, dyp = xdtb[:, pair], dyb[:, pair]
                dxs_r, das_r = [], []
                for r, j in enumerate((2 * jp, 2 * jp + 1)):
                    seg = acum[:, j * HEAD_DIM:j * HEAD_DIM + 1] - acum_t[j:j + 1, :]
                    lm = jnp.exp(jnp.where(causal, seg, NEG_BIG))
                    lmt = jnp.exp(jnp.where(anti, -seg, NEG_BIG))
                    mm, mmt = cb * lm, cbt * lmt
                    keep = first if r == 0 else jnp.logical_not(first)
                    dyk = jnp.where(keep, dyp, 0)
                    dxs_r.append(_dot(mmt, dyp))
                    dm = _dot(dyk, xp, tb=True)
                    dcb = dcb + dm * lm
                    corner = jnp.dot(upper, (dm * mm).astype(BF16), preferred_element_type=F32)
                    corner = jnp.where(strict, corner, 0.0)
                    das_r.append(_dot(corner[:, :q // 2] + corner[:, q // 2:], ones))
                dxdt_parts.append(jnp.where(first, dxs_r[0], dxs_r[1]))
                da_parts.append(jnp.where(first, das_r[0], das_r[1]))
            dxdt = jnp.concatenate(dxdt_parts, axis=1) + bdh * ee
            da_intra = jnp.concatenate(da_parts, axis=1)
            dxs_ref[rws, :] = dxdt * dt + dsk_ref[...] * dyv
            dc_ref[rws, :] = dc_state + _dot(dcb, bmat)
            db_ref[rws, :] = db_state + _dot(dcb, cmat, ta=True)
            ri = lax.broadcasted_iota(jnp.int32, (gw, gw), 0) // HEAD_DIM
            ci = lax.broadcasted_iota(jnp.int32, (gw, gw), 1) // HEAD_DIM
            blockdiag = (ri == ci).astype(BF16)

            def head_sum(v):
                hi, lo = _split2(v)
                return (jnp.dot(hi, blockdiag, preferred_element_type=F32)
                        + jnp.dot(lo, blockdiag, preferred_element_type=F32))

            xt = xe * bdh
            small = jnp.concatenate([
                jnp.sum(xt, axis=0, keepdims=True) + gam * jnp.sum(dh * h_all, axis=0, keepdims=True),
                jnp.sum(dyv * xs, axis=0, keepdims=True), jnp.zeros((6, gw), F32)], axis=0)
            small = head_sum(small)
            rows = lax.broadcasted_iota(jnp.int32, (q, 1), 0)
            da_local = head_sum(dyv * zc * ea - xt) + jnp.where(rows == q - 1, small[0:1, :], 0.0)
            hi, lo = _split2(da_local)
            d_dta = (da_intra + jnp.dot(upper, hi, preferred_element_type=F32)
                     + jnp.dot(upper, lo, preferred_element_type=F32))
            d_raw = (d_dta * a + head_sum(dxdt * xs)) * sg_dt
            ddt_ref[rws, :] = d_raw
            p_bias = jnp.sum(d_raw, axis=0, keepdims=True)
            p_alog = jnp.sum(d_dta * dt, axis=0, keepdims=True) * a
            return dh_in, (p_bias, p_alog, small[1:2, :], p_wn)

        dh = dh_ref[...]
        sums = None
        for sub in reversed(range(SCAN_CHUNKS)):
            dh, part = chunk(sub, dh)
            sums = part if sums is None else tuple(s + p for s, p in zip(sums, part))
        dh_ref[...] = dh
        p_bias, p_alog, p_dsk, p_wn = sums

        @pl.when(step == 0)
        def _():
            dbias_ref[...] = p_bias
            dalog_ref[...] = p_alog
            ddsk_ref[...] = p_dsk
            dwn_ref[...] = p_wn

        @pl.when(step > 0)
        def _():
            dbias_ref[...] += p_bias
            dalog_ref[...] += p_alog
            ddsk_ref[...] += p_dsk
            dwn_ref[...] += p_wn

    wide = jax.ShapeDtypeStruct((l, GROUPS * gw), F32)
    narrow = jax.ShapeDtypeStruct((l, GROUPS * STATE), F32)
    vshape = jax.ShapeDtypeStruct((1, GROUPS * gw), F32)
    return _call(
        body, name=name, grid=(GROUPS, nsteps),
        in_specs=[xs_s, bm_s, cm_s, dtc_s, dtt_s, biasc_s, hcol, vec, hcol, vec, st_s, xs_s, xs_s, xs_s, vec],
        out_specs=[xs_s, bc_s, bc_s, xs_s, vec, vec, vec, xs_s, vec],
        out_shape=[wide, narrow, narrow, wide, vshape, vshape, vshape, jax.ShapeDtypeStruct(proj.shape, BF16), vshape],
        scratch=[pltpu.VMEM((STATE, gw), F32)], sem=("parallel", "arbitrary"),
        args=(xbc, xbc, xbc, proj, dtrt, biasc, biast, alogx, alogt, dskx, states, dmixed, y, proj, wn), ride=ride)


def _exchange(srcs, scatter, *, name):
    ride = _Ride(srcs, scatter)

    def body(*refs):
        src, dst, sems = refs[:ride.n], refs[ride.n:2 * ride.n], refs[2 * ride.n:]
        ride.start(src, dst, sems)
        ride.relay(src, dst, sems)
        ride.finish(src, dst, sems)

    return pl.pallas_call(body, name=name, in_specs=ride.in_specs, out_specs=ride.out_specs, out_shape=ride.out_shape,
                          scratch_shapes=ride.scratch)(*srcs)


def _adamw(parts, w, m, v, *, name, tr=256):
    lead = w.ndim == 3
    r, c = w.shape[-2:]
    tr = _tile(r, tr, 16)
    c1, c2 = 1.0 / (1.0 - ADAM_B1 ** ADAM_STEP), 1.0 / (1.0 - ADAM_B2 ** ADAM_STEP)

    def body(p_ref, w_ref, m_ref, v_ref, g_ref, d_ref, nm_ref, nv_ref):
        g = p_ref[0].astype(F32)
        for p in range(1, N_DEV):
            g = g + p_ref[p].astype(F32)
        nm = ADAM_B1 * m_ref[...] + (1.0 - ADAM_B1) * g
        nv = ADAM_B2 * v_ref[...] + (1.0 - ADAM_B2) * (g * g)
        g_ref[...] = g
        nm_ref[...] = nm
        nv_ref[...] = nv
        d_ref[...] = -ADAM_LR * ((nm * c1) / (jnp.sqrt(nv * c2) + ADAM_EPS) + ADAM_WD * w_ref[...])

    blk = pl.BlockSpec((None, tr, c), lambda i: (0, i, 0)) if lead else pl.BlockSpec((tr, c), lambda i: (i, 0))
    out = jax.ShapeDtypeStruct(w.shape, F32)
    return pl.pallas_call(
        body, name=name, grid=(r // tr,), in_specs=[pl.BlockSpec((N_DEV, tr, c), lambda i: (0, i, 0)), blk, blk, blk],
        out_specs=[blk, blk, blk, blk], out_shape=[out, out, out, out], compiler_params=_params("parallel"),
    )(parts, w, m, v)


def kernel(x, attn_norm_w, w_in, conv_w, conv_b, dt_bias, a_log, d_skip, ssd_norm_w, pool_w, pool_scale, w_out, ffn_norm_w, w_gate, w_up, w_down, final_norm_w, loss_target, m_attn_norm_w, m_w_in, m_conv_w, m_conv_b, m_dt_bias, m_a_log, m_d_skip, m_ssd_norm_w, m_pool_w, m_pool_scale, m_w_out, m_ffn_norm_w, m_w_gate, m_w_up, m_w_down, m_final_norm_w, v_attn_norm_w, v_w_in, v_conv_w, v_conv_b, v_dt_bias, v_a_log, v_d_skip, v_ssd_norm_w, v_pool_w, v_pool_scale, v_w_out, v_ffn_norm_w, v_w_gate, v_w_up, v_w_down, v_final_norm_w):
    l, d = x.shape[1], x.shape[2]
    heads = dt_bias.shape[1]
    hpg = heads // GROUPS
    d_ssm = heads * HEAD_DIM
    conv_ch = conv_b.shape[1]
    d_pool = pool_scale.shape[1]
    pg = d_pool // GROUPS
    d_mix = d_ssm + d_pool
    d_ff = w_gate.shape[2] * N_DEV
    d_in = w_in.shape[2] * N_DEV
    dt_pad = -(-heads // LANE) * LANE
    o_u, o_xbc, o_dt = d_ssm, d_ssm + d_pool, d_ssm + d_pool + conv_ch
    d_inp = o_dt + dt_pad
    tn_in = _tile(d_inp, 2560, LANE)
    g_dt, g_u = d_ssm + conv_ch, d_ssm + conv_ch + heads

    x2, tgt = x[0], loss_target[0]

    tr_ = lambda t: jnp.transpose(t, (0, 2, 1))
    c_in = d_in // N_DEV
    h0, gi, gp, gc = _rms_fwd(x2, attn_norm_w, name="attn_norm",
                              ride=_Ride([tr_(w_in)[0].astype(BF16), pool_w[0].astype(BF16), conv_w[0]], False))
    win = gi.reshape(d_in, d)
    wp = jnp.concatenate([win[:d_ssm], win[g_u:], win[d_ssm:g_dt], win[g_dt:g_u],
                          jnp.zeros((dt_pad - heads, d), BF16)], axis=0)
    pw = jnp.transpose(gp, (1, 0, 2, 3)).reshape(GROUPS, pg, pg)
    cw = jnp.transpose(gc, (1, 0, 2)).reshape(CONV_K, conv_ch)

    def expand(vec):
        return jnp.repeat(vec, HEAD_DIM, axis=1)

    def per_head(vec):
        return vec[:, ::HEAD_DIM]

    alog_x, dsk_x = expand(a_log), expand(d_skip)
    bias_w = jnp.pad(dt_bias, ((0, 0), (0, dt_pad - heads)))
    bias_c, alog_c = dt_bias.reshape(GROUPS, hpg, 1), a_log.reshape(GROUPS, hpg, 1)

    proj, go, gg = _mm(h0, wp, name="in_proj", tb=True, tm=512, tn=tn_in, tk=d,
                       ride=_Ride([w_out[0].astype(BF16), tr_(w_gate)[0].astype(BF16)], False))
    wo = go.reshape(d_mix, d)
    xbc = _conv_fwd(proj, o_xbc, cw, conv_b, name="conv_fwd")
    dt_raw = proj[:, o_dt:o_dt + heads]
    dtrt = jnp.transpose(dt_raw.reshape(l, GROUPS, hpg), (1, 2, 0))
    y, states, mixed, gu = _scan_fwd(xbc, proj, o_dt, dtrt, bias_w, bias_c, alog_x, alog_c, dsk_x, ssd_norm_w, d_mix,
                                     name="ssd_fwd", ride=_Ride([tr_(w_up)[0].astype(BF16)], False))
    pooled = _pool_fwd(proj, o_u, d_pool, name="pool_fwd")
    pool_raw, mixed = _pool_mix_fwd(pooled, pw, pool_scale, mixed, d_ssm, name="pool_mix")
    h1 = _mm(mixed, wo, name="out_proj", tm=512, tk=d_mix, residual=x2)
    h1n = _rms_fwd(h1, ffn_norm_w, name="ffn_norm")
    wide = lambda t: t.reshape(N_DEV // FFN_GROUP, -1, d)
    gg, gu = wide(gg), wide(gu)
    gate, up, act, gd = _gate_up(h1n, gg, gu, name="gate_up", ride=_Ride([w_down[0].astype(BF16)], False))
    gd = wide(gd)
    h2 = _mm_shards([(act, gd)], name="down_proj", per_step=1, residual=h1)

    loss11, dh2b, g_final = _loss_head(h2, final_norm_w.reshape(1, d), tgt, name="loss_head")
    dgate, dup = _gate_up_bwd(dh2b, gd, gate, up, name="gate_up_bwd")
    shards = lambda t: t.reshape(N_DEV, -1, d)
    g_wd = shards(_grad_shards(act, dh2b, name="grad_w_down"))
    dh1n, r_wd = _mm_shards([(dgate, gg), (dup, gu)], name="dgate_dup", out_dtype=BF16, tm=512, per_step=1,
                            ride=_Ride([g_wd], True))
    g_wg = shards(_grad_shards(dgate, h1n, name="grad_w_gate"))
    g_wu = shards(_grad_shards(dup, h1n, name="grad_w_up"))
    dh1b, g_ffn = _rms_bwd(h1, ffn_norm_w, dh1n, dh2b, BF16, name="ffn_norm_bwd")
    dmixed = _mm(dh1b, wo, name="dmixed", tb=True, tk=d)
    g_wo = _mm(mixed, dh1b, name="grad_w_out", ta=True, out_dtype=BF16, tm=1024, tn=1024, tk=2048)
    draw, dpooled, g_pscale = _pool_mix_bwd(dmixed, d_ssm, pool_raw, pool_scale, pw, name="pool_mix_bwd")
    g_pw = _group_mm_tn(pooled, draw, name="grad_pool_w")
    s_pw = jnp.transpose(g_pw.reshape(GROUPS, N_DEV, pg // N_DEV, pg), (1, 0, 2, 3)).astype(BF16)
    dxs, dbm, dcm, ddtx, g_bias, g_alog, g_dsk, dproj, g_ssdn, r_wg, r_wu, r_wo, r_pw = _scan_bwd(
        xbc, proj, o_dt, dtrt, bias_w, bias_c, alog_x, alog_c, dsk_x, states, dmixed, y, ssd_norm_w, name="ssd_bwd",
        ride=_Ride([g_wg, g_wu, g_wo.reshape(N_DEV, d_mix // N_DEV, d), s_pw], True))
    dproj = _pool_bwd(dpooled, dproj, o_u, name="pool_bwd")
    n_bc = GROUPS * STATE
    segs = [(dxs, 0, d_ssm), (dbm, d_ssm, n_bc), (dcm, d_ssm + n_bc, n_bc)]
    g_cw, g_cb = [], []
    for si, (dseg, c0, width) in enumerate(segs):
        dproj, b, c_ = _conv_bwd(proj, o_xbc + c0, cw[:, c0:c0 + width], conv_b[:, c0:c0 + width], dseg, dproj,
                                 name=f"conv_bwd{si}")
        g_cw.append(b)
        g_cb.append(c_)
    dproj = _put_heads(ddtx, dproj, o_dt, dt_pad, name="put_ddt")
    g_wp = _mm(dproj, h0, name="grad_w_in", ta=True, out_dtype=BF16, tm=384, tn=d, tk=2048)
    g_win = jnp.concatenate([g_wp[:d_ssm], g_wp[o_xbc:o_dt + heads], g_wp[o_u:o_xbc]], axis=0).reshape(N_DEV, c_in, d)
    def cols8(g):
        return jnp.transpose(g.reshape(g.shape[0], N_DEV, g.shape[1] // N_DEV), (1, 0, 2))
    dh0, r_in, r_cw = _mm(dproj, wp, name="dh0", out_dtype=BF16, tk=tn_in,
                          ride=_Ride([g_win, cols8(jnp.concatenate(g_cw, axis=1))], True))
    dx, g_attn = _rms_bwd(x2, attn_norm_w, dh0, dh1b, F32, name="attn_norm_bwd")

    smalls = [g_attn, jnp.concatenate(g_cb, axis=1), per_head(g_bias), per_head(g_alog), per_head(g_dsk), g_ssdn,
              g_pscale, g_ffn, g_final]
    small_w = [attn_norm_w, conv_b, dt_bias, a_log, d_skip, ssd_norm_w, pool_scale, ffn_norm_w, final_norm_w.reshape(1, d)]
    small_m = [m_attn_norm_w, m_conv_b, m_dt_bias, m_a_log, m_d_skip, m_ssd_norm_w, m_pool_scale, m_ffn_norm_w,
               m_final_norm_w.reshape(1, d)]
    small_v = [v_attn_norm_w, v_conv_b, v_dt_bias, v_a_log, v_d_skip, v_ssd_norm_w, v_pool_scale, v_ffn_norm_w,
               v_final_norm_w.reshape(1, d)]
    sizes = [s.shape[1] for s in smalls]
    n_small = sum(sizes)
    n_pad = -(-n_small // (16 * LANE)) * (16 * LANE)
    rows = n_pad // LANE
    def pack(vs):
        return jnp.pad(jnp.concatenate(vs, axis=1), ((0, 0), (0, n_pad - n_small))).reshape(rows, LANE)
    (r_small,) = _exchange([pack(smalls)], False, name="gather_small_grads")

    def big(parts, w, m, v, nm):
        shp = w.shape
        if w.ndim == 3:
            return _adamw(parts, w, m, v, name=nm)
        r2 = lambda t: t.reshape(-1, shp[-1])
        outs = _adamw(parts.reshape(N_DEV, -1, shp[-1]), r2(w), r2(m), r2(v), name=nm)
        return [o.reshape(shp) for o in outs]

    def big_t(parts, w, m, v, nm):
        return [tr_(o) for o in _adamw(parts, tr_(w), tr_(m), tr_(v), name=nm)]

    def big_flat(parts, w, m, v, nm):
        fl = lambda t: tr_(t).reshape(-1, LANE)
        outs = _adamw(parts.reshape(N_DEV, -1, LANE), fl(w), fl(m), fl(v), name=nm, tr=2048)
        return [tr_(o.reshape(1, w.shape[2], w.shape[1])) for o in outs]

    res = {
        "w_in": big_flat(r_in, w_in, m_w_in, v_w_in, "adamw_w_in"),
        "conv_w": big(r_cw, conv_w, m_conv_w, v_conv_w, "adamw_conv_w"),
        "pool_w": big(r_pw, pool_w, m_pool_w, v_pool_w, "adamw_pool_w"),
        "w_out": big(r_wo, w_out, m_w_out, v_w_out, "adamw_w_out"),
        "w_gate": big_t(r_wg, w_gate, m_w_gate, v_w_gate, "adamw_w_gate"),
        "w_up": big_t(r_wu, w_up, m_w_up, v_w_up, "adamw_w_up"),
        "w_down": big(r_wd, w_down, m_w_down, v_w_down, "adamw_w_down"),
    }
    s_out = _adamw(r_small, pack(small_w), pack(small_m), pack(small_v), name="adamw_small")
    names = ["attn_norm_w", "conv_b", "dt_bias", "a_log", "d_skip", "ssd_norm_w", "pool_scale", "ffn_norm_w", "final_norm_w"]
    offs = [sum(sizes[:i]) for i in range(len(sizes))]
    for i, nm in enumerate(names):
        shp = (d,) if nm == "final_norm_w" else (1, sizes[i])
        res[nm] = [o.reshape(1, n_pad)[:, offs[i]:offs[i] + sizes[i]].reshape(shp) for o in s_out]

    loss = lax.psum(loss11[0, 0], ("x", "y", "c"))
    order = ["attn_norm_w", "w_in", "conv_w", "conv_b", "dt_bias", "a_log", "d_skip", "ssd_norm_w", "pool_w", "pool_scale",
             "w_out", "ffn_norm_w", "w_gate", "w_up", "w_down", "final_norm_w"]
    outs = [loss, dx.reshape(x.shape)]
    for part in range(4):
        outs += [res[nm][part] for nm in order]
    return tuple(outs)
```

```python
import functools
import math

import jax
import jax.numpy as jnp
from jax import lax
from jax.experimental import pallas as pl
from jax.experimental.pallas import tpu as pltpu

F32 = jnp.float32
BF16 = jnp.bfloat16

NORM_EPS = 1e-5
HEAD_DIM = 64
STATE = 128
CHUNK = 256
GROUPS = 4
CONV_K = 4
POOL_WINDOWS = (2, 4, 8, 16)
POOL_HALO = 16
CONV_HALO = 8
LANE = 128
N_DEV = 8
ADAM_LR, ADAM_B1, ADAM_B2, ADAM_EPS, ADAM_WD, ADAM_STEP = 0.001, 0.9, 0.999, 1e-08, 0.01, 10
NEG_BIG = -1e30
FFN_GROUP = 4
MESH_ID = pl.DeviceIdType.MESH


def _tile(n, pref, mult):
    if n <= pref:
        return n
    t = (pref // mult) * mult
    while t >= mult:
        if n % t == 0:
            return t
        t -= mult
    return n


def _params(*sem):
    return pltpu.CompilerParams(dimension_semantics=sem)


ANY = pl.BlockSpec(memory_space=pl.ANY)


def _peer(k):
    x, y, c = lax.axis_index("x"), lax.axis_index("y"), lax.axis_index("c")
    return (1 - x if k & 4 else x, 1 - y if k & 2 else y, 1 - c if k & 1 else c)


def _my_index():
    return 4 * lax.axis_index("x") + 2 * lax.axis_index("y") + lax.axis_index("c")


class _Ride:
    def __init__(self, srcs, scatter):
        self.srcs, self.scatter, self.n = list(srcs), scatter, len(srcs)
        self.two_level = not scatter
        self.in_specs = [ANY] * self.n
        self.out_specs = [ANY] * self.n
        self.out_shape = [jax.ShapeDtypeStruct((N_DEV,) + tuple(s.shape[1:] if scatter else s.shape), s.dtype)
                          for s in self.srcs]
        self.scratch = [pltpu.SemaphoreType.DMA((self.n, N_DEV - 1)), pltpu.SemaphoreType.DMA((self.n, N_DEV - 1)),
                        pltpu.SemaphoreType.DMA((self.n,))]

    def _copies(self, src, dst, sems):
        send_sems, recv_sems, local_sems = sems
        me = _my_index()
        local = [pltpu.make_async_copy(src[t].at[me] if self.scatter else src[t], dst[t].at[me], local_sems.at[t])
                 for t in range(self.n)]
        remote = {}
        for k in range(1, N_DEV):
            peer, pidx = _peer(k), me ^ k
            remote[k] = [pltpu.make_async_remote_copy(
                src_ref=src[t].at[pidx] if self.scatter else src[t], dst_ref=dst[t].at[me],
                send_sem=send_sems.at[t, k - 1], recv_sem=recv_sems.at[t, k - 1],
                device_id=peer, device_id_type=MESH_ID) for t in range(self.n)]
        return local, remote

    def _forwards(self, dst, sems):
        send_sems, recv_sems, _ = sems
        me = _my_index()
        return {k: [pltpu.make_async_remote_copy(
            src_ref=dst[t].at[me ^ k], dst_ref=dst[t].at[me ^ k],
            send_sem=send_sems.at[t, (k ^ 1) - 1], recv_sem=recv_sems.at[t, (k ^ 1) - 1],
            device_id=_peer(1), device_id_type=MESH_ID) for t in range(self.n)] for k in (2, 4, 6)}

    def start(self, src, dst, sems):
        local, remote = self._copies(src, dst, sems)
        for cp in local:
            cp.start()
        for k in (2, 4, 6, 1) if self.two_level else range(1, N_DEV):
            for cp in remote[k]:
                cp.start()

    def relay(self, src, dst, sems):
        if not self.two_level:
            return
        _, remote = self._copies(src, dst, sems)
        fwd = self._forwards(dst, sems)
        for k in (2, 4, 6):
            for t in range(self.n):
                remote[k][t].wait_recv()
                fwd[k][t].start()

    def finish(self, src, dst, sems):
        local, remote = self._copies(src, dst, sems)
        if self.two_level:
            fwd = self._forwards(dst, sems)
            for k in (1, 3, 5, 7):
                for cp in remote[k]:
                    cp.wait_recv()
            for k in (1, 2, 4, 6):
                for cp in remote[k]:
                    cp.wait_send()
            for k in (2, 4, 6):
                for cp in fwd[k]:
                    cp.wait_send()
        else:
            for k in range(1, N_DEV):
                for cp in remote[k]:
                    cp.wait_send()
            for k in range(1, N_DEV):
                for cp in remote[k]:
                    cp.wait_recv()
        for lc in local:
            lc.wait()


RELAY_AT = 0.75


def _call(body, *, name, grid, in_specs, out_specs, out_shape, args, sem, scratch=(), ride=None):
    in_specs, out_specs, out_shape, scratch = list(in_specs), list(out_specs), list(out_shape), list(scratch)
    if ride is None:
        return pl.pallas_call(body, name=name, grid=grid, in_specs=in_specs, out_specs=out_specs, out_shape=out_shape,
                              scratch_shapes=scratch, compiler_params=_params(*sem))(*args)
    n_in, n_out, n_sc, nr = len(in_specs), len(out_specs), len(scratch), ride.n

    def full(*refs):
        ins, csrc = refs[:n_in], refs[n_in:n_in + nr]
        o0 = n_in + nr
        outs, cdst = refs[o0:o0 + n_out], refs[o0 + n_out:o0 + n_out + nr]
        s0 = o0 + n_out + nr
        sc, csem = refs[s0:s0 + n_sc], refs[s0 + n_sc:]
        ids = [pl.program_id(ax) for ax in range(len(grid))]
        step = functools.reduce(lambda acc, ig: acc * ig[1] + ig[0], zip(ids, grid), 0)
        total = functools.reduce(lambda p, g: p * g, grid, 1)

        @pl.when(step == 0)
        def _():
            ride.start(csrc, cdst, csem)

        body(*ins, *outs, *sc)

        @pl.when(step == min(int(total * RELAY_AT), total - 1))
        def _():
            ride.relay(csrc, cdst, csem)

        @pl.when(step == total - 1)
        def _():
            ride.finish(csrc, cdst, csem)

    return pl.pallas_call(
        full, name=name, grid=grid, in_specs=in_specs + ride.in_specs, out_specs=out_specs + ride.out_specs,
        out_shape=out_shape + ride.out_shape, scratch_shapes=scratch + ride.scratch,
        compiler_params=_params(*(["arbitrary"] * len(grid))),
    )(*args, *ride.srcs)


def _sigmoid(x):
    return 1.0 / (1.0 + jnp.exp(-x))


def _softplus(x):
    return jnp.maximum(x, 0.0) + jnp.log(1.0 + jnp.exp(-jnp.abs(x)))


def _dot(a, b, ta=False, tb=False):
    dn = (((0 if ta else 1,), (1 if tb else 0,)), ((), ()))
    return lax.dot_general(a.astype(BF16), b.astype(BF16), dn, preferred_element_type=F32)


def _mm(a, b, *, name, ta=False, tb=False, out_dtype=F32, tm=1024, tn=1024, tk=2048, residual=None, ride=None):
    m = a.shape[1] if ta else a.shape[0]
    k = a.shape[0] if ta else a.shape[1]
    n = b.shape[0] if tb else b.shape[1]
    assert k == (b.shape[1] if tb else b.shape[0])
    tm, tn = _tile(m, tm, LANE if ta else 8), _tile(n, tn, LANE)
    tk = _tile(k, tk, LANE if not (ta and tb) else LANE)
    nk = k // tk
    has_res = residual is not None

    def body(*refs):
        a_ref, b_ref = refs[0], refs[1]
        r_ref = refs[2] if has_res else None
        o_ref = refs[3] if has_res else refs[2]
        part = _dot(a_ref[...], b_ref[...], ta, tb)

        def finish(acc):
            if has_res:
                acc = acc + r_ref[...]
            o_ref[...] = acc.astype(o_ref.dtype)

        if nk == 1:
            finish(part)
        else:
            acc_ref = refs[-1]
            kk = pl.program_id(2)

            @pl.when(kk == 0)
            def _():
                acc_ref[...] = part

            @pl.when(kk > 0)
            def _():
                acc_ref[...] += part

            @pl.when(kk == nk - 1)
            def _():
                finish(acc_ref[...])

    a_spec = pl.BlockSpec((tk, tm), lambda i, j, kk: (kk, i)) if ta else pl.BlockSpec((tm, tk), lambda i, j, kk: (i, kk))
    b_spec = pl.BlockSpec((tn, tk), lambda i, j, kk: (j, kk)) if tb else pl.BlockSpec((tk, tn), lambda i, j, kk: (kk, j))
    o_spec = pl.BlockSpec((tm, tn), lambda i, j, kk: (i, j))
    in_specs = [a_spec, b_spec] + ([o_spec] if has_res else [])
    args = (a, b) + ((residual,) if has_res else ())
    res = _call(body, name=name, grid=(m // tm, n // tn, nk), in_specs=in_specs, out_specs=[o_spec],
                out_shape=[jax.ShapeDtypeStruct((m, n), out_dtype)], args=args,
                scratch=[pltpu.VMEM((tm, tn), F32)] if nk > 1 else [], sem=("parallel", "parallel", "arbitrary"), ride=ride)
    return res[0] if ride is None else res


def _gate_up(h, wg, wu, *, name, tm=256, ride=None):
    m, k = h.shape
    ns, c, _ = wg.shape
    tm = _tile(m, tm, 16)

    def body(h_ref, wg_ref, wu_ref, g_ref, u_ref, a_ref):
        hb = h_ref[...]
        g = _dot(hb, wg_ref[...], tb=True)
        u = _dot(hb, wu_ref[...], tb=True)
        g_ref[...] = g
        u_ref[...] = u
        a_ref[...] = (g * _sigmoid(g) * u).astype(a_ref.dtype)

    o_spec = pl.BlockSpec((None, tm, c), lambda s, i: (s, i, 0))
    w_spec = pl.BlockSpec((None, c, k), lambda s, i: (s, 0, 0), pipeline_mode=pl.Buffered(1))
    return _call(body, name=name, grid=(ns, m // tm),
                 in_specs=[pl.BlockSpec((tm, k), lambda s, i: (i, 0)), w_spec, w_spec], out_specs=[o_spec, o_spec, o_spec],
                 out_shape=[jax.ShapeDtypeStruct((ns, m, c), F32), jax.ShapeDtypeStruct((ns, m, c), F32),
                            jax.ShapeDtypeStruct((ns, m, c), BF16)],
                 args=(h, wg, wu), sem=("parallel", "parallel"), ride=ride)


def _gate_up_bwd(dh, wd, g, u, *, name, tm=256):
    m, k = dh.shape
    ns, c, _ = wd.shape
    tm = _tile(m, tm, 16)

    def body(dh_ref, wd_ref, g_ref, u_ref, dg_ref, du_ref):
        da = _dot(dh_ref[...], wd_ref[...], tb=True)
        g = g_ref[...]
        s = _sigmoid(g)
        dg_ref[...] = (da * u_ref[...] * (s * (1.0 + g * (1.0 - s)))).astype(dg_ref.dtype)
        du_ref[...] = (da * (g * s)).astype(du_ref.dtype)

    o_spec = pl.BlockSpec((None, tm, c), lambda s, i: (s, i, 0))
    w_spec = pl.BlockSpec((None, c, k), lambda s, i: (s, 0, 0), pipeline_mode=pl.Buffered(1))
    return _call(body, name=name, grid=(ns, m // tm),
                 in_specs=[pl.BlockSpec((tm, k), lambda s, i: (i, 0)), w_spec, o_spec, o_spec],
                 out_specs=[o_spec, o_spec],
                 out_shape=[jax.ShapeDtypeStruct((ns, m, c), BF16), jax.ShapeDtypeStruct((ns, m, c), BF16)],
                 args=(dh, wd, g, u), sem=("parallel", "parallel"))


def _mm_shards(pairs, *, name, tb=False, residual=None, out_dtype=F32, tm=1024, tn=1024, per_step=2, ride=None):
    ns, m, c = pairs[0][0].shape
    n = pairs[0][1].shape[1] if tb else pairs[0][1].shape[2]
    tm, tn = _tile(m, tm, 16), _tile(n, tn, LANE)
    npair = len(pairs)
    has_res = residual is not None
    assert ns % per_step == 0
    nsteps = ns // per_step

    def body(*refs):
        o_ref, acc_ref = refs[-2], refs[-1]
        s = pl.program_id(2)
        part = None
        for p in range(npair):
            for q in range(per_step):
                term = _dot(refs[2 * p][q], refs[2 * p + 1][q], tb=tb)
                part = term if part is None else part + term

        @pl.when(s == 0)
        def _():
            acc_ref[...] = part

        @pl.when(s > 0)
        def _():
            acc_ref[...] += part

        @pl.when(s == nsteps - 1)
        def _():
            acc = acc_ref[...]
            if has_res:
                acc = acc + refs[2 * npair][...]
            o_ref[...] = acc.astype(o_ref.dtype)

    a_spec = pl.BlockSpec((per_step, tm, c), lambda i, j, s: (s, i, 0))
    b_spec = (pl.BlockSpec((per_step, tn, c), lambda i, j, s: (s, j, 0)) if tb
              else pl.BlockSpec((per_step, c, tn), lambda i, j, s: (s, 0, j)))
    o_spec = pl.BlockSpec((tm, tn), lambda i, j, s: (i, j))
    args = [t for pr in pairs for t in pr] + ([residual] if has_res else [])
    res = _call(body, name=name, grid=(m // tm, n // tn, nsteps),
                in_specs=[a_spec, b_spec] * npair + ([o_spec] if has_res else []), out_specs=[o_spec],
                out_shape=[jax.ShapeDtypeStruct((m, n), out_dtype)], args=args, scratch=[pltpu.VMEM((tm, tn), F32)],
                sem=("parallel", "parallel", "arbitrary"), ride=ride)
    return res[0] if ride is None else res


def _grad_shards(a, b, *, name, tk=2048, tc=1408, tn=1024):
    ns, l, c = a.shape
    d = b.shape[1]
    tk, tc, tn = _tile(l, tk, 16), _tile(c, tc, LANE), _tile(d, tn, LANE)
    nk = l // tk

    def body(a_ref, b_ref, o_ref, acc_ref):
        kk = pl.program_id(3)
        part = _dot(a_ref[...], b_ref[...], ta=True)

        @pl.when(kk == 0)
        def _():
            acc_ref[...] = part

        @pl.when(kk > 0)
        def _():
            acc_ref[...] += part

        @pl.when(kk == nk - 1)
        def _():
            o_ref[...] = acc_ref[...].astype(o_ref.dtype)

    return pl.pallas_call(
        body, name=name, grid=(ns, c // tc, d // tn, nk),
        in_specs=[pl.BlockSpec((None, tk, tc), lambda s, i, j, kk: (s, kk, i)),
                  pl.BlockSpec((tk, tn), lambda s, i, j, kk: (kk, j))],
        out_specs=pl.BlockSpec((None, tc, tn), lambda s, i, j, kk: (s, i, j)),
        out_shape=jax.ShapeDtypeStruct((ns, c, d), BF16), scratch_shapes=[pltpu.VMEM((tc, tn), F32)],
        compiler_params=_params("parallel", "parallel", "parallel", "arbitrary"),
    )(a, b)


def _rms_fwd(x, w, *, name, tl=512, ride=None):
    l, d = x.shape
    tl = _tile(l, tl, 16)

    def body(x_ref, w_ref, o_ref):
        xv = x_ref[...]
        r = lax.rsqrt(jnp.mean(xv * xv, axis=-1, keepdims=True) + NORM_EPS)
        o_ref[...] = (xv * r * w_ref[...]).astype(o_ref.dtype)

    row = pl.BlockSpec((tl, d), lambda i: (i, 0))
    res = _call(body, name=name, grid=(l // tl,), in_specs=[row, pl.BlockSpec((1, d), lambda i: (0, 0))],
                out_specs=[row], out_shape=[jax.ShapeDtypeStruct((l, d), BF16)], args=(x, w), sem=("parallel",), ride=ride)
    return res[0] if ride is None else res


def _rms_bwd(x, w, dy, add, out_dtype, *, name, tl=256):
    l, d = x.shape
    tl = _tile(l, tl, 16)

    def body(x_ref, w_ref, dy_ref, add_ref, dx_ref, dw_ref):
        i = pl.program_id(0)
        xv = x_ref[...]
        r = lax.rsqrt(jnp.mean(xv * xv, axis=-1, keepdims=True) + NORM_EPS)
        xh = xv * r
        dyv = dy_ref[...].astype(F32)
        dxh = dyv * w_ref[...]
        dx = r * (dxh - xh * jnp.mean(dxh * xh, axis=-1, keepdims=True)) + add_ref[...].astype(F32)
        dx_ref[...] = dx.astype(dx_ref.dtype)
        part = jnp.sum(dyv * xh, axis=0, keepdims=True)

        @pl.when(i == 0)
        def _():
            dw_ref[...] = part

        @pl.when(i > 0)
        def _():
            dw_ref[...] += part

    row = pl.BlockSpec((tl, d), lambda i: (i, 0))
    vec = pl.BlockSpec((1, d), lambda i: (0, 0))
    return pl.pallas_call(
        body, name=name, grid=(l // tl,), in_specs=[row, vec, row, row], out_specs=[row, vec],
        out_shape=[jax.ShapeDtypeStruct((l, d), out_dtype), jax.ShapeDtypeStruct((1, d), F32)],
        compiler_params=_params("arbitrary"),
    )(x, w, dy, add)


def _loss_head(h, w, target, *, name, tl=256):
    l, d = h.shape
    tl = _tile(l, tl, 16)

    def body(x_ref, w_ref, t_ref, loss_ref, dxb_ref, dw_ref):
        i = pl.program_id(0)
        xv = x_ref[...]
        r = lax.rsqrt(jnp.mean(xv * xv, axis=-1, keepdims=True) + NORM_EPS)
        xh = xv * r
        wv = w_ref[...]
        diff = xh * wv - t_ref[...]
        lpart = 0.5 * jnp.sum(jnp.mean(diff * diff, axis=-1, keepdims=True), axis=0, keepdims=True)
        dyv = diff * (1.0 / d)
        dxh = dyv * wv
        dx = r * (dxh - xh * jnp.mean(dxh * xh, axis=-1, keepdims=True))
        dxb_ref[...] = dx.astype(BF16)
        part = jnp.sum(dyv * xh, axis=0, keepdims=True)

        @pl.when(i == 0)
        def _():
            dw_ref[...] = part
            loss_ref[...] = lpart

        @pl.when(i > 0)
        def _():
            dw_ref[...] += part
            loss_ref[...] += lpart

    row = pl.BlockSpec((tl, d), lambda i: (i, 0))
    vec = pl.BlockSpec((1, d), lambda i: (0, 0))
    one = pl.BlockSpec((1, 1), lambda i: (0, 0))
    return pl.pallas_call(
        body, name=name, grid=(l // tl,), in_specs=[row, vec, row], out_specs=[one, row, vec],
        out_shape=[jax.ShapeDtypeStruct((1, 1), F32), jax.ShapeDtypeStruct((l, d), BF16), jax.ShapeDtypeStruct((1, d), F32)],
        compiler_params=_params("arbitrary"),
    )(h, w, target)


def _conv_pre(ext, w_ref):
    acc = ext * w_ref[CONV_K - 1:CONV_K, :]
    for k in range(CONV_K - 1):
        acc = acc + pltpu.roll(ext, CONV_K - 1 - k, axis=0) * w_ref[k:k + 1, :]
    return acc


def _conv_fwd(proj, col0, cw, cb, *, name, tl=512, tc=1024):
    l = proj.shape[0]
    c = cw.shape[1]
    tl, tc = _tile(l, tl, 16), _tile(math.gcd(c, col0), tc, LANE)
    assert col0 % tc == 0
    off, hb = col0 // tc, tl // CONV_HALO

    def body(x_ref, halo_ref, w_ref, b_ref, o_ref):
        i = pl.program_id(0)
        halo = jnp.where(i > 0, halo_ref[...], 0.0)
        ext = jnp.concatenate([halo, x_ref[...]], axis=0)
        pre = _conv_pre(ext, w_ref)[CONV_HALO:] + b_ref[...]
        o_ref[...] = pre * _sigmoid(pre)

    return pl.pallas_call(
        body, name=name, grid=(l // tl, c // tc),
        in_specs=[pl.BlockSpec((tl, tc), lambda i, j: (i, off + j)),
                  pl.BlockSpec((CONV_HALO, tc), lambda i, j: (jnp.maximum(i * hb - 1, 0), off + j)),
                  pl.BlockSpec((CONV_K, tc), lambda i, j: (0, j)), pl.BlockSpec((1, tc), lambda i, j: (0, j))],
        out_specs=pl.BlockSpec((tl, tc), lambda i, j: (i, j)),
        out_shape=jax.ShapeDtypeStruct((l, c), F32), compiler_params=_params("parallel", "parallel"),
    )(proj, proj, cw, cb)


def _conv_bwd(proj, col0, cw, cb, dout, dproj, *, name, tl=512, tc=1024):
    l = proj.shape[0]
    c = dout.shape[1]
    tl, tc = _tile(l, tl, 16), _tile(math.gcd(c, col0), tc, LANE)
    assert col0 % tc == 0
    off, hb, nt = col0 // tc, tl // CONV_HALO, l // tl
    n_ext = tl + CONV_HALO

    def body(x_ref, prev_ref, nxt_ref, d_ref, dn_ref, w_ref, b_ref, dproj_in, dx_ref, dw_ref, db_ref):
        del dproj_in
        i = pl.program_id(1)
        last = i == nt - 1
        halo = jnp.where(i > 0, prev_ref[...], 0.0)
        ext = jnp.concatenate([halo, x_ref[...], nxt_ref[...]], axis=0)
        pre = _conv_pre(ext, w_ref)[CONV_HALO:] + b_ref[...]
        dext = jnp.concatenate([d_ref[...], jnp.where(last, 0.0, dn_ref[...])], axis=0)
        s = _sigmoid(pre)
        dpre = dext * (s * (1.0 + pre * (1.0 - s)))
        dx = dpre * w_ref[CONV_K - 1:CONV_K, :]
        for k in range(CONV_K - 1):
            dx = dx + pltpu.roll(dpre, n_ext - (CONV_K - 1 - k), axis=0) * w_ref[k:k + 1, :]
        dx_ref[...] = dx[:tl].astype(dx_ref.dtype)
        dp = dpre[:tl]
        rows = [jnp.sum(dp * pltpu.roll(ext, CONV_K - 1 - k, axis=0)[CONV_HALO:CONV_HALO + tl], axis=0, keepdims=True)
                for k in range(CONV_K - 1)]
        rows.append(jnp.sum(dp * ext[CONV_HALO:CONV_HALO + tl], axis=0, keepdims=True))
        dwp = jnp.concatenate(rows, axis=0)
        dbp = jnp.sum(dp, axis=0, keepdims=True)

        @pl.when(i == 0)
        def _():
            dw_ref[...] = dwp
            db_ref[...] = dbp

        @pl.when(i > 0)
        def _():
            dw_ref[...] += dwp
            db_ref[...] += dbp

    cur = lambda j, i: (i, j)
    nxt = lambda j, i: (jnp.minimum((i + 1) * hb, nt * hb - 1), j)
    return pl.pallas_call(
        body, name=name, grid=(c // tc, nt),
        in_specs=[pl.BlockSpec((tl, tc), lambda j, i: (i, off + j)),
                  pl.BlockSpec((CONV_HALO, tc), lambda j, i: (jnp.maximum(i * hb - 1, 0), off + j)),
                  pl.BlockSpec((CONV_HALO, tc), lambda j, i: (jnp.minimum((i + 1) * hb, nt * hb - 1), off + j)),
                  pl.BlockSpec((tl, tc), cur), pl.BlockSpec((CONV_HALO, tc), nxt),
                  pl.BlockSpec((CONV_K, tc), lambda j, i: (0, j)), pl.BlockSpec((1, tc), lambda j, i: (0, j)), ANY],
        out_specs=[pl.BlockSpec((tl, tc), lambda j, i: (i, off + j)), pl.BlockSpec((CONV_K, tc), lambda j, i: (0, j)),
                   pl.BlockSpec((1, tc), lambda j, i: (0, j))],
        out_shape=[jax.ShapeDtypeStruct(dproj.shape, dproj.dtype), jax.ShapeDtypeStruct((CONV_K, c), F32),
                   jax.ShapeDtypeStruct((1, c), F32)],
        input_output_aliases={7: 0}, compiler_params=_params("parallel", "arbitrary"),
    )(proj, proj, proj, dout, dout, cw, cb, dproj)


def _pool_fwd(proj, col0, d_pool, *, name, tl=256):
    l = proj.shape[0]
    tl = _tile(l, tl, POOL_HALO)
    assert col0 % d_pool == 0
    off, hb, pg = col0 // d_pool, tl // POOL_HALO, d_pool // GROUPS

    def body(u_ref, halo_ref, o_ref):
        i = pl.program_id(0)
        halo = jnp.where(i > 0, halo_ref[...], 0.0)
        ext = jnp.concatenate([halo, u_ref[...]], axis=0)
        t = i * tl + lax.broadcasted_iota(jnp.int32, (tl, 1), 0)
        s, width = ext, 1
        for gi, win in enumerate(POOL_WINDOWS):
            while width < win:
                s = s + pltpu.roll(s, width, axis=0)
                width *= 2
            cnt = jnp.minimum(t + 1, win).astype(F32)
            sl = slice(gi * pg, (gi + 1) * pg)
            o_ref[:, sl] = (s[POOL_HALO:, sl] / cnt - ext[POOL_HALO:, sl]).astype(o_ref.dtype)

    return pl.pallas_call(
        body, name=name, grid=(l // tl,),
        in_specs=[pl.BlockSpec((tl, d_pool), lambda i: (i, off)),
                  pl.BlockSpec((POOL_HALO, d_pool), lambda i: (jnp.maximum(i * hb - 1, 0), off))],
        out_specs=pl.BlockSpec((tl, d_pool), lambda i: (i, 0)),
        out_shape=jax.ShapeDtypeStruct((l, d_pool), BF16), compiler_params=_params("parallel"),
    )(proj, proj)


def _pool_bwd(dp, dproj, col0, *, name, tl=256):
    l, d_pool = dp.shape
    tl = _tile(l, tl, POOL_HALO)
    assert col0 % d_pool == 0
    off, hb, nt, pg = col0 // d_pool, tl // POOL_HALO, l // tl, d_pool // GROUPS
    n_ext = tl + POOL_HALO

    def body(d_ref, nxt_ref, dproj_in, o_ref):
        del dproj_in
        i = pl.program_id(0)
        ext = jnp.concatenate([d_ref[...], jnp.where(i == nt - 1, 0.0, nxt_ref[...])], axis=0)
        t = i * tl + lax.broadcasted_iota(jnp.int32, (n_ext, 1), 0)
        for gi, win in enumerate(POOL_WINDOWS):
            sl = slice(gi * pg, (gi + 1) * pg)
            dg = ext[:, sl]
            s = dg / jnp.minimum(t + 1, win).astype(F32)
            width = 1
            while width < win:
                s = s + pltpu.roll(s, n_ext - width, axis=0)
                width *= 2
            o_ref[:, sl] = (s[:tl] - dg[:tl]).astype(o_ref.dtype)

    return pl.pallas_call(
        body, name=name, grid=(nt,),
        in_specs=[pl.BlockSpec((tl, d_pool), lambda i: (i, 0)),
                  pl.BlockSpec((POOL_HALO, d_pool), lambda i: (jnp.minimum((i + 1) * hb, nt * hb - 1), 0)), ANY],
        out_specs=pl.BlockSpec((tl, d_pool), lambda i: (i, off)),
        out_shape=jax.ShapeDtypeStruct(dproj.shape, dproj.dtype), input_output_aliases={2: 0},
        compiler_params=_params("parallel"),
    )(dp, dp, dproj)


def _put_heads(src, dst, col0, width, *, name, tl=512):
    l, c = src.shape
    tl = _tile(l, tl, 16)
    assert col0 % width == 0 and width % LANE == 0 and c // HEAD_DIM <= width

    def body(s_ref, dst_in, o_ref):
        del dst_in
        pick = (lax.broadcasted_iota(jnp.int32, (c, width), 0)
                == HEAD_DIM * lax.broadcasted_iota(jnp.int32, (c, width), 1)).astype(BF16)
        hi, lo = _split2(s_ref[...])
        o_ref[...] = (jnp.dot(hi, pick, preferred_element_type=F32)
                      + jnp.dot(lo, pick, preferred_element_type=F32)).astype(o_ref.dtype)

    return pl.pallas_call(
        body, name=name, grid=(l // tl,), in_specs=[pl.BlockSpec((tl, c), lambda i: (i, 0)), ANY],
        out_specs=pl.BlockSpec((tl, width), lambda i: (i, col0 // width)),
        out_shape=jax.ShapeDtypeStruct(dst.shape, dst.dtype), input_output_aliases={1: 0},
        compiler_params=_params("parallel"),
    )(src, dst)


def _pool_mix_fwd(pooled, pw, scale, mixed, col0, *, name, tm=1024):
    l, d_pool = pooled.shape
    pg = d_pool // GROUPS
    tm = _tile(l, tm, 16)
    assert col0 % pg == 0
    off = col0 // pg

    def body(a_ref, w_ref, s_ref, mixed_in, raw_ref, mixed_ref):
        del mixed_in
        raw = _dot(a_ref[...], w_ref[...])
        raw_ref[...] = raw
        mixed_ref[...] = (raw * s_ref[...]).astype(mixed_ref.dtype)

    blk = pl.BlockSpec((tm, pg), lambda i, g: (i, g))
    return pl.pallas_call(
        body, name=name, grid=(l // tm, GROUPS),
        in_specs=[blk, pl.BlockSpec((None, pg, pg), lambda i, g: (g, 0, 0)), pl.BlockSpec((1, pg), lambda i, g: (0, g)),
                  pl.BlockSpec(memory_space=pl.ANY)],
        out_specs=[blk, pl.BlockSpec((tm, pg), lambda i, g: (i, off + g))],
        out_shape=[jax.ShapeDtypeStruct((l, d_pool), F32), jax.ShapeDtypeStruct(mixed.shape, mixed.dtype)],
        input_output_aliases={3: 1}, compiler_params=_params("parallel", "parallel"),
    )(pooled, pw, scale, mixed)


def _pool_mix_bwd(dmixed, col0, raw, scale, pw, *, name, tm=1024):
    l, d_pool = raw.shape
    pg = d_pool // GROUPS
    tm = _tile(l, tm, 16)
    assert col0 % pg == 0
    off = col0 // pg

    def body(d_ref, raw_ref, s_ref, w_ref, draw_ref, dp_ref, ds_ref):
        i = pl.program_id(1)
        dv = d_ref[...]
        draw = (dv * s_ref[...]).astype(BF16)
        draw_ref[...] = draw
        dp_ref[...] = _dot(draw, w_ref[...], tb=True)
        part = jnp.sum(dv * raw_ref[...], axis=0, keepdims=True)

        @pl.when(i == 0)
        def _():
            ds_ref[...] = part

        @pl.when(i > 0)
        def _():
            ds_ref[...] += part

    blk = pl.BlockSpec((tm, pg), lambda g, i: (i, g))
    vec = pl.BlockSpec((1, pg), lambda g, i: (0, g))
    return pl.pallas_call(
        body, name=name, grid=(GROUPS, l // tm),
        in_specs=[pl.BlockSpec((tm, pg), lambda g, i: (i, off + g)), blk, vec,
                  pl.BlockSpec((None, pg, pg), lambda g, i: (g, 0, 0))],
        out_specs=[blk, blk, vec],
        out_shape=[jax.ShapeDtypeStruct((l, d_pool), BF16), jax.ShapeDtypeStruct((l, d_pool), F32),
                   jax.ShapeDtypeStruct((1, d_pool), F32)],
        compiler_params=_params("parallel", "arbitrary"),
    )(dmixed, raw, scale, pw)


def _group_mm_tn(a, b, *, name, tk=1024):
    l, d_pool = a.shape
    pg = d_pool // GROUPS
    tk = _tile(l, tk, 16)

    def body(a_ref, b_ref, o_ref):
        kk = pl.program_id(1)
        part = _dot(a_ref[...], b_ref[...], ta=True)

        @pl.when(kk == 0)
        def _():
            o_ref[...] = part

        @pl.when(kk > 0)
        def _():
            o_ref[...] += part

    blk = pl.BlockSpec((tk, pg), lambda g, kk: (kk, g))
    return pl.pallas_call(
        body, name=name, grid=(GROUPS, l // tk), in_specs=[blk, blk],
        out_specs=pl.BlockSpec((None, pg, pg), lambda g, kk: (g, 0, 0)),
        out_shape=jax.ShapeDtypeStruct((GROUPS, pg, pg), F32), compiler_params=_params("parallel", "arbitrary"),
    )(a, b)


def _split3(v):
    hi = v.astype(BF16)
    r1 = v - hi.astype(F32)
    mid = r1.astype(BF16)
    lo = (r1 - mid.astype(F32)).astype(BF16)
    return hi, mid, lo


def _tri_dot(tri, v):
    hi, mid, lo = _split3(v)
    d = lambda p: jnp.dot(tri, p, preferred_element_type=F32)
    return d(hi) + d(mid) + d(lo)


def _dot_tri(v, tri):
    hi, mid, lo = _split3(v)
    d = lambda p: jnp.dot(p, tri, preferred_element_type=F32)
    return d(hi) + d(mid) + d(lo)


def _split2(v):
    hi = v.astype(BF16)
    return hi, (v - hi.astype(F32)).astype(BF16)


SCAN_CHUNKS = 4


def _scan_specs(hpg, nc, order):
    gw, rows = hpg * HEAD_DIM, SCAN_CHUNKS * CHUNK
    b_off, c_off = (GROUPS * gw) // STATE, (GROUPS * gw) // STATE + GROUPS
    xs = pl.BlockSpec((rows, gw), lambda g, c: (order(c), g))
    bm = pl.BlockSpec((rows, STATE), lambda g, c: (order(c), b_off + g))
    cm = pl.BlockSpec((rows, STATE), lambda g, c: (order(c), c_off + g))
    dtt = pl.BlockSpec((None, hpg, rows), lambda g, c: (g, 0, order(c)))
    vec = pl.BlockSpec((1, gw), lambda g, c: (0, g))
    hcol = pl.BlockSpec((None, hpg, 1), lambda g, c: (g, 0, 0))
    st = pl.BlockSpec((None, SCAN_CHUNKS, STATE, gw), lambda g, c: (g, order(c), 0, 0))
    return xs, bm, cm, dtt, vec, hcol, st


def _scan_common(raw, raw_t, alogx, alogt, hpg, want_sigmoid=False):
    q, gw, width = CHUNK, hpg * HEAD_DIM, raw.shape[1]
    spread = (lax.broadcasted_iota(jnp.int32, (width, gw), 0)
              == hpg * pl.program_id(0) + lax.broadcasted_iota(jnp.int32, (width, gw), 1) // HEAD_DIM).astype(BF16)
    dt = _dot_tri(_softplus(raw), spread)
    sg = None
    if want_sigmoid:
        hi, lo = _split2(_sigmoid(raw))
        sg = jnp.dot(hi, spread, preferred_element_type=F32) + jnp.dot(lo, spread, preferred_element_type=F32)
    a = -jnp.exp(alogx)
    dtt = _softplus(raw_t)
    at = -jnp.exp(alogt)
    row = lax.broadcasted_iota(jnp.int32, (q, q), 0)
    col = lax.broadcasted_iota(jnp.int32, (q, q), 1)
    lower = (row >= col).astype(BF16)
    upper = (row <= col).astype(BF16)
    acum = _tri_dot(lower, dt * a)
    acum_t = _dot_tri(dtt * at, upper)
    return dt, sg, a, acum, acum_t, row, col, upper


def _gated(yv, z):
    s = _sigmoid(z)
    sz = z * s
    gv = yv * sz
    return gv, lax.rsqrt(jnp.mean(gv * gv, axis=-1, keepdims=True) + NORM_EPS), sz, s


def _scan_fwd(xbc, proj, dt_col, dtrt, biasc, biast, alogx, alogt, dskx, wn, d_mix, *, name, ride=None):
    l = xbc.shape[0]
    hpg = dtrt.shape[1]
    gw, q, nc = hpg * HEAD_DIM, CHUNK, l // CHUNK
    assert hpg % 2 == 0 and nc % SCAN_CHUNKS == 0
    xs_s, bm_s, cm_s, dtt_s, vec, hcol, st_s = _scan_specs(hpg, nc, lambda c: c)
    width = biasc.shape[1]
    dtc_s = pl.BlockSpec((SCAN_CHUNKS * q, width), lambda g, c: (c, dt_col // width))
    biasc_s = pl.BlockSpec((1, width), lambda g, c: (0, 0))

    def body(xs_ref, b_ref, c_ref, dtc_ref, dtrt_ref, biasc_ref, biast_ref, alogx_ref, alogt_ref, dsk_ref, z_ref, wn_ref,
             y_ref, st_ref, mixed_ref, h_ref):
        @pl.when(pl.program_id(1) == 0)
        def _():
            h_ref[...] = jnp.zeros_like(h_ref)

        def chunk(sub, h_all):
            rows = slice(sub * q, (sub + 1) * q)
            dt, _, _, acum, acum_t, row, col, _ = _scan_common(
                dtc_ref[rows, :] + biasc_ref[...], dtrt_ref[:, rows] + biast_ref[...], alogx_ref[...], alogt_ref[...], hpg)
            causal = row >= col
            xs = xs_ref[rows, :]
            xdt = xs * dt
            xdtb = xdt.astype(BF16)
            bmat, cmat = b_ref[rows, :], c_ref[rows, :]
            cb = _dot(cmat, bmat, tb=True)
            st_ref[sub] = h_all
            rest = _dot(cmat, h_all) * jnp.exp(acum) + dsk_ref[...] * xs
            first = lax.broadcasted_iota(jnp.int32, (q, 2 * HEAD_DIM), 1) < HEAD_DIM
            parts = []
            for jp in range(hpg // 2):
                pair = slice(2 * jp * HEAD_DIM, (2 * jp + 2) * HEAD_DIM)
                xp = xdtb[:, pair]
                ys = []
                for j in (2 * jp, 2 * jp + 1):
                    a_col, a_row = acum[:, j * HEAD_DIM:j * HEAD_DIM + 1], acum_t[j:j + 1, :]
                    lm = jnp.exp(jnp.where(causal, a_col - a_row, NEG_BIG))
                    ys.append(_dot(cb * lm, xp))
                parts.append(jnp.where(first, ys[0], ys[1]))
            yv = jnp.concatenate(parts, axis=1) + rest
            y_ref[rows, :] = yv
            gv, r, _, _ = _gated(yv, z_ref[rows, :])
            mixed_ref[rows, :] = (gv * r * wn_ref[...]).astype(mixed_ref.dtype)
            a_last = acum[q - 1:q, :]
            return jnp.exp(a_last) * h_all + _dot(bmat, xdt * jnp.exp(a_last - acum), ta=True)

        h_all = h_ref[...]
        for sub in range(SCAN_CHUNKS):
            h_all = chunk(sub, h_all)
        h_ref[...] = h_all

    return _call(
        body, name=name, grid=(GROUPS, nc // SCAN_CHUNKS),
        in_specs=[xs_s, bm_s, cm_s, dtc_s, dtt_s, biasc_s, hcol, vec, hcol, vec, xs_s, vec], out_specs=[xs_s, st_s, xs_s],
        out_shape=[jax.ShapeDtypeStruct((l, GROUPS * gw), F32), jax.ShapeDtypeStruct((GROUPS, nc, STATE, gw), F32),
                   jax.ShapeDtypeStruct((l, d_mix), BF16)],
        scratch=[pltpu.VMEM((STATE, gw), F32)], sem=("parallel", "arbitrary"),
        args=(xbc, xbc, xbc, proj, dtrt, biasc, biast, alogx, alogt, dskx, proj, wn), ride=ride)


def _scan_bwd(xbc, proj, dt_col, dtrt, biasc, biast, alogx, alogt, dskx, states, dmixed, y, wn, *, name, ride=None):
    l = xbc.shape[0]
    hpg = dtrt.shape[1]
    gw, q, nc = hpg * HEAD_DIM, CHUNK, l // CHUNK
    nsteps = nc // SCAN_CHUNKS
    rev = lambda c: nsteps - 1 - c
    xs_s, bm_s, cm_s, dtt_s, vec, hcol, st_s = _scan_specs(hpg, nc, rev)
    bc_s = pl.BlockSpec((SCAN_CHUNKS * q, STATE), lambda g, c: (rev(c), g))
    width = biasc.shape[1]
    dtc_s = pl.BlockSpec((SCAN_CHUNKS * q, width), lambda g, c: (rev(c), dt_col // width))
    biasc_s = pl.BlockSpec((1, width), lambda g, c: (0, 0))

    def body(xs_ref, b_ref, c_ref, dtc_ref, dtrt_ref, biasc_ref, biast_ref, alogx_ref, alogt_ref, dsk_ref, st_ref,
             dmix_ref, y_ref, z_ref, wn_ref,
             dxs_ref, db_ref, dc_ref, ddt_ref, dbias_ref, dalog_ref, ddsk_ref, dz_ref, dwn_ref, dh_ref):
        step = pl.program_id(1)

        @pl.when(step == 0)
        def _():
            dh_ref[...] = jnp.zeros_like(dh_ref)

        def chunk(sub, dh):
            rws = slice(sub * q, (sub + 1) * q)
            yv, z, dmix = y_ref[rws, :], z_ref[rws, :], dmix_ref[rws, :]
            gv, rn, sz, sg = _gated(yv, z)
            gh = gv * rn
            dgh = dmix * wn_ref[...]
            dg = rn * (dgh - gh * jnp.mean(dgh * gh, axis=-1, keepdims=True))
            dyv = dg * sz
            dz_ref[rws, :] = (dg * yv * (sg * (1.0 + z * (1.0 - sg)))).astype(dz_ref.dtype)
            p_wn = jnp.sum(dmix * gh, axis=0, keepdims=True)

            dt, sg_dt, a, acum, acum_t, row, col, upper = _scan_common(
                dtc_ref[rws, :] + biasc_ref[...], dtrt_ref[:, rws] + biast_ref[...], alogx_ref[...], alogt_ref[...], hpg,
                want_sigmoid=True)
            causal, anti, strict = row >= col, row <= col, row > col
            xs = xs_ref[rws, :]
            xdt = xs * dt
            xdtb, dyb = xdt.astype(BF16), dyv.astype(BF16)
            bmat, cmat = b_ref[rws, :], c_ref[rws, :]
            cb = _dot(cmat, bmat, tb=True)
            cbt = _dot(bmat, cmat, tb=True)
            h_all = st_ref[sub]
            a_last = acum[q - 1:q, :]
            ea, ee, gam = jnp.exp(acum), jnp.exp(a_last - acum), jnp.exp(a_last)
            zc = _dot(cmat, h_all)
            bdh = _dot(bmat, dh)
            dz = dyv * ea
            xe = xdt * ee
            dc_state = _dot(dz, h_all, tb=True)
            db_state = _dot(xe, dh, tb=True)
            dh_in = gam * dh + _dot(cmat, dz, ta=True)
            lane = lax.broadcasted_iota(jnp.int32, (q, 2 * HEAD_DIM), 1)
            first = lane < HEAD_DIM
            ones = jnp.ones((q // 2, 2 * HEAD_DIM), BF16)
            dcb = jnp.zeros((q, q), F32)
            dxdt_parts, da_parts = [], []
            for jp in range(hpg // 2):
                pair = slice(2 * jp * HEAD_DIM, (2 * jp + 2) * HEAD_DIM)
                xp, dyp = xdtb[:, pair], dyb[:, pair]
                dxs_r, das_r = [], []
                for r, j in enumerate((2 * jp, 2 * jp + 1)):
                    seg = acum[:, j * HEAD_DIM:j * HEAD_DIM + 1] - acum_t[j:j + 1, :]
                    lm = jnp.exp(jnp.where(causal, seg, NEG_BIG))
                    lmt = jnp.exp(jnp.where(anti, -seg, NEG_BIG))
                    mm, mmt = cb * lm, cbt * lmt
                    keep = first if r == 0 else jnp.logical_not(first)
                    dyk = jnp.where(keep, dyp, 0)
                    dxs_r.append(_dot(mmt, dyp))
                    dm = _dot(dyk, xp, tb=True)
                    dcb = dcb + dm * lm
                    corner = jnp.dot(upper, (dm * mm).astype(BF16), preferred_element_type=F32)
                    corner = jnp.where(strict, corner, 0.0)
                    das_r.append(_dot(corner[:, :q // 2] + corner[:, q // 2:], ones))
                dxdt_parts.append(jnp.where(first, dxs_r[0], dxs_r[1]))
                da_parts.append(jnp.where(first, das_r[0], das_r[1]))
            dxdt = jnp.concatenate(dxdt_parts, axis=1) + bdh * ee
            da_intra = jnp.concatenate(da_parts, axis=1)
            dxs_ref[rws, :] = dxdt * dt + dsk_ref[...] * dyv
            dc_ref[rws, :] = dc_state + _dot(dcb, bmat)
            db_ref[rws, :] = db_state + _dot(dcb, cmat, ta=True)
            ri = lax.broadcasted_iota(jnp.int32, (gw, gw), 0) // HEAD_DIM
            ci = lax.broadcasted_iota(jnp.int32, (gw, gw), 1) // HEAD_DIM
            blockdiag = (ri == ci).astype(BF16)

            def head_sum(v):
                hi, lo = _split2(v)
                return (jnp.dot(hi, blockdiag, preferred_element_type=F32)
                        + jnp.dot(lo, blockdiag, preferred_element_type=F32))

            xt = xe * bdh
            small = jnp.concatenate([
                jnp.sum(xt, axis=0, keepdims=True) + gam * jnp.sum(dh * h_all, axis=0, keepdims=True),
                jnp.sum(dyv * xs, axis=0, keepdims=True), jnp.zeros((6, gw), F32)], axis=0)
            small = head_sum(small)
            rows = lax.broadcasted_iota(jnp.int32, (q, 1), 0)
            da_local = head_sum(dyv * zc * ea - xt) + jnp.where(rows == q - 1, small[0:1, :], 0.0)
            hi, lo = _split2(da_local)
            d_dta = (da_intra + jnp.dot(upper, hi, preferred_element_type=F32)
                     + jnp.dot(upper, lo, preferred_element_type=F32))
            d_raw = (d_dta * a + head_sum(dxdt * xs)) * sg_dt
            ddt_ref[rws, :] = d_raw
            p_bias = jnp.sum(d_raw, axis=0, keepdims=True)
            p_alog = jnp.sum(d_dta * dt, axis=0, keepdims=True) * a
            return dh_in, (p_bias, p_alog, small[1:2, :], p_wn)

        dh = dh_ref[...]
        sums = None
        for sub in reversed(range(SCAN_CHUNKS)):
            dh, part = chunk(sub, dh)
            sums = part if sums is None else tuple(s + p for s, p in zip(sums, part))
        dh_ref[...] = dh
        p_bias, p_alog, p_dsk, p_wn = sums

        @pl.when(step == 0)
        def _():
            dbias_ref[...] = p_bias
            dalog_ref[...] = p_alog
            ddsk_ref[...] = p_dsk
            dwn_ref[...] = p_wn

        @pl.when(step > 0)
        def _():
            dbias_ref[...] += p_bias
            dalog_ref[...] += p_alog
            ddsk_ref[...] += p_dsk
            dwn_ref[...] += p_wn

    wide = jax.ShapeDtypeStruct((l, GROUPS * gw), F32)
    narrow = jax.ShapeDtypeStruct((l, GROUPS * STATE), F32)
    vshape = jax.ShapeDtypeStruct((1, GROUPS * gw), F32)
    return _call(
        body, name=name, grid=(GROUPS, nsteps),
        in_specs=[xs_s, bm_s, cm_s, dtc_s, dtt_s, biasc_s, hcol, vec, hcol, vec, st_s, xs_s, xs_s, xs_s, vec],
        out_specs=[xs_s, bc_s, bc_s, xs_s, vec, vec, vec, xs_s, vec],
        out_shape=[wide, narrow, narrow, wide, vshape, vshape, vshape, jax.ShapeDtypeStruct(proj.shape, BF16), vshape],
        scratch=[pltpu.VMEM((STATE, gw), F32)], sem=("parallel", "arbitrary"),
        args=(xbc, xbc, xbc, proj, dtrt, biasc, biast, alogx, alogt, dskx, states, dmixed, y, proj, wn), ride=ride)


def _exchange(srcs, scatter, *, name):
    ride = _Ride(srcs, scatter)

    def body(*refs):
        src, dst, sems = refs[:ride.n], refs[ride.n:2 * ride.n], refs[2 * ride.n:]
        ride.start(src, dst, sems)
        ride.relay(src, dst, sems)
        ride.finish(src, dst, sems)

    return pl.pallas_call(body, name=name, in_specs=ride.in_specs, out_specs=ride.out_specs, out_shape=ride.out_shape,
                          scratch_shapes=ride.scratch)(*srcs)


def _adamw(parts, w, m, v, *, name, tr=256):
    lead = w.ndim == 3
    r, c = w.shape[-2:]
    tr = _tile(r, tr, 16)
    c1, c2 = 1.0 / (1.0 - ADAM_B1 ** ADAM_STEP), 1.0 / (1.0 - ADAM_B2 ** ADAM_STEP)

    def body(p_ref, w_ref, m_ref, v_ref, g_ref, d_ref, nm_ref, nv_ref):
        g = p_ref[0].astype(F32)
        for p in range(1, N_DEV):
            g = g + p_ref[p].astype(F32)
        nm = ADAM_B1 * m_ref[...] + (1.0 - ADAM_B1) * g
        nv = ADAM_B2 * v_ref[...] + (1.0 - ADAM_B2) * (g * g)
        g_ref[...] = g
        nm_ref[...] = nm
        nv_ref[...] = nv
        d_ref[...] = -ADAM_LR * ((nm * c1) / (jnp.sqrt(nv * c2) + ADAM_EPS) + ADAM_WD * w_ref[...])

    blk = pl.BlockSpec((None, tr, c), lambda i: (0, i, 0)) if lead else pl.BlockSpec((tr, c), lambda i: (i, 0))
    out = jax.ShapeDtypeStruct(w.shape, F32)
    return pl.pallas_call(
        body, name=name, grid=(r // tr,), in_specs=[pl.BlockSpec((N_DEV, tr, c), lambda i: (0, i, 0)), blk, blk, blk],
        out_specs=[blk, blk, blk, blk], out_shape=[out, out, out, out], compiler_params=_params("parallel"),
    )(parts, w, m, v)


def kernel(x, attn_norm_w, w_in, conv_w, conv_b, dt_bias, a_log, d_skip, ssd_norm_w, pool_w, pool_scale, w_out, ffn_norm_w, w_gate, w_up, w_down, final_norm_w, loss_target, m_attn_norm_w, m_w_in, m_conv_w, m_conv_b, m_dt_bias, m_a_log, m_d_skip, m_ssd_norm_w, m_pool_w, m_pool_scale, m_w_out, m_ffn_norm_w, m_w_gate, m_w_up, m_w_down, m_final_norm_w, v_attn_norm_w, v_w_in, v_conv_w, v_conv_b, v_dt_bias, v_a_log, v_d_skip, v_ssd_norm_w, v_pool_w, v_pool_scale, v_w_out, v_ffn_norm_w, v_w_gate, v_w_up, v_w_down, v_final_norm_w):
    l, d = x.shape[1], x.shape[2]
    heads = dt_bias.shape[1]
    hpg = heads // GROUPS
    d_ssm = heads * HEAD_DIM
    conv_ch = conv_b.shape[1]
    d_pool = pool_scale.shape[1]
    pg = d_pool // GROUPS
    d_mix = d_ssm + d_pool
    d_ff = w_gate.shape[2] * N_DEV
    d_in = w_in.shape[2] * N_DEV
    dt_pad = -(-heads // LANE) * LANE
    o_u, o_xbc, o_dt = d_ssm, d_ssm + d_pool, d_ssm + d_pool + conv_ch
    d_inp = o_dt + dt_pad
    tn_in = _tile(d_inp, 2560, LANE)
    g_dt, g_u = d_ssm + conv_ch, d_ssm + conv_ch + heads

    x2, tgt = x[0], loss_target[0]

    tr_ = lambda t: jnp.transpose(t, (0, 2, 1))
    c_in = d_in // N_DEV
    h0, gi, gp, gc = _rms_fwd(x2, attn_norm_w, name="attn_norm",
                              ride=_Ride([tr_(w_in)[0].astype(BF16), pool_w[0].astype(BF16), conv_w[0]], False))
    win = gi.reshape(d_in, d)
    wp = jnp.concatenate([win[:d_ssm], win[g_u:], win[d_ssm:g_dt], win[g_dt:g_u],
                          jnp.zeros((dt_pad - heads, d), BF16)], axis=0)
    pw = jnp.transpose(gp, (1, 0, 2, 3)).reshape(GROUPS, pg, pg)
    cw = jnp.transpose(gc, (1, 0, 2)).reshape(CONV_K, conv_ch)

    def expand(vec):
        return jnp.repeat(vec, HEAD_DIM, axis=1)

    def per_head(vec):
        return vec[:, ::HEAD_DIM]

    alog_x, dsk_x = expand(a_log), expand(d_skip)
    bias_w = jnp.pad(dt_bias, ((0, 0), (0, dt_pad - heads)))
    bias_c, alog_c = dt_bias.reshape(GROUPS, hpg, 1), a_log.reshape(GROUPS, hpg, 1)

    proj, go, gg = _mm(h0, wp, name="in_proj", tb=True, tm=512, tn=tn_in, tk=d,
                       ride=_Ride([w_out[0].astype(BF16), tr_(w_gate)[0].astype(BF16)], False))
    wo = go.reshape(d_mix, d)
    xbc = _conv_fwd(proj, o_xbc, cw, conv_b, name="conv_fwd")
    dt_raw = proj[:, o_dt:o_dt + heads]
    dtrt = jnp.transpose(dt_raw.reshape(l, GROUPS, hpg), (1, 2, 0))
    y, states, mixed, gu = _scan_fwd(xbc, proj, o_dt, dtrt, bias_w, bias_c, alog_x, alog_c, dsk_x, ssd_norm_w, d_mix,
                                     name="ssd_fwd", ride=_Ride([tr_(w_up)[0].astype(BF16)], False))
    pooled = _pool_fwd(proj, o_u, d_pool, name="pool_fwd")
    pool_raw, mixed = _pool_mix_fwd(pooled, pw, pool_scale, mixed, d_ssm, name="pool_mix")
    h1 = _mm(mixed, wo, name="out_proj", tm=512, tk=d_mix, residual=x2)
    h1n = _rms_fwd(h1, ffn_norm_w, name="ffn_norm")
    wide = lambda t: t.reshape(N_DEV // FFN_GROUP, -1, d)
    gg, gu = wide(gg), wide(gu)
    gate, up, act, gd = _gate_up(h1n, gg, gu, name="gate_up", ride=_Ride([w_down[0].astype(BF16)], False))
    gd = wide(gd)
    h2 = _mm_shards([(act, gd)], name="down_proj", per_step=1, residual=h1)

    loss11, dh2b, g_final = _loss_head(h2, final_norm_w.reshape(1, d), tgt, name="loss_head")
    dgate, dup = _gate_up_bwd(dh2b, gd, gate, up, name="gate_up_bwd")
    shards = lambda t: t.reshape(N_DEV, -1, d)
    g_wd = shards(_grad_shards(act, dh2b, name="grad_w_down"))
    dh1n, r_wd = _mm_shards([(dgate, gg), (dup, gu)], name="dgate_dup", out_dtype=BF16, tm=512, per_step=1,
                            ride=_Ride([g_wd], True))
    g_wg = shards(_grad_shards(dgate, h1n, name="grad_w_gate"))
    g_wu = shards(_grad_shards(dup, h1n, name="grad_w_up"))
    dh1b, g_ffn = _rms_bwd(h1, ffn_norm_w, dh1n, dh2b, BF16, name="ffn_norm_bwd")
    dmixed = _mm(dh1b, wo, name="dmixed", tb=True, tk=d)
    g_wo = _mm(mixed, dh1b, name="grad_w_out", ta=True, out_dtype=BF16, tm=1024, tn=1024, tk=2048)
    draw, dpooled, g_pscale = _pool_mix_bwd(dmixed, d_ssm, pool_raw, pool_scale, pw, name="pool_mix_bwd")
    g_pw = _group_mm_tn(pooled, draw, name="grad_pool_w")
    s_pw = jnp.transpose(g_pw.reshape(GROUPS, N_DEV, pg // N_DEV, pg), (1, 0, 2, 3)).astype(BF16)
    dxs, dbm, dcm, ddtx, g_bias, g_alog, g_dsk, dproj, g_ssdn, r_wg, r_wu = _scan_bwd(
        xbc, proj, o_dt, dtrt, bias_w, bias_c, alog_x, alog_c, dsk_x, states, dmixed, y, ssd_norm_w, name="ssd_bwd",
        ride=_Ride([g_wg, g_wu], True))
    dproj = _pool_bwd(dpooled, dproj, o_u, name="pool_bwd")
    n_bc = GROUPS * STATE
    segs = [(dxs, 0, d_ssm), (dbm, d_ssm, n_bc), (dcm, d_ssm + n_bc, n_bc)]
    g_cw, g_cb = [], []
    for si, (dseg, c0, width) in enumerate(segs):
        dproj, b, c_ = _conv_bwd(proj, o_xbc + c0, cw[:, c0:c0 + width], conv_b[:, c0:c0 + width], dseg, dproj,
                                 name=f"conv_bwd{si}")
        g_cw.append(b)
        g_cb.append(c_)
    dproj = _put_heads(ddtx, dproj, o_dt, dt_pad, name="put_ddt")
    g_wp, r_wo, r_pw = _mm(dproj, h0, name="grad_w_in", ta=True, out_dtype=BF16, tm=384, tn=d, tk=2048,
                           ride=_Ride([g_wo.reshape(N_DEV, d_mix // N_DEV, d), s_pw], True))
    g_win = jnp.concatenate([g_wp[:d_ssm], g_wp[o_xbc:o_dt + heads], g_wp[o_u:o_xbc]], axis=0).reshape(N_DEV, c_in, d)
    def cols8(g):
        return jnp.transpose(g.reshape(g.shape[0], N_DEV, g.shape[1] // N_DEV), (1, 0, 2))
    dh0, r_in, r_cw = _mm(dproj, wp, name="dh0", out_dtype=BF16, tk=tn_in,
                          ride=_Ride([g_win, cols8(jnp.concatenate(g_cw, axis=1))], True))
    dx, g_attn = _rms_bwd(x2, attn_norm_w, dh0, dh1b, F32, name="attn_norm_bwd")

    smalls = [g_attn, jnp.concatenate(g_cb, axis=1), per_head(g_bias), per_head(g_alog), per_head(g_dsk), g_ssdn,
              g_pscale, g_ffn, g_final]
    small_w = [attn_norm_w, conv_b, dt_bias, a_log, d_skip, ssd_norm_w, pool_scale, ffn_norm_w, final_norm_w.reshape(1, d)]
    small_m = [m_attn_norm_w, m_conv_b, m_dt_bias, m_a_log, m_d_skip, m_ssd_norm_w, m_pool_scale, m_ffn_norm_w,
               m_final_norm_w.reshape(1, d)]
    small_v = [v_attn_norm_w, v_conv_b, v_dt_bias, v_a_log, v_d_skip, v_ssd_norm_w, v_pool_scale, v_ffn_norm_w,
               v_final_norm_w.reshape(1, d)]
    sizes = [s.shape[1] for s in smalls]
    n_small = sum(sizes)
    n_pad = -(-n_small // (16 * LANE)) * (16 * LANE)
    rows = n_pad // LANE
    def pack(vs):
        return jnp.pad(jnp.concatenate(vs, axis=1), ((0, 0), (0, n_pad - n_small))).reshape(rows, LANE)
    (r_small,) = _exchange([pack(smalls)], False, name="gather_small_grads")

    def big(parts, w, m, v, nm):
        shp = w.shape
        if w.ndim == 3:
            return _adamw(parts, w, m, v, name=nm)
        r2 = lambda t: t.reshape(-1, shp[-1])
        outs = _adamw(parts.reshape(N_DEV, -1, shp[-1]), r2(w), r2(m), r2(v), name=nm)
        return [o.reshape(shp) for o in outs]

    def big_t(parts, w, m, v, nm):
        return [tr_(o) for o in _adamw(parts, tr_(w), tr_(m), tr_(v), name=nm)]

    def big_flat(parts, w, m, v, nm):
        fl = lambda t: tr_(t).reshape(-1, LANE)
        outs = _adamw(parts.reshape(N_DEV, -1, LANE), fl(w), fl(m), fl(v), name=nm, tr=2048)
        return [tr_(o.reshape(1, w.shape[2], w.shape[1])) for o in outs]

    res = {
        "w_in": big_flat(r_in, w_in, m_w_in, v_w_in, "adamw_w_in"),
        "conv_w": big(r_cw, conv_w, m_conv_w, v_conv_w, "adamw_conv_w"),
        "pool_w": big(r_pw, pool_w, m_pool_w, v_pool_w, "adamw_pool_w"),
        "w_out": big(r_wo, w_out, m_w_out, v_w_out, "adamw_w_out"),
        "w_gate": big_t(r_wg, w_gate, m_w_gate, v_w_gate, "adamw_w_gate"),
        "w_up": big_t(r_wu, w_up, m_w_up, v_w_up, "adamw_w_up"),
        "w_down": big(r_wd, w_down, m_w_down, v_w_down, "adamw_w_down"),
    }
    s_out = _adamw(r_small, pack(small_w), pack(small_m), pack(small_v), name="adamw_small")
    names = ["attn_norm_w", "conv_b", "dt_bias", "a_log", "d_skip", "ssd_norm_w", "pool_scale", "ffn_norm_w", "final_norm_w"]
    offs = [sum(sizes[:i]) for i in range(len(sizes))]
    for i, nm in enumerate(names):
        shp = (d,) if nm == "final_norm_w" else (1, sizes[i])
        res[nm] = [o.reshape(1, n_pad)[:, offs[i]:offs[i] + sizes[i]].reshape(shp) for o in s_out]

    loss = lax.psum(loss11[0, 0], ("x", "y", "c"))
    order = ["attn_norm_w", "w_in", "conv_w", "conv_b", "dt_bias", "a_log", "d_skip", "ssd_norm_w", "pool_w", "pool_scale",
             "w_out", "ffn_norm_w", "w_gate", "w_up", "w_down", "final_norm_w"]
    outs = [loss, dx.reshape(x.shape)]
    for part in range(4):
        outs += [res[nm][part] for nm in order]
    return tuple(outs)
```
